```python
import jax, jax.numpy as jnp
from jax import lax
import numpy as np

D_MODEL = 2048
BATCH = 4
SEQ = 4096
DEPTH = 2

RMS_EPS = 1e-6
RET_HEADS = 4
RET_QK_DIM = 256
RET_V_DIM = 256
RET_CHUNK = 128
ROPE_BASE = 10000.0
GMLP_GROUPS = 4
GMLP_GROUP_DIM = 256
GMLP_CHUNK = 128
DSA_HEADS = 8
DSA_HEAD_DIM = 128
DSA_LATENT = 256
IDX_HEADS = 8
IDX_DIM = 64
DSA_TOPK_MAX = 256
DSA_QBLOCK = 128
MOE_GROUPS = 4
MOE_EXPERTS_PER_GROUP = 8
MOE_TOPK = 2
MOE_HIDDEN = 512

RET_QK = RET_HEADS * RET_QK_DIM
RET_V = RET_HEADS * RET_V_DIM
GMLP_WIDTH = GMLP_GROUPS * GMLP_GROUP_DIM
DSA_Q = DSA_HEADS * DSA_HEAD_DIM
IDX_Q = IDX_HEADS * IDX_DIM
IN_SPLITS = (RET_QK, RET_QK, RET_V, RET_V, GMLP_WIDTH, GMLP_WIDTH, DSA_Q, DSA_LATENT,
             IDX_Q, IDX_DIM, IDX_HEADS, D_MODEL, D_MODEL, D_MODEL)
IN_WIDTH = sum(IN_SPLITS)

kernel_name = "hybrid_retention_gmlp_dsa_hmoe"


def rms_norm(x, g):
    xf = x.astype(jnp.float32)
    y = xf * lax.rsqrt(jnp.mean(xf * xf, axis=-1, keepdims=True) + RMS_EPS)
    return (y * g.astype(jnp.float32)).astype(x.dtype)


def layer_norm(x, g, b):
    xf = x.astype(jnp.float32)
    mu = jnp.mean(xf, axis=-1, keepdims=True)
    var = jnp.mean(jnp.square(xf - mu), axis=-1, keepdims=True)
    y = (xf - mu) * lax.rsqrt(var + RMS_EPS)
    return (y * g.astype(jnp.float32) + b.astype(jnp.float32)).astype(x.dtype)


def split_in_projection(proj):
    offsets = np.cumsum(IN_SPLITS)[:-1].tolist()
    return jnp.split(proj, offsets, axis=-1)


def rotary(x, pos):
    half = x.shape[-1] // 2
    inv = ROPE_BASE ** (-jnp.arange(half, dtype=jnp.float32) / half)
    ang = pos.astype(jnp.float32)[:, None] * inv[None, :]
    cos = jnp.cos(ang)[None, :, None, :]
    sin = jnp.sin(ang)[None, :, None, :]
    x1, x2 = x[..., :half].astype(jnp.float32), x[..., half:].astype(jnp.float32)
    return jnp.concatenate([x1 * cos - x2 * sin, x1 * sin + x2 * cos], axis=-1).astype(x.dtype)


def retention(q, k, v, gate):
    B, S, H, dk = q.shape
    dv = v.shape[-1]
    C = RET_CHUNK
    N = S // C
    pos = jnp.arange(S)
    q = rotary(q, pos)
    k = rotary(k, pos) * (dk ** -0.5)
    log_g = jnp.log(1.0 - 2.0 ** (-5.0 - jnp.arange(H, dtype=jnp.float32)))
    i = jnp.arange(C, dtype=jnp.float32)
    diff = i[:, None] - i[None, :]
    decay = jnp.where(diff >= 0, jnp.exp(log_g[:, None, None] * jnp.maximum(diff, 0.0)), 0.0)
    xi = jnp.exp(log_g[None, :] * (i[:, None] + 1.0))[None, :, :, None]
    zeta = jnp.exp(log_g[None, :] * (C - 1.0 - i[:, None]))[None, :, :, None]
    g_chunk = jnp.exp(log_g * C)[None, :, None, None]

    def to_chunks(t):
        return t.reshape(B, N, C, H, t.shape[-1]).swapaxes(0, 1)

    def step(R, inp):
        qc, kc, vc = inp
        inner = jnp.einsum('bihd,bjhd->bhij', qc, kc) * decay
        o = jnp.einsum('bhij,bjhe->bihe', inner, vc) + jnp.einsum('bihd,bhde->bihe', qc, R) * xi
        R = R * g_chunk + jnp.einsum('bjhd,bjhe->bhde', kc * zeta, vc)
        return R, o

    R0 = jnp.zeros((B, H, dk, dv), jnp.float32)
    _, o = lax.scan(step, R0, (to_chunks(q), to_chunks(k), to_chunks(v)))
    o = o.swapaxes(0, 1).reshape(B, S, H, dv).astype(jnp.float32)
    o = o * lax.rsqrt(jnp.mean(o * o, axis=-1, keepdims=True) + RMS_EPS)
    o = o.reshape(B, S, H * dv).astype(gate.dtype)
    return jax.nn.silu(gate) * o


def chunked_spatial_gating(u, v, ln_g, ln_b, w_s, b_s):
    B, S, _ = v.shape
    C = GMLP_CHUNK
    N = S // C
    v = layer_norm(v, ln_g, ln_b)
    vc = v.reshape(B, N, C, GMLP_GROUPS, GMLP_GROUP_DIM)
    mask = jnp.tril(jnp.ones((C, C), dtype=bool))
    w = jnp.where(mask[None], w_s, 0.0).astype(v.dtype)
    mixed = jnp.einsum('gij,bnjgc->bnigc', w, vc) + b_s.T[None, None, :, :, None]
    return u * mixed.reshape(B, S, GMLP_WIDTH)


def dsa_attention(q, c_kv, q_idx, k_idx, w_idx, w_uk, w_uv):
    B, S, H, dh = q.shape
    k_sel = min(DSA_TOPK_MAX, S // 4)
    nb = S // DSA_QBLOCK
    q_abs = jnp.einsum('bshd,hcd->bshc', q, w_uk) * (dh ** -0.5)
    key_pos = jnp.arange(S)

    def blocks(t):
        return t.reshape(B, nb, DSA_QBLOCK, *t.shape[2:]).swapaxes(0, 1)

    def attend_block(args):
        blk, qa, qi, wi = args
        q_pos = blk * DSA_QBLOCK + jnp.arange(DSA_QBLOCK)
        causal = key_pos[None, :] <= q_pos[:, None]
        logits = jnp.einsum('bqhd,bsd->bqhs', qi, k_idx) * (IDX_DIM ** -0.5)
        score = jnp.einsum('bqh,bqhs->bqs', wi.astype(jnp.float32),
                           jax.nn.relu(logits).astype(jnp.float32))
        score = jnp.where(causal[None], score, -jnp.inf)
        _, sel = lax.top_k(score, k_sel)
        valid = sel <= q_pos[None, :, None]
        c_sel = jax.vmap(lambda c, ix: c[ix])(c_kv, sel)
        s = jnp.einsum('bqhc,bqkc->bqhk', qa, c_sel).astype(jnp.float32)
        s = jnp.where(valid[:, :, None, :], s, -jnp.inf)
        p = jax.nn.softmax(s, axis=-1).astype(c_sel.dtype)
        return jnp.einsum('bqhk,bqkc->bqhc', p, c_sel)

    o_lat = lax.map(attend_block, (jnp.arange(nb), blocks(q_abs), blocks(q_idx), blocks(w_idx)))
    o_lat = o_lat.swapaxes(0, 1).reshape(B, S, H, DSA_LATENT)
    o = jnp.einsum('bshc,hcd->bshd', o_lat, w_uv)
    return o.reshape(B, S, H * dh)


def hierarchical_moe(h, w_grp, b_grp, w_exp, b_exp, w1, w3, w2):
    B, S, D = h.shape
    G, E = MOE_GROUPS, MOE_EXPERTS_PER_GROUP
    t = h.reshape(B * S, D)
    grp_prob = jax.nn.softmax((t @ w_grp).astype(jnp.float32) + b_grp.astype(jnp.float32), axis=-1)
    g_sel = jnp.argmax(grp_prob, axis=-1)
    p_g = jnp.max(grp_prob, axis=-1)
    grp_onehot = jax.nn.one_hot(g_sel, G, dtype=jnp.float32)
    exp_logits = jnp.einsum('td,gde->tge', t, w_exp).astype(jnp.float32) + b_exp.astype(jnp.float32)
    logits_sel = jnp.sum(grp_onehot[:, :, None] * exp_logits, axis=1)
    top_v, top_i = lax.top_k(jax.nn.softmax(logits_sel, axis=-1), MOE_TOPK)
    top_v = top_v / jnp.sum(top_v, axis=-1, keepdims=True)
    w_in_group = jnp.sum(jax.nn.one_hot(top_i, E, dtype=jnp.float32) * top_v[..., None], axis=1)
    combine = (grp_onehot[:, :, None] * (p_g[:, None] * w_in_group)[:, None, :]).astype(t.dtype)
    y = jnp.zeros_like(t)
    for g in range(G):
        hid = jax.nn.silu(jnp.einsum('td,edf->tef', t, w1[g])) * jnp.einsum('td,edf->tef', t, w3[g])
        y = y + jnp.einsum('tef,efd->td', hid * combine[:, g, :, None], w2[g])
    return y.reshape(B, S, D)


def setup_inputs(seed: int = 0) -> dict:
    key = jax.random.key(seed)
    ks = jax.random.split(key, 24)
    f32 = jnp.float32
    L, D = DEPTH, D_MODEL
    G, E, F = MOE_GROUPS, MOE_EXPERTS_PER_GROUP, MOE_HIDDEN

    def nrm(k, shape, fan_in):
        return jax.random.normal(k, shape, f32) * (fan_in ** -0.5)

    def gain(k, shape):
        return 1.0 + 0.02 * jax.random.normal(k, shape, f32)

    return {
        "x": jax.random.normal(ks[0], (BATCH, SEQ, D), f32),
        "norm_mix_g": gain(ks[1], (L, D)),
        "w_in": nrm(ks[2], (L, D, IN_WIDTH), D),
        "w_ret_o": nrm(ks[3], (L, RET_V, D), RET_V),
        "gmlp_ln_g": gain(ks[4], (L, GMLP_WIDTH)),
        "gmlp_ln_b": 0.02 * jax.random.normal(ks[5], (L, GMLP_WIDTH), f32),
        "gmlp_w_s": nrm(ks[6], (L, GMLP_GROUPS, GMLP_CHUNK, GMLP_CHUNK), GMLP_CHUNK),
        "gmlp_b_s": gain(ks[7], (L, GMLP_GROUPS, GMLP_CHUNK)),
        "w_gmlp_o": nrm(ks[8], (L, GMLP_WIDTH, D), GMLP_WIDTH),
        "dsa_kv_norm_g": gain(ks[9], (L, DSA_LATENT)),
        "dsa_w_uk": nrm(ks[10], (L, DSA_HEADS, DSA_LATENT, DSA_HEAD_DIM), DSA_LATENT),
        "dsa_w_uv": nrm(ks[11], (L, DSA_HEADS, DSA_LATENT, DSA_HEAD_DIM), DSA_LATENT),
        "w_dsa_o": nrm(ks[12], (L, DSA_Q, D), DSA_Q),
        "w_out": nrm(ks[13], (L, D, D), D),
        "norm_ffn_g": gain(ks[14], (L, D)),
        "moe_w_group": nrm(ks[15], (L, D, G), D),
        "moe_b_group": 0.01 * jax.random.normal(ks[16], (L, G), f32),
        "moe_w_expert": nrm(ks[17], (L, G, D, E), D),
        "moe_b_expert": 0.01 * jax.random.normal(ks[18], (L, G, E), f32),
        "moe_w1": nrm(ks[19], (L, G, E, D, F), D),
        "moe_w3": nrm(ks[20], (L, G, E, D, F), D),
        "moe_w2": nrm(ks[21], (L, G, E, F, D), F),
        "final_norm_g": gain(ks[22], (D,)),
    }


def reference(x, norm_mix_g, w_in, w_ret_o, gmlp_ln_g, gmlp_ln_b, gmlp_w_s, gmlp_b_s, w_gmlp_o,
              dsa_kv_norm_g, dsa_w_uk, dsa_w_uv, w_dsa_o, w_out, norm_ffn_g,
              moe_w_group, moe_b_group, moe_w_expert, moe_b_expert, moe_w1, moe_w3, moe_w2,
              final_norm_g):
    B, S, _ = x.shape
    for l in range(DEPTH):
        h = rms_norm(x, norm_mix_g[l])
        proj = h @ w_in[l]
        (rq, rk, rv, rg, gu, gv, dq, dc, iq, ik, iw,
         gate_a, gate_b, gate_c) = split_in_projection(proj)
        ret = retention(rq.reshape(B, S, RET_HEADS, RET_QK_DIM),
                        rk.reshape(B, S, RET_HEADS, RET_QK_DIM),
                        rv.reshape(B, S, RET_HEADS, RET_V_DIM), rg)
        gm = chunked_spatial_gating(jax.nn.gelu(gu, approximate=False), jax.nn.gelu(gv, approximate=False),
                                    gmlp_ln_g[l], gmlp_ln_b[l], gmlp_w_s[l], gmlp_b_s[l])
        ds = dsa_attention(dq.reshape(B, S, DSA_HEADS, DSA_HEAD_DIM),
                           rms_norm(dc, dsa_kv_norm_g[l]),
                           iq.reshape(B, S, IDX_HEADS, IDX_DIM), ik,
                           iw * (IDX_HEADS ** -0.5), dsa_w_uk[l], dsa_w_uv[l])
        merged = (jax.nn.sigmoid(gate_a) * (ret @ w_ret_o[l])
                  + jax.nn.sigmoid(gate_b) * (gm @ w_gmlp_o[l])
                  + jax.nn.sigmoid(gate_c) * (ds @ w_dsa_o[l]))
        x = x + merged @ w_out[l]
        h = rms_norm(x, norm_ffn_g[l])
        x = x + hierarchical_moe(h, moe_w_group[l], moe_b_group[l], moe_w_expert[l], moe_b_expert[l],
                                 moe_w1[l], moe_w3[l], moe_w2[l])
    return rms_norm(x, final_norm_g)
```

```python
import functools
import math

import jax
import jax.numpy as jnp
from jax import lax
from jax.experimental import pallas as pl
from jax.experimental.pallas import tpu as pltpu

F32 = jnp.float32
BF16 = jnp.bfloat16

RMS_EPS = 1e-6
RET_HEADS = 4
RET_DIM = 256
CHUNK = 128
ROPE_BASE = 10000.0
GMLP_GROUPS = 4
GMLP_GROUP_DIM = 256
DSA_HEADS = 8
DSA_HEAD_DIM = 128
DSA_LATENT = 256
IDX_HEADS = 8
IDX_DIM = 64
DSA_TOPK_MAX = 256
MOE_GROUPS = 4
MOE_EXPERTS_PER_GROUP = 8
MOE_EXPERTS = MOE_GROUPS * MOE_EXPERTS_PER_GROUP
MOE_HIDDEN = 512
LANES = 128
NEG_BIG = -1e30

COL_RQ, COL_RK, COL_RV, COL_RG = 0, 1024, 2048, 3072
COL_GU, COL_GV = 4096, 5120
COL_DQ, COL_IQ, COL_DC, COL_IKW = 6144, 7168, 7680, 7936
COL_GATES = 8192
PROJ_WIDTH = 14336
VMEM_LIMIT = 56 * 1024 * 1024


def _params(*sem):
    return pltpu.CompilerParams(dimension_semantics=sem, vmem_limit_bytes=VMEM_LIMIT)


def _rms(x, g):
    return x * lax.rsqrt(jnp.mean(x * x, axis=-1, keepdims=True) + RMS_EPS) * g


def _dot(a, b):
    return jnp.dot(a, b, preferred_element_type=F32)


def _dot_nt(a, b):
    return lax.dot_general(a, b, (((1,), (1,)), ((), ())), preferred_element_type=F32)


def _norm_matmul_kernel(x_ref, g_ref, w_ref, o_ref, h_ref):
    @pl.when(pl.program_id(1) == 0)
    def _():
        h_ref[...] = _rms(x_ref[...], g_ref[...]).astype(BF16)

    o_ref[...] = _dot(h_ref[...], w_ref[...]).astype(o_ref.dtype)


def norm_matmul(x, g, w, *, tm=1024, tn=512, out_dtype=BF16):
    T, D = x.shape
    N = w.shape[1]
    return pl.pallas_call(
        _norm_matmul_kernel,
        grid=(T // tm, N // tn),
        in_specs=[pl.BlockSpec((tm, D), lambda i, j: (i, 0)),
                  pl.BlockSpec((1, D), lambda i, j: (0, 0)),
                  pl.BlockSpec((D, tn), lambda i, j: (0, j))],
        out_specs=pl.BlockSpec((tm, tn), lambda i, j: (i, j)),
        out_shape=jax.ShapeDtypeStruct((T, N), out_dtype),
        scratch_shapes=[pltpu.VMEM((tm, D), BF16)],
        compiler_params=_params("parallel", "arbitrary"),
        name="norm_in_proj",
    )(x, g, w)


def _retention_kernel(q_ref, k_ref, v_ref, gate_ref, cos_ref, sin_ref, decay_ref, xi_ref,
                      zeta_ref, gchunk_ref, o_ref, r_ref):
    @pl.when(pl.program_id(1) == 0)
    def _():
        r_ref[...] = jnp.zeros_like(r_ref)

    cos = cos_ref[...]
    sin = sin_ref[...]
    half = RET_DIM // 2

    def rotary(t):
        t1, t2 = t[:, :half], t[:, half:]
        return jnp.concatenate([t1 * cos - t2 * sin, t1 * sin + t2 * cos], axis=-1)

    for h in range(RET_HEADS):
        cols = slice(h * RET_DIM, (h + 1) * RET_DIM)
        q = rotary(q_ref[:, cols].astype(F32)).astype(BF16)
        k = rotary(k_ref[:, cols].astype(F32)) * (RET_DIM ** -0.5)
        v = v_ref[:, cols]
        inner = _dot_nt(q, k.astype(BF16)) * decay_ref[h]
        r_old = r_ref[h]
        o = _dot(inner.astype(BF16), v) + _dot(q, r_old.astype(BF16)) * xi_ref[:, h:h + 1]
        kz = (k * zeta_ref[:, h:h + 1]).astype(BF16)
        r_ref[h] = r_old * gchunk_ref[h] + _dot(kz.T, v)
        o = o * lax.rsqrt(jnp.mean(o * o, axis=-1, keepdims=True) + RMS_EPS)
        gate = gate_ref[:, cols].astype(F32)
        o_ref[:, cols] = (gate * jax.nn.sigmoid(gate) * o).astype(o_ref.dtype)


def retention(proj, batch, seq):
    T = proj.shape[0]
    n_chunks = seq // CHUNK
    width = RET_HEADS * RET_DIM
    half = RET_DIM // 2
    inv = ROPE_BASE ** (-jnp.arange(half, dtype=F32) / half)
    ang = jnp.arange(seq).astype(F32)[:, None] * inv[None, :]
    cos, sin = jnp.cos(ang), jnp.sin(ang)
    log_g = jnp.log(1.0 - 2.0 ** (-5.0 - jnp.arange(RET_HEADS, dtype=F32)))
    i = jnp.arange(CHUNK, dtype=F32)
    diff = i[:, None] - i[None, :]
    decay = jnp.where(diff >= 0, jnp.exp(log_g[:, None, None] * jnp.maximum(diff, 0.0)), 0.0)
    xi = jnp.exp(log_g[None, :] * (i[:, None] + 1.0))
    zeta = jnp.exp(log_g[None, :] * (CHUNK - 1.0 - i[:, None]))
    gchunk = jnp.exp(log_g * CHUNK)

    def col_spec(col):
        return pl.BlockSpec((CHUNK, width), lambda b, n: (b * n_chunks + n, col // width))

    return pl.pallas_call(
        _retention_kernel,
        grid=(batch, n_chunks),
        in_specs=[col_spec(COL_RQ), col_spec(COL_RK), col_spec(COL_RV), col_spec(COL_RG),
                  pl.BlockSpec((CHUNK, half), lambda b, n: (n, 0)),
                  pl.BlockSpec((CHUNK, half), lambda b, n: (n, 0)),
                  pl.BlockSpec((RET_HEADS, CHUNK, CHUNK), lambda b, n: (0, 0, 0)),
                  pl.BlockSpec((CHUNK, RET_HEADS), lambda b, n: (0, 0)),
                  pl.BlockSpec((CHUNK, RET_HEADS), lambda b, n: (0, 0)),
                  pl.BlockSpec(memory_space=pltpu.SMEM)],
        out_specs=pl.BlockSpec((CHUNK, width), lambda b, n: (b * n_chunks + n, 0)),
        out_shape=jax.ShapeDtypeStruct((T, width), BF16),
        scratch_shapes=[pltpu.VMEM((RET_HEADS, RET_DIM, RET_DIM), F32)],
        compiler_params=_params("parallel", "arbitrary"),
        name="retention",
    )(proj, proj, proj, proj, cos, sin, decay, xi, zeta, gchunk)


def _gelu(x):
    return 0.5 * x * (1.0 + lax.erf(x * (2.0 ** -0.5)))


def _gmlp_kernel(u_ref, v_ref, lng_ref, lnb_ref, ws_ref, bs_ref, o_ref):
    v = _gelu(v_ref[...].astype(F32))
    mu = jnp.mean(v, axis=-1, keepdims=True)
    var = jnp.mean(jnp.square(v - mu), axis=-1, keepdims=True)
    vn = ((v - mu) * lax.rsqrt(var + RMS_EPS) * lng_ref[...] + lnb_ref[...]).astype(BF16)
    row = lax.broadcasted_iota(jnp.int32, (CHUNK, CHUNK), 0)
    col = lax.broadcasted_iota(jnp.int32, (CHUNK, CHUNK), 1)
    for g in range(GMLP_GROUPS):
        cols = slice(g * GMLP_GROUP_DIM, (g + 1) * GMLP_GROUP_DIM)
        w = jnp.where(row >= col, ws_ref[g], 0.0).astype(BF16)
        mixed = _dot(w, vn[:, cols]) + bs_ref[:, g:g + 1]
        u = _gelu(u_ref[:, cols].astype(F32))
        o_ref[:, cols] = (u * mixed).astype(o_ref.dtype)


def gmlp(proj, ln_g, ln_b, w_s, b_s):
    T = proj.shape[0]
    width = GMLP_GROUPS * GMLP_GROUP_DIM
    return pl.pallas_call(
        _gmlp_kernel,
        grid=(T // CHUNK,),
        in_specs=[pl.BlockSpec((CHUNK, width), lambda i: (i, COL_GU // width)),
                  pl.BlockSpec((CHUNK, width), lambda i: (i, COL_GV // width)),
                  pl.BlockSpec((1, width), lambda i: (0, 0)),
                  pl.BlockSpec((1, width), lambda i: (0, 0)),
                  pl.BlockSpec((GMLP_GROUPS, CHUNK, CHUNK), lambda i: (0, 0, 0)),
                  pl.BlockSpec((CHUNK, GMLP_GROUPS), lambda i: (0, 0))],
        out_specs=pl.BlockSpec((CHUNK, width), lambda i: (i, 0)),
        out_shape=jax.ShapeDtypeStruct((T, width), BF16),
        compiler_params=_params("parallel"),
        name="gmlp",
    )(proj, proj, ln_g.reshape(1, width), ln_b.reshape(1, width), w_s, b_s.T)


def _dsa_prep_kernel(dq_ref, dc_ref, g_ref, wuk_ref, qabs_ref, ckv_ref):
    ckv_ref[...] = _rms(dc_ref[...].astype(F32), g_ref[...]).astype(ckv_ref.dtype)
    for h in range(DSA_HEADS):
        q = dq_ref[:, h * DSA_HEAD_DIM:(h + 1) * DSA_HEAD_DIM]
        qabs_ref[0, h] = (_dot_nt(q, wuk_ref[h]) * (DSA_HEAD_DIM ** -0.5)).astype(qabs_ref.dtype)


def dsa_prep(proj, kv_norm_g, w_uk):
    T = proj.shape[0]
    nblk = T // CHUNK
    qw = DSA_HEADS * DSA_HEAD_DIM
    return pl.pallas_call(
        _dsa_prep_kernel,
        grid=(nblk,),
        in_specs=[pl.BlockSpec((CHUNK, qw), lambda i: (i, COL_DQ // qw)),
                  pl.BlockSpec((CHUNK, DSA_LATENT), lambda i: (i, COL_DC // DSA_LATENT)),
                  pl.BlockSpec((1, DSA_LATENT), lambda i: (0, 0)),
                  pl.BlockSpec((DSA_HEADS, DSA_LATENT, DSA_HEAD_DIM), lambda i: (0, 0, 0))],
        out_specs=[pl.BlockSpec((1, DSA_HEADS, CHUNK, DSA_LATENT), lambda i: (i, 0, 0, 0)),
                   pl.BlockSpec((CHUNK, DSA_LATENT), lambda i: (i, 0))],
        out_shape=[jax.ShapeDtypeStruct((nblk, DSA_HEADS, CHUNK, DSA_LATENT), BF16),
                   jax.ShapeDtypeStruct((T, DSA_LATENT), BF16)],
        compiler_params=_params("parallel"),
        name="dsa_prep",
    )(proj, proj, kv_norm_g.reshape(1, DSA_LATENT), w_uk)


def _dsa_kernel(iq_ref, wq_ref, kidx_ref, ckv_ref, qa_ref, wuv_ref, o_ref,
                score_ref, m_ref, l_ref, acc_ref, *, k_sel):
    blk = pl.program_id(1)
    n_tiles = blk + 1
    ksel_f = float(k_sel)
    row = lax.broadcasted_iota(jnp.int32, (CHUNK, CHUNK), 0)
    col = lax.broadcasted_iota(jnp.int32, (CHUNK, CHUNK), 1)

    iq = iq_ref[...]
    wi = wq_ref[:, IDX_DIM:IDX_DIM + IDX_HEADS].astype(F32) * (IDX_HEADS ** -0.5)

    def score_body(j, carry):
        kt = kidx_ref[pl.ds(pl.multiple_of(j * CHUNK, CHUNK), CHUNK), :][:, :IDX_DIM]
        sc = jnp.zeros((CHUNK, CHUNK), F32)
        for h in range(IDX_HEADS):
            lg = _dot_nt(iq[:, h * IDX_DIM:(h + 1) * IDX_DIM], kt) * (IDX_DIM ** -0.5)
            sc = sc + wi[:, h:h + 1] * jnp.maximum(lg, 0.0)
        sc = jnp.where((j < blk) | (col <= row), sc, -jnp.inf)
        score_ref[j] = sc
        return carry

    lax.fori_loop(0, n_tiles, score_body, 0)

    def reduce_tiles(fn, init, combine):
        def body(j, acc):
            return combine(acc, fn(score_ref[j]))
        return lax.fori_loop(0, n_tiles, body, jnp.full((CHUNK, CHUNK), init, F32))

    def count(pred):
        acc = reduce_tiles(lambda s: pred(s).astype(F32), 0.0, jnp.add)
        return jnp.sum(acc, axis=1, keepdims=True)

    n_causal = (blk * CHUNK + 1 + lax.broadcasted_iota(jnp.int32, (CHUNK, 1), 0)).astype(F32)
    need = n_causal > ksel_f
    hi0 = jnp.max(reduce_tiles(lambda s: s, -jnp.inf, jnp.maximum), axis=1, keepdims=True)
    lo0 = jnp.min(reduce_tiles(lambda s: jnp.where(s > -jnp.inf, s, jnp.inf), jnp.inf,
                               jnp.minimum), axis=1, keepdims=True)
    cnt_hi0 = count(lambda s: s >= hi0)
    lo0 = jnp.where(cnt_hi0 >= ksel_f, hi0, lo0)
    cnt_lo0 = jnp.where(cnt_hi0 >= ksel_f, cnt_hi0, n_causal)
    active0 = need & (cnt_lo0 != ksel_f) & (lo0 < hi0)

    def bisect_cond(state):
        return jnp.max(state[3]) > 0.0

    def bisect_body(state):
        lo, hi, cnt_lo, active = state
        mid = lo + 0.5 * (hi - lo)
        c = count(lambda s: s >= mid)
        ge = c >= ksel_f
        live = active > 0.0
        moved = (mid > lo) & (mid < hi)
        new_lo = jnp.where(live & ge, mid, lo)
        new_hi = jnp.where(live & ~ge, mid, hi)
        new_cnt = jnp.where(live & ge, c, cnt_lo)
        new_active = live & moved & (new_cnt != ksel_f)
        return new_lo, new_hi, new_cnt, new_active.astype(F32)

    lo, _, _, _ = lax.while_loop(bisect_cond, bisect_body,
                                 (lo0, hi0, cnt_lo0, active0.astype(F32)))
    thr = jnp.where(need, lo, -jnp.inf)
    n_gt = count(lambda s: s > thr)
    tie_quota = jnp.where(need, ksel_f - n_gt, 0.0)

    m_ref[...] = jnp.full_like(m_ref, NEG_BIG)
    l_ref[...] = jnp.zeros_like(l_ref)
    acc_ref[...] = jnp.zeros_like(acc_ref)
    qa = qa_ref[0].reshape(DSA_HEADS * CHUNK, DSA_LATENT)
    strict_upper = (row < col).astype(BF16)

    def attn_body(j, tie_seen):
        sc = score_ref[j]
        eq = sc == thr
        eq_f = eq.astype(F32)
        prefix = _dot(eq_f.astype(BF16), strict_upper) + tie_seen
        sel = (sc > thr) | (eq & (prefix < tie_quota))
        c_t = ckv_ref[pl.ds(pl.multiple_of(j * CHUNK, CHUNK), CHUNK), :]
        s = _dot_nt(qa, c_t).reshape(DSA_HEADS, CHUNK, CHUNK)
        s = jnp.where(sel[None], s, NEG_BIG)
        m_old = m_ref[...]
        m_new = jnp.maximum(m_old, jnp.max(s, axis=-1, keepdims=True))
        alpha = jnp.exp(m_old - m_new)
        p = jnp.exp(s - m_new)
        l_ref[...] = alpha * l_ref[...] + jnp.sum(p, axis=-1, keepdims=True)
        pv = _dot(p.astype(BF16).reshape(DSA_HEADS * CHUNK, CHUNK), c_t)
        acc_ref[...] = alpha * acc_ref[...] + pv.reshape(DSA_HEADS, CHUNK, DSA_LATENT)
        m_ref[...] = m_new
        return tie_seen + jnp.sum(eq_f, axis=1, keepdims=True)

    lax.fori_loop(0, n_tiles, attn_body, jnp.zeros((CHUNK, 1), F32))

    o_lat = (acc_ref[...] / l_ref[...]).astype(BF16)
    for h in range(DSA_HEADS):
        o_ref[:, h * DSA_HEAD_DIM:(h + 1) * DSA_HEAD_DIM] = _dot(o_lat[h], wuv_ref[h]).astype(o_ref.dtype)


def dsa_attention(proj, q_abs, ckv, w_uv, batch, seq):
    T = proj.shape[0]
    nb = seq // CHUNK
    k_sel = min(DSA_TOPK_MAX, seq // 4)
    iq_w = IDX_HEADS * IDX_DIM
    out_w = DSA_HEADS * DSA_HEAD_DIM
    return pl.pallas_call(
        functools.partial(_dsa_kernel, k_sel=k_sel),
        grid=(batch, nb),
        in_specs=[pl.BlockSpec((CHUNK, iq_w), lambda b, i: (b * nb + i, COL_IQ // iq_w)),
                  pl.BlockSpec((CHUNK, LANES), lambda b, i: (b * nb + i, COL_IKW // LANES)),
                  pl.BlockSpec((seq, LANES), lambda b, i: (b, COL_IKW // LANES)),
                  pl.BlockSpec((seq, DSA_LATENT), lambda b, i: (b, 0)),
                  pl.BlockSpec((1, DSA_HEADS, CHUNK, DSA_LATENT), lambda b, i: (b * nb + i, 0, 0, 0)),
                  pl.BlockSpec((DSA_HEADS, DSA_LATENT, DSA_HEAD_DIM), lambda b, i: (0, 0, 0))],
        out_specs=pl.BlockSpec((CHUNK, out_w), lambda b, i: (b * nb + i, 0)),
        out_shape=jax.ShapeDtypeStruct((T, out_w), BF16),
        scratch_shapes=[pltpu.VMEM((nb, CHUNK, CHUNK), F32),
                        pltpu.VMEM((DSA_HEADS, CHUNK, 1), F32),
                        pltpu.VMEM((DSA_HEADS, CHUNK, 1), F32),
                        pltpu.VMEM((DSA_HEADS, CHUNK, DSA_LATENT), F32)],
        compiler_params=_params("parallel", "arbitrary"),
        name="dsa_attention",
    )(proj, proj, proj, ckv, q_abs, w_uv)


def _merge_kernel(ret_ref, gm_ref, ds_ref, wr_ref, wg_ref, wd_ref, ga_ref, gb_ref, gc_ref, o_ref):
    def branch(a_ref, w_ref, gate_ref):
        return jax.nn.sigmoid(gate_ref[...].astype(F32)) * _dot(a_ref[...], w_ref[...])

    merged = branch(ret_ref, wr_ref, ga_ref) + branch(gm_ref, wg_ref, gb_ref) + branch(ds_ref, wd_ref, gc_ref)
    o_ref[...] = merged.astype(o_ref.dtype)


def merge_branches(ret, gm, ds, w_ret_o, w_gmlp_o, w_dsa_o, proj, d_model, *, tm=1024, tn=512):
    T, K = ret.shape
    gate_blk = COL_GATES // tn
    per_gate = d_model // tn
    act = pl.BlockSpec((tm, K), lambda i, j: (i, 0))
    wgt = pl.BlockSpec((K, tn), lambda i, j: (0, j))

    def gate(which):
        return pl.BlockSpec((tm, tn), lambda i, j: (i, gate_blk + which * per_gate + j))

    return pl.pallas_call(
        _merge_kernel,
        grid=(T // tm, d_model // tn),
        in_specs=[act, act, act, wgt, wgt, wgt, gate(0), gate(1), gate(2)],
        out_specs=pl.BlockSpec((tm, tn), lambda i, j: (i, j)),
        out_shape=jax.ShapeDtypeStruct((T, d_model), BF16),
        compiler_params=_params("parallel", "arbitrary"),
        name="merge_branches",
    )(ret, gm, ds, w_ret_o, w_gmlp_o, w_dsa_o, proj, proj, proj)


def _out_proj_kernel(a_ref, w_ref, x_ref, o_ref):
    o_ref[...] = x_ref[...] + _dot(a_ref[...], w_ref[...])


def out_proj_residual(merged, w_out, x, *, tm=1024, tn=512):
    T, K = merged.shape
    N = w_out.shape[1]
    return pl.pallas_call(
        _out_proj_kernel,
        grid=(T // tm, N // tn),
        in_specs=[pl.BlockSpec((tm, K), lambda i, j: (i, 0)),
                  pl.BlockSpec((K, tn), lambda i, j: (0, j)),
                  pl.BlockSpec((tm, tn), lambda i, j: (i, j))],
        out_specs=pl.BlockSpec((tm, tn), lambda i, j: (i, j)),
        out_shape=jax.ShapeDtypeStruct((T, N), F32),
        compiler_params=_params("parallel", "arbitrary"),
        name="out_proj_residual",
    )(merged, w_out, x)


def _router_kernel(x_ref, g_ref, w_ref, b_ref, h_ref, comb_ref):
    h = _rms(x_ref[...], g_ref[...]).astype(BF16)
    h_ref[...] = h
    logits = _dot(h, w_ref[...]) + b_ref[...]
    lane = lax.broadcasted_iota(jnp.int32, logits.shape, 1)
    big = jnp.int32(LANES)

    def first_argmax(vals, valid):
        v = jnp.where(valid, vals, -jnp.inf)
        m = jnp.max(v, axis=-1, keepdims=True)
        idx = jnp.min(jnp.where(valid & (v == m), lane, big), axis=-1, keepdims=True)
        return m, idx

    is_grp = lane < MOE_GROUPS
    g_max, g_sel = first_argmax(logits, is_grp)
    g_den = jnp.sum(jnp.where(is_grp, jnp.exp(logits - g_max), 0.0), axis=-1, keepdims=True)
    p_g = 1.0 / g_den
    e_lo = MOE_GROUPS + g_sel * MOE_EXPERTS_PER_GROUP
    in_grp = (lane >= e_lo) & (lane < e_lo + MOE_EXPERTS_PER_GROUP)
    m1, i1 = first_argmax(logits, in_grp)
    m2, i2 = first_argmax(logits, in_grp & (lane != i1))
    e2 = jnp.exp(m2 - m1)
    w1 = p_g / (1.0 + e2)
    w2 = p_g * e2 / (1.0 + e2)
    comb = jnp.where(lane == i1, w1, 0.0) + jnp.where(lane == i2, w2, 0.0)
    comb_ref[...] = comb


def moe_router(x, g, w_router, b_router, *, tm=512):
    T, D = x.shape
    return pl.pallas_call(
        _router_kernel,
        grid=(T // tm,),
        in_specs=[pl.BlockSpec((tm, D), lambda i: (i, 0)),
                  pl.BlockSpec((1, D), lambda i: (0, 0)),
                  pl.BlockSpec((D, LANES), lambda i: (0, 0)),
                  pl.BlockSpec((1, LANES), lambda i: (0, 0))],
        out_specs=[pl.BlockSpec((tm, D), lambda i: (i, 0)),
                   pl.BlockSpec((tm, LANES), lambda i: (i, 0))],
        out_shape=[jax.ShapeDtypeStruct((T, D), BF16),
                   jax.ShapeDtypeStruct((T, LANES), F32)],
        compiler_params=_params("parallel"),
        name="moe_router",
    )(x, g, w_router, b_router)


def _moe_dense_kernel(h_ref, comb_ref, w1_ref, w3_ref, w2_ref, x_ref, o_ref, acc_ref):
    e = pl.program_id(1)

    @pl.when(e == 0)
    def _():
        acc_ref[...] = jnp.zeros_like(acc_ref)

    h = h_ref[...]
    lane = lax.broadcasted_iota(jnp.int32, comb_ref.shape, 1)
    c = jnp.sum(jnp.where(lane == e + MOE_GROUPS, comb_ref[...], 0.0), axis=-1, keepdims=True)
    a = _dot(h, w1_ref[0])
    hid = a * jax.nn.sigmoid(a) * _dot(h, w3_ref[0])
    acc_ref[...] += _dot((hid * c).astype(BF16), w2_ref[0])

    @pl.when(e == pl.num_programs(1) - 1)
    def _():
        o_ref[...] = x_ref[...] + acc_ref[...]


def moe_dense(h, comb, w1, w3, w2, x, *, tm=512):
    T, D = h.shape
    E, _, F = w1.shape
    return pl.pallas_call(
        _moe_dense_kernel,
        grid=(T // tm, E),
        in_specs=[pl.BlockSpec((tm, D), lambda i, e: (i, 0)),
                  pl.BlockSpec((tm, LANES), lambda i, e: (i, 0)),
                  pl.BlockSpec((1, D, F), lambda i, e: (e, 0, 0)),
                  pl.BlockSpec((1, D, F), lambda i, e: (e, 0, 0)),
                  pl.BlockSpec((1, F, D), lambda i, e: (e, 0, 0)),
                  pl.BlockSpec((tm, D), lambda i, e: (i, 0))],
        out_specs=pl.BlockSpec((tm, D), lambda i, e: (i, 0)),
        out_shape=jax.ShapeDtypeStruct((T, D), F32),
        scratch_shapes=[pltpu.VMEM((tm, D), F32)],
        compiler_params=_params("parallel", "arbitrary"),
        name="moe_experts",
    )(h, comb, w1, w3, w2, x)


def _final_norm_kernel(x_ref, g_ref, o_ref):
    o_ref[...] = _rms(x_ref[...], g_ref[...])


def final_norm(x, g, *, tm=512):
    T, D = x.shape
    return pl.pallas_call(
        _final_norm_kernel,
        grid=(T // tm,),
        in_specs=[pl.BlockSpec((tm, D), lambda i: (i, 0)), pl.BlockSpec((1, D), lambda i: (0, 0))],
        out_specs=pl.BlockSpec((tm, D), lambda i: (i, 0)),
        out_shape=jax.ShapeDtypeStruct((T, D), F32),
        compiler_params=_params("parallel"),
        name="final_norm",
    )(x, g)


def _pack_w_in(w):
    D = w.shape[0]
    widths = (1024, 1024, 1024, 1024, 1024, 1024, 1024, 256, 512, 64, 8, D, D, D)
    offs = [0]
    for wd in widths:
        offs.append(offs[-1] + wd)
    rq, rk, rv, rg, gu, gv, dq, dc, iq, ik, iw, ga, gb, gc = (
        w[:, offs[i]:offs[i + 1]] for i in range(len(widths)))
    ikw_pad = jnp.zeros((D, COL_GATES - COL_IKW - IDX_DIM - IDX_HEADS), w.dtype)
    packed = jnp.concatenate([rq, rk, rv, rg, gu, gv, dq, iq, dc, ik, iw, ikw_pad, ga, gb, gc], axis=1)
    assert packed.shape[1] == PROJ_WIDTH
    return packed.astype(BF16)


def _pack_router(w_group, b_group, w_expert, b_expert):
    D = w_group.shape[0]
    w_e = jnp.transpose(w_expert, (1, 0, 2)).reshape(D, MOE_EXPERTS)
    pad = LANES - MOE_GROUPS - MOE_EXPERTS
    w = jnp.concatenate([w_group, w_e, jnp.zeros((D, pad), w_group.dtype)], axis=1).astype(BF16)
    b = jnp.concatenate([b_group, b_expert.reshape(MOE_EXPERTS), jnp.zeros((pad,), b_group.dtype)])
    return w, b.reshape(1, LANES).astype(F32)


def kernel(x, norm_mix_g, w_in, w_ret_o, gmlp_ln_g, gmlp_ln_b, gmlp_w_s, gmlp_b_s, w_gmlp_o, dsa_kv_norm_g, dsa_w_uk, dsa_w_uv, w_dsa_o, w_out, norm_ffn_g, moe_w_group, moe_b_group, moe_w_expert, moe_b_expert, moe_w1, moe_w3, moe_w2, final_norm_g):
    B, S, D = x.shape
    depth = w_in.shape[0]
    assert S % CHUNK == 0 and D == 2048
    xt = x.reshape(B * S, D)
    for l in range(depth):
        proj = norm_matmul(xt, norm_mix_g[l].reshape(1, D), _pack_w_in(w_in[l]))
        ret = retention(proj, B, S)
        gm = gmlp(proj, gmlp_ln_g[l], gmlp_ln_b[l], gmlp_w_s[l], gmlp_b_s[l])
        q_abs, ckv = dsa_prep(proj, dsa_kv_norm_g[l], dsa_w_uk[l].astype(BF16))
        ds = dsa_attention(proj, q_abs, ckv, dsa_w_uv[l].astype(BF16), B, S)
        merged = merge_branches(ret, gm, ds, w_ret_o[l].astype(BF16), w_gmlp_o[l].astype(BF16),
                                w_dsa_o[l].astype(BF16), proj, D)
        xt = out_proj_residual(merged, w_out[l].astype(BF16), xt)
        w_router, b_router = _pack_router(moe_w_group[l], moe_b_group[l], moe_w_expert[l], moe_b_expert[l])
        h, comb = moe_router(xt, norm_ffn_g[l].reshape(1, D), w_router, b_router)
        F = moe_w1.shape[-1]
        xt = moe_dense(h, comb,
                       moe_w1[l].reshape(MOE_EXPERTS, D, F).astype(BF16),
                       moe_w3[l].reshape(MOE_EXPERTS, D, F).astype(BF16),
                       moe_w2[l].reshape(MOE_EXPERTS, F, D).astype(BF16), xt)
    return final_norm(xt, final_norm_g.reshape(1, D)).reshape(B, S, D)
```

```python
import functools
import math

import jax
import jax.numpy as jnp
from jax import lax
from jax.experimental import pallas as pl
from jax.experimental.pallas import tpu as pltpu

F32 = jnp.float32
BF16 = jnp.bfloat16

RMS_EPS = 1e-6
RET_HEADS = 4
RET_DIM = 256
CHUNK = 128
ROPE_BASE = 10000.0
GMLP_GROUPS = 4
GMLP_GROUP_DIM = 256
DSA_HEADS = 8
DSA_HEAD_DIM = 128
DSA_LATENT = 256
IDX_HEADS = 8
IDX_DIM = 64
DSA_TOPK_MAX = 256
DSA_BLOCK = 256
MOE_GROUPS = 4
MOE_EXPERTS_PER_GROUP = 8
MOE_EXPERTS = MOE_GROUPS * MOE_EXPERTS_PER_GROUP
MOE_HIDDEN = 512
LANES = 128
NEG_BIG = -1e30

COL_RQ, COL_RK, COL_RV, COL_RG = 0, 1024, 2048, 3072
COL_GU, COL_GV = 4096, 5120
COL_DQ, COL_IQ, COL_DC, COL_IKW = 6144, 7168, 7680, 7936
COL_GATES = 8192
PROJ_WIDTH = 14336
VMEM_LIMIT = 56 * 1024 * 1024


def _params(*sem):
    return pltpu.CompilerParams(dimension_semantics=sem, vmem_limit_bytes=VMEM_LIMIT)


def _rms(x, g):
    return x * lax.rsqrt(jnp.mean(x * x, axis=-1, keepdims=True) + RMS_EPS) * g


def _dot(a, b):
    return jnp.dot(a, b, preferred_element_type=F32)


def _dot_nt(a, b):
    return lax.dot_general(a, b, (((1,), (1,)), ((), ())), preferred_element_type=F32)


def _norm_matmul_kernel(x_ref, g_ref, w_ref, o_ref, h_ref):
    @pl.when(pl.program_id(1) == 0)
    def _():
        h_ref[...] = _rms(x_ref[...], g_ref[...]).astype(BF16)

    o_ref[...] = _dot(h_ref[...], w_ref[...]).astype(o_ref.dtype)


def norm_matmul(x, g, w, *, tm=1024, tn=512, out_dtype=BF16):
    T, D = x.shape
    N = w.shape[1]
    return pl.pallas_call(
        _norm_matmul_kernel,
        grid=(T // tm, N // tn),
        in_specs=[pl.BlockSpec((tm, D), lambda i, j: (i, 0)),
                  pl.BlockSpec((1, D), lambda i, j: (0, 0)),
                  pl.BlockSpec((D, tn), lambda i, j: (0, j))],
        out_specs=pl.BlockSpec((tm, tn), lambda i, j: (i, j)),
        out_shape=jax.ShapeDtypeStruct((T, N), out_dtype),
        scratch_shapes=[pltpu.VMEM((tm, D), BF16)],
        compiler_params=_params("parallel", "arbitrary"),
        name="norm_in_proj",
    )(x, g, w)


def _retention_kernel(q_ref, k_ref, v_ref, gate_ref, cos_ref, sin_ref, decay_ref, xi_ref,
                      zeta_ref, gchunk_ref, o_ref, r_ref):
    @pl.when(pl.program_id(1) == 0)
    def _():
        r_ref[...] = jnp.zeros_like(r_ref)

    cos = cos_ref[...]
    sin = sin_ref[...]
    half = RET_DIM // 2

    def rotary(t):
        t1, t2 = t[:, :half], t[:, half:]
        return jnp.concatenate([t1 * cos - t2 * sin, t1 * sin + t2 * cos], axis=-1)

    for h in range(RET_HEADS):
        cols = slice(h * RET_DIM, (h + 1) * RET_DIM)
        q = rotary(q_ref[:, cols].astype(F32)).astype(BF16)
        k = rotary(k_ref[:, cols].astype(F32)) * (RET_DIM ** -0.5)
        v = v_ref[:, cols]
        inner = _dot_nt(q, k.astype(BF16)) * decay_ref[h]
        r_old = r_ref[h]
        o = _dot(inner.astype(BF16), v) + _dot(q, r_old.astype(BF16)) * xi_ref[:, h:h + 1]
        kz = (k * zeta_ref[:, h:h + 1]).astype(BF16)
        r_ref[h] = r_old * gchunk_ref[h] + _dot(kz.T, v)
        o = o * lax.rsqrt(jnp.mean(o * o, axis=-1, keepdims=True) + RMS_EPS)
        gate = gate_ref[:, cols].astype(F32)
        o_ref[:, cols] = (gate * jax.nn.sigmoid(gate) * o).astype(o_ref.dtype)


def retention(proj, batch, seq):
    T = proj.shape[0]
    n_chunks = seq // CHUNK
    width = RET_HEADS * RET_DIM
    half = RET_DIM // 2
    inv = ROPE_BASE ** (-jnp.arange(half, dtype=F32) / half)
    ang = jnp.arange(seq).astype(F32)[:, None] * inv[None, :]
    cos, sin = jnp.cos(ang), jnp.sin(ang)
    log_g = jnp.log(1.0 - 2.0 ** (-5.0 - jnp.arange(RET_HEADS, dtype=F32)))
    i = jnp.arange(CHUNK, dtype=F32)
    diff = i[:, None] - i[None, :]
    decay = jnp.where(diff >= 0, jnp.exp(log_g[:, None, None] * jnp.maximum(diff, 0.0)), 0.0)
    xi = jnp.exp(log_g[None, :] * (i[:, None] + 1.0))
    zeta = jnp.exp(log_g[None, :] * (CHUNK - 1.0 - i[:, None]))
    gchunk = jnp.exp(log_g * CHUNK)

    def col_spec(col):
        return pl.BlockSpec((CHUNK, width), lambda b, n: (b * n_chunks + n, col // width))

    return pl.pallas_call(
        _retention_kernel,
        grid=(batch, n_chunks),
        in_specs=[col_spec(COL_RQ), col_spec(COL_RK), col_spec(COL_RV), col_spec(COL_RG),
                  pl.BlockSpec((CHUNK, half), lambda b, n: (n, 0)),
                  pl.BlockSpec((CHUNK, half), lambda b, n: (n, 0)),
                  pl.BlockSpec((RET_HEADS, CHUNK, CHUNK), lambda b, n: (0, 0, 0)),
                  pl.BlockSpec((CHUNK, RET_HEADS), lambda b, n: (0, 0)),
                  pl.BlockSpec((CHUNK, RET_HEADS), lambda b, n: (0, 0)),
                  pl.BlockSpec(memory_space=pltpu.SMEM)],
        out_specs=pl.BlockSpec((CHUNK, width), lambda b, n: (b * n_chunks + n, 0)),
        out_shape=jax.ShapeDtypeStruct((T, width), BF16),
        scratch_shapes=[pltpu.VMEM((RET_HEADS, RET_DIM, RET_DIM), F32)],
        compiler_params=_params("parallel", "arbitrary"),
        name="retention",
    )(proj, proj, proj, proj, cos, sin, decay, xi, zeta, gchunk)


def _gelu(x):
    return 0.5 * x * (1.0 + lax.erf(x * (2.0 ** -0.5)))


def _gmlp_kernel(u_ref, v_ref, lng_ref, lnb_ref, ws_ref, bs_ref, o_ref):
    v = _gelu(v_ref[...].astype(F32))
    mu = jnp.mean(v, axis=-1, keepdims=True)
    var = jnp.mean(jnp.square(v - mu), axis=-1, keepdims=True)
    vn = ((v - mu) * lax.rsqrt(var + RMS_EPS) * lng_ref[...] + lnb_ref[...]).astype(BF16)
    row = lax.broadcasted_iota(jnp.int32, (CHUNK, CHUNK), 0)
    col = lax.broadcasted_iota(jnp.int32, (CHUNK, CHUNK), 1)
    for g in range(GMLP_GROUPS):
        cols = slice(g * GMLP_GROUP_DIM, (g + 1) * GMLP_GROUP_DIM)
        w = jnp.where(row >= col, ws_ref[g], 0.0).astype(BF16)
        mixed = _dot(w, vn[:, cols]) + bs_ref[:, g:g + 1]
        u = _gelu(u_ref[:, cols].astype(F32))
        o_ref[:, cols] = (u * mixed).astype(o_ref.dtype)


def gmlp(proj, ln_g, ln_b, w_s, b_s):
    T = proj.shape[0]
    width = GMLP_GROUPS * GMLP_GROUP_DIM
    return pl.pallas_call(
        _gmlp_kernel,
        grid=(T // CHUNK,),
        in_specs=[pl.BlockSpec((CHUNK, width), lambda i: (i, COL_GU // width)),
                  pl.BlockSpec((CHUNK, width), lambda i: (i, COL_GV // width)),
                  pl.BlockSpec((1, width), lambda i: (0, 0)),
                  pl.BlockSpec((1, width), lambda i: (0, 0)),
                  pl.BlockSpec((GMLP_GROUPS, CHUNK, CHUNK), lambda i: (0, 0, 0)),
                  pl.BlockSpec((CHUNK, GMLP_GROUPS), lambda i: (0, 0))],
        out_specs=pl.BlockSpec((CHUNK, width), lambda i: (i, 0)),
        out_shape=jax.ShapeDtypeStruct((T, width), BF16),
        compiler_params=_params("parallel"),
        name="gmlp",
    )(proj, proj, ln_g.reshape(1, width), ln_b.reshape(1, width), w_s, b_s.T)


def _dsa_prep_kernel(dq_ref, dc_ref, iq_ref, g_ref, wuk_ref, qabs_ref, iqp_ref, ckv_ref, ckvt_ref):
    c = _rms(dc_ref[...].astype(F32), g_ref[...])
    ckv_ref[...] = c.astype(ckv_ref.dtype)
    ckvt_ref[0] = c.T.astype(ckvt_ref.dtype)
    zeros = jnp.zeros((DSA_BLOCK, LANES - IDX_DIM), iqp_ref.dtype)
    for h in range(DSA_HEADS):
        q = dq_ref[:, h * DSA_HEAD_DIM:(h + 1) * DSA_HEAD_DIM]
        qabs_ref[0, h] = (_dot_nt(q, wuk_ref[h]) * (DSA_HEAD_DIM ** -0.5)).astype(qabs_ref.dtype)
    for h in range(IDX_HEADS):
        iqp_ref[0, h] = jnp.concatenate([iq_ref[:, h * IDX_DIM:(h + 1) * IDX_DIM], zeros], axis=-1)


def dsa_prep(proj, kv_norm_g, w_uk):
    T = proj.shape[0]
    nblk = T // DSA_BLOCK
    qw = DSA_HEADS * DSA_HEAD_DIM
    iq_w = IDX_HEADS * IDX_DIM
    return pl.pallas_call(
        _dsa_prep_kernel,
        grid=(nblk,),
        in_specs=[pl.BlockSpec((DSA_BLOCK, qw), lambda i: (i, COL_DQ // qw)),
                  pl.BlockSpec((DSA_BLOCK, DSA_LATENT), lambda i: (i, COL_DC // DSA_LATENT)),
                  pl.BlockSpec((DSA_BLOCK, iq_w), lambda i: (i, COL_IQ // iq_w)),
                  pl.BlockSpec((1, DSA_LATENT), lambda i: (0, 0)),
                  pl.BlockSpec((DSA_HEADS, DSA_LATENT, DSA_HEAD_DIM), lambda i: (0, 0, 0))],
        out_specs=[pl.BlockSpec((1, DSA_HEADS, DSA_BLOCK, DSA_LATENT), lambda i: (i, 0, 0, 0)),
                   pl.BlockSpec((1, IDX_HEADS, DSA_BLOCK, LANES), lambda i: (i, 0, 0, 0)),
                   pl.BlockSpec((DSA_BLOCK, DSA_LATENT), lambda i: (i, 0)),
                   pl.BlockSpec((1, DSA_LATENT, DSA_BLOCK), lambda i: (i, 0, 0))],
        out_shape=[jax.ShapeDtypeStruct((nblk, DSA_HEADS, DSA_BLOCK, DSA_LATENT), BF16),
                   jax.ShapeDtypeStruct((nblk, IDX_HEADS, DSA_BLOCK, LANES), BF16),
                   jax.ShapeDtypeStruct((T, DSA_LATENT), BF16),
                   jax.ShapeDtypeStruct((nblk, DSA_LATENT, DSA_BLOCK), BF16)],
        compiler_params=_params("parallel"),
        name="dsa_prep",
    )(proj, proj, proj, kv_norm_g.reshape(1, DSA_LATENT), w_uk)


def _dsa_kernel(iqp_ref, wq_ref, kidx_ref, ckv_ref, ckvt_ref, qa_ref, wuv_ref, o_ref,
                score_ref, m_ref, l_ref, acc_ref, *, k_sel):
    QB = DSA_BLOCK
    blk = pl.program_id(1)
    n_tiles = blk + 1
    ksel_f = float(k_sel)
    key_in_tile = lax.broadcasted_iota(jnp.int32, (QB, QB), 0)
    query_in_blk = lax.broadcasted_iota(jnp.int32, (QB, QB), 1)

    def fold8(x, op):
        return op(x.reshape(QB // 8, 8, QB), axis=0)

    def tile_rows(j):
        return pl.ds(pl.multiple_of(j * QB, QB), QB)

    wi_t = wq_ref[...].astype(F32).T * (IDX_HEADS ** -0.5)

    def score_body(j, carry):
        kt = kidx_ref[tile_rows(j), :]
        sc = jnp.zeros((QB, QB), F32)
        for h in range(IDX_HEADS):
            lg = _dot_nt(kt, iqp_ref[0, h]) * (IDX_DIM ** -0.5)
            sc = sc + wi_t[IDX_DIM + h:IDX_DIM + h + 1, :] * jnp.maximum(lg, 0.0)
        score_ref[j] = jnp.where((j < blk) | (key_in_tile <= query_in_blk), sc, -jnp.inf)
        return carry

    lax.fori_loop(0, n_tiles, score_body, 0)

    def reduce_tiles(fn, op, fold, init):
        def body(j, acc):
            return op(acc, fold8(fn(score_ref[j]), fold))
        acc = lax.fori_loop(0, n_tiles, body, jnp.full((8, QB), init, F32))
        return fold(acc, axis=0, keepdims=True)

    def count(pred):
        return reduce_tiles(lambda s: pred(s).astype(F32), jnp.add, jnp.sum, 0.0)

    n_causal = (blk * QB + 1 + lax.broadcasted_iota(jnp.int32, (1, QB), 1)).astype(F32)
    need = n_causal > ksel_f
    hi0 = reduce_tiles(lambda s: s, jnp.maximum, jnp.max, -jnp.inf)
    lo0 = reduce_tiles(lambda s: jnp.where(s > -jnp.inf, s, jnp.inf), jnp.minimum, jnp.min, jnp.inf)
    cnt_hi0 = count(lambda s: s >= hi0)
    cnt_ge0 = count(lambda s: s >= 0.0)
    cnt_gt0 = count(lambda s: s > 0.0)
    above0 = cnt_gt0 >= ksel_f
    below0 = cnt_ge0 < ksel_f
    lo0 = jnp.where(below0, lo0, 0.0)
    cnt_lo0 = jnp.where(below0, n_causal, cnt_ge0)
    hi0 = jnp.where(above0, hi0, 0.0)
    at_top = above0 & (cnt_hi0 >= ksel_f)
    lo0 = jnp.where(at_top, hi0, lo0)
    cnt_lo0 = jnp.where(at_top, cnt_hi0, cnt_lo0)
    active0 = need & (above0 | below0) & (cnt_lo0 != ksel_f) & (lo0 < hi0)

    def bisect_cond(state):
        return jnp.max(state[3]) > 0.0

    def bisect_body(state):
        lo, hi, cnt_lo, active = state
        mid = lo + 0.5 * (hi - lo)
        c = count(lambda s: s >= mid)
        ge = c >= ksel_f
        live = active > 0.0
        moved = (mid > lo) & (mid < hi)
        new_lo = jnp.where(live & ge, mid, lo)
        new_hi = jnp.where(live & ~ge, mid, hi)
        new_cnt = jnp.where(live & ge, c, cnt_lo)
        new_active = live & moved & (new_cnt != ksel_f)
        return new_lo, new_hi, new_cnt, new_active.astype(F32)

    lo = lax.while_loop(bisect_cond, bisect_body, (lo0, hi0, cnt_lo0, active0.astype(F32)))[0]
    thr = jnp.where(need, lo, -jnp.inf)
    n_gt = count(lambda s: s > thr)
    tie_quota = jnp.where(need, ksel_f - n_gt, 0.0)

    m_ref[...] = jnp.full_like(m_ref, NEG_BIG)
    l_ref[...] = jnp.zeros_like(l_ref)
    acc_ref[...] = jnp.zeros_like(acc_ref)
    strict_lower = (query_in_blk < key_in_tile).astype(BF16)

    def attn_body(j, tie_seen):
        sc = score_ref[j]
        eq = sc == thr
        eq_f = eq.astype(F32)
        prefix = _dot(strict_lower, eq_f.astype(BF16)) + tie_seen
        sel = (sc > thr) | (eq & (prefix < tie_quota))
        bias = jnp.where(sel, 0.0, NEG_BIG)
        c_t = ckv_ref[tile_rows(j), :]
        ct_t = ckvt_ref[j]
        for h in range(DSA_HEADS):
            s = _dot_nt(c_t, qa_ref[0, h]) + bias
            m_old = m_ref[h]
            m_new = jnp.maximum(m_old, jnp.max(fold8(s, jnp.max), axis=0, keepdims=True))
            alpha = jnp.exp(m_old - m_new)
            p = jnp.exp(s - m_new)
            l_ref[h] = alpha * l_ref[h] + jnp.sum(fold8(p, jnp.sum), axis=0, keepdims=True)
            acc_ref[h] = alpha * acc_ref[h] + _dot(ct_t, p.astype(BF16))
            m_ref[h] = m_new
        return tie_seen + jnp.sum(fold8(eq_f, jnp.sum), axis=0, keepdims=True)

    lax.fori_loop(0, n_tiles, attn_body, jnp.zeros((1, QB), F32))

    for h in range(DSA_HEADS):
        o_lat = (acc_ref[h] / l_ref[h]).T.astype(BF16)
        o_ref[:, h * DSA_HEAD_DIM:(h + 1) * DSA_HEAD_DIM] = _dot(o_lat, wuv_ref[h]).astype(o_ref.dtype)


def dsa_attention(proj, q_abs, iq_pad, ckv, ckv_t, w_uv, batch, seq):
    T = proj.shape[0]
    nb = seq // DSA_BLOCK
    k_sel = min(DSA_TOPK_MAX, seq // 4)
    out_w = DSA_HEADS * DSA_HEAD_DIM
    return pl.pallas_call(
        functools.partial(_dsa_kernel, k_sel=k_sel),
        grid=(batch, nb),
        in_specs=[pl.BlockSpec((1, IDX_HEADS, DSA_BLOCK, LANES), lambda b, i: (b * nb + i, 0, 0, 0)),
                  pl.BlockSpec((DSA_BLOCK, LANES), lambda b, i: (b * nb + i, COL_IKW // LANES)),
                  pl.BlockSpec((seq, LANES), lambda b, i: (b, COL_IKW // LANES)),
                  pl.BlockSpec((seq, DSA_LATENT), lambda b, i: (b, 0)),
                  pl.BlockSpec((nb, DSA_LATENT, DSA_BLOCK), lambda b, i: (b, 0, 0)),
                  pl.BlockSpec((1, DSA_HEADS, DSA_BLOCK, DSA_LATENT), lambda b, i: (b * nb + i, 0, 0, 0)),
                  pl.BlockSpec((DSA_HEADS, DSA_LATENT, DSA_HEAD_DIM), lambda b, i: (0, 0, 0))],
        out_specs=pl.BlockSpec((DSA_BLOCK, out_w), lambda b, i: (b * nb + i, 0)),
        out_shape=jax.ShapeDtypeStruct((T, out_w), BF16),
        scratch_shapes=[pltpu.VMEM((nb, DSA_BLOCK, DSA_BLOCK), F32),
                        pltpu.VMEM((DSA_HEADS, 1, DSA_BLOCK), F32),
                        pltpu.VMEM((DSA_HEADS, 1, DSA_BLOCK), F32),
                        pltpu.VMEM((DSA_HEADS, DSA_LATENT, DSA_BLOCK), F32)],
        compiler_params=_params("parallel", "arbitrary"),
        name="dsa_attention",
    )(iq_pad, proj, proj, ckv, ckv_t, q_abs, w_uv)


def _merge_kernel(ret_ref, gm_ref, ds_ref, wr_ref, wg_ref, wd_ref, ga_ref, gb_ref, gc_ref, o_ref):
    def branch(a_ref, w_ref, gate_ref):
        return jax.nn.sigmoid(gate_ref[...].astype(F32)) * _dot(a_ref[...], w_ref[...])

    merged = branch(ret_ref, wr_ref, ga_ref) + branch(gm_ref, wg_ref, gb_ref) + branch(ds_ref, wd_ref, gc_ref)
    o_ref[...] = merged.astype(o_ref.dtype)


def merge_branches(ret, gm, ds, w_ret_o, w_gmlp_o, w_dsa_o, proj, d_model, *, tm=1024, tn=512):
    T, K = ret.shape
    gate_blk = COL_GATES // tn
    per_gate = d_model // tn
    act = pl.BlockSpec((tm, K), lambda i, j: (i, 0))
    wgt = pl.BlockSpec((K, tn), lambda i, j: (0, j))

    def gate(which):
        return pl.BlockSpec((tm, tn), lambda i, j: (i, gate_blk + which * per_gate + j))

    return pl.pallas_call(
        _merge_kernel,
        grid=(T // tm, d_model // tn),
        in_specs=[act, act, act, wgt, wgt, wgt, gate(0), gate(1), gate(2)],
        out_specs=pl.BlockSpec((tm, tn), lambda i, j: (i, j)),
        out_shape=jax.ShapeDtypeStruct((T, d_model), BF16),
        compiler_params=_params("parallel", "arbitrary"),
        name="merge_branches",
    )(ret, gm, ds, w_ret_o, w_gmlp_o, w_dsa_o, proj, proj, proj)


def _out_proj_kernel(a_ref, w_ref, x_ref, o_ref):
    o_ref[...] = x_ref[...] + _dot(a_ref[...], w_ref[...])


def out_proj_residual(merged, w_out, x, *, tm=1024, tn=512):
    T, K = merged.shape
    N = w_out.shape[1]
    return pl.pallas_call(
        _out_proj_kernel,
        grid=(T // tm, N // tn),
        in_specs=[pl.BlockSpec((tm, K), lambda i, j: (i, 0)),
                  pl.BlockSpec((K, tn), lambda i, j: (0, j)),
                  pl.BlockSpec((tm, tn), lambda i, j: (i, j))],
        out_specs=pl.BlockSpec((tm, tn), lambda i, j: (i, j)),
        out_shape=jax.ShapeDtypeStruct((T, N), F32),
        compiler_params=_params("parallel", "arbitrary"),
        name="out_proj_residual",
    )(merged, w_out, x)


def _router_kernel(x_ref, g_ref, w_ref, b_ref, h_ref, comb_ref):
    h = _rms(x_ref[...], g_ref[...]).astype(BF16)
    h_ref[...] = h
    logits = _dot(h, w_ref[...]) + b_ref[...]
    lane = lax.broadcasted_iota(jnp.int32, logits.shape, 1)
    big = jnp.int32(LANES)

    def first_argmax(vals, valid):
        v = jnp.where(valid, vals, -jnp.inf)
        m = jnp.max(v, axis=-1, keepdims=True)
        idx = jnp.min(jnp.where(valid & (v == m), lane, big), axis=-1, keepdims=True)
        return m, idx

    is_grp = lane < MOE_GROUPS
    g_max, g_sel = first_argmax(logits, is_grp)
    g_den = jnp.sum(jnp.where(is_grp, jnp.exp(logits - g_max), 0.0), axis=-1, keepdims=True)
    p_g = 1.0 / g_den
    e_lo = MOE_GROUPS + g_sel * MOE_EXPERTS_PER_GROUP
    in_grp = (lane >= e_lo) & (lane < e_lo + MOE_EXPERTS_PER_GROUP)
    m1, i1 = first_argmax(logits, in_grp)
    m2, i2 = first_argmax(logits, in_grp & (lane != i1))
    e2 = jnp.exp(m2 - m1)
    w1 = p_g / (1.0 + e2)
    w2 = p_g * e2 / (1.0 + e2)
    comb = jnp.where(lane == i1, w1, 0.0) + jnp.where(lane == i2, w2, 0.0)
    comb_ref[...] = comb


def moe_router(x, g, w_router, b_router, *, tm=512):
    T, D = x.shape
    return pl.pallas_call(
        _router_kernel,
        grid=(T // tm,),
        in_specs=[pl.BlockSpec((tm, D), lambda i: (i, 0)),
                  pl.BlockSpec((1, D), lambda i: (0, 0)),
                  pl.BlockSpec((D, LANES), lambda i: (0, 0)),
                  pl.BlockSpec((1, LANES), lambda i: (0, 0))],
        out_specs=[pl.BlockSpec((tm, D), lambda i: (i, 0)),
                   pl.BlockSpec((tm, LANES), lambda i: (i, 0))],
        out_shape=[jax.ShapeDtypeStruct((T, D), BF16),
                   jax.ShapeDtypeStruct((T, LANES), F32)],
        compiler_params=_params("parallel"),
        name="moe_router",
    )(x, g, w_router, b_router)


def _moe_dense_kernel(h_ref, comb_ref, w1_ref, w3_ref, w2_ref, x_ref, o_ref, acc_ref):
    e = pl.program_id(1)

    @pl.when(e == 0)
    def _():
        acc_ref[...] = jnp.zeros_like(acc_ref)

    h = h_ref[...]
    lane = lax.broadcasted_iota(jnp.int32, comb_ref.shape, 1)
    c = jnp.sum(jnp.where(lane == e + MOE_GROUPS, comb_ref[...], 0.0), axis=-1, keepdims=True)
    a = _dot(h, w1_ref[0])
    hid = a * jax.nn.sigmoid(a) * _dot(h, w3_ref[0])
    acc_ref[...] += _dot((hid * c).astype(BF16), w2_ref[0])

    @pl.when(e == pl.num_programs(1) - 1)
    def _():
        o_ref[...] = x_ref[...] + acc_ref[...]


def moe_dense(h, comb, w1, w3, w2, x, *, tm=512):
    T, D = h.shape
    E, _, F = w1.shape
    return pl.pallas_call(
        _moe_dense_kernel,
        grid=(T // tm, E),
        in_specs=[pl.BlockSpec((tm, D), lambda i, e: (i, 0)),
                  pl.BlockSpec((tm, LANES), lambda i, e: (i, 0)),
                  pl.BlockSpec((1, D, F), lambda i, e: (e, 0, 0)),
                  pl.BlockSpec((1, D, F), lambda i, e: (e, 0, 0)),
                  pl.BlockSpec((1, F, D), lambda i, e: (e, 0, 0)),
                  pl.BlockSpec((tm, D), lambda i, e: (i, 0))],
        out_specs=pl.BlockSpec((tm, D), lambda i, e: (i, 0)),
        out_shape=jax.ShapeDtypeStruct((T, D), F32),
        scratch_shapes=[pltpu.VMEM((tm, D), F32)],
        compiler_params=_params("parallel", "arbitrary"),
        name="moe_experts",
    )(h, comb, w1, w3, w2, x)


def _final_norm_kernel(x_ref, g_ref, o_ref):
    o_ref[...] = _rms(x_ref[...], g_ref[...])


def final_norm(x, g, *, tm=512):
    T, D = x.shape
    return pl.pallas_call(
        _final_norm_kernel,
        grid=(T // tm,),
        in_specs=[pl.BlockSpec((tm, D), lambda i: (i, 0)), pl.BlockSpec((1, D), lambda i: (0, 0))],
        out_specs=pl.BlockSpec((tm, D), lambda i: (i, 0)),
        out_shape=jax.ShapeDtypeStruct((T, D), F32),
        compiler_params=_params("parallel"),
        name="final_norm",
    )(x, g)


def _pack_w_in(w):
    D = w.shape[0]
    widths = (1024, 1024, 1024, 1024, 1024, 1024, 1024, 256, 512, 64, 8, D, D, D)
    offs = [0]
    for wd in widths:
        offs.append(offs[-1] + wd)
    rq, rk, rv, rg, gu, gv, dq, dc, iq, ik, iw, ga, gb, gc = (
        w[:, offs[i]:offs[i + 1]] for i in range(len(widths)))
    ikw_pad = jnp.zeros((D, COL_GATES - COL_IKW - IDX_DIM - IDX_HEADS), w.dtype)
    packed = jnp.concatenate([rq, rk, rv, rg, gu, gv, dq, iq, dc, ik, iw, ikw_pad, ga, gb, gc], axis=1)
    assert packed.shape[1] == PROJ_WIDTH
    return packed.astype(BF16)


def _pack_router(w_group, b_group, w_expert, b_expert):
    D = w_group.shape[0]
    w_e = jnp.transpose(w_expert, (1, 0, 2)).reshape(D, MOE_EXPERTS)
    pad = LANES - MOE_GROUPS - MOE_EXPERTS
    w = jnp.concatenate([w_group, w_e, jnp.zeros((D, pad), w_group.dtype)], axis=1).astype(BF16)
    b = jnp.concatenate([b_group, b_expert.reshape(MOE_EXPERTS), jnp.zeros((pad,), b_group.dtype)])
    return w, b.reshape(1, LANES).astype(F32)


def kernel(x, norm_mix_g, w_in, w_ret_o, gmlp_ln_g, gmlp_ln_b, gmlp_w_s, gmlp_b_s, w_gmlp_o, dsa_kv_norm_g, dsa_w_uk, dsa_w_uv, w_dsa_o, w_out, norm_ffn_g, moe_w_group, moe_b_group, moe_w_expert, moe_b_expert, moe_w1, moe_w3, moe_w2, final_norm_g):
    B, S, D = x.shape
    depth = w_in.shape[0]
    assert S % DSA_BLOCK == 0 and D == 2048
    xt = x.reshape(B * S, D)
    for l in range(depth):
        proj = norm_matmul(xt, norm_mix_g[l].reshape(1, D), _pack_w_in(w_in[l]))
        ret = retention(proj, B, S)
        gm = gmlp(proj, gmlp_ln_g[l], gmlp_ln_b[l], gmlp_w_s[l], gmlp_b_s[l])
        q_abs, iq_pad, ckv, ckv_t = dsa_prep(proj, dsa_kv_norm_g[l], dsa_w_uk[l].astype(BF16))
        ds = dsa_attention(proj, q_abs, iq_pad, ckv, ckv_t, dsa_w_uv[l].astype(BF16), B, S)
        merged = merge_branches(ret, gm, ds, w_ret_o[l].astype(BF16), w_gmlp_o[l].astype(BF16),
                                w_dsa_o[l].astype(BF16), proj, D)
        xt = out_proj_residual(merged, w_out[l].astype(BF16), xt)
        w_router, b_router = _pack_router(moe_w_group[l], moe_b_group[l], moe_w_expert[l], moe_b_expert[l])
        h, comb = moe_router(xt, norm_ffn_g[l].reshape(1, D), w_router, b_router)
        F = moe_w1.shape[-1]
        xt = moe_dense(h, comb,
                       moe_w1[l].reshape(MOE_EXPERTS, D, F).astype(BF16),
                       moe_w3[l].reshape(MOE_EXPERTS, D, F).astype(BF16),
                       moe_w2[l].reshape(MOE_EXPERTS, F, D).astype(BF16), xt)
    return final_norm(xt, final_norm_g.reshape(1, D)).reshape(B, S, D)
```

```python
import functools
import math

import jax
import jax.numpy as jnp
from jax import lax
from jax.experimental import pallas as pl
from jax.experimental.pallas import tpu as pltpu

F32 = jnp.float32
BF16 = jnp.bfloat16

RMS_EPS = 1e-6
RET_HEADS = 4
RET_DIM = 256
CHUNK = 128
ROPE_BASE = 10000.0
GMLP_GROUPS = 4
GMLP_GROUP_DIM = 256
DSA_HEADS = 8
DSA_HEAD_DIM = 128
DSA_LATENT = 256
IDX_HEADS = 8
IDX_DIM = 64
DSA_TOPK_MAX = 256
DSA_BLOCK = 256
MOE_GROUPS = 4
MOE_EXPERTS_PER_GROUP = 8
MOE_EXPERTS = MOE_GROUPS * MOE_EXPERTS_PER_GROUP
MOE_HIDDEN = 512
MOE_TOPK = 2
MOE_TILE = 256
LANES = 128
NEG_BIG = -1e30

COL_RQ, COL_RK, COL_RV, COL_RG = 0, 1024, 2048, 3072
COL_GU, COL_GV = 4096, 5120
COL_DQ, COL_IQ, COL_DC, COL_IKW = 6144, 7168, 7680, 7936
COL_GATES = 8192
PROJ_WIDTH = 14336
VMEM_LIMIT = 56 * 1024 * 1024


def _params(*sem):
    return pltpu.CompilerParams(dimension_semantics=sem, vmem_limit_bytes=VMEM_LIMIT)


def _rms(x, g):
    return x * lax.rsqrt(jnp.mean(x * x, axis=-1, keepdims=True) + RMS_EPS) * g


def _dot(a, b):
    return jnp.dot(a, b, preferred_element_type=F32)


def _dot_nt(a, b):
    return lax.dot_general(a, b, (((1,), (1,)), ((), ())), preferred_element_type=F32)


def _norm_matmul_kernel(x_ref, g_ref, w_ref, o_ref, h_ref):
    @pl.when(pl.program_id(1) == 0)
    def _():
        h_ref[...] = _rms(x_ref[...], g_ref[...]).astype(BF16)

    o_ref[...] = _dot(h_ref[...], w_ref[...]).astype(o_ref.dtype)


def norm_matmul(x, g, w, *, tm=1024, tn=512, out_dtype=BF16):
    T, D = x.shape
    N = w.shape[1]
    return pl.pallas_call(
        _norm_matmul_kernel,
        grid=(T // tm, N // tn),
        in_specs=[pl.BlockSpec((tm, D), lambda i, j: (i, 0)),
                  pl.BlockSpec((1, D), lambda i, j: (0, 0)),
                  pl.BlockSpec((D, tn), lambda i, j: (0, j))],
        out_specs=pl.BlockSpec((tm, tn), lambda i, j: (i, j)),
        out_shape=jax.ShapeDtypeStruct((T, N), out_dtype),
        scratch_shapes=[pltpu.VMEM((tm, D), BF16)],
        compiler_params=_params("parallel", "arbitrary"),
        name="norm_in_proj",
    )(x, g, w)


def _retention_kernel(q_ref, k_ref, v_ref, gate_ref, cos_ref, sin_ref, decay_ref, xi_ref,
                      zeta_ref, gchunk_ref, o_ref, r_ref):
    @pl.when(pl.program_id(1) == 0)
    def _():
        r_ref[...] = jnp.zeros_like(r_ref)

    cos = cos_ref[...]
    sin = sin_ref[...]
    half = RET_DIM // 2

    def rotary(t):
        t1, t2 = t[:, :half], t[:, half:]
        return jnp.concatenate([t1 * cos - t2 * sin, t1 * sin + t2 * cos], axis=-1)

    for h in range(RET_HEADS):
        cols = slice(h * RET_DIM, (h + 1) * RET_DIM)
        q = rotary(q_ref[:, cols].astype(F32)).astype(BF16)
        k = rotary(k_ref[:, cols].astype(F32)) * (RET_DIM ** -0.5)
        v = v_ref[:, cols]
        inner = _dot_nt(q, k.astype(BF16)) * decay_ref[h]
        r_old = r_ref[h]
        o = _dot(inner.astype(BF16), v) + _dot(q, r_old.astype(BF16)) * xi_ref[:, h:h + 1]
        kz = (k * zeta_ref[:, h:h + 1]).astype(BF16)
        r_ref[h] = r_old * gchunk_ref[h] + _dot(kz.T, v)
        o = o * lax.rsqrt(jnp.mean(o * o, axis=-1, keepdims=True) + RMS_EPS)
        gate = gate_ref[:, cols].astype(F32)
        o_ref[:, cols] = (gate * jax.nn.sigmoid(gate) * o).astype(o_ref.dtype)


def retention(proj, batch, seq):
    T = proj.shape[0]
    n_chunks = seq // CHUNK
    width = RET_HEADS * RET_DIM
    half = RET_DIM // 2
    inv = ROPE_BASE ** (-jnp.arange(half, dtype=F32) / half)
    ang = jnp.arange(seq).astype(F32)[:, None] * inv[None, :]
    cos, sin = jnp.cos(ang), jnp.sin(ang)
    log_g = jnp.log(1.0 - 2.0 ** (-5.0 - jnp.arange(RET_HEADS, dtype=F32)))
    i = jnp.arange(CHUNK, dtype=F32)
    diff = i[:, None] - i[None, :]
    decay = jnp.where(diff >= 0, jnp.exp(log_g[:, None, None] * jnp.maximum(diff, 0.0)), 0.0)
    xi = jnp.exp(log_g[None, :] * (i[:, None] + 1.0))
    zeta = jnp.exp(log_g[None, :] * (CHUNK - 1.0 - i[:, None]))
    gchunk = jnp.exp(log_g * CHUNK)

    def col_spec(col):
        return pl.BlockSpec((CHUNK, width), lambda b, n: (b * n_chunks + n, col // width))

    return pl.pallas_call(
        _retention_kernel,
        grid=(batch, n_chunks),
        in_specs=[col_spec(COL_RQ), col_spec(COL_RK), col_spec(COL_RV), col_spec(COL_RG),
                  pl.BlockSpec((CHUNK, half), lambda b, n: (n, 0)),
                  pl.BlockSpec((CHUNK, half), lambda b, n: (n, 0)),
                  pl.BlockSpec((RET_HEADS, CHUNK, CHUNK), lambda b, n: (0, 0, 0)),
                  pl.BlockSpec((CHUNK, RET_HEADS), lambda b, n: (0, 0)),
                  pl.BlockSpec((CHUNK, RET_HEADS), lambda b, n: (0, 0)),
                  pl.BlockSpec(memory_space=pltpu.SMEM)],
        out_specs=pl.BlockSpec((CHUNK, width), lambda b, n: (b * n_chunks + n, 0)),
        out_shape=jax.ShapeDtypeStruct((T, width), BF16),
        scratch_shapes=[pltpu.VMEM((RET_HEADS, RET_DIM, RET_DIM), F32)],
        compiler_params=_params("parallel", "arbitrary"),
        name="retention",
    )(proj, proj, proj, proj, cos, sin, decay, xi, zeta, gchunk)


def _gelu(x):
    return 0.5 * x * (1.0 + lax.erf(x * (2.0 ** -0.5)))


def _gmlp_kernel(u_ref, v_ref, lng_ref, lnb_ref, ws_ref, bs_ref, o_ref):
    v = _gelu(v_ref[...].astype(F32))
    mu = jnp.mean(v, axis=-1, keepdims=True)
    var = jnp.mean(jnp.square(v - mu), axis=-1, keepdims=True)
    vn = ((v - mu) * lax.rsqrt(var + RMS_EPS) * lng_ref[...] + lnb_ref[...]).astype(BF16)
    row = lax.broadcasted_iota(jnp.int32, (CHUNK, CHUNK), 0)
    col = lax.broadcasted_iota(jnp.int32, (CHUNK, CHUNK), 1)
    for g in range(GMLP_GROUPS):
        cols = slice(g * GMLP_GROUP_DIM, (g + 1) * GMLP_GROUP_DIM)
        w = jnp.where(row >= col, ws_ref[g], 0.0).astype(BF16)
        mixed = _dot(w, vn[:, cols]) + bs_ref[:, g:g + 1]
        u = _gelu(u_ref[:, cols].astype(F32))
        o_ref[:, cols] = (u * mixed).astype(o_ref.dtype)


def gmlp(proj, ln_g, ln_b, w_s, b_s):
    T = proj.shape[0]
    width = GMLP_GROUPS * GMLP_GROUP_DIM
    return pl.pallas_call(
        _gmlp_kernel,
        grid=(T // CHUNK,),
        in_specs=[pl.BlockSpec((CHUNK, width), lambda i: (i, COL_GU // width)),
                  pl.BlockSpec((CHUNK, width), lambda i: (i, COL_GV // width)),
                  pl.BlockSpec((1, width), lambda i: (0, 0)),
                  pl.BlockSpec((1, width), lambda i: (0, 0)),
                  pl.BlockSpec((GMLP_GROUPS, CHUNK, CHUNK), lambda i: (0, 0, 0)),
                  pl.BlockSpec((CHUNK, GMLP_GROUPS), lambda i: (0, 0))],
        out_specs=pl.BlockSpec((CHUNK, width), lambda i: (i, 0)),
        out_shape=jax.ShapeDtypeStruct((T, width), BF16),
        compiler_params=_params("parallel"),
        name="gmlp",
    )(proj, proj, ln_g.reshape(1, width), ln_b.reshape(1, width), w_s, b_s.T)


def _dsa_prep_kernel(dq_ref, dc_ref, iq_ref, g_ref, wuk_ref, qabs_ref, iqp_ref, ckv_ref, ckvt_ref):
    c = _rms(dc_ref[...].astype(F32), g_ref[...])
    ckv_ref[...] = c.astype(ckv_ref.dtype)
    ckvt_ref[0] = c.T.astype(ckvt_ref.dtype)
    zeros = jnp.zeros((DSA_BLOCK, LANES - IDX_DIM), iqp_ref.dtype)
    for h in range(DSA_HEADS):
        q = dq_ref[:, h * DSA_HEAD_DIM:(h + 1) * DSA_HEAD_DIM]
        qabs_ref[0, h] = (_dot_nt(q, wuk_ref[h]) * (DSA_HEAD_DIM ** -0.5)).astype(qabs_ref.dtype)
    for h in range(IDX_HEADS):
        iqp_ref[0, h] = jnp.concatenate([iq_ref[:, h * IDX_DIM:(h + 1) * IDX_DIM], zeros], axis=-1)


def dsa_prep(proj, kv_norm_g, w_uk):
    T = proj.shape[0]
    nblk = T // DSA_BLOCK
    qw = DSA_HEADS * DSA_HEAD_DIM
    iq_w = IDX_HEADS * IDX_DIM
    return pl.pallas_call(
        _dsa_prep_kernel,
        grid=(nblk,),
        in_specs=[pl.BlockSpec((DSA_BLOCK, qw), lambda i: (i, COL_DQ // qw)),
                  pl.BlockSpec((DSA_BLOCK, DSA_LATENT), lambda i: (i, COL_DC // DSA_LATENT)),
                  pl.BlockSpec((DSA_BLOCK, iq_w), lambda i: (i, COL_IQ // iq_w)),
                  pl.BlockSpec((1, DSA_LATENT), lambda i: (0, 0)),
                  pl.BlockSpec((DSA_HEADS, DSA_LATENT, DSA_HEAD_DIM), lambda i: (0, 0, 0))],
        out_specs=[pl.BlockSpec((1, DSA_HEADS, DSA_BLOCK, DSA_LATENT), lambda i: (i, 0, 0, 0)),
                   pl.BlockSpec((1, IDX_HEADS, DSA_BLOCK, LANES), lambda i: (i, 0, 0, 0)),
                   pl.BlockSpec((DSA_BLOCK, DSA_LATENT), lambda i: (i, 0)),
                   pl.BlockSpec((1, DSA_LATENT, DSA_BLOCK), lambda i: (i, 0, 0))],
        out_shape=[jax.ShapeDtypeStruct((nblk, DSA_HEADS, DSA_BLOCK, DSA_LATENT), BF16),
                   jax.ShapeDtypeStruct((nblk, IDX_HEADS, DSA_BLOCK, LANES), BF16),
                   jax.ShapeDtypeStruct((T, DSA_LATENT), BF16),
                   jax.ShapeDtypeStruct((nblk, DSA_LATENT, DSA_BLOCK), BF16)],
        compiler_params=_params("parallel"),
        name="dsa_prep",
    )(proj, proj, proj, kv_norm_g.reshape(1, DSA_LATENT), w_uk)


def _dsa_kernel(iqp_ref, wq_ref, kidx_ref, ckv_ref, ckvt_ref, qa_ref, wuv_ref, o_ref,
                score_ref, m_ref, l_ref, acc_ref, *, k_sel):
    QB = DSA_BLOCK
    blk = pl.program_id(1)
    n_tiles = blk + 1
    ksel_f = float(k_sel)
    key_in_tile = lax.broadcasted_iota(jnp.int32, (QB, QB), 0)
    query_in_blk = lax.broadcasted_iota(jnp.int32, (QB, QB), 1)

    def fold8(x, op):
        return op(x.reshape(QB // 8, 8, QB), axis=0)

    def tile_rows(j):
        return pl.ds(pl.multiple_of(j * QB, QB), QB)

    wi_t = wq_ref[...].astype(F32).T * (IDX_HEADS ** -0.5)

    def score_body(j, carry):
        kt = kidx_ref[tile_rows(j), :]
        sc = jnp.zeros((QB, QB), F32)
        for h in range(IDX_HEADS):
            lg = _dot_nt(kt, iqp_ref[0, h]) * (IDX_DIM ** -0.5)
            sc = sc + wi_t[IDX_DIM + h:IDX_DIM + h + 1, :] * jnp.maximum(lg, 0.0)
        score_ref[j] = jnp.where((j < blk) | (key_in_tile <= query_in_blk), sc, -jnp.inf)
        return carry

    lax.fori_loop(0, n_tiles, score_body, 0)

    def reduce_tiles(fn, op, fold, init):
        def body(j, acc):
            return op(acc, fold8(fn(score_ref[j]), fold))
        acc = lax.fori_loop(0, n_tiles, body, jnp.full((8, QB), init, F32))
        return fold(acc, axis=0, keepdims=True)

    def count(pred):
        return reduce_tiles(lambda s: pred(s).astype(F32), jnp.add, jnp.sum, 0.0)

    n_causal = (blk * QB + 1 + lax.broadcasted_iota(jnp.int32, (1, QB), 1)).astype(F32)
    need = n_causal > ksel_f
    hi0 = reduce_tiles(lambda s: s, jnp.maximum, jnp.max, -jnp.inf)
    lo0 = reduce_tiles(lambda s: jnp.where(s > -jnp.inf, s, jnp.inf), jnp.minimum, jnp.min, jnp.inf)
    cnt_hi0 = count(lambda s: s >= hi0)
    cnt_ge0 = count(lambda s: s >= 0.0)
    cnt_gt0 = count(lambda s: s > 0.0)
    above0 = cnt_gt0 >= ksel_f
    below0 = cnt_ge0 < ksel_f
    lo0 = jnp.where(below0, lo0, 0.0)
    cnt_lo0 = jnp.where(below0, n_causal, cnt_ge0)
    hi0 = jnp.where(above0, hi0, 0.0)
    at_top = above0 & (cnt_hi0 >= ksel_f)
    lo0 = jnp.where(at_top, hi0, lo0)
    cnt_lo0 = jnp.where(at_top, cnt_hi0, cnt_lo0)
    active0 = need & (above0 | below0) & (cnt_lo0 != ksel_f) & (lo0 < hi0)

    def bisect_cond(state):
        return jnp.max(state[3]) > 0.0

    def bisect_body(state):
        lo, hi, cnt_lo, active = state
        mid = lo + 0.5 * (hi - lo)
        c = count(lambda s: s >= mid)
        ge = c >= ksel_f
        live = active > 0.0
        moved = (mid > lo) & (mid < hi)
        new_lo = jnp.where(live & ge, mid, lo)
        new_hi = jnp.where(live & ~ge, mid, hi)
        new_cnt = jnp.where(live & ge, c, cnt_lo)
        new_active = live & moved & (new_cnt != ksel_f)
        return new_lo, new_hi, new_cnt, new_active.astype(F32)

    lo = lax.while_loop(bisect_cond, bisect_body, (lo0, hi0, cnt_lo0, active0.astype(F32)))[0]
    thr = jnp.where(need, lo, -jnp.inf)
    n_gt = count(lambda s: s > thr)
    tie_quota = jnp.where(need, ksel_f - n_gt, 0.0)

    m_ref[...] = jnp.full_like(m_ref, NEG_BIG)
    l_ref[...] = jnp.zeros_like(l_ref)
    acc_ref[...] = jnp.zeros_like(acc_ref)
    strict_lower = (query_in_blk < key_in_tile).astype(BF16)

    def attn_body(j, tie_seen):
        sc = score_ref[j]
        eq = sc == thr
        eq_f = eq.astype(F32)
        prefix = _dot(strict_lower, eq_f.astype(BF16)) + tie_seen
        sel = (sc > thr) | (eq & (prefix < tie_quota))
        bias = jnp.where(sel, 0.0, NEG_BIG)
        c_t = ckv_ref[tile_rows(j), :]
        ct_t = ckvt_ref[j]
        for h in range(DSA_HEADS):
            s = _dot_nt(c_t, qa_ref[0, h]) + bias
            m_old = m_ref[h]
            m_new = jnp.maximum(m_old, jnp.max(fold8(s, jnp.max), axis=0, keepdims=True))
            alpha = jnp.exp(m_old - m_new)
            p = jnp.exp(s - m_new)
            l_ref[h] = alpha * l_ref[h] + jnp.sum(fold8(p, jnp.sum), axis=0, keepdims=True)
            acc_ref[h] = alpha * acc_ref[h] + _dot(ct_t, p.astype(BF16))
            m_ref[h] = m_new
        return tie_seen + jnp.sum(fold8(eq_f, jnp.sum), axis=0, keepdims=True)

    lax.fori_loop(0, n_tiles, attn_body, jnp.zeros((1, QB), F32))

    for h in range(DSA_HEADS):
        o_lat = (acc_ref[h] / l_ref[h]).T.astype(BF16)
        o_ref[:, h * DSA_HEAD_DIM:(h + 1) * DSA_HEAD_DIM] = _dot(o_lat, wuv_ref[h]).astype(o_ref.dtype)


def dsa_attention(proj, q_abs, iq_pad, ckv, ckv_t, w_uv, batch, seq):
    T = proj.shape[0]
    nb = seq // DSA_BLOCK
    k_sel = min(DSA_TOPK_MAX, seq // 4)
    out_w = DSA_HEADS * DSA_HEAD_DIM
    return pl.pallas_call(
        functools.partial(_dsa_kernel, k_sel=k_sel),
        grid=(batch, nb),
        in_specs=[pl.BlockSpec((1, IDX_HEADS, DSA_BLOCK, LANES), lambda b, i: (b * nb + i, 0, 0, 0)),
                  pl.BlockSpec((DSA_BLOCK, LANES), lambda b, i: (b * nb + i, COL_IKW // LANES)),
                  pl.BlockSpec((seq, LANES), lambda b, i: (b, COL_IKW // LANES)),
                  pl.BlockSpec((seq, DSA_LATENT), lambda b, i: (b, 0)),
                  pl.BlockSpec((nb, DSA_LATENT, DSA_BLOCK), lambda b, i: (b, 0, 0)),
                  pl.BlockSpec((1, DSA_HEADS, DSA_BLOCK, DSA_LATENT), lambda b, i: (b * nb + i, 0, 0, 0)),
                  pl.BlockSpec((DSA_HEADS, DSA_LATENT, DSA_HEAD_DIM), lambda b, i: (0, 0, 0))],
        out_specs=pl.BlockSpec((DSA_BLOCK, out_w), lambda b, i: (b * nb + i, 0)),
        out_shape=jax.ShapeDtypeStruct((T, out_w), BF16),
        scratch_shapes=[pltpu.VMEM((nb, DSA_BLOCK, DSA_BLOCK), F32),
                        pltpu.VMEM((DSA_HEADS, 1, DSA_BLOCK), F32),
                        pltpu.VMEM((DSA_HEADS, 1, DSA_BLOCK), F32),
                        pltpu.VMEM((DSA_HEADS, DSA_LATENT, DSA_BLOCK), F32)],
        compiler_params=_params("parallel", "arbitrary"),
        name="dsa_attention",
    )(iq_pad, proj, proj, ckv, ckv_t, q_abs, w_uv)


def _merge_kernel(ret_ref, gm_ref, ds_ref, wr_ref, wg_ref, wd_ref, ga_ref, gb_ref, gc_ref, o_ref):
    def branch(a_ref, w_ref, gate_ref):
        return jax.nn.sigmoid(gate_ref[...].astype(F32)) * _dot(a_ref[...], w_ref[...])

    merged = branch(ret_ref, wr_ref, ga_ref) + branch(gm_ref, wg_ref, gb_ref) + branch(ds_ref, wd_ref, gc_ref)
    o_ref[...] = merged.astype(o_ref.dtype)


def merge_branches(ret, gm, ds, w_ret_o, w_gmlp_o, w_dsa_o, proj, d_model, *, tm=1024, tn=512):
    T, K = ret.shape
    gate_blk = COL_GATES // tn
    per_gate = d_model // tn
    act = pl.BlockSpec((tm, K), lambda i, j: (i, 0))
    wgt = pl.BlockSpec((K, tn), lambda i, j: (0, j))

    def gate(which):
        return pl.BlockSpec((tm, tn), lambda i, j: (i, gate_blk + which * per_gate + j))

    return pl.pallas_call(
        _merge_kernel,
        grid=(T // tm, d_model // tn),
        in_specs=[act, act, act, wgt, wgt, wgt, gate(0), gate(1), gate(2)],
        out_specs=pl.BlockSpec((tm, tn), lambda i, j: (i, j)),
        out_shape=jax.ShapeDtypeStruct((T, d_model), BF16),
        compiler_params=_params("parallel", "arbitrary"),
        name="merge_branches",
    )(ret, gm, ds, w_ret_o, w_gmlp_o, w_dsa_o, proj, proj, proj)


def _out_proj_kernel(a_ref, w_ref, x_ref, o_ref):
    o_ref[...] = x_ref[...] + _dot(a_ref[...], w_ref[...])


def out_proj_residual(merged, w_out, x, *, tm=1024, tn=512):
    T, K = merged.shape
    N = w_out.shape[1]
    return pl.pallas_call(
        _out_proj_kernel,
        grid=(T // tm, N // tn),
        in_specs=[pl.BlockSpec((tm, K), lambda i, j: (i, 0)),
                  pl.BlockSpec((K, tn), lambda i, j: (0, j)),
                  pl.BlockSpec((tm, tn), lambda i, j: (i, j))],
        out_specs=pl.BlockSpec((tm, tn), lambda i, j: (i, j)),
        out_shape=jax.ShapeDtypeStruct((T, N), F32),
        compiler_params=_params("parallel", "arbitrary"),
        name="out_proj_residual",
    )(merged, w_out, x)


def _router_kernel(x_ref, g_ref, w_ref, b_ref, h_ref, ids_ref, wts_ref):
    h = _rms(x_ref[...], g_ref[...])
    h_ref[...] = h
    logits = _dot(h.astype(BF16), w_ref[...]) + b_ref[...]
    lane = lax.broadcasted_iota(jnp.int32, logits.shape, 1)
    big = jnp.int32(LANES)

    def first_argmax(vals, valid):
        v = jnp.where(valid, vals, -jnp.inf)
        m = jnp.max(v, axis=-1, keepdims=True)
        idx = jnp.min(jnp.where(valid & (v == m), lane, big), axis=-1, keepdims=True)
        return m, idx

    is_grp = lane < MOE_GROUPS
    g_max, g_sel = first_argmax(logits, is_grp)
    g_den = jnp.sum(jnp.where(is_grp, jnp.exp(logits - g_max), 0.0), axis=-1, keepdims=True)
    p_g = 1.0 / g_den
    e_lo = MOE_GROUPS + g_sel * MOE_EXPERTS_PER_GROUP
    in_grp = (lane >= e_lo) & (lane < e_lo + MOE_EXPERTS_PER_GROUP)
    m1, i1 = first_argmax(logits, in_grp)
    m2, i2 = first_argmax(logits, in_grp & (lane != i1))
    e2 = jnp.exp(m2 - m1)
    w1 = p_g / (1.0 + e2)
    w2 = p_g * e2 / (1.0 + e2)
    ids_ref[...] = jnp.where(lane == 0, i1 - MOE_GROUPS, jnp.where(lane == 1, i2 - MOE_GROUPS, 0))
    wts_ref[...] = jnp.where(lane == 0, w1, jnp.where(lane == 1, w2, 0.0))


def moe_router(x, g, w_router, b_router, *, tm=512):
    T, D = x.shape
    return pl.pallas_call(
        _router_kernel,
        grid=(T // tm,),
        in_specs=[pl.BlockSpec((tm, D), lambda i: (i, 0)),
                  pl.BlockSpec((1, D), lambda i: (0, 0)),
                  pl.BlockSpec((D, LANES), lambda i: (0, 0)),
                  pl.BlockSpec((1, LANES), lambda i: (0, 0))],
        out_specs=[pl.BlockSpec((tm, D), lambda i: (i, 0)),
                   pl.BlockSpec((tm, LANES), lambda i: (i, 0)),
                   pl.BlockSpec((tm, LANES), lambda i: (i, 0))],
        out_shape=[jax.ShapeDtypeStruct((T, D), F32),
                   jax.ShapeDtypeStruct((T, LANES), jnp.int32),
                   jax.ShapeDtypeStruct((T, LANES), F32)],
        compiler_params=_params("parallel"),
        name="moe_router",
    )(x, g, w_router, b_router)


def _route_tables(ids, wts, n_rows):
    T = ids.shape[0]
    e = ids[:, :MOE_TOPK].reshape(-1)
    onehot = (e[:, None] == jnp.arange(MOE_EXPERTS, dtype=jnp.int32)[None, :]).astype(jnp.int32)
    rank = jnp.sum((jnp.cumsum(onehot, axis=0) - onehot) * onehot, axis=1)
    counts = jnp.sum(onehot, axis=0)
    padded = (counts + MOE_TILE - 1) // MOE_TILE * MOE_TILE
    ends = jnp.cumsum(padded)
    pos = (ends - padded)[e] + rank
    src_token = jnp.zeros((n_rows,), jnp.int32).at[pos].set(jnp.arange(MOE_TOPK * T, dtype=jnp.int32) // MOE_TOPK)
    row_w = jnp.zeros((n_rows,), F32).at[pos].set(wts[:, :MOE_TOPK].reshape(-1))
    tile_start = jnp.arange(n_rows // MOE_TILE, dtype=jnp.int32) * MOE_TILE
    tile_expert = jnp.sum((ends[None, :] <= tile_start[:, None]).astype(jnp.int32), axis=1)
    tile_expert = jnp.minimum(tile_expert, MOE_EXPERTS - 1)
    n_used = (ends[-1] // MOE_TILE).astype(jnp.int32).reshape(1)
    return src_token, row_w.reshape(n_rows, 1), pos.reshape(T, MOE_TOPK).astype(jnp.int32), tile_expert, n_used


def _start_row_gather(idx_ref, base, src_hbm, dst_ref, sem, n):
    def issue(r, carry):
        pltpu.make_async_copy(src_hbm.at[pl.ds(idx_ref[base + r], 1)], dst_ref.at[pl.ds(r, 1)], sem).start()
        return carry
    lax.fori_loop(0, n, issue, 0)


def _wait_row_gather(src_hbm, dst_ref, sem, n):
    pltpu.make_async_copy(src_hbm.at[pl.ds(0, n)], dst_ref, sem).wait()


def _gather_rows_kernel(idx_ref, src_hbm, o_ref, sem):
    n = o_ref.shape[0]
    _start_row_gather(idx_ref, pl.program_id(0) * n, src_hbm, o_ref, sem, n)
    _wait_row_gather(src_hbm, o_ref, sem, n)


def gather_rows(src, idx, *, tile):
    P = idx.shape[0]
    D = src.shape[1]
    return pl.pallas_call(
        _gather_rows_kernel,
        grid_spec=pltpu.PrefetchScalarGridSpec(
            num_scalar_prefetch=1,
            grid=(P // tile,),
            in_specs=[pl.BlockSpec(memory_space=pl.ANY)],
            out_specs=pl.BlockSpec((tile, D), lambda i, idx: (i, 0)),
            scratch_shapes=[pltpu.SemaphoreType.DMA(())]),
        out_shape=jax.ShapeDtypeStruct((P, D), src.dtype),
        compiler_params=_params("arbitrary"),
        name="moe_gather_rows",
    )(idx, src)


def _moe_experts_kernel(tile_expert_ref, n_used_ref, x_ref, rw_ref, w1_ref, w3_ref, w2_ref, o_ref):
    i = pl.program_id(0)

    @pl.when(i < n_used_ref[0])
    def _():
        x = x_ref[...].astype(BF16)
        a = _dot(x, w1_ref[0])
        hid = a * jax.nn.sigmoid(a) * _dot(x, w3_ref[0])
        o_ref[...] = _dot((hid * rw_ref[...]).astype(BF16), w2_ref[0])

    @pl.when(i >= n_used_ref[0])
    def _():
        o_ref[...] = jnp.zeros_like(o_ref)


def moe_experts(xs, row_w, tile_expert, n_used, w1, w3, w2):
    P, D = xs.shape
    E, _, F = w1.shape

    def wspec(shape):
        return pl.BlockSpec((1,) + shape, lambda i, te, nu: (te[i], 0, 0))

    return pl.pallas_call(
        _moe_experts_kernel,
        grid_spec=pltpu.PrefetchScalarGridSpec(
            num_scalar_prefetch=2,
            grid=(P // MOE_TILE,),
            in_specs=[pl.BlockSpec((MOE_TILE, D), lambda i, te, nu: (i, 0)),
                      pl.BlockSpec((MOE_TILE, 1), lambda i, te, nu: (i, 0)),
                      wspec((D, F)), wspec((D, F)), wspec((F, D))],
            out_specs=pl.BlockSpec((MOE_TILE, D), lambda i, te, nu: (i, 0))),
        out_shape=jax.ShapeDtypeStruct((P, D), F32),
        compiler_params=_params("arbitrary"),
        name="moe_experts",
    )(tile_expert, n_used, xs, row_w, w1, w3, w2)


def _moe_combine_kernel(pos_ref, ys_hbm, x_ref, g_ref, o_ref, y0_ref, y1_ref, sem, *, final_norm):
    n = x_ref.shape[0]
    base = pl.program_id(0) * n
    for slot, y_ref in enumerate((y0_ref, y1_ref)):
        def issue(r, carry, slot=slot, y_ref=y_ref):
            row = pos_ref[(base + r) * MOE_TOPK + slot]
            pltpu.make_async_copy(ys_hbm.at[pl.ds(row, 1)], y_ref.at[pl.ds(r, 1)], sem.at[slot]).start()
            return carry
        lax.fori_loop(0, n, issue, 0)
    for slot, y_ref in enumerate((y0_ref, y1_ref)):
        _wait_row_gather(ys_hbm, y_ref, sem.at[slot], n)
    out = x_ref[...] + (y0_ref[...] + y1_ref[...])
    o_ref[...] = _rms(out, g_ref[...]) if final_norm else out


def moe_combine(ys, pos, x, g, *, final_norm, tile=256):
    T, D = x.shape
    return pl.pallas_call(
        functools.partial(_moe_combine_kernel, final_norm=final_norm),
        grid_spec=pltpu.PrefetchScalarGridSpec(
            num_scalar_prefetch=1,
            grid=(T // tile,),
            in_specs=[pl.BlockSpec(memory_space=pl.ANY),
                      pl.BlockSpec((tile, D), lambda i, pos: (i, 0)),
                      pl.BlockSpec((1, D), lambda i, pos: (0, 0))],
            out_specs=pl.BlockSpec((tile, D), lambda i, pos: (i, 0)),
            scratch_shapes=[pltpu.VMEM((tile, D), F32), pltpu.VMEM((tile, D), F32),
                            pltpu.SemaphoreType.DMA((MOE_TOPK,))]),
        out_shape=jax.ShapeDtypeStruct((T, D), F32),
        compiler_params=_params("arbitrary"),
        name="moe_combine",
    )(pos.reshape(-1), ys, x, g)


def _pack_w_in(w):
    D = w.shape[0]
    widths = (1024, 1024, 1024, 1024, 1024, 1024, 1024, 256, 512, 64, 8, D, D, D)
    offs = [0]
    for wd in widths:
        offs.append(offs[-1] + wd)
    rq, rk, rv, rg, gu, gv, dq, dc, iq, ik, iw, ga, gb, gc = (
        w[:, offs[i]:offs[i + 1]] for i in range(len(widths)))
    ikw_pad = jnp.zeros((D, COL_GATES - COL_IKW - IDX_DIM - IDX_HEADS), w.dtype)
    packed = jnp.concatenate([rq, rk, rv, rg, gu, gv, dq, iq, dc, ik, iw, ikw_pad, ga, gb, gc], axis=1)
    assert packed.shape[1] == PROJ_WIDTH
    return packed.astype(BF16)


def _pack_router(w_group, b_group, w_expert, b_expert):
    D = w_group.shape[0]
    w_e = jnp.transpose(w_expert, (1, 0, 2)).reshape(D, MOE_EXPERTS)
    pad = LANES - MOE_GROUPS - MOE_EXPERTS
    w = jnp.concatenate([w_group, w_e, jnp.zeros((D, pad), w_group.dtype)], axis=1).astype(BF16)
    b = jnp.concatenate([b_group, b_expert.reshape(MOE_EXPERTS), jnp.zeros((pad,), b_group.dtype)])
    return w, b.reshape(1, LANES).astype(F32)


def kernel(x, norm_mix_g, w_in, w_ret_o, gmlp_ln_g, gmlp_ln_b, gmlp_w_s, gmlp_b_s, w_gmlp_o, dsa_kv_norm_g, dsa_w_uk, dsa_w_uv, w_dsa_o, w_out, norm_ffn_g, moe_w_group, moe_b_group, moe_w_expert, moe_b_expert, moe_w1, moe_w3, moe_w2, final_norm_g):
    B, S, D = x.shape
    depth = w_in.shape[0]
    assert S % DSA_BLOCK == 0 and D == 2048
    xt = x.reshape(B * S, D)
    for l in range(depth):
        proj = norm_matmul(xt, norm_mix_g[l].reshape(1, D), _pack_w_in(w_in[l]))
        ret = retention(proj, B, S)
        gm = gmlp(proj, gmlp_ln_g[l], gmlp_ln_b[l], gmlp_w_s[l], gmlp_b_s[l])
        q_abs, iq_pad, ckv, ckv_t = dsa_prep(proj, dsa_kv_norm_g[l], dsa_w_uk[l].astype(BF16))
        ds = dsa_attention(proj, q_abs, iq_pad, ckv, ckv_t, dsa_w_uv[l].astype(BF16), B, S)
        merged = merge_branches(ret, gm, ds, w_ret_o[l].astype(BF16), w_gmlp_o[l].astype(BF16),
                                w_dsa_o[l].astype(BF16), proj, D)
        xt = out_proj_residual(merged, w_out[l].astype(BF16), xt)
        w_router, b_router = _pack_router(moe_w_group[l], moe_b_group[l], moe_w_expert[l], moe_b_expert[l])
        h, ids, wts = moe_router(xt, norm_ffn_g[l].reshape(1, D), w_router, b_router)
        n_rows = MOE_TOPK * B * S + MOE_EXPERTS * MOE_TILE
        src_token, row_w, pos, tile_expert, n_used = _route_tables(ids, wts, n_rows)
        xs = gather_rows(h, src_token, tile=MOE_TILE)
        F = moe_w1.shape[-1]
        ys = moe_experts(xs, row_w, tile_expert, n_used,
                         moe_w1[l].reshape(MOE_EXPERTS, D, F).astype(BF16),
                         moe_w3[l].reshape(MOE_EXPERTS, D, F).astype(BF16),
                         moe_w2[l].reshape(MOE_EXPERTS, F, D).astype(BF16))
        last = l == depth - 1
        xt = moe_combine(ys, pos, xt, final_norm_g.reshape(1, D), final_norm=last)
    return xt.reshape(B, S, D)
```

```python
import functools
import math

import jax
import jax.numpy as jnp
from jax import lax
from jax.experimental import pallas as pl
from jax.experimental.pallas import tpu as pltpu

F32 = jnp.float32
BF16 = jnp.bfloat16

RMS_EPS = 1e-6
RET_HEADS = 4
RET_DIM = 256
CHUNK = 128
ROPE_BASE = 10000.0
GMLP_GROUPS = 4
GMLP_GROUP_DIM = 256
DSA_HEADS = 8
DSA_HEAD_DIM = 128
DSA_LATENT = 256
IDX_HEADS = 8
IDX_DIM = 64
DSA_TOPK_MAX = 256
DSA_BLOCK = 256
MOE_GROUPS = 4
MOE_EXPERTS_PER_GROUP = 8
MOE_EXPERTS = MOE_GROUPS * MOE_EXPERTS_PER_GROUP
MOE_HIDDEN = 512
MOE_TOPK = 2
MOE_TILE = 256
LANES = 128
ROW_TILE = 8
NEG_BIG = -1e30

COL_RQ, COL_RK, COL_RV, COL_RG = 0, 1024, 2048, 3072
COL_GU, COL_GV = 4096, 5120
COL_DQ, COL_IQ, COL_DC, COL_IKW = 6144, 7168, 7680, 7936
COL_GATES = 8192
PROJ_WIDTH = 14336
VMEM_LIMIT = 56 * 1024 * 1024


def _params(*sem):
    return pltpu.CompilerParams(dimension_semantics=sem, vmem_limit_bytes=VMEM_LIMIT)


def _rms(x, g):
    return x * lax.rsqrt(jnp.mean(x * x, axis=-1, keepdims=True) + RMS_EPS) * g


def _dot(a, b):
    return jnp.dot(a, b, preferred_element_type=F32)


def _dot_nt(a, b):
    return lax.dot_general(a, b, (((1,), (1,)), ((), ())), preferred_element_type=F32)


def _norm_matmul_kernel(x_ref, g_ref, w_ref, o_ref, h_ref):
    @pl.when(pl.program_id(1) == 0)
    def _():
        h_ref[...] = _rms(x_ref[...], g_ref[...]).astype(BF16)

    o_ref[...] = _dot(h_ref[...], w_ref[...]).astype(o_ref.dtype)


def norm_matmul(x, g, w, *, tm=1024, tn=512, out_dtype=BF16):
    T, D = x.shape
    N = w.shape[1]
    return pl.pallas_call(
        _norm_matmul_kernel,
        grid=(T // tm, N // tn),
        in_specs=[pl.BlockSpec((tm, D), lambda i, j: (i, 0)),
                  pl.BlockSpec((1, D), lambda i, j: (0, 0)),
                  pl.BlockSpec((D, tn), lambda i, j: (0, j))],
        out_specs=pl.BlockSpec((tm, tn), lambda i, j: (i, j)),
        out_shape=jax.ShapeDtypeStruct((T, N), out_dtype),
        scratch_shapes=[pltpu.VMEM((tm, D), BF16)],
        compiler_params=_params("parallel", "arbitrary"),
        name="norm_in_proj",
    )(x, g, w)


def _retention_kernel(q_ref, k_ref, v_ref, gate_ref, cos_ref, sin_ref, decay_ref, xi_ref,
                      zeta_ref, gchunk_ref, o_ref, r_ref):
    @pl.when(pl.program_id(1) == 0)
    def _():
        r_ref[...] = jnp.zeros_like(r_ref)

    cos = cos_ref[...]
    sin = sin_ref[...]
    half = RET_DIM // 2

    def rotary(t):
        t1, t2 = t[:, :half], t[:, half:]
        return jnp.concatenate([t1 * cos - t2 * sin, t1 * sin + t2 * cos], axis=-1)

    for h in range(RET_HEADS):
        cols = slice(h * RET_DIM, (h + 1) * RET_DIM)
        q = rotary(q_ref[:, cols].astype(F32)).astype(BF16)
        k = rotary(k_ref[:, cols].astype(F32)) * (RET_DIM ** -0.5)
        v = v_ref[:, cols]
        inner = _dot_nt(q, k.astype(BF16)) * decay_ref[h]
        r_old = r_ref[h]
        o = _dot(inner.astype(BF16), v) + _dot(q, r_old.astype(BF16)) * xi_ref[:, h:h + 1]
        kz = (k * zeta_ref[:, h:h + 1]).astype(BF16)
        r_ref[h] = r_old * gchunk_ref[h] + _dot(kz.T, v)
        o = o * lax.rsqrt(jnp.mean(o * o, axis=-1, keepdims=True) + RMS_EPS)
        gate = gate_ref[:, cols].astype(F32)
        o_ref[:, cols] = (gate * jax.nn.sigmoid(gate) * o).astype(o_ref.dtype)


def retention(proj, batch, seq):
    T = proj.shape[0]
    n_chunks = seq // CHUNK
    width = RET_HEADS * RET_DIM
    half = RET_DIM // 2
    inv = ROPE_BASE ** (-jnp.arange(half, dtype=F32) / half)
    ang = jnp.arange(seq).astype(F32)[:, None] * inv[None, :]
    cos, sin = jnp.cos(ang), jnp.sin(ang)
    log_g = jnp.log(1.0 - 2.0 ** (-5.0 - jnp.arange(RET_HEADS, dtype=F32)))
    i = jnp.arange(CHUNK, dtype=F32)
    diff = i[:, None] - i[None, :]
    decay = jnp.where(diff >= 0, jnp.exp(log_g[:, None, None] * jnp.maximum(diff, 0.0)), 0.0)
    xi = jnp.exp(log_g[None, :] * (i[:, None] + 1.0))
    zeta = jnp.exp(log_g[None, :] * (CHUNK - 1.0 - i[:, None]))
    gchunk = jnp.exp(log_g * CHUNK)

    def col_spec(col):
        return pl.BlockSpec((CHUNK, width), lambda b, n: (b * n_chunks + n, col // width))

    return pl.pallas_call(
        _retention_kernel,
        grid=(batch, n_chunks),
        in_specs=[col_spec(COL_RQ), col_spec(COL_RK), col_spec(COL_RV), col_spec(COL_RG),
                  pl.BlockSpec((CHUNK, half), lambda b, n: (n, 0)),
                  pl.BlockSpec((CHUNK, half), lambda b, n: (n, 0)),
                  pl.BlockSpec((RET_HEADS, CHUNK, CHUNK), lambda b, n: (0, 0, 0)),
                  pl.BlockSpec((CHUNK, RET_HEADS), lambda b, n: (0, 0)),
                  pl.BlockSpec((CHUNK, RET_HEADS), lambda b, n: (0, 0)),
                  pl.BlockSpec(memory_space=pltpu.SMEM)],
        out_specs=pl.BlockSpec((CHUNK, width), lambda b, n: (b * n_chunks + n, 0)),
        out_shape=jax.ShapeDtypeStruct((T, width), BF16),
        scratch_shapes=[pltpu.VMEM((RET_HEADS, RET_DIM, RET_DIM), F32)],
        compiler_params=_params("parallel", "arbitrary"),
        name="retention",
    )(proj, proj, proj, proj, cos, sin, decay, xi, zeta, gchunk)


def _gelu(x):
    return 0.5 * x * (1.0 + lax.erf(x * (2.0 ** -0.5)))


def _gmlp_kernel(u_ref, v_ref, lng_ref, lnb_ref, ws_ref, bs_ref, o_ref):
    v = _gelu(v_ref[...].astype(F32))
    mu = jnp.mean(v, axis=-1, keepdims=True)
    var = jnp.mean(jnp.square(v - mu), axis=-1, keepdims=True)
    vn = ((v - mu) * lax.rsqrt(var + RMS_EPS) * lng_ref[...] + lnb_ref[...]).astype(BF16)
    row = lax.broadcasted_iota(jnp.int32, (CHUNK, CHUNK), 0)
    col = lax.broadcasted_iota(jnp.int32, (CHUNK, CHUNK), 1)
    for g in range(GMLP_GROUPS):
        cols = slice(g * GMLP_GROUP_DIM, (g + 1) * GMLP_GROUP_DIM)
        w = jnp.where(row >= col, ws_ref[g], 0.0).astype(BF16)
        mixed = _dot(w, vn[:, cols]) + bs_ref[:, g:g + 1]
        u = _gelu(u_ref[:, cols].astype(F32))
        o_ref[:, cols] = (u * mixed).astype(o_ref.dtype)


def gmlp(proj, ln_g, ln_b, w_s, b_s):
    T = proj.shape[0]
    width = GMLP_GROUPS * GMLP_GROUP_DIM
    return pl.pallas_call(
        _gmlp_kernel,
        grid=(T // CHUNK,),
        in_specs=[pl.BlockSpec((CHUNK, width), lambda i: (i, COL_GU // width)),
                  pl.BlockSpec((CHUNK, width), lambda i: (i, COL_GV // width)),
                  pl.BlockSpec((1, width), lambda i: (0, 0)),
                  pl.BlockSpec((1, width), lambda i: (0, 0)),
                  pl.BlockSpec((GMLP_GROUPS, CHUNK, CHUNK), lambda i: (0, 0, 0)),
                  pl.BlockSpec((CHUNK, GMLP_GROUPS), lambda i: (0, 0))],
        out_specs=pl.BlockSpec((CHUNK, width), lambda i: (i, 0)),
        out_shape=jax.ShapeDtypeStruct((T, width), BF16),
        compiler_params=_params("parallel"),
        name="gmlp",
    )(proj, proj, ln_g.reshape(1, width), ln_b.reshape(1, width), w_s, b_s.T)


def _dsa_prep_kernel(dq_ref, dc_ref, iq_ref, g_ref, wuk_ref, qabs_ref, iqp_ref, ckv_ref, ckvt_ref):
    c = _rms(dc_ref[...].astype(F32), g_ref[...])
    ckv_ref[...] = c.astype(ckv_ref.dtype)
    ckvt_ref[0] = c.T.astype(ckvt_ref.dtype)
    zeros = jnp.zeros((DSA_BLOCK, LANES - IDX_DIM), iqp_ref.dtype)
    for h in range(DSA_HEADS):
        q = dq_ref[:, h * DSA_HEAD_DIM:(h + 1) * DSA_HEAD_DIM]
        qabs_ref[0, h] = (_dot_nt(q, wuk_ref[h]) * (DSA_HEAD_DIM ** -0.5)).astype(qabs_ref.dtype)
    for h in range(IDX_HEADS):
        iqp_ref[0, h] = jnp.concatenate([iq_ref[:, h * IDX_DIM:(h + 1) * IDX_DIM], zeros], axis=-1)


def dsa_prep(proj, kv_norm_g, w_uk):
    T = proj.shape[0]
    nblk = T // DSA_BLOCK
    qw = DSA_HEADS * DSA_HEAD_DIM
    iq_w = IDX_HEADS * IDX_DIM
    return pl.pallas_call(
        _dsa_prep_kernel,
        grid=(nblk,),
        in_specs=[pl.BlockSpec((DSA_BLOCK, qw), lambda i: (i, COL_DQ // qw)),
                  pl.BlockSpec((DSA_BLOCK, DSA_LATENT), lambda i: (i, COL_DC // DSA_LATENT)),
                  pl.BlockSpec((DSA_BLOCK, iq_w), lambda i: (i, COL_IQ // iq_w)),
                  pl.BlockSpec((1, DSA_LATENT), lambda i: (0, 0)),
                  pl.BlockSpec((DSA_HEADS, DSA_LATENT, DSA_HEAD_DIM), lambda i: (0, 0, 0))],
        out_specs=[pl.BlockSpec((1, DSA_HEADS, DSA_BLOCK, DSA_LATENT), lambda i: (i, 0, 0, 0)),
                   pl.BlockSpec((1, IDX_HEADS, DSA_BLOCK, LANES), lambda i: (i, 0, 0, 0)),
                   pl.BlockSpec((DSA_BLOCK, DSA_LATENT), lambda i: (i, 0)),
                   pl.BlockSpec((1, DSA_LATENT, DSA_BLOCK), lambda i: (i, 0, 0))],
        out_shape=[jax.ShapeDtypeStruct((nblk, DSA_HEADS, DSA_BLOCK, DSA_LATENT), BF16),
                   jax.ShapeDtypeStruct((nblk, IDX_HEADS, DSA_BLOCK, LANES), BF16),
                   jax.ShapeDtypeStruct((T, DSA_LATENT), BF16),
                   jax.ShapeDtypeStruct((nblk, DSA_LATENT, DSA_BLOCK), BF16)],
        compiler_params=_params("parallel"),
        name="dsa_prep",
    )(proj, proj, proj, kv_norm_g.reshape(1, DSA_LATENT), w_uk)


def _dsa_kernel(iqp_ref, wq_ref, kidx_ref, ckv_ref, ckvt_ref, qa_ref, wuv_ref, o_ref,
                score_ref, m_ref, l_ref, acc_ref, *, k_sel):
    QB = DSA_BLOCK
    blk = pl.program_id(1)
    n_tiles = blk + 1
    ksel_f = float(k_sel)
    key_in_tile = lax.broadcasted_iota(jnp.int32, (QB, QB), 0)
    query_in_blk = lax.broadcasted_iota(jnp.int32, (QB, QB), 1)

    def fold8(x, op):
        return op(x.reshape(QB // 8, 8, QB), axis=0)

    def tile_rows(j):
        return pl.ds(pl.multiple_of(j * QB, QB), QB)

    wi_t = wq_ref[...].astype(F32).T * (IDX_HEADS ** -0.5)

    def score_body(j, carry):
        kt = kidx_ref[tile_rows(j), :]
        sc = jnp.zeros((QB, QB), F32)
        for h in range(IDX_HEADS):
            lg = _dot_nt(kt, iqp_ref[0, h]) * (IDX_DIM ** -0.5)
            sc = sc + wi_t[IDX_DIM + h:IDX_DIM + h + 1, :] * jnp.maximum(lg, 0.0)
        score_ref[j] = jnp.where((j < blk) | (key_in_tile <= query_in_blk), sc, -jnp.inf)
        return carry

    lax.fori_loop(0, n_tiles, score_body, 0)

    def reduce_tiles(fn, op, fold, init):
        def body(j, acc):
            return op(acc, fold8(fn(score_ref[j]), fold))
        acc = lax.fori_loop(0, n_tiles, body, jnp.full((8, QB), init, F32))
        return fold(acc, axis=0, keepdims=True)

    def count(pred):
        return reduce_tiles(lambda s: pred(s).astype(F32), jnp.add, jnp.sum, 0.0)

    n_causal = (blk * QB + 1 + lax.broadcasted_iota(jnp.int32, (1, QB), 1)).astype(F32)
    need = n_causal > ksel_f
    hi0 = reduce_tiles(lambda s: s, jnp.maximum, jnp.max, -jnp.inf)
    lo0 = reduce_tiles(lambda s: jnp.where(s > -jnp.inf, s, jnp.inf), jnp.minimum, jnp.min, jnp.inf)
    cnt_hi0 = count(lambda s: s >= hi0)
    cnt_ge0 = count(lambda s: s >= 0.0)
    cnt_gt0 = count(lambda s: s > 0.0)
    above0 = cnt_gt0 >= ksel_f
    below0 = cnt_ge0 < ksel_f
    lo0 = jnp.where(below0, lo0, 0.0)
    cnt_lo0 = jnp.where(below0, n_causal, cnt_ge0)
    hi0 = jnp.where(above0, hi0, 0.0)
    at_top = above0 & (cnt_hi0 >= ksel_f)
    lo0 = jnp.where(at_top, hi0, lo0)
    cnt_lo0 = jnp.where(at_top, cnt_hi0, cnt_lo0)
    active0 = need & (above0 | below0) & (cnt_lo0 != ksel_f) & (lo0 < hi0)

    def bisect_cond(state):
        return jnp.max(state[3]) > 0.0

    def bisect_body(state):
        lo, hi, cnt_lo, active = state
        mid = lo + 0.5 * (hi - lo)
        c = count(lambda s: s >= mid)
        ge = c >= ksel_f
        live = active > 0.0
        moved = (mid > lo) & (mid < hi)
        new_lo = jnp.where(live & ge, mid, lo)
        new_hi = jnp.where(live & ~ge, mid, hi)
        new_cnt = jnp.where(live & ge, c, cnt_lo)
        new_active = live & moved & (new_cnt != ksel_f)
        return new_lo, new_hi, new_cnt, new_active.astype(F32)

    lo = lax.while_loop(bisect_cond, bisect_body, (lo0, hi0, cnt_lo0, active0.astype(F32)))[0]
    thr = jnp.where(need, lo, -jnp.inf)
    n_gt = count(lambda s: s > thr)
    tie_quota = jnp.where(need, ksel_f - n_gt, 0.0)

    m_ref[...] = jnp.full_like(m_ref, NEG_BIG)
    l_ref[...] = jnp.zeros_like(l_ref)
    acc_ref[...] = jnp.zeros_like(acc_ref)
    strict_lower = (query_in_blk < key_in_tile).astype(BF16)

    def attn_body(j, tie_seen):
        sc = score_ref[j]
        eq = sc == thr
        eq_f = eq.astype(F32)
        prefix = _dot(strict_lower, eq_f.astype(BF16)) + tie_seen
        sel = (sc > thr) | (eq & (prefix < tie_quota))
        bias = jnp.where(sel, 0.0, NEG_BIG)
        c_t = ckv_ref[tile_rows(j), :]
        ct_t = ckvt_ref[j]
        for h in range(DSA_HEADS):
            s = _dot_nt(c_t, qa_ref[0, h]) + bias
            m_old = m_ref[h]
            m_new = jnp.maximum(m_old, jnp.max(fold8(s, jnp.max), axis=0, keepdims=True))
            alpha = jnp.exp(m_old - m_new)
            p = jnp.exp(s - m_new)
            l_ref[h] = alpha * l_ref[h] + jnp.sum(fold8(p, jnp.sum), axis=0, keepdims=True)
            acc_ref[h] = alpha * acc_ref[h] + _dot(ct_t, p.astype(BF16))
            m_ref[h] = m_new
        return tie_seen + jnp.sum(fold8(eq_f, jnp.sum), axis=0, keepdims=True)

    lax.fori_loop(0, n_tiles, attn_body, jnp.zeros((1, QB), F32))

    for h in range(DSA_HEADS):
        o_lat = (acc_ref[h] / l_ref[h]).T.astype(BF16)
        o_ref[:, h * DSA_HEAD_DIM:(h + 1) * DSA_HEAD_DIM] = _dot(o_lat, wuv_ref[h]).astype(o_ref.dtype)


def dsa_attention(proj, q_abs, iq_pad, ckv, ckv_t, w_uv, batch, seq):
    T = proj.shape[0]
    nb = seq // DSA_BLOCK
    k_sel = min(DSA_TOPK_MAX, seq // 4)
    out_w = DSA_HEADS * DSA_HEAD_DIM
    return pl.pallas_call(
        functools.partial(_dsa_kernel, k_sel=k_sel),
        grid=(batch, nb),
        in_specs=[pl.BlockSpec((1, IDX_HEADS, DSA_BLOCK, LANES), lambda b, i: (b * nb + i, 0, 0, 0)),
                  pl.BlockSpec((DSA_BLOCK, LANES), lambda b, i: (b * nb + i, COL_IKW // LANES)),
                  pl.BlockSpec((seq, LANES), lambda b, i: (b, COL_IKW // LANES)),
                  pl.BlockSpec((seq, DSA_LATENT), lambda b, i: (b, 0)),
                  pl.BlockSpec((nb, DSA_LATENT, DSA_BLOCK), lambda b, i: (b, 0, 0)),
                  pl.BlockSpec((1, DSA_HEADS, DSA_BLOCK, DSA_LATENT), lambda b, i: (b * nb + i, 0, 0, 0)),
                  pl.BlockSpec((DSA_HEADS, DSA_LATENT, DSA_HEAD_DIM), lambda b, i: (0, 0, 0))],
        out_specs=pl.BlockSpec((DSA_BLOCK, out_w), lambda b, i: (b * nb + i, 0)),
        out_shape=jax.ShapeDtypeStruct((T, out_w), BF16),
        scratch_shapes=[pltpu.VMEM((nb, DSA_BLOCK, DSA_BLOCK), F32),
                        pltpu.VMEM((DSA_HEADS, 1, DSA_BLOCK), F32),
                        pltpu.VMEM((DSA_HEADS, 1, DSA_BLOCK), F32),
                        pltpu.VMEM((DSA_HEADS, DSA_LATENT, DSA_BLOCK), F32)],
        compiler_params=_params("parallel", "arbitrary"),
        name="dsa_attention",
    )(iq_pad, proj, proj, ckv, ckv_t, q_abs, w_uv)


def _merge_kernel(ret_ref, gm_ref, ds_ref, wr_ref, wg_ref, wd_ref, ga_ref, gb_ref, gc_ref, o_ref):
    def branch(a_ref, w_ref, gate_ref):
        return jax.nn.sigmoid(gate_ref[...].astype(F32)) * _dot(a_ref[...], w_ref[...])

    merged = branch(ret_ref, wr_ref, ga_ref) + branch(gm_ref, wg_ref, gb_ref) + branch(ds_ref, wd_ref, gc_ref)
    o_ref[...] = merged.astype(o_ref.dtype)


def merge_branches(ret, gm, ds, w_ret_o, w_gmlp_o, w_dsa_o, proj, d_model, *, tm=1024, tn=512):
    T, K = ret.shape
    gate_blk = COL_GATES // tn
    per_gate = d_model // tn
    act = pl.BlockSpec((tm, K), lambda i, j: (i, 0))
    wgt = pl.BlockSpec((K, tn), lambda i, j: (0, j))

    def gate(which):
        return pl.BlockSpec((tm, tn), lambda i, j: (i, gate_blk + which * per_gate + j))

    return pl.pallas_call(
        _merge_kernel,
        grid=(T // tm, d_model // tn),
        in_specs=[act, act, act, wgt, wgt, wgt, gate(0), gate(1), gate(2)],
        out_specs=pl.BlockSpec((tm, tn), lambda i, j: (i, j)),
        out_shape=jax.ShapeDtypeStruct((T, d_model), BF16),
        compiler_params=_params("parallel", "arbitrary"),
        name="merge_branches",
    )(ret, gm, ds, w_ret_o, w_gmlp_o, w_dsa_o, proj, proj, proj)


def _out_proj_kernel(a_ref, w_ref, x_ref, o_ref):
    o_ref[...] = x_ref[...] + _dot(a_ref[...], w_ref[...])


def out_proj_residual(merged, w_out, x, *, tm=1024, tn=512):
    T, K = merged.shape
    N = w_out.shape[1]
    return pl.pallas_call(
        _out_proj_kernel,
        grid=(T // tm, N // tn),
        in_specs=[pl.BlockSpec((tm, K), lambda i, j: (i, 0)),
                  pl.BlockSpec((K, tn), lambda i, j: (0, j)),
                  pl.BlockSpec((tm, tn), lambda i, j: (i, j))],
        out_specs=pl.BlockSpec((tm, tn), lambda i, j: (i, j)),
        out_shape=jax.ShapeDtypeStruct((T, N), F32),
        compiler_params=_params("parallel", "arbitrary"),
        name="out_proj_residual",
    )(merged, w_out, x)


def _pack_rows(h_bf16, o_ref):
    n, d = h_bf16.shape
    bits = lax.bitcast_convert_type(h_bf16.astype(F32), jnp.uint32)
    packed = (bits[:, :d // 2] >> 16) | (bits[:, d // 2:] & jnp.uint32(0xFFFF0000))
    for c in range(d // 2 // LANES):
        o_ref[pl.ds(c, n, stride=ROW_TILE), :] = packed[:, c * LANES:(c + 1) * LANES]


def _unpack_rows(x_ref, o_ref):
    n, d = o_ref.shape
    for c in range(d // 2 // LANES):
        w = x_ref[pl.ds(c, n, stride=ROW_TILE), :]
        lo = lax.bitcast_convert_type(w << 16, F32)
        hi = lax.bitcast_convert_type(w & jnp.uint32(0xFFFF0000), F32)
        o_ref[:, c * LANES:(c + 1) * LANES] = lo.astype(o_ref.dtype)
        o_ref[:, d // 2 + c * LANES:d // 2 + (c + 1) * LANES] = hi.astype(o_ref.dtype)


def _router_kernel(x_ref, g_ref, w_ref, b_ref, hp_ref, ids_ref, wts_ref):
    h = _rms(x_ref[...], g_ref[...]).astype(BF16)
    _pack_rows(h, hp_ref)
    logits = _dot(h, w_ref[...]) + b_ref[...]
    lane = lax.broadcasted_iota(jnp.int32, logits.shape, 1)
    big = jnp.int32(LANES)

    def first_argmax(vals, valid):
        v = jnp.where(valid, vals, -jnp.inf)
        m = jnp.max(v, axis=-1, keepdims=True)
        idx = jnp.min(jnp.where(valid & (v == m), lane, big), axis=-1, keepdims=True)
        return m, idx

    is_grp = lane < MOE_GROUPS
    g_max, g_sel = first_argmax(logits, is_grp)
    g_den = jnp.sum(jnp.where(is_grp, jnp.exp(logits - g_max), 0.0), axis=-1, keepdims=True)
    p_g = 1.0 / g_den
    e_lo = MOE_GROUPS + g_sel * MOE_EXPERTS_PER_GROUP
    in_grp = (lane >= e_lo) & (lane < e_lo + MOE_EXPERTS_PER_GROUP)
    m1, i1 = first_argmax(logits, in_grp)
    m2, i2 = first_argmax(logits, in_grp & (lane != i1))
    e2 = jnp.exp(m2 - m1)
    w1 = p_g / (1.0 + e2)
    w2 = p_g * e2 / (1.0 + e2)
    ids_ref[...] = jnp.where(lane == 0, i1 - MOE_GROUPS, jnp.where(lane == 1, i2 - MOE_GROUPS, 0))
    wts_ref[...] = jnp.where(lane == 0, w1, jnp.where(lane == 1, w2, 0.0))


def moe_router(x, g, w_router, b_router, *, tm=512):
    T, D = x.shape
    assert D == 2 * ROW_TILE * LANES
    return pl.pallas_call(
        _router_kernel,
        grid=(T // tm,),
        in_specs=[pl.BlockSpec((tm, D), lambda i: (i, 0)),
                  pl.BlockSpec((1, D), lambda i: (0, 0)),
                  pl.BlockSpec((D, LANES), lambda i: (0, 0)),
                  pl.BlockSpec((1, LANES), lambda i: (0, 0))],
        out_specs=[pl.BlockSpec((tm * ROW_TILE, LANES), lambda i: (i, 0)),
                   pl.BlockSpec((tm, LANES), lambda i: (i, 0)),
                   pl.BlockSpec((tm, LANES), lambda i: (i, 0))],
        out_shape=[jax.ShapeDtypeStruct((T * ROW_TILE, LANES), jnp.uint32),
                   jax.ShapeDtypeStruct((T, LANES), jnp.int32),
                   jax.ShapeDtypeStruct((T, LANES), F32)],
        compiler_params=_params("parallel"),
        name="moe_router",
    )(x, g, w_router, b_router)


def _route_tables(ids, n_rows):
    T = ids.shape[0]
    e = ids[:, :MOE_TOPK].reshape(-1)
    onehot = (e[:, None] == jnp.arange(MOE_EXPERTS, dtype=jnp.int32)[None, :]).astype(jnp.int32)
    rank = jnp.sum((jnp.cumsum(onehot, axis=0) - onehot) * onehot, axis=1)
    counts = jnp.sum(onehot, axis=0)
    padded = (counts + MOE_TILE - 1) // MOE_TILE * MOE_TILE
    ends = jnp.cumsum(padded)
    pos = ((ends - padded)[e] + rank).astype(jnp.int32)
    tile_start = jnp.arange(n_rows // MOE_TILE, dtype=jnp.int32) * MOE_TILE
    tile_expert = jnp.sum((ends[None, :] <= tile_start[:, None]).astype(jnp.int32), axis=1)
    tile_expert = jnp.minimum(tile_expert, MOE_EXPERTS - 1)
    n_used = (ends[-1] // MOE_TILE).astype(jnp.int32).reshape(1)
    return pos, tile_expert, n_used


def _token_rows(r, rows_per_token):
    return pl.ds(pl.multiple_of(r * rows_per_token, rows_per_token), rows_per_token)


def _scatter_rows_kernel(pos_ref, hp_ref, xs_in_hbm, xs_hbm, sem):
    del xs_in_hbm
    n = hp_ref.shape[0] // ROW_TILE
    base = pl.program_id(0) * n
    for slot in range(MOE_TOPK):
        def issue(r, carry, slot=slot):
            row = pos_ref[(base + r) * MOE_TOPK + slot]
            pltpu.make_async_copy(hp_ref.at[_token_rows(r, ROW_TILE)],
                                  xs_hbm.at[_token_rows(row, ROW_TILE)], sem.at[slot]).start()
            return carry
        lax.fori_loop(0, n, issue, 0)
    for slot in range(MOE_TOPK):
        pltpu.make_async_copy(hp_ref, xs_hbm.at[pl.ds(0, n * ROW_TILE)], sem.at[slot]).wait()


def scatter_rows(hp, pos, n_rows, *, tile=256):
    T = hp.shape[0] // ROW_TILE
    return pl.pallas_call(
        _scatter_rows_kernel,
        grid_spec=pltpu.PrefetchScalarGridSpec(
            num_scalar_prefetch=1,
            grid=(T // tile,),
            in_specs=[pl.BlockSpec((tile * ROW_TILE, LANES), lambda i, pos: (i, 0)),
                      pl.BlockSpec(memory_space=pl.ANY)],
            out_specs=pl.BlockSpec(memory_space=pl.ANY),
            scratch_shapes=[pltpu.SemaphoreType.DMA((MOE_TOPK,))]),
        out_shape=jax.ShapeDtypeStruct((n_rows * ROW_TILE, LANES), jnp.uint32),
        input_output_aliases={2: 0},
        compiler_params=_params("arbitrary"),
        name="moe_scatter_rows",
    )(pos, hp, jnp.zeros((n_rows * ROW_TILE, LANES), jnp.uint32))


def _moe_experts_kernel(tile_expert_ref, n_used_ref, x_ref, w1_ref, w3_ref, w2_ref, o_ref,
                        xb_ref, w1b_ref, w3b_ref, w2b_ref):
    i = pl.program_id(0)
    n, d = xb_ref.shape
    out_rows = d // LANES

    @pl.when((i == 0) | (tile_expert_ref[i] != tile_expert_ref[jnp.maximum(i - 1, 0)]))
    def _():
        w1b_ref[...] = w1_ref[0].astype(BF16)
        w3b_ref[...] = w3_ref[0].astype(BF16)
        w2b_ref[...] = w2_ref[0].astype(BF16)

    @pl.when(i < n_used_ref[0])
    def _():
        _unpack_rows(x_ref, xb_ref)
        x = xb_ref[...]
        a = _dot(x, w1b_ref[...])
        hid = a * jax.nn.sigmoid(a) * _dot(x, w3b_ref[...])
        y = _dot(hid.astype(BF16), w2b_ref[...])
        for c in range(out_rows):
            o_ref[pl.ds(c, n, stride=out_rows), :] = y[:, c * LANES:(c + 1) * LANES]

    @pl.when(i >= n_used_ref[0])
    def _():
        o_ref[...] = jnp.zeros_like(o_ref)


def moe_experts(xs, tile_expert, n_used, w1, w3, w2):
    E, D, F = w1.shape
    P = xs.shape[0] // ROW_TILE
    out_rows = D // LANES

    def wspec(shape):
        return pl.BlockSpec((1,) + shape, lambda i, te, nu: (te[i], 0, 0))

    return pl.pallas_call(
        _moe_experts_kernel,
        grid_spec=pltpu.PrefetchScalarGridSpec(
            num_scalar_prefetch=2,
            grid=(P // MOE_TILE,),
            in_specs=[pl.BlockSpec((MOE_TILE * ROW_TILE, LANES), lambda i, te, nu: (i, 0)),
                      wspec((D, F)), wspec((D, F)), wspec((F, D))],
            out_specs=pl.BlockSpec((MOE_TILE * out_rows, LANES), lambda i, te, nu: (i, 0)),
            scratch_shapes=[pltpu.VMEM((MOE_TILE, D), BF16), pltpu.VMEM((D, F), BF16),
                            pltpu.VMEM((D, F), BF16), pltpu.VMEM((F, D), BF16)]),
        out_shape=jax.ShapeDtypeStruct((P * out_rows, LANES), F32),
        compiler_params=_params("arbitrary"),
        name="moe_experts",
    )(tile_expert, n_used, xs, w1, w3, w2)


def _moe_combine_kernel(pos_ref, ys_hbm, x_ref, wts_ref, g_ref, o_ref, y0_ref, y1_ref, sem, *,
                        final_norm):
    n, d = x_ref.shape
    rows = d // LANES
    base = pl.program_id(0) * n
    for slot, y_ref in enumerate((y0_ref, y1_ref)):
        def issue(r, carry, slot=slot, y_ref=y_ref):
            row = pos_ref[(base + r) * MOE_TOPK + slot]
            pltpu.make_async_copy(ys_hbm.at[_token_rows(row, rows)], y_ref.at[_token_rows(r, rows)],
                                  sem.at[slot]).start()
            return carry
        lax.fori_loop(0, n, issue, 0)
    for slot, y_ref in enumerate((y0_ref, y1_ref)):
        pltpu.make_async_copy(ys_hbm.at[pl.ds(0, n * rows)], y_ref, sem.at[slot]).wait()
    w0 = wts_ref[:, 0:1]
    w1 = wts_ref[:, 1:2]
    for c in range(rows):
        cols = slice(c * LANES, (c + 1) * LANES)
        o_ref[:, cols] = x_ref[:, cols] + (w0 * y0_ref[pl.ds(c, n, stride=rows), :]
                                           + w1 * y1_ref[pl.ds(c, n, stride=rows), :])
    if final_norm:
        o_ref[...] = _rms(o_ref[...], g_ref[...])


def moe_combine(ys, pos, wts, x, g, *, final_norm, tile=256):
    T, D = x.shape
    rows = D // LANES
    return pl.pallas_call(
        functools.partial(_moe_combine_kernel, final_norm=final_norm),
        grid_spec=pltpu.PrefetchScalarGridSpec(
            num_scalar_prefetch=1,
            grid=(T // tile,),
            in_specs=[pl.BlockSpec(memory_space=pl.ANY),
                      pl.BlockSpec((tile, D), lambda i, pos: (i, 0)),
                      pl.BlockSpec((tile, LANES), lambda i, pos: (i, 0)),
                      pl.BlockSpec((1, D), lambda i, pos: (0, 0))],
            out_specs=pl.BlockSpec((tile, D), lambda i, pos: (i, 0)),
            scratch_shapes=[pltpu.VMEM((tile * rows, LANES), F32), pltpu.VMEM((tile * rows, LANES), F32),
                            pltpu.SemaphoreType.DMA((MOE_TOPK,))]),
        out_shape=jax.ShapeDtypeStruct((T, D), F32),
        compiler_params=_params("arbitrary"),
        name="moe_combine",
    )(pos, ys, x, wts, g)


def _pack_w_in(w):
    D = w.shape[0]
    widths = (1024, 1024, 1024, 1024, 1024, 1024, 1024, 256, 512, 64, 8, D, D, D)
    offs = [0]
    for wd in widths:
        offs.append(offs[-1] + wd)
    rq, rk, rv, rg, gu, gv, dq, dc, iq, ik, iw, ga, gb, gc = (
        w[:, offs[i]:offs[i + 1]] for i in range(len(widths)))
    ikw_pad = jnp.zeros((D, COL_GATES - COL_IKW - IDX_DIM - IDX_HEADS), w.dtype)
    packed = jnp.concatenate([rq, rk, rv, rg, gu, gv, dq, iq, dc, ik, iw, ikw_pad, ga, gb, gc], axis=1)
    assert packed.shape[1] == PROJ_WIDTH
    return packed.astype(BF16)


def _pack_router(w_group, b_group, w_expert, b_expert):
    D = w_group.shape[0]
    w_e = jnp.transpose(w_expert, (1, 0, 2)).reshape(D, MOE_EXPERTS)
    pad = LANES - MOE_GROUPS - MOE_EXPERTS
    w = jnp.concatenate([w_group, w_e, jnp.zeros((D, pad), w_group.dtype)], axis=1).astype(BF16)
    b = jnp.concatenate([b_group, b_expert.reshape(MOE_EXPERTS), jnp.zeros((pad,), b_group.dtype)])
    return w, b.reshape(1, LANES).astype(F32)


def kernel(x, norm_mix_g, w_in, w_ret_o, gmlp_ln_g, gmlp_ln_b, gmlp_w_s, gmlp_b_s, w_gmlp_o, dsa_kv_norm_g, dsa_w_uk, dsa_w_uv, w_dsa_o, w_out, norm_ffn_g, moe_w_group, moe_b_group, moe_w_expert, moe_b_expert, moe_w1, moe_w3, moe_w2, final_norm_g):
    B, S, D = x.shape
    depth = w_in.shape[0]
    assert S % DSA_BLOCK == 0 and D == 2048
    xt = x.reshape(B * S, D)
    for l in range(depth):
        proj = norm_matmul(xt, norm_mix_g[l].reshape(1, D), _pack_w_in(w_in[l]))
        ret = retention(proj, B, S)
        gm = gmlp(proj, gmlp_ln_g[l], gmlp_ln_b[l], gmlp_w_s[l], gmlp_b_s[l])
        q_abs, iq_pad, ckv, ckv_t = dsa_prep(proj, dsa_kv_norm_g[l], dsa_w_uk[l].astype(BF16))
        ds = dsa_attention(proj, q_abs, iq_pad, ckv, ckv_t, dsa_w_uv[l].astype(BF16), B, S)
        merged = merge_branches(ret, gm, ds, w_ret_o[l].astype(BF16), w_gmlp_o[l].astype(BF16),
                                w_dsa_o[l].astype(BF16), proj, D)
        xt = out_proj_residual(merged, w_out[l].astype(BF16), xt)
        w_router, b_router = _pack_router(moe_w_group[l], moe_b_group[l], moe_w_expert[l], moe_b_expert[l])
        hp, ids, wts = moe_router(xt, norm_ffn_g[l].reshape(1, D), w_router, b_router)
        n_rows = MOE_TOPK * B * S + MOE_EXPERTS * MOE_TILE
        pos, tile_expert, n_used = _route_tables(ids, n_rows)
        xs = scatter_rows(hp, pos, n_rows)
        F = moe_w1.shape[-1]
        ys = moe_experts(xs, tile_expert, n_used,
                         moe_w1[l].reshape(MOE_EXPERTS, D, F),
                         moe_w3[l].reshape(MOE_EXPERTS, D, F),
                         moe_w2[l].reshape(MOE_EXPERTS, F, D))
        last = l == depth - 1
        xt = moe_combine(ys, pos, wts, xt, final_norm_g.reshape(1, D), final_norm=last)
    return xt.reshape(B, S, D)
```

```python
import functools
import math

import jax
import jax.numpy as jnp
from jax import lax
from jax.experimental import pallas as pl
from jax.experimental.pallas import tpu as pltpu

F32 = jnp.float32
BF16 = jnp.bfloat16

RMS_EPS = 1e-6
RET_HEADS = 4
RET_DIM = 256
CHUNK = 128
ROPE_BASE = 10000.0
GMLP_GROUPS = 4
GMLP_GROUP_DIM = 256
DSA_HEADS = 8
DSA_HEAD_DIM = 128
DSA_LATENT = 256
IDX_HEADS = 8
IDX_DIM = 64
DSA_TOPK_MAX = 256
DSA_BLOCK = 256
MOE_GROUPS = 4
MOE_EXPERTS_PER_GROUP = 8
MOE_EXPERTS = MOE_GROUPS * MOE_EXPERTS_PER_GROUP
MOE_HIDDEN = 512
MOE_TOPK = 2
MOE_TILE = 256
LANES = 128
ROW_TILE = 8
DMA_ISSUE_UNROLL = 8
NEG_BIG = -1e30

COL_RQ, COL_RK, COL_RV, COL_RG = 0, 1024, 2048, 3072
COL_GU, COL_GV = 4096, 5120
COL_DQ, COL_IQ, COL_DC, COL_IKW = 6144, 7168, 7680, 7936
COL_GATES = 8192
PROJ_WIDTH = 14336
VMEM_LIMIT = 56 * 1024 * 1024


def _params(*sem):
    return pltpu.CompilerParams(dimension_semantics=sem, vmem_limit_bytes=VMEM_LIMIT)


def _rms(x, g):
    return x * lax.rsqrt(jnp.mean(x * x, axis=-1, keepdims=True) + RMS_EPS) * g


def _dot(a, b):
    return jnp.dot(a, b, preferred_element_type=F32)


def _dot_nt(a, b):
    return lax.dot_general(a, b, (((1,), (1,)), ((), ())), preferred_element_type=F32)


def _norm_matmul_kernel(x_ref, g_ref, w_ref, o_ref, h_ref):
    @pl.when(pl.program_id(1) == 0)
    def _():
        h_ref[...] = _rms(x_ref[...], g_ref[...]).astype(BF16)

    o_ref[...] = _dot(h_ref[...], w_ref[...]).astype(o_ref.dtype)


def norm_matmul(x, g, w, *, tm=1024, tn=512, out_dtype=BF16):
    T, D = x.shape
    N = w.shape[1]
    return pl.pallas_call(
        _norm_matmul_kernel,
        grid=(T // tm, N // tn),
        in_specs=[pl.BlockSpec((tm, D), lambda i, j: (i, 0)),
                  pl.BlockSpec((1, D), lambda i, j: (0, 0)),
                  pl.BlockSpec((D, tn), lambda i, j: (0, j))],
        out_specs=pl.BlockSpec((tm, tn), lambda i, j: (i, j)),
        out_shape=jax.ShapeDtypeStruct((T, N), out_dtype),
        scratch_shapes=[pltpu.VMEM((tm, D), BF16)],
        compiler_params=_params("parallel", "arbitrary"),
        name="norm_in_proj",
    )(x, g, w)


def _retention_kernel(q_ref, k_ref, v_ref, gate_ref, cos_ref, sin_ref, decay_ref, xi_ref,
                      zeta_ref, gchunk_ref, o_ref, r_ref):
    @pl.when(pl.program_id(1) == 0)
    def _():
        r_ref[...] = jnp.zeros_like(r_ref)

    cos = cos_ref[...]
    sin = sin_ref[...]
    half = RET_DIM // 2

    def rotary(t):
        t1, t2 = t[:, :half], t[:, half:]
        return jnp.concatenate([t1 * cos - t2 * sin, t1 * sin + t2 * cos], axis=-1)

    for h in range(RET_HEADS):
        cols = slice(h * RET_DIM, (h + 1) * RET_DIM)
        q = rotary(q_ref[:, cols].astype(F32)).astype(BF16)
        k = rotary(k_ref[:, cols].astype(F32)) * (RET_DIM ** -0.5)
        v = v_ref[:, cols]
        inner = _dot_nt(q, k.astype(BF16)) * decay_ref[h]
        r_old = r_ref[h]
        o = _dot(inner.astype(BF16), v) + _dot(q, r_old.astype(BF16)) * xi_ref[:, h:h + 1]
        kz = (k * zeta_ref[:, h:h + 1]).astype(BF16)
        r_ref[h] = r_old * gchunk_ref[h] + _dot(kz.T, v)
        o = o * lax.rsqrt(jnp.mean(o * o, axis=-1, keepdims=True) + RMS_EPS)
        gate = gate_ref[:, cols].astype(F32)
        o_ref[:, cols] = (gate * jax.nn.sigmoid(gate) * o).astype(o_ref.dtype)


def retention(proj, batch, seq):
    T = proj.shape[0]
    n_chunks = seq // CHUNK
    width = RET_HEADS * RET_DIM
    half = RET_DIM // 2
    inv = ROPE_BASE ** (-jnp.arange(half, dtype=F32) / half)
    ang = jnp.arange(seq).astype(F32)[:, None] * inv[None, :]
    cos, sin = jnp.cos(ang), jnp.sin(ang)
    log_g = jnp.log(1.0 - 2.0 ** (-5.0 - jnp.arange(RET_HEADS, dtype=F32)))
    i = jnp.arange(CHUNK, dtype=F32)
    diff = i[:, None] - i[None, :]
    decay = jnp.where(diff >= 0, jnp.exp(log_g[:, None, None] * jnp.maximum(diff, 0.0)), 0.0)
    xi = jnp.exp(log_g[None, :] * (i[:, None] + 1.0))
    zeta = jnp.exp(log_g[None, :] * (CHUNK - 1.0 - i[:, None]))
    gchunk = jnp.exp(log_g * CHUNK)

    def col_spec(col):
        return pl.BlockSpec((CHUNK, width), lambda b, n: (b * n_chunks + n, col // width))

    return pl.pallas_call(
        _retention_kernel,
        grid=(batch, n_chunks),
        in_specs=[col_spec(COL_RQ), col_spec(COL_RK), col_spec(COL_RV), col_spec(COL_RG),
                  pl.BlockSpec((CHUNK, half), lambda b, n: (n, 0)),
                  pl.BlockSpec((CHUNK, half), lambda b, n: (n, 0)),
                  pl.BlockSpec((RET_HEADS, CHUNK, CHUNK), lambda b, n: (0, 0, 0)),
                  pl.BlockSpec((CHUNK, RET_HEADS), lambda b, n: (0, 0)),
                  pl.BlockSpec((CHUNK, RET_HEADS), lambda b, n: (0, 0)),
                  pl.BlockSpec(memory_space=pltpu.SMEM)],
        out_specs=pl.BlockSpec((CHUNK, width), lambda b, n: (b * n_chunks + n, 0)),
        out_shape=jax.ShapeDtypeStruct((T, width), BF16),
        scratch_shapes=[pltpu.VMEM((RET_HEADS, RET_DIM, RET_DIM), F32)],
        compiler_params=_params("parallel", "arbitrary"),
        name="retention",
    )(proj, proj, proj, proj, cos, sin, decay, xi, zeta, gchunk)


def _gelu(x):
    return 0.5 * x * (1.0 + lax.erf(x * (2.0 ** -0.5)))


def _gmlp_kernel(u_ref, v_ref, lng_ref, lnb_ref, ws_ref, bs_ref, o_ref):
    v = _gelu(v_ref[...].astype(F32))
    mu = jnp.mean(v, axis=-1, keepdims=True)
    var = jnp.mean(jnp.square(v - mu), axis=-1, keepdims=True)
    vn = ((v - mu) * lax.rsqrt(var + RMS_EPS) * lng_ref[...] + lnb_ref[...]).astype(BF16)
    row = lax.broadcasted_iota(jnp.int32, (CHUNK, CHUNK), 0)
    col = lax.broadcasted_iota(jnp.int32, (CHUNK, CHUNK), 1)
    for g in range(GMLP_GROUPS):
        cols = slice(g * GMLP_GROUP_DIM, (g + 1) * GMLP_GROUP_DIM)
        w = jnp.where(row >= col, ws_ref[g], 0.0).astype(BF16)
        mixed = _dot(w, vn[:, cols]) + bs_ref[:, g:g + 1]
        u = _gelu(u_ref[:, cols].astype(F32))
        o_ref[:, cols] = (u * mixed).astype(o_ref.dtype)


def gmlp(proj, ln_g, ln_b, w_s, b_s):
    T = proj.shape[0]
    width = GMLP_GROUPS * GMLP_GROUP_DIM
    return pl.pallas_call(
        _gmlp_kernel,
        grid=(T // CHUNK,),
        in_specs=[pl.BlockSpec((CHUNK, width), lambda i: (i, COL_GU // width)),
                  pl.BlockSpec((CHUNK, width), lambda i: (i, COL_GV // width)),
                  pl.BlockSpec((1, width), lambda i: (0, 0)),
                  pl.BlockSpec((1, width), lambda i: (0, 0)),
                  pl.BlockSpec((GMLP_GROUPS, CHUNK, CHUNK), lambda i: (0, 0, 0)),
                  pl.BlockSpec((CHUNK, GMLP_GROUPS), lambda i: (0, 0))],
        out_specs=pl.BlockSpec((CHUNK, width), lambda i: (i, 0)),
        out_shape=jax.ShapeDtypeStruct((T, width), BF16),
        compiler_params=_params("parallel"),
        name="gmlp",
    )(proj, proj, ln_g.reshape(1, width), ln_b.reshape(1, width), w_s, b_s.T)


def _dsa_prep_kernel(dq_ref, dc_ref, iq_ref, g_ref, wuk_ref, qabs_ref, iqp_ref, ckv_ref, ckvt_ref):
    c = _rms(dc_ref[...].astype(F32), g_ref[...])
    ckv_ref[...] = c.astype(ckv_ref.dtype)
    ckvt_ref[0] = c.T.astype(ckvt_ref.dtype)
    zeros = jnp.zeros((DSA_BLOCK, LANES - IDX_DIM), iqp_ref.dtype)
    for h in range(DSA_HEADS):
        q = dq_ref[:, h * DSA_HEAD_DIM:(h + 1) * DSA_HEAD_DIM]
        qabs_ref[0, h] = (_dot_nt(q, wuk_ref[h]) * (DSA_HEAD_DIM ** -0.5)).astype(qabs_ref.dtype)
    for h in range(IDX_HEADS):
        iqp_ref[0, h] = jnp.concatenate([iq_ref[:, h * IDX_DIM:(h + 1) * IDX_DIM], zeros], axis=-1)


def dsa_prep(proj, kv_norm_g, w_uk):
    T = proj.shape[0]
    nblk = T // DSA_BLOCK
    qw = DSA_HEADS * DSA_HEAD_DIM
    iq_w = IDX_HEADS * IDX_DIM
    return pl.pallas_call(
        _dsa_prep_kernel,
        grid=(nblk,),
        in_specs=[pl.BlockSpec((DSA_BLOCK, qw), lambda i: (i, COL_DQ // qw)),
                  pl.BlockSpec((DSA_BLOCK, DSA_LATENT), lambda i: (i, COL_DC // DSA_LATENT)),
                  pl.BlockSpec((DSA_BLOCK, iq_w), lambda i: (i, COL_IQ // iq_w)),
                  pl.BlockSpec((1, DSA_LATENT), lambda i: (0, 0)),
                  pl.BlockSpec((DSA_HEADS, DSA_LATENT, DSA_HEAD_DIM), lambda i: (0, 0, 0))],
        out_specs=[pl.BlockSpec((1, DSA_HEADS, DSA_BLOCK, DSA_LATENT), lambda i: (i, 0, 0, 0)),
                   pl.BlockSpec((1, IDX_HEADS, DSA_BLOCK, LANES), lambda i: (i, 0, 0, 0)),
                   pl.BlockSpec((DSA_BLOCK, DSA_LATENT), lambda i: (i, 0)),
                   pl.BlockSpec((1, DSA_LATENT, DSA_BLOCK), lambda i: (i, 0, 0))],
        out_shape=[jax.ShapeDtypeStruct((nblk, DSA_HEADS, DSA_BLOCK, DSA_LATENT), BF16),
                   jax.ShapeDtypeStruct((nblk, IDX_HEADS, DSA_BLOCK, LANES), BF16),
                   jax.ShapeDtypeStruct((T, DSA_LATENT), BF16),
                   jax.ShapeDtypeStruct((nblk, DSA_LATENT, DSA_BLOCK), BF16)],
        compiler_params=_params("parallel"),
        name="dsa_prep",
    )(proj, proj, proj, kv_norm_g.reshape(1, DSA_LATENT), w_uk)


def _dsa_kernel(iqp_ref, wq_ref, kidx_ref, ckv_ref, ckvt_ref, qa_ref, wuv_ref, o_ref,
                score_ref, m_ref, l_ref, acc_ref, *, k_sel):
    QB = DSA_BLOCK
    blk = pl.program_id(1)
    n_tiles = blk + 1
    ksel_f = float(k_sel)
    key_in_tile = lax.broadcasted_iota(jnp.int32, (QB, QB), 0)
    query_in_blk = lax.broadcasted_iota(jnp.int32, (QB, QB), 1)

    def fold8(x, op):
        return op(x.reshape(QB // 8, 8, QB), axis=0)

    def tile_rows(j):
        return pl.ds(pl.multiple_of(j * QB, QB), QB)

    wi_t = wq_ref[...].astype(F32).T * (IDX_HEADS ** -0.5)

    def score_body(j, carry):
        kt = kidx_ref[tile_rows(j), :]
        sc = jnp.zeros((QB, QB), F32)
        for h in range(IDX_HEADS):
            lg = _dot_nt(kt, iqp_ref[0, h]) * (IDX_DIM ** -0.5)
            sc = sc + wi_t[IDX_DIM + h:IDX_DIM + h + 1, :] * jnp.maximum(lg, 0.0)
        score_ref[j] = jnp.where((j < blk) | (key_in_tile <= query_in_blk), sc, -jnp.inf)
        return carry

    lax.fori_loop(0, n_tiles, score_body, 0)

    def reduce_tiles(fn, op, fold, init):
        def body(j, acc):
            return op(acc, fold8(fn(score_ref[j]), fold))
        acc = lax.fori_loop(0, n_tiles, body, jnp.full((8, QB), init, F32))
        return fold(acc, axis=0, keepdims=True)

    def count(pred):
        return reduce_tiles(lambda s: pred(s).astype(F32), jnp.add, jnp.sum, 0.0)

    n_causal = (blk * QB + 1 + lax.broadcasted_iota(jnp.int32, (1, QB), 1)).astype(F32)
    need = n_causal > ksel_f
    hi0 = reduce_tiles(lambda s: s, jnp.maximum, jnp.max, -jnp.inf)
    lo0 = reduce_tiles(lambda s: jnp.where(s > -jnp.inf, s, jnp.inf), jnp.minimum, jnp.min, jnp.inf)
    cnt_hi0 = count(lambda s: s >= hi0)
    cnt_ge0 = count(lambda s: s >= 0.0)
    cnt_gt0 = count(lambda s: s > 0.0)
    above0 = cnt_gt0 >= ksel_f
    below0 = cnt_ge0 < ksel_f
    lo0 = jnp.where(below0, lo0, 0.0)
    cnt_lo0 = jnp.where(below0, n_causal, cnt_ge0)
    hi0 = jnp.where(above0, hi0, 0.0)
    at_top = above0 & (cnt_hi0 >= ksel_f)
    lo0 = jnp.where(at_top, hi0, lo0)
    cnt_lo0 = jnp.where(at_top, cnt_hi0, cnt_lo0)
    active0 = need & (above0 | below0) & (cnt_lo0 != ksel_f) & (lo0 < hi0)

    def bisect_cond(state):
        return jnp.max(state[3]) > 0.0

    def bisect_body(state):
        lo, hi, cnt_lo, active = state
        mid = lo + 0.5 * (hi - lo)
        c = count(lambda s: s >= mid)
        ge = c >= ksel_f
        live = active > 0.0
        moved = (mid > lo) & (mid < hi)
        new_lo = jnp.where(live & ge, mid, lo)
        new_hi = jnp.where(live & ~ge, mid, hi)
        new_cnt = jnp.where(live & ge, c, cnt_lo)
        new_active = live & moved & (new_cnt != ksel_f)
        return new_lo, new_hi, new_cnt, new_active.astype(F32)

    lo = lax.while_loop(bisect_cond, bisect_body, (lo0, hi0, cnt_lo0, active0.astype(F32)))[0]
    thr = jnp.where(need, lo, -jnp.inf)
    n_gt = count(lambda s: s > thr)
    tie_quota = jnp.where(need, ksel_f - n_gt, 0.0)

    m_ref[...] = jnp.full_like(m_ref, NEG_BIG)
    l_ref[...] = jnp.zeros_like(l_ref)
    acc_ref[...] = jnp.zeros_like(acc_ref)
    strict_lower = (query_in_blk < key_in_tile).astype(BF16)

    def attn_body(j, tie_seen):
        sc = score_ref[j]
        eq = sc == thr
        eq_f = eq.astype(F32)
        prefix = _dot(strict_lower, eq_f.astype(BF16)) + tie_seen
        sel = (sc > thr) | (eq & (prefix < tie_quota))
        bias = jnp.where(sel, 0.0, NEG_BIG)
        c_t = ckv_ref[tile_rows(j), :]
        ct_t = ckvt_ref[j]
        for h in range(DSA_HEADS):
            s = _dot_nt(c_t, qa_ref[0, h]) + bias
            m_old = m_ref[h]
            m_new = jnp.maximum(m_old, jnp.max(fold8(s, jnp.max), axis=0, keepdims=True))
            alpha = jnp.exp(m_old - m_new)
            p = jnp.exp(s - m_new)
            l_ref[h] = alpha * l_ref[h] + jnp.sum(fold8(p, jnp.sum), axis=0, keepdims=True)
            acc_ref[h] = alpha * acc_ref[h] + _dot(ct_t, p.astype(BF16))
            m_ref[h] = m_new
        return tie_seen + jnp.sum(fold8(eq_f, jnp.sum), axis=0, keepdims=True)

    lax.fori_loop(0, n_tiles, attn_body, jnp.zeros((1, QB), F32))

    for h in range(DSA_HEADS):
        o_lat = (acc_ref[h] / l_ref[h]).T.astype(BF16)
        o_ref[:, h * DSA_HEAD_DIM:(h + 1) * DSA_HEAD_DIM] = _dot(o_lat, wuv_ref[h]).astype(o_ref.dtype)


def dsa_attention(proj, q_abs, iq_pad, ckv, ckv_t, w_uv, batch, seq):
    T = proj.shape[0]
    nb = seq // DSA_BLOCK
    k_sel = min(DSA_TOPK_MAX, seq // 4)
    out_w = DSA_HEADS * DSA_HEAD_DIM
    return pl.pallas_call(
        functools.partial(_dsa_kernel, k_sel=k_sel),
        grid=(batch, nb),
        in_specs=[pl.BlockSpec((1, IDX_HEADS, DSA_BLOCK, LANES), lambda b, i: (b * nb + i, 0, 0, 0)),
                  pl.BlockSpec((DSA_BLOCK, LANES), lambda b, i: (b * nb + i, COL_IKW // LANES)),
                  pl.BlockSpec((seq, LANES), lambda b, i: (b, COL_IKW // LANES)),
                  pl.BlockSpec((seq, DSA_LATENT), lambda b, i: (b, 0)),
                  pl.BlockSpec((nb, DSA_LATENT, DSA_BLOCK), lambda b, i: (b, 0, 0)),
                  pl.BlockSpec((1, DSA_HEADS, DSA_BLOCK, DSA_LATENT), lambda b, i: (b * nb + i, 0, 0, 0)),
                  pl.BlockSpec((DSA_HEADS, DSA_LATENT, DSA_HEAD_DIM), lambda b, i: (0, 0, 0))],
        out_specs=pl.BlockSpec((DSA_BLOCK, out_w), lambda b, i: (b * nb + i, 0)),
        out_shape=jax.ShapeDtypeStruct((T, out_w), BF16),
        scratch_shapes=[pltpu.VMEM((nb, DSA_BLOCK, DSA_BLOCK), F32),
                        pltpu.VMEM((DSA_HEADS, 1, DSA_BLOCK), F32),
                        pltpu.VMEM((DSA_HEADS, 1, DSA_BLOCK), F32),
                        pltpu.VMEM((DSA_HEADS, DSA_LATENT, DSA_BLOCK), F32)],
        compiler_params=_params("parallel", "arbitrary"),
        name="dsa_attention",
    )(iq_pad, proj, proj, ckv, ckv_t, q_abs, w_uv)


def _merge_kernel(ret_ref, gm_ref, ds_ref, wr_ref, wg_ref, wd_ref, ga_ref, gb_ref, gc_ref, o_ref):
    def branch(a_ref, w_ref, gate_ref):
        return jax.nn.sigmoid(gate_ref[...].astype(F32)) * _dot(a_ref[...], w_ref[...])

    merged = branch(ret_ref, wr_ref, ga_ref) + branch(gm_ref, wg_ref, gb_ref) + branch(ds_ref, wd_ref, gc_ref)
    o_ref[...] = merged.astype(o_ref.dtype)


def merge_branches(ret, gm, ds, w_ret_o, w_gmlp_o, w_dsa_o, proj, d_model, *, tm=1024, tn=512):
    T, K = ret.shape
    gate_blk = COL_GATES // tn
    per_gate = d_model // tn
    act = pl.BlockSpec((tm, K), lambda i, j: (i, 0))
    wgt = pl.BlockSpec((K, tn), lambda i, j: (0, j))

    def gate(which):
        return pl.BlockSpec((tm, tn), lambda i, j: (i, gate_blk + which * per_gate + j))

    return pl.pallas_call(
        _merge_kernel,
        grid=(T // tm, d_model // tn),
        in_specs=[act, act, act, wgt, wgt, wgt, gate(0), gate(1), gate(2)],
        out_specs=pl.BlockSpec((tm, tn), lambda i, j: (i, j)),
        out_shape=jax.ShapeDtypeStruct((T, d_model), BF16),
        compiler_params=_params("parallel", "arbitrary"),
        name="merge_branches",
    )(ret, gm, ds, w_ret_o, w_gmlp_o, w_dsa_o, proj, proj, proj)


def _out_proj_kernel(a_ref, w_ref, x_ref, o_ref):
    o_ref[...] = x_ref[...] + _dot(a_ref[...], w_ref[...])


def out_proj_residual(merged, w_out, x, *, tm=1024, tn=512):
    T, K = merged.shape
    N = w_out.shape[1]
    return pl.pallas_call(
        _out_proj_kernel,
        grid=(T // tm, N // tn),
        in_specs=[pl.BlockSpec((tm, K), lambda i, j: (i, 0)),
                  pl.BlockSpec((K, tn), lambda i, j: (0, j)),
                  pl.BlockSpec((tm, tn), lambda i, j: (i, j))],
        out_specs=pl.BlockSpec((tm, tn), lambda i, j: (i, j)),
        out_shape=jax.ShapeDtypeStruct((T, N), F32),
        compiler_params=_params("parallel", "arbitrary"),
        name="out_proj_residual",
    )(merged, w_out, x)


def _pack_rows(h_bf16, o_ref):
    n, d = h_bf16.shape
    bits = lax.bitcast_convert_type(h_bf16.astype(F32), jnp.uint32)
    packed = (bits[:, :d // 2] >> 16) | (bits[:, d // 2:] & jnp.uint32(0xFFFF0000))
    for c in range(d // 2 // LANES):
        o_ref[pl.ds(c, n, stride=ROW_TILE), :] = packed[:, c * LANES:(c + 1) * LANES]


def _unpack_rows(x_ref, o_ref):
    n, d = o_ref.shape
    for c in range(d // 2 // LANES):
        w = x_ref[pl.ds(c, n, stride=ROW_TILE), :]
        lo = lax.bitcast_convert_type(w << 16, F32)
        hi = lax.bitcast_convert_type(w & jnp.uint32(0xFFFF0000), F32)
        o_ref[:, c * LANES:(c + 1) * LANES] = lo.astype(o_ref.dtype)
        o_ref[:, d // 2 + c * LANES:d // 2 + (c + 1) * LANES] = hi.astype(o_ref.dtype)


def _router_kernel(x_ref, g_ref, w_ref, b_ref, hp_ref, ids_ref, wts_ref):
    h = _rms(x_ref[...], g_ref[...]).astype(BF16)
    _pack_rows(h, hp_ref)
    logits = _dot(h, w_ref[...]) + b_ref[...]
    lane = lax.broadcasted_iota(jnp.int32, logits.shape, 1)
    big = jnp.int32(LANES)

    def first_argmax(vals, valid):
        v = jnp.where(valid, vals, -jnp.inf)
        m = jnp.max(v, axis=-1, keepdims=True)
        idx = jnp.min(jnp.where(valid & (v == m), lane, big), axis=-1, keepdims=True)
        return m, idx

    is_grp = lane < MOE_GROUPS
    g_max, g_sel = first_argmax(logits, is_grp)
    g_den = jnp.sum(jnp.where(is_grp, jnp.exp(logits - g_max), 0.0), axis=-1, keepdims=True)
    p_g = 1.0 / g_den
    e_lo = MOE_GROUPS + g_sel * MOE_EXPERTS_PER_GROUP
    in_grp = (lane >= e_lo) & (lane < e_lo + MOE_EXPERTS_PER_GROUP)
    m1, i1 = first_argmax(logits, in_grp)
    m2, i2 = first_argmax(logits, in_grp & (lane != i1))
    e2 = jnp.exp(m2 - m1)
    w1 = p_g / (1.0 + e2)
    w2 = p_g * e2 / (1.0 + e2)
    ids_ref[...] = jnp.where(lane == 0, i1 - MOE_GROUPS, jnp.where(lane == 1, i2 - MOE_GROUPS, 0))
    wts_ref[...] = jnp.where(lane == 0, w1, jnp.where(lane == 1, w2, 0.0))


def moe_router(x, g, w_router, b_router, *, tm=512):
    T, D = x.shape
    assert D == 2 * ROW_TILE * LANES
    return pl.pallas_call(
        _router_kernel,
        grid=(T // tm,),
        in_specs=[pl.BlockSpec((tm, D), lambda i: (i, 0)),
                  pl.BlockSpec((1, D), lambda i: (0, 0)),
                  pl.BlockSpec((D, LANES), lambda i: (0, 0)),
                  pl.BlockSpec((1, LANES), lambda i: (0, 0))],
        out_specs=[pl.BlockSpec((tm * ROW_TILE, LANES), lambda i: (i, 0)),
                   pl.BlockSpec((tm, LANES), lambda i: (i, 0)),
                   pl.BlockSpec((tm, LANES), lambda i: (i, 0))],
        out_shape=[jax.ShapeDtypeStruct((T * ROW_TILE, LANES), jnp.uint32),
                   jax.ShapeDtypeStruct((T, LANES), jnp.int32),
                   jax.ShapeDtypeStruct((T, LANES), F32)],
        compiler_params=_params("parallel"),
        name="moe_router",
    )(x, g, w_router, b_router)


def _route_tables(ids, n_rows):
    T = ids.shape[0]
    e = ids[:, :MOE_TOPK].reshape(-1)
    onehot = (e[:, None] == jnp.arange(MOE_EXPERTS, dtype=jnp.int32)[None, :]).astype(jnp.int32)
    rank = jnp.sum((jnp.cumsum(onehot, axis=0) - onehot) * onehot, axis=1)
    counts = jnp.sum(onehot, axis=0)
    padded = (counts + MOE_TILE - 1) // MOE_TILE * MOE_TILE
    ends = jnp.cumsum(padded)
    pos = ((ends - padded)[e] + rank).astype(jnp.int32)
    tile_start = jnp.arange(n_rows // MOE_TILE, dtype=jnp.int32) * MOE_TILE
    tile_expert = jnp.sum((ends[None, :] <= tile_start[:, None]).astype(jnp.int32), axis=1)
    tile_expert = jnp.minimum(tile_expert, MOE_EXPERTS - 1)
    n_used = (ends[-1] // MOE_TILE).astype(jnp.int32).reshape(1)
    return pos, tile_expert, n_used


def _token_rows(r, rows_per_token):
    return pl.ds(pl.multiple_of(r * rows_per_token, rows_per_token), rows_per_token)


def _scatter_rows_kernel(pos_ref, hp_ref, xs_in_hbm, xs_hbm, sem):
    del xs_in_hbm
    n = hp_ref.shape[0] // ROW_TILE
    base = pl.program_id(0) * n
    for slot in range(MOE_TOPK):
        def issue(r, carry, slot=slot):
            row = pos_ref[(base + r) * MOE_TOPK + slot]
            pltpu.make_async_copy(hp_ref.at[_token_rows(r, ROW_TILE)],
                                  xs_hbm.at[_token_rows(row, ROW_TILE)], sem.at[slot]).start()
            return carry
        lax.fori_loop(0, n, issue, 0, unroll=DMA_ISSUE_UNROLL)
    for slot in range(MOE_TOPK):
        pltpu.make_async_copy(hp_ref, xs_hbm.at[pl.ds(0, n * ROW_TILE)], sem.at[slot]).wait()


def scatter_rows(hp, pos, n_rows, *, tile=256):
    T = hp.shape[0] // ROW_TILE
    return pl.pallas_call(
        _scatter_rows_kernel,
        grid_spec=pltpu.PrefetchScalarGridSpec(
            num_scalar_prefetch=1,
            grid=(T // tile,),
            in_specs=[pl.BlockSpec((tile * ROW_TILE, LANES), lambda i, pos: (i, 0)),
                      pl.BlockSpec(memory_space=pl.ANY)],
            out_specs=pl.BlockSpec(memory_space=pl.ANY),
            scratch_shapes=[pltpu.SemaphoreType.DMA((MOE_TOPK,))]),
        out_shape=jax.ShapeDtypeStruct((n_rows * ROW_TILE, LANES), jnp.uint32),
        input_output_aliases={2: 0},
        compiler_params=_params("arbitrary"),
        name="moe_scatter_rows",
    )(pos, hp, jnp.zeros((n_rows * ROW_TILE, LANES), jnp.uint32))


def _moe_experts_kernel(tile_expert_ref, n_used_ref, x_ref, w1_ref, w3_ref, w2_ref, o_ref,
                        xb_ref, w1b_ref, w3b_ref, w2b_ref):
    i = pl.program_id(0)

    @pl.when((i == 0) | (tile_expert_ref[i] != tile_expert_ref[jnp.maximum(i - 1, 0)]))
    def _():
        w1b_ref[...] = w1_ref[0].astype(BF16)
        w3b_ref[...] = w3_ref[0].astype(BF16)
        w2b_ref[...] = w2_ref[0].astype(BF16)

    @pl.when(i < n_used_ref[0])
    def _():
        _unpack_rows(x_ref, xb_ref)
        x = xb_ref[...]
        a = _dot(x, w1b_ref[...])
        hid = a * jax.nn.sigmoid(a) * _dot(x, w3b_ref[...])
        o_ref[...] = _dot(hid.astype(BF16), w2b_ref[...])

    @pl.when(i >= n_used_ref[0])
    def _():
        o_ref[...] = jnp.zeros_like(o_ref)


def moe_experts(xs, tile_expert, n_used, w1, w3, w2):
    E, D, F = w1.shape
    P = xs.shape[0] // ROW_TILE

    def wspec(shape):
        return pl.BlockSpec((1,) + shape, lambda i, te, nu: (te[i], 0, 0))

    return pl.pallas_call(
        _moe_experts_kernel,
        grid_spec=pltpu.PrefetchScalarGridSpec(
            num_scalar_prefetch=2,
            grid=(P // MOE_TILE,),
            in_specs=[pl.BlockSpec((MOE_TILE * ROW_TILE, LANES), lambda i, te, nu: (i, 0)),
                      wspec((D, F)), wspec((D, F)), wspec((F, D))],
            out_specs=pl.BlockSpec((MOE_TILE, D), lambda i, te, nu: (i, 0)),
            scratch_shapes=[pltpu.VMEM((MOE_TILE, D), BF16), pltpu.VMEM((D, F), BF16),
                            pltpu.VMEM((D, F), BF16), pltpu.VMEM((F, D), BF16)]),
        out_shape=jax.ShapeDtypeStruct((P, D), F32),
        compiler_params=_params("arbitrary"),
        name="moe_experts",
    )(tile_expert, n_used, xs, w1, w3, w2)


def _moe_combine_kernel(pos_ref, ys_hbm, x_ref, wts_ref, g_ref, o_ref, y0_ref, y1_ref, sem, *,
                        final_norm):
    n = x_ref.shape[0]
    base = pl.program_id(0) * n
    for slot, y_ref in enumerate((y0_ref, y1_ref)):
        def issue(r, carry, slot=slot, y_ref=y_ref):
            row = pos_ref[(base + r) * MOE_TOPK + slot]
            pltpu.make_async_copy(ys_hbm.at[pl.ds(row, 1)], y_ref.at[pl.ds(r, 1)], sem.at[slot]).start()
            return carry
        lax.fori_loop(0, n, issue, 0, unroll=DMA_ISSUE_UNROLL)
    for slot, y_ref in enumerate((y0_ref, y1_ref)):
        pltpu.make_async_copy(ys_hbm.at[pl.ds(0, n)], y_ref, sem.at[slot]).wait()
    out = x_ref[...] + (wts_ref[:, 0:1] * y0_ref[...] + wts_ref[:, 1:2] * y1_ref[...])
    o_ref[...] = _rms(out, g_ref[...]) if final_norm else out


def moe_combine(ys, pos, wts, x, g, *, final_norm, tile=256):
    T, D = x.shape
    return pl.pallas_call(
        functools.partial(_moe_combine_kernel, final_norm=final_norm),
        grid_spec=pltpu.PrefetchScalarGridSpec(
            num_scalar_prefetch=1,
            grid=(T // tile,),
            in_specs=[pl.BlockSpec(memory_space=pl.ANY),
                      pl.BlockSpec((tile, D), lambda i, pos: (i, 0)),
                      pl.BlockSpec((tile, LANES), lambda i, pos: (i, 0)),
                      pl.BlockSpec((1, D), lambda i, pos: (0, 0))],
            out_specs=pl.BlockSpec((tile, D), lambda i, pos: (i, 0)),
            scratch_shapes=[pltpu.VMEM((tile, D), F32), pltpu.VMEM((tile, D), F32),
                            pltpu.SemaphoreType.DMA((MOE_TOPK,))]),
        out_shape=jax.ShapeDtypeStruct((T, D), F32),
        compiler_params=_params("arbitrary"),
        name="moe_combine",
    )(pos, ys, x, wts, g)


def _pack_w_in(w):
    D = w.shape[0]
    widths = (1024, 1024, 1024, 1024, 1024, 1024, 1024, 256, 512, 64, 8, D, D, D)
    offs = [0]
    for wd in widths:
        offs.append(offs[-1] + wd)
    rq, rk, rv, rg, gu, gv, dq, dc, iq, ik, iw, ga, gb, gc = (
        w[:, offs[i]:offs[i + 1]] for i in range(len(widths)))
    ikw_pad = jnp.zeros((D, COL_GATES - COL_IKW - IDX_DIM - IDX_HEADS), w.dtype)
    packed = jnp.concatenate([rq, rk, rv, rg, gu, gv, dq, iq, dc, ik, iw, ikw_pad, ga, gb, gc], axis=1)
    assert packed.shape[1] == PROJ_WIDTH
    return packed.astype(BF16)


def _pack_router(w_group, b_group, w_expert, b_expert):
    D = w_group.shape[0]
    w_e = jnp.transpose(w_expert, (1, 0, 2)).reshape(D, MOE_EXPERTS)
    pad = LANES - MOE_GROUPS - MOE_EXPERTS
    w = jnp.concatenate([w_group, w_e, jnp.zeros((D, pad), w_group.dtype)], axis=1).astype(BF16)
    b = jnp.concatenate([b_group, b_expert.reshape(MOE_EXPERTS), jnp.zeros((pad,), b_group.dtype)])
    return w, b.reshape(1, LANES).astype(F32)


def kernel(x, norm_mix_g, w_in, w_ret_o, gmlp_ln_g, gmlp_ln_b, gmlp_w_s, gmlp_b_s, w_gmlp_o, dsa_kv_norm_g, dsa_w_uk, dsa_w_uv, w_dsa_o, w_out, norm_ffn_g, moe_w_group, moe_b_group, moe_w_expert, moe_b_expert, moe_w1, moe_w3, moe_w2, final_norm_g):
    B, S, D = x.shape
    depth = w_in.shape[0]
    assert S % DSA_BLOCK == 0 and D == 2048
    xt = x.reshape(B * S, D)
    for l in range(depth):
        proj = norm_matmul(xt, norm_mix_g[l].reshape(1, D), _pack_w_in(w_in[l]))
        ret = retention(proj, B, S)
        gm = gmlp(proj, gmlp_ln_g[l], gmlp_ln_b[l], gmlp_w_s[l], gmlp_b_s[l])
        q_abs, iq_pad, ckv, ckv_t = dsa_prep(proj, dsa_kv_norm_g[l], dsa_w_uk[l].astype(BF16))
        ds = dsa_attention(proj, q_abs, iq_pad, ckv, ckv_t, dsa_w_uv[l].astype(BF16), B, S)
        merged = merge_branches(ret, gm, ds, w_ret_o[l].astype(BF16), w_gmlp_o[l].astype(BF16),
                                w_dsa_o[l].astype(BF16), proj, D)
        xt = out_proj_residual(merged, w_out[l].astype(BF16), xt)
        w_router, b_router = _pack_router(moe_w_group[l], moe_b_group[l], moe_w_expert[l], moe_b_expert[l])
        hp, ids, wts = moe_router(xt, norm_ffn_g[l].reshape(1, D), w_router, b_router)
        n_rows = MOE_TOPK * B * S + MOE_EXPERTS * MOE_TILE
        pos, tile_expert, n_used = _route_tables(ids, n_rows)
        xs = scatter_rows(hp, pos, n_rows)
        F = moe_w1.shape[-1]
        ys = moe_experts(xs, tile_expert + l * MOE_EXPERTS, n_used,
                         moe_w1.reshape(depth * MOE_EXPERTS, D, F),
                         moe_w3.reshape(depth * MOE_EXPERTS, D, F),
                         moe_w2.reshape(depth * MOE_EXPERTS, F, D))
        last = l == depth - 1
        xt = moe_combine(ys, pos, wts, xt, final_norm_g.reshape(1, D), final_norm=last)
    return xt.reshape(B, S, D)
```

```python
import functools
import math

import jax
import jax.numpy as jnp
from jax import lax
from jax.experimental import pallas as pl
from jax.experimental.pallas import tpu as pltpu

F32 = jnp.float32
BF16 = jnp.bfloat16

RMS_EPS = 1e-6
RET_HEADS = 4
RET_DIM = 256
CHUNK = 128
ROPE_BASE = 10000.0
GMLP_GROUPS = 4
GMLP_GROUP_DIM = 256
DSA_HEADS = 8
DSA_HEAD_DIM = 128
DSA_LATENT = 256
IDX_HEADS = 8
IDX_DIM = 64
DSA_TOPK_MAX = 256
DSA_BLOCK = 256
MOE_GROUPS = 4
MOE_EXPERTS_PER_GROUP = 8
MOE_EXPERTS = MOE_GROUPS * MOE_EXPERTS_PER_GROUP
MOE_HIDDEN = 512
MOE_TOPK = 2
MOE_TILE = 256
LANES = 128
ROW_TILE = 8
DMA_ISSUE_UNROLL = 8
NEG_BIG = -1e30
LOG2_E = math.log2(math.e)

COL_RQ, COL_RK, COL_RV, COL_RG = 0, 1024, 2048, 3072
COL_GU, COL_GV = 4096, 5120
COL_DQ, COL_IQ, COL_DC, COL_IKW = 6144, 7168, 7680, 7936
COL_GATES = 8192
PROJ_WIDTH = 14336
VMEM_LIMIT = 56 * 1024 * 1024


def _params(*sem):
    return pltpu.CompilerParams(dimension_semantics=sem, vmem_limit_bytes=VMEM_LIMIT)


def _rms(x, g):
    return x * lax.rsqrt(jnp.mean(x * x, axis=-1, keepdims=True) + RMS_EPS) * g


def _dot(a, b):
    return jnp.dot(a, b, preferred_element_type=F32)


def _dot_nt(a, b):
    return lax.dot_general(a, b, (((1,), (1,)), ((), ())), preferred_element_type=F32)


def _norm_matmul_kernel(x_ref, g_ref, w_ref, o_ref, h_ref):
    @pl.when(pl.program_id(1) == 0)
    def _():
        h_ref[...] = _rms(x_ref[...], g_ref[...]).astype(BF16)

    o_ref[...] = _dot(h_ref[...], w_ref[...]).astype(o_ref.dtype)


def norm_matmul(x, g, w, *, tm=1024, tn=512, out_dtype=BF16):
    T, D = x.shape
    N = w.shape[1]
    return pl.pallas_call(
        _norm_matmul_kernel,
        grid=(T // tm, N // tn),
        in_specs=[pl.BlockSpec((tm, D), lambda i, j: (i, 0)),
                  pl.BlockSpec((1, D), lambda i, j: (0, 0)),
                  pl.BlockSpec((D, tn), lambda i, j: (0, j))],
        out_specs=pl.BlockSpec((tm, tn), lambda i, j: (i, j)),
        out_shape=jax.ShapeDtypeStruct((T, N), out_dtype),
        scratch_shapes=[pltpu.VMEM((tm, D), BF16)],
        compiler_params=_params("parallel", "arbitrary"),
        name="norm_in_proj",
    )(x, g, w)


def _retention_kernel(q_ref, k_ref, v_ref, gate_ref, cos_ref, sin_ref, decay_ref, xi_ref,
                      zeta_ref, gchunk_ref, o_ref, r_ref):
    @pl.when(pl.program_id(1) == 0)
    def _():
        r_ref[...] = jnp.zeros_like(r_ref)

    cos = cos_ref[...]
    sin = sin_ref[...]
    half = RET_DIM // 2

    def rotary(t):
        t1, t2 = t[:, :half], t[:, half:]
        return jnp.concatenate([t1 * cos - t2 * sin, t1 * sin + t2 * cos], axis=-1)

    for h in range(RET_HEADS):
        cols = slice(h * RET_DIM, (h + 1) * RET_DIM)
        q = rotary(q_ref[:, cols].astype(F32)).astype(BF16)
        k = rotary(k_ref[:, cols].astype(F32)) * (RET_DIM ** -0.5)
        v = v_ref[:, cols]
        inner = _dot_nt(q, k.astype(BF16)) * decay_ref[h]
        r_old = r_ref[h]
        o = _dot(inner.astype(BF16), v) + _dot(q, r_old.astype(BF16)) * xi_ref[:, h:h + 1]
        kz = (k * zeta_ref[:, h:h + 1]).astype(BF16)
        r_ref[h] = r_old * gchunk_ref[h] + _dot(kz.T, v)
        o = o * lax.rsqrt(jnp.mean(o * o, axis=-1, keepdims=True) + RMS_EPS)
        gate = gate_ref[:, cols].astype(F32)
        o_ref[:, cols] = (gate * jax.nn.sigmoid(gate) * o).astype(o_ref.dtype)


def retention(proj, batch, seq):
    T = proj.shape[0]
    n_chunks = seq // CHUNK
    width = RET_HEADS * RET_DIM
    half = RET_DIM // 2
    inv = ROPE_BASE ** (-jnp.arange(half, dtype=F32) / half)
    ang = jnp.arange(seq).astype(F32)[:, None] * inv[None, :]
    cos, sin = jnp.cos(ang), jnp.sin(ang)
    log_g = jnp.log(1.0 - 2.0 ** (-5.0 - jnp.arange(RET_HEADS, dtype=F32)))
    i = jnp.arange(CHUNK, dtype=F32)
    diff = i[:, None] - i[None, :]
    decay = jnp.where(diff >= 0, jnp.exp(log_g[:, None, None] * jnp.maximum(diff, 0.0)), 0.0)
    xi = jnp.exp(log_g[None, :] * (i[:, None] + 1.0))
    zeta = jnp.exp(log_g[None, :] * (CHUNK - 1.0 - i[:, None]))
    gchunk = jnp.exp(log_g * CHUNK)

    def col_spec(col):
        return pl.BlockSpec((CHUNK, width), lambda b, n: (b * n_chunks + n, col // width))

    return pl.pallas_call(
        _retention_kernel,
        grid=(batch, n_chunks),
        in_specs=[col_spec(COL_RQ), col_spec(COL_RK), col_spec(COL_RV), col_spec(COL_RG),
                  pl.BlockSpec((CHUNK, half), lambda b, n: (n, 0)),
                  pl.BlockSpec((CHUNK, half), lambda b, n: (n, 0)),
                  pl.BlockSpec((RET_HEADS, CHUNK, CHUNK), lambda b, n: (0, 0, 0)),
                  pl.BlockSpec((CHUNK, RET_HEADS), lambda b, n: (0, 0)),
                  pl.BlockSpec((CHUNK, RET_HEADS), lambda b, n: (0, 0)),
                  pl.BlockSpec(memory_space=pltpu.SMEM)],
        out_specs=pl.BlockSpec((CHUNK, width), lambda b, n: (b * n_chunks + n, 0)),
        out_shape=jax.ShapeDtypeStruct((T, width), BF16),
        scratch_shapes=[pltpu.VMEM((RET_HEADS, RET_DIM, RET_DIM), F32)],
        compiler_params=_params("parallel", "arbitrary"),
        name="retention",
    )(proj, proj, proj, proj, cos, sin, decay, xi, zeta, gchunk)


def _gelu(x):
    return 0.5 * x * (1.0 + lax.erf(x * (2.0 ** -0.5)))


def _gmlp_kernel(u_ref, v_ref, lng_ref, lnb_ref, ws_ref, bs_ref, o_ref):
    v = _gelu(v_ref[...].astype(F32))
    mu = jnp.mean(v, axis=-1, keepdims=True)
    var = jnp.mean(jnp.square(v - mu), axis=-1, keepdims=True)
    vn = ((v - mu) * lax.rsqrt(var + RMS_EPS) * lng_ref[...] + lnb_ref[...]).astype(BF16)
    row = lax.broadcasted_iota(jnp.int32, (CHUNK, CHUNK), 0)
    col = lax.broadcasted_iota(jnp.int32, (CHUNK, CHUNK), 1)
    for g in range(GMLP_GROUPS):
        cols = slice(g * GMLP_GROUP_DIM, (g + 1) * GMLP_GROUP_DIM)
        w = jnp.where(row >= col, ws_ref[g], 0.0).astype(BF16)
        mixed = _dot(w, vn[:, cols]) + bs_ref[:, g:g + 1]
        u = _gelu(u_ref[:, cols].astype(F32))
        o_ref[:, cols] = (u * mixed).astype(o_ref.dtype)


def gmlp(proj, ln_g, ln_b, w_s, b_s):
    T = proj.shape[0]
    width = GMLP_GROUPS * GMLP_GROUP_DIM
    return pl.pallas_call(
        _gmlp_kernel,
        grid=(T // CHUNK,),
        in_specs=[pl.BlockSpec((CHUNK, width), lambda i: (i, COL_GU // width)),
                  pl.BlockSpec((CHUNK, width), lambda i: (i, COL_GV // width)),
                  pl.BlockSpec((1, width), lambda i: (0, 0)),
                  pl.BlockSpec((1, width), lambda i: (0, 0)),
                  pl.BlockSpec((GMLP_GROUPS, CHUNK, CHUNK), lambda i: (0, 0, 0)),
                  pl.BlockSpec((CHUNK, GMLP_GROUPS), lambda i: (0, 0))],
        out_specs=pl.BlockSpec((CHUNK, width), lambda i: (i, 0)),
        out_shape=jax.ShapeDtypeStruct((T, width), BF16),
        compiler_params=_params("parallel"),
        name="gmlp",
    )(proj, proj, ln_g.reshape(1, width), ln_b.reshape(1, width), w_s, b_s.T)


def _dsa_prep_kernel(dq_ref, dc_ref, iq_ref, g_ref, wuk_ref, qabs_ref, iqp_ref, ckv_ref, ckvt_ref):
    c = _rms(dc_ref[...].astype(F32), g_ref[...])
    ckv_ref[...] = c.astype(ckv_ref.dtype)
    ckvt_ref[0, :DSA_LATENT, :] = c.T.astype(ckvt_ref.dtype)
    ckvt_ref[0, DSA_LATENT:, :] = jnp.ones((ROW_TILE, DSA_BLOCK), ckvt_ref.dtype)
    zeros = jnp.zeros((DSA_BLOCK, LANES - IDX_DIM), iqp_ref.dtype)
    for h in range(DSA_HEADS):
        q = dq_ref[:, h * DSA_HEAD_DIM:(h + 1) * DSA_HEAD_DIM]
        qabs_ref[0, h] = (_dot_nt(q, wuk_ref[h]) * (DSA_HEAD_DIM ** -0.5 * LOG2_E)).astype(qabs_ref.dtype)
    for h in range(IDX_HEADS):
        iqp_ref[0, h] = jnp.concatenate([iq_ref[:, h * IDX_DIM:(h + 1) * IDX_DIM], zeros], axis=-1)


def dsa_prep(proj, kv_norm_g, w_uk):
    T = proj.shape[0]
    nblk = T // DSA_BLOCK
    qw = DSA_HEADS * DSA_HEAD_DIM
    iq_w = IDX_HEADS * IDX_DIM
    return pl.pallas_call(
        _dsa_prep_kernel,
        grid=(nblk,),
        in_specs=[pl.BlockSpec((DSA_BLOCK, qw), lambda i: (i, COL_DQ // qw)),
                  pl.BlockSpec((DSA_BLOCK, DSA_LATENT), lambda i: (i, COL_DC // DSA_LATENT)),
                  pl.BlockSpec((DSA_BLOCK, iq_w), lambda i: (i, COL_IQ // iq_w)),
                  pl.BlockSpec((1, DSA_LATENT), lambda i: (0, 0)),
                  pl.BlockSpec((DSA_HEADS, DSA_LATENT, DSA_HEAD_DIM), lambda i: (0, 0, 0))],
        out_specs=[pl.BlockSpec((1, DSA_HEADS, DSA_BLOCK, DSA_LATENT), lambda i: (i, 0, 0, 0)),
                   pl.BlockSpec((1, IDX_HEADS, DSA_BLOCK, LANES), lambda i: (i, 0, 0, 0)),
                   pl.BlockSpec((DSA_BLOCK, DSA_LATENT), lambda i: (i, 0)),
                   pl.BlockSpec((1, DSA_LATENT + ROW_TILE, DSA_BLOCK), lambda i: (i, 0, 0))],
        out_shape=[jax.ShapeDtypeStruct((nblk, DSA_HEADS, DSA_BLOCK, DSA_LATENT), BF16),
                   jax.ShapeDtypeStruct((nblk, IDX_HEADS, DSA_BLOCK, LANES), BF16),
                   jax.ShapeDtypeStruct((T, DSA_LATENT), BF16),
                   jax.ShapeDtypeStruct((nblk, DSA_LATENT + ROW_TILE, DSA_BLOCK), BF16)],
        compiler_params=_params("parallel"),
        name="dsa_prep",
    )(proj, proj, proj, kv_norm_g.reshape(1, DSA_LATENT), w_uk)


def _dsa_kernel(iqp_ref, wq_ref, kidx_ref, ckv_ref, ckvt_ref, qa_ref, wuv_ref, o_ref,
                score_ref, m_ref, acc_ref, *, k_sel):
    QB = DSA_BLOCK
    blk = pl.program_id(1)
    n_tiles = blk + 1
    ksel_f = float(k_sel)
    key_in_tile = lax.broadcasted_iota(jnp.int32, (QB, QB), 0)
    query_in_blk = lax.broadcasted_iota(jnp.int32, (QB, QB), 1)

    def fold8(x, op):
        return op(x.reshape(QB // 8, 8, QB), axis=0)

    def tile_rows(j):
        return pl.ds(pl.multiple_of(j * QB, QB), QB)

    wi_t = wq_ref[...].astype(F32).T * (IDX_HEADS ** -0.5 * IDX_DIM ** -0.5)

    def score_body(j, stats):
        s_max, s_min, n_ge0, n_gt0 = stats
        kt = kidx_ref[tile_rows(j), :]
        sc = jnp.zeros((QB, QB), F32)
        for h in range(IDX_HEADS):
            lg = _dot_nt(kt, iqp_ref[0, h])
            sc = sc + wi_t[IDX_DIM + h:IDX_DIM + h + 1, :] * jnp.maximum(lg, 0.0)
        causal = (j < blk) | (key_in_tile <= query_in_blk)
        masked = jnp.where(causal, sc, -jnp.inf)
        score_ref[j] = masked
        return (jnp.maximum(s_max, fold8(masked, jnp.max)),
                jnp.minimum(s_min, fold8(jnp.where(causal, sc, jnp.inf), jnp.min)),
                n_ge0 + fold8((masked >= 0.0).astype(F32), jnp.sum),
                n_gt0 + fold8((masked > 0.0).astype(F32), jnp.sum))

    stats = lax.fori_loop(0, n_tiles, score_body,
                          (jnp.full((8, QB), -jnp.inf, F32), jnp.full((8, QB), jnp.inf, F32),
                           jnp.zeros((8, QB), F32), jnp.zeros((8, QB), F32)))
    hi0 = jnp.max(stats[0], axis=0, keepdims=True)
    lo0 = jnp.min(stats[1], axis=0, keepdims=True)
    cnt_ge0 = jnp.sum(stats[2], axis=0, keepdims=True)
    cnt_gt0 = jnp.sum(stats[3], axis=0, keepdims=True)

    def count(pred):
        def body(j, acc):
            return acc + fold8(pred(score_ref[j]).astype(F32), jnp.sum)
        acc = lax.fori_loop(0, n_tiles, body, jnp.zeros((8, QB), F32))
        return jnp.sum(acc, axis=0, keepdims=True)

    n_causal = (blk * QB + 1 + lax.broadcasted_iota(jnp.int32, (1, QB), 1)).astype(F32)
    need = n_causal > ksel_f
    cnt_hi0 = count(lambda s: s >= hi0)
    above0 = cnt_gt0 >= ksel_f
    below0 = cnt_ge0 < ksel_f
    lo0 = jnp.where(below0, lo0, 0.0)
    cnt_lo0 = jnp.where(below0, n_causal, cnt_ge0)
    hi0 = jnp.where(above0, hi0, 0.0)
    at_top = above0 & (cnt_hi0 >= ksel_f)
    lo0 = jnp.where(at_top, hi0, lo0)
    cnt_lo0 = jnp.where(at_top, cnt_hi0, cnt_lo0)
    active0 = need & (above0 | below0) & (cnt_lo0 != ksel_f) & (lo0 < hi0)

    def bisect_cond(state):
        return jnp.max(state[3]) > 0.0

    def bisect_body(state):
        lo, hi, cnt_lo, active = state
        mid = lo + 0.5 * (hi - lo)
        c = count(lambda s: s >= mid)
        ge = c >= ksel_f
        live = active > 0.0
        moved = (mid > lo) & (mid < hi)
        new_lo = jnp.where(live & ge, mid, lo)
        new_hi = jnp.where(live & ~ge, mid, hi)
        new_cnt = jnp.where(live & ge, c, cnt_lo)
        new_active = live & moved & (new_cnt != ksel_f)
        return new_lo, new_hi, new_cnt, new_active.astype(F32)

    lo = lax.while_loop(bisect_cond, bisect_body, (lo0, hi0, cnt_lo0, active0.astype(F32)))[0]
    thr = jnp.where(need, lo, -jnp.inf)
    n_gt = count(lambda s: s > thr)
    tie_quota = jnp.where(need, ksel_f - n_gt, 0.0)

    m_ref[...] = jnp.full_like(m_ref, NEG_BIG)
    acc_ref[...] = jnp.zeros_like(acc_ref)
    strict_lower = (query_in_blk < key_in_tile).astype(BF16)

    def attn_body(j, tie_seen):
        sc = score_ref[j]
        eq = sc == thr
        eq_f = eq.astype(F32)
        prefix = _dot(strict_lower, eq_f.astype(BF16)) + tie_seen
        sel = (sc > thr) | (eq & (prefix < tie_quota))
        bias = jnp.where(sel, 0.0, NEG_BIG)
        c_t = ckv_ref[tile_rows(j), :]
        ct_t = ckvt_ref[j]
        for h in range(DSA_HEADS):
            s = _dot_nt(c_t, qa_ref[0, h]) + bias
            m_old = m_ref[h]
            m_new = jnp.maximum(m_old, jnp.max(fold8(s, jnp.max), axis=0, keepdims=True))
            alpha = jnp.exp2(m_old - m_new)
            p = jnp.exp2(s - m_new)
            acc_ref[h] = alpha * acc_ref[h] + _dot(ct_t, p.astype(BF16))
            m_ref[h] = m_new
        return tie_seen + jnp.sum(fold8(eq_f, jnp.sum), axis=0, keepdims=True)

    lax.fori_loop(0, n_tiles, attn_body, jnp.zeros((1, QB), F32))

    for h in range(DSA_HEADS):
        acc = acc_ref[h]
        o_lat = (acc[:DSA_LATENT] / acc[DSA_LATENT:DSA_LATENT + 1]).T.astype(BF16)
        o_ref[:, h * DSA_HEAD_DIM:(h + 1) * DSA_HEAD_DIM] = _dot(o_lat, wuv_ref[h]).astype(o_ref.dtype)


def dsa_attention(proj, q_abs, iq_pad, ckv, ckv_t, w_uv, batch, seq):
    T = proj.shape[0]
    nb = seq // DSA_BLOCK
    k_sel = min(DSA_TOPK_MAX, seq // 4)
    out_w = DSA_HEADS * DSA_HEAD_DIM
    return pl.pallas_call(
        functools.partial(_dsa_kernel, k_sel=k_sel),
        grid=(batch, nb),
        in_specs=[pl.BlockSpec((1, IDX_HEADS, DSA_BLOCK, LANES), lambda b, i: (b * nb + i, 0, 0, 0)),
                  pl.BlockSpec((DSA_BLOCK, LANES), lambda b, i: (b * nb + i, COL_IKW // LANES)),
                  pl.BlockSpec((seq, LANES), lambda b, i: (b, COL_IKW // LANES)),
                  pl.BlockSpec((seq, DSA_LATENT), lambda b, i: (b, 0)),
                  pl.BlockSpec((nb, DSA_LATENT + ROW_TILE, DSA_BLOCK), lambda b, i: (b, 0, 0)),
                  pl.BlockSpec((1, DSA_HEADS, DSA_BLOCK, DSA_LATENT), lambda b, i: (b * nb + i, 0, 0, 0)),
                  pl.BlockSpec((DSA_HEADS, DSA_LATENT, DSA_HEAD_DIM), lambda b, i: (0, 0, 0))],
        out_specs=pl.BlockSpec((DSA_BLOCK, out_w), lambda b, i: (b * nb + i, 0)),
        out_shape=jax.ShapeDtypeStruct((T, out_w), BF16),
        scratch_shapes=[pltpu.VMEM((nb, DSA_BLOCK, DSA_BLOCK), F32),
                        pltpu.VMEM((DSA_HEADS, 1, DSA_BLOCK), F32),
                        pltpu.VMEM((DSA_HEADS, DSA_LATENT + ROW_TILE, DSA_BLOCK), F32)],
        compiler_params=_params("parallel", "arbitrary"),
        name="dsa_attention",
    )(iq_pad, proj, proj, ckv, ckv_t, q_abs, w_uv)


def _merge_kernel(ret_ref, gm_ref, ds_ref, wr_ref, wg_ref, wd_ref, ga_ref, gb_ref, gc_ref, o_ref):
    def branch(a_ref, w_ref, gate_ref):
        return jax.nn.sigmoid(gate_ref[...].astype(F32)) * _dot(a_ref[...], w_ref[...])

    merged = branch(ret_ref, wr_ref, ga_ref) + branch(gm_ref, wg_ref, gb_ref) + branch(ds_ref, wd_ref, gc_ref)
    o_ref[...] = merged.astype(o_ref.dtype)


def merge_branches(ret, gm, ds, w_ret_o, w_gmlp_o, w_dsa_o, proj, d_model, *, tm=1024, tn=512):
    T, K = ret.shape
    gate_blk = COL_GATES // tn
    per_gate = d_model // tn
    act = pl.BlockSpec((tm, K), lambda i, j: (i, 0))
    wgt = pl.BlockSpec((K, tn), lambda i, j: (0, j))

    def gate(which):
        return pl.BlockSpec((tm, tn), lambda i, j: (i, gate_blk + which * per_gate + j))

    return pl.pallas_call(
        _merge_kernel,
        grid=(T // tm, d_model // tn),
        in_specs=[act, act, act, wgt, wgt, wgt, gate(0), gate(1), gate(2)],
        out_specs=pl.BlockSpec((tm, tn), lambda i, j: (i, j)),
        out_shape=jax.ShapeDtypeStruct((T, d_model), BF16),
        compiler_params=_params("parallel", "arbitrary"),
        name="merge_branches",
    )(ret, gm, ds, w_ret_o, w_gmlp_o, w_dsa_o, proj, proj, proj)


def _out_proj_kernel(a_ref, w_ref, x_ref, o_ref):
    o_ref[...] = x_ref[...] + _dot(a_ref[...], w_ref[...])


def out_proj_residual(merged, w_out, x, *, tm=1024, tn=512):
    T, K = merged.shape
    N = w_out.shape[1]
    return pl.pallas_call(
        _out_proj_kernel,
        grid=(T // tm, N // tn),
        in_specs=[pl.BlockSpec((tm, K), lambda i, j: (i, 0)),
                  pl.BlockSpec((K, tn), lambda i, j: (0, j)),
                  pl.BlockSpec((tm, tn), lambda i, j: (i, j))],
        out_specs=pl.BlockSpec((tm, tn), lambda i, j: (i, j)),
        out_shape=jax.ShapeDtypeStruct((T, N), F32),
        compiler_params=_params("parallel", "arbitrary"),
        name="out_proj_residual",
    )(merged, w_out, x)


def _pack_rows(h_bf16, o_ref):
    n, d = h_bf16.shape
    bits = lax.bitcast_convert_type(h_bf16.astype(F32), jnp.uint32)
    packed = (bits[:, :d // 2] >> 16) | (bits[:, d // 2:] & jnp.uint32(0xFFFF0000))
    for c in range(d // 2 // LANES):
        o_ref[pl.ds(c, n, stride=ROW_TILE), :] = packed[:, c * LANES:(c + 1) * LANES]


def _unpack_rows(x_ref, o_ref):
    n, d = o_ref.shape
    for c in range(d // 2 // LANES):
        w = x_ref[pl.ds(c, n, stride=ROW_TILE), :]
        lo = lax.bitcast_convert_type(w << 16, F32)
        hi = lax.bitcast_convert_type(w & jnp.uint32(0xFFFF0000), F32)
        o_ref[:, c * LANES:(c + 1) * LANES] = lo.astype(o_ref.dtype)
        o_ref[:, d // 2 + c * LANES:d // 2 + (c + 1) * LANES] = hi.astype(o_ref.dtype)


def _router_kernel(x_ref, g_ref, w_ref, b_ref, hp_ref, ids_ref, wts_ref):
    h = _rms(x_ref[...], g_ref[...]).astype(BF16)
    _pack_rows(h, hp_ref)
    logits = _dot(h, w_ref[...]) + b_ref[...]
    lane = lax.broadcasted_iota(jnp.int32, logits.shape, 1)
    big = jnp.int32(LANES)

    def first_argmax(vals, valid):
        v = jnp.where(valid, vals, -jnp.inf)
        m = jnp.max(v, axis=-1, keepdims=True)
        idx = jnp.min(jnp.where(valid & (v == m), lane, big), axis=-1, keepdims=True)
        return m, idx

    is_grp = lane < MOE_GROUPS
    g_max, g_sel = first_argmax(logits, is_grp)
    g_den = jnp.sum(jnp.where(is_grp, jnp.exp(logits - g_max), 0.0), axis=-1, keepdims=True)
    p_g = 1.0 / g_den
    e_lo = MOE_GROUPS + g_sel * MOE_EXPERTS_PER_GROUP
    in_grp = (lane >= e_lo) & (lane < e_lo + MOE_EXPERTS_PER_GROUP)
    m1, i1 = first_argmax(logits, in_grp)
    m2, i2 = first_argmax(logits, in_grp & (lane != i1))
    e2 = jnp.exp(m2 - m1)
    w1 = p_g / (1.0 + e2)
    w2 = p_g * e2 / (1.0 + e2)
    ids_ref[...] = jnp.where(lane == 0, i1 - MOE_GROUPS, jnp.where(lane == 1, i2 - MOE_GROUPS, 0))
    wts_ref[...] = jnp.where(lane == 0, w1, jnp.where(lane == 1, w2, 0.0))


def moe_router(x, g, w_router, b_router, *, tm=512):
    T, D = x.shape
    assert D == 2 * ROW_TILE * LANES
    return pl.pallas_call(
        _router_kernel,
        grid=(T // tm,),
        in_specs=[pl.BlockSpec((tm, D), lambda i: (i, 0)),
                  pl.BlockSpec((1, D), lambda i: (0, 0)),
                  pl.BlockSpec((D, LANES), lambda i: (0, 0)),
                  pl.BlockSpec((1, LANES), lambda i: (0, 0))],
        out_specs=[pl.BlockSpec((tm * ROW_TILE, LANES), lambda i: (i, 0)),
                   pl.BlockSpec((tm, LANES), lambda i: (i, 0)),
                   pl.BlockSpec((tm, LANES), lambda i: (i, 0))],
        out_shape=[jax.ShapeDtypeStruct((T * ROW_TILE, LANES), jnp.uint32),
                   jax.ShapeDtypeStruct((T, LANES), jnp.int32),
                   jax.ShapeDtypeStruct((T, LANES), F32)],
        compiler_params=_params("parallel"),
        name="moe_router",
    )(x, g, w_router, b_router)


def _route_tables(ids, n_rows):
    T = ids.shape[0]
    e = ids[:, :MOE_TOPK].reshape(-1)
    onehot = (e[:, None] == jnp.arange(MOE_EXPERTS, dtype=jnp.int32)[None, :]).astype(jnp.int32)
    rank = jnp.sum((jnp.cumsum(onehot, axis=0) - onehot) * onehot, axis=1)
    counts = jnp.sum(onehot, axis=0)
    padded = (counts + MOE_TILE - 1) // MOE_TILE * MOE_TILE
    ends = jnp.cumsum(padded)
    pos = ((ends - padded)[e] + rank).astype(jnp.int32)
    tile_start = jnp.arange(n_rows // MOE_TILE, dtype=jnp.int32) * MOE_TILE
    tile_expert = jnp.sum((ends[None, :] <= tile_start[:, None]).astype(jnp.int32), axis=1)
    tile_expert = jnp.minimum(tile_expert, MOE_EXPERTS - 1)
    n_used = (ends[-1] // MOE_TILE).astype(jnp.int32).reshape(1)
    return pos, tile_expert, n_used


def _token_rows(r, rows_per_token):
    return pl.ds(pl.multiple_of(r * rows_per_token, rows_per_token), rows_per_token)


def _scatter_rows_kernel(pos_ref, hp_ref, xs_in_hbm, xs_hbm, sem):
    del xs_in_hbm
    n = hp_ref.shape[0] // ROW_TILE
    base = pl.program_id(0) * n
    for slot in range(MOE_TOPK):
        def issue(r, carry, slot=slot):
            row = pos_ref[(base + r) * MOE_TOPK + slot]
            pltpu.make_async_copy(hp_ref.at[_token_rows(r, ROW_TILE)],
                                  xs_hbm.at[_token_rows(row, ROW_TILE)], sem.at[slot]).start()
            return carry
        lax.fori_loop(0, n, issue, 0, unroll=DMA_ISSUE_UNROLL)
    for slot in range(MOE_TOPK):
        pltpu.make_async_copy(hp_ref, xs_hbm.at[pl.ds(0, n * ROW_TILE)], sem.at[slot]).wait()


def scatter_rows(hp, pos, n_rows, *, tile=256):
    T = hp.shape[0] // ROW_TILE
    return pl.pallas_call(
        _scatter_rows_kernel,
        grid_spec=pltpu.PrefetchScalarGridSpec(
            num_scalar_prefetch=1,
            grid=(T // tile,),
            in_specs=[pl.BlockSpec((tile * ROW_TILE, LANES), lambda i, pos: (i, 0)),
                      pl.BlockSpec(memory_space=pl.ANY)],
            out_specs=pl.BlockSpec(memory_space=pl.ANY),
            scratch_shapes=[pltpu.SemaphoreType.DMA((MOE_TOPK,))]),
        out_shape=jax.ShapeDtypeStruct((n_rows * ROW_TILE, LANES), jnp.uint32),
        input_output_aliases={2: 0},
        compiler_params=_params("arbitrary"),
        name="moe_scatter_rows",
    )(pos, hp, jnp.zeros((n_rows * ROW_TILE, LANES), jnp.uint32))


def _moe_experts_kernel(tile_expert_ref, n_used_ref, x_ref, w1_ref, w3_ref, w2_ref, o_ref,
                        xb_ref, w1b_ref, w3b_ref, w2b_ref):
    i = pl.program_id(0)

    @pl.when((i == 0) | (tile_expert_ref[i] != tile_expert_ref[jnp.maximum(i - 1, 0)]))
    def _():
        w1b_ref[...] = w1_ref[0].astype(BF16)
        w3b_ref[...] = w3_ref[0].astype(BF16)
        w2b_ref[...] = w2_ref[0].astype(BF16)

    @pl.when(i < n_used_ref[0])
    def _():
        _unpack_rows(x_ref, xb_ref)
        x = xb_ref[...]
        a = _dot(x, w1b_ref[...])
        hid = a * jax.nn.sigmoid(a) * _dot(x, w3b_ref[...])
        o_ref[...] = _dot(hid.astype(BF16), w2b_ref[...])

    @pl.when(i >= n_used_ref[0])
    def _():
        o_ref[...] = jnp.zeros_like(o_ref)


def moe_experts(xs, tile_expert, n_used, w1, w3, w2):
    E, D, F = w1.shape
    P = xs.shape[0] // ROW_TILE

    def wspec(shape):
        return pl.BlockSpec((1,) + shape, lambda i, te, nu: (te[i], 0, 0))

    return pl.pallas_call(
        _moe_experts_kernel,
        grid_spec=pltpu.PrefetchScalarGridSpec(
            num_scalar_prefetch=2,
            grid=(P // MOE_TILE,),
            in_specs=[pl.BlockSpec((MOE_TILE * ROW_TILE, LANES), lambda i, te, nu: (i, 0)),
                      wspec((D, F)), wspec((D, F)), wspec((F, D))],
            out_specs=pl.BlockSpec((MOE_TILE, D), lambda i, te, nu: (i, 0)),
            scratch_shapes=[pltpu.VMEM((MOE_TILE, D), BF16), pltpu.VMEM((D, F), BF16),
                            pltpu.VMEM((D, F), BF16), pltpu.VMEM((F, D), BF16)]),
        out_shape=jax.ShapeDtypeStruct((P, D), F32),
        compiler_params=_params("arbitrary"),
        name="moe_experts",
    )(tile_expert, n_used, xs, w1, w3, w2)


def _moe_combine_kernel(pos_ref, ys_hbm, x_ref, wts_ref, g_ref, o_ref, y0_ref, y1_ref, sem, *,
                        final_norm):
    n = x_ref.shape[0]
    base = pl.program_id(0) * n
    for slot, y_ref in enumerate((y0_ref, y1_ref)):
        def issue(r, carry, slot=slot, y_ref=y_ref):
            row = pos_ref[(base + r) * MOE_TOPK + slot]
            pltpu.make_async_copy(ys_hbm.at[pl.ds(row, 1)], y_ref.at[pl.ds(r, 1)], sem.at[slot]).start()
            return carry
        lax.fori_loop(0, n, issue, 0, unroll=DMA_ISSUE_UNROLL)
    for slot, y_ref in enumerate((y0_ref, y1_ref)):
        pltpu.make_async_copy(ys_hbm.at[pl.ds(0, n)], y_ref, sem.at[slot]).wait()
    out = x_ref[...] + (wts_ref[:, 0:1] * y0_ref[...] + wts_ref[:, 1:2] * y1_ref[...])
    o_ref[...] = _rms(out, g_ref[...]) if final_norm else out


def moe_combine(ys, pos, wts, x, g, *, final_norm, tile=256):
    T, D = x.shape
    return pl.pallas_call(
        functools.partial(_moe_combine_kernel, final_norm=final_norm),
        grid_spec=pltpu.PrefetchScalarGridSpec(
            num_scalar_prefetch=1,
            grid=(T // tile,),
            in_specs=[pl.BlockSpec(memory_space=pl.ANY),
                      pl.BlockSpec((tile, D), lambda i, pos: (i, 0)),
                      pl.BlockSpec((tile, LANES), lambda i, pos: (i, 0)),
                      pl.BlockSpec((1, D), lambda i, pos: (0, 0))],
            out_specs=pl.BlockSpec((tile, D), lambda i, pos: (i, 0)),
            scratch_shapes=[pltpu.VMEM((tile, D), F32), pltpu.VMEM((tile, D), F32),
                            pltpu.SemaphoreType.DMA((MOE_TOPK,))]),
        out_shape=jax.ShapeDtypeStruct((T, D), F32),
        compiler_params=_params("arbitrary"),
        name="moe_combine",
    )(pos, ys, x, wts, g)


def _pack_w_in(w):
    D = w.shape[0]
    widths = (1024, 1024, 1024, 1024, 1024, 1024, 1024, 256, 512, 64, 8, D, D, D)
    offs = [0]
    for wd in widths:
        offs.append(offs[-1] + wd)
    rq, rk, rv, rg, gu, gv, dq, dc, iq, ik, iw, ga, gb, gc = (
        w[:, offs[i]:offs[i + 1]] for i in range(len(widths)))
    ikw_pad = jnp.zeros((D, COL_GATES - COL_IKW - IDX_DIM - IDX_HEADS), w.dtype)
    packed = jnp.concatenate([rq, rk, rv, rg, gu, gv, dq, iq, dc, ik, iw, ikw_pad, ga, gb, gc], axis=1)
    assert packed.shape[1] == PROJ_WIDTH
    return packed.astype(BF16)


def _pack_router(w_group, b_group, w_expert, b_expert):
    D = w_group.shape[0]
    w_e = jnp.transpose(w_expert, (1, 0, 2)).reshape(D, MOE_EXPERTS)
    pad = LANES - MOE_GROUPS - MOE_EXPERTS
    w = jnp.concatenate([w_group, w_e, jnp.zeros((D, pad), w_group.dtype)], axis=1).astype(BF16)
    b = jnp.concatenate([b_group, b_expert.reshape(MOE_EXPERTS), jnp.zeros((pad,), b_group.dtype)])
    return w, b.reshape(1, LANES).astype(F32)


def kernel(x, norm_mix_g, w_in, w_ret_o, gmlp_ln_g, gmlp_ln_b, gmlp_w_s, gmlp_b_s, w_gmlp_o, dsa_kv_norm_g, dsa_w_uk, dsa_w_uv, w_dsa_o, w_out, norm_ffn_g, moe_w_group, moe_b_group, moe_w_expert, moe_b_expert, moe_w1, moe_w3, moe_w2, final_norm_g):
    B, S, D = x.shape
    depth = w_in.shape[0]
    assert S % DSA_BLOCK == 0 and D == 2048
    xt = x.reshape(B * S, D)
    for l in range(depth):
        proj = norm_matmul(xt, norm_mix_g[l].reshape(1, D), _pack_w_in(w_in[l]))
        ret = retention(proj, B, S)
        gm = gmlp(proj, gmlp_ln_g[l], gmlp_ln_b[l], gmlp_w_s[l], gmlp_b_s[l])
        q_abs, iq_pad, ckv, ckv_t = dsa_prep(proj, dsa_kv_norm_g[l], dsa_w_uk[l].astype(BF16))
        ds = dsa_attention(proj, q_abs, iq_pad, ckv, ckv_t, dsa_w_uv[l].astype(BF16), B, S)
        merged = merge_branches(ret, gm, ds, w_ret_o[l].astype(BF16), w_gmlp_o[l].astype(BF16),
                                w_dsa_o[l].astype(BF16), proj, D)
        xt = out_proj_residual(merged, w_out[l].astype(BF16), xt)
        w_router, b_router = _pack_router(moe_w_group[l], moe_b_group[l], moe_w_expert[l], moe_b_expert[l])
        hp, ids, wts = moe_router(xt, norm_ffn_g[l].reshape(1, D), w_router, b_router)
        n_rows = MOE_TOPK * B * S + MOE_EXPERTS * MOE_TILE
        pos, tile_expert, n_used = _route_tables(ids, n_rows)
        xs = scatter_rows(hp, pos, n_rows)
        F = moe_w1.shape[-1]
        ys = moe_experts(xs, tile_expert + l * MOE_EXPERTS, n_used,
                         moe_w1.reshape(depth * MOE_EXPERTS, D, F),
                         moe_w3.reshape(depth * MOE_EXPERTS, D, F),
                         moe_w2.reshape(depth * MOE_EXPERTS, F, D))
        last = l == depth - 1
        xt = moe_combine(ys, pos, wts, xt, final_norm_g.reshape(1, D), final_norm=last)
    return xt.reshape(B, S, D)
```

```python
import functools
import math

import jax
import jax.numpy as jnp
from jax import lax
from jax.experimental import pallas as pl
from jax.experimental.pallas import tpu as pltpu

F32 = jnp.float32
BF16 = jnp.bfloat16

RMS_EPS = 1e-6
RET_HEADS = 4
RET_DIM = 256
CHUNK = 128
ROPE_BASE = 10000.0
GMLP_GROUPS = 4
GMLP_GROUP_DIM = 256
DSA_HEADS = 8
DSA_HEAD_DIM = 128
DSA_LATENT = 256
IDX_HEADS = 8
IDX_DIM = 64
DSA_TOPK_MAX = 256
DSA_BLOCK = 256
MOE_GROUPS = 4
MOE_EXPERTS_PER_GROUP = 8
MOE_EXPERTS = MOE_GROUPS * MOE_EXPERTS_PER_GROUP
MOE_HIDDEN = 512
MOE_TOPK = 2
MOE_TILE = 256
LANES = 128
ROW_TILE = 8
DMA_ISSUE_UNROLL = 8
NEG_BIG = -1e30
LOG2_E = math.log2(math.e)

COL_RQ, COL_RK, COL_RV, COL_RG = 0, 1024, 2048, 3072
COL_GU, COL_GV = 4096, 5120
COL_DQ, COL_IQ, COL_DC, COL_IKW = 6144, 7168, 7680, 7936
COL_GATES = 8192
PROJ_WIDTH = 14336
VMEM_LIMIT = 56 * 1024 * 1024


def _params(*sem):
    return pltpu.CompilerParams(dimension_semantics=sem, vmem_limit_bytes=VMEM_LIMIT)


def _rms(x, g):
    return x * lax.rsqrt(jnp.mean(x * x, axis=-1, keepdims=True) + RMS_EPS) * g


def _dot(a, b):
    return jnp.dot(a, b, preferred_element_type=F32)


def _dot_nt(a, b):
    return lax.dot_general(a, b, (((1,), (1,)), ((), ())), preferred_element_type=F32)


def _norm_matmul_kernel(x_ref, g_ref, w_ref, o_ref, h_ref):
    @pl.when(pl.program_id(1) == 0)
    def _():
        h_ref[...] = _rms(x_ref[...], g_ref[...]).astype(BF16)

    o_ref[...] = _dot(h_ref[...], w_ref[...]).astype(o_ref.dtype)


def norm_matmul(x, g, w, *, tm=1024, tn=512, out_dtype=BF16):
    T, D = x.shape
    N = w.shape[1]
    return pl.pallas_call(
        _norm_matmul_kernel,
        grid=(T // tm, N // tn),
        in_specs=[pl.BlockSpec((tm, D), lambda i, j: (i, 0)),
                  pl.BlockSpec((1, D), lambda i, j: (0, 0)),
                  pl.BlockSpec((D, tn), lambda i, j: (0, j))],
        out_specs=pl.BlockSpec((tm, tn), lambda i, j: (i, j)),
        out_shape=jax.ShapeDtypeStruct((T, N), out_dtype),
        scratch_shapes=[pltpu.VMEM((tm, D), BF16)],
        compiler_params=_params("parallel", "arbitrary"),
        name="norm_in_proj",
    )(x, g, w)


def _retention_kernel(q_ref, k_ref, v_ref, gate_ref, cos_ref, sin_ref, decay_ref, xi_ref,
                      zeta_ref, gchunk_ref, o_ref, r_ref):
    @pl.when(pl.program_id(0) == 0)
    def _():
        r_ref[...] = jnp.zeros_like(r_ref)

    cos = cos_ref[...]
    sin = sin_ref[...]
    half = RET_DIM // 2

    def rotary(t):
        t1, t2 = t[:, :half], t[:, half:]
        return jnp.concatenate([t1 * cos - t2 * sin, t1 * sin + t2 * cos], axis=-1)

    for b in range(q_ref.shape[0]):
        for h in range(RET_HEADS):
            cols = slice(h * RET_DIM, (h + 1) * RET_DIM)
            q = rotary(q_ref[b, :, cols].astype(F32)).astype(BF16)
            k = rotary(k_ref[b, :, cols].astype(F32)) * (RET_DIM ** -0.5)
            v = v_ref[b, :, cols]
            inner = _dot_nt(q, k.astype(BF16)) * decay_ref[h]
            r_old = r_ref[b, h]
            o = _dot(inner.astype(BF16), v) + _dot(q, r_old.astype(BF16)) * xi_ref[:, h:h + 1]
            kz = (k * zeta_ref[:, h:h + 1]).astype(BF16)
            r_ref[b, h] = r_old * gchunk_ref[h] + _dot(kz.T, v)
            o = o * lax.rsqrt(jnp.mean(o * o, axis=-1, keepdims=True) + RMS_EPS)
            gate = gate_ref[b, :, cols].astype(F32)
            o_ref[b, :, cols] = (gate * jax.nn.sigmoid(gate) * o).astype(o_ref.dtype)


def retention(proj, batch, seq):
    T = proj.shape[0]
    n_chunks = seq // CHUNK
    width = RET_HEADS * RET_DIM
    half = RET_DIM // 2
    inv = ROPE_BASE ** (-jnp.arange(half, dtype=F32) / half)
    ang = jnp.arange(seq).astype(F32)[:, None] * inv[None, :]
    cos, sin = jnp.cos(ang), jnp.sin(ang)
    log_g = jnp.log(1.0 - 2.0 ** (-5.0 - jnp.arange(RET_HEADS, dtype=F32)))
    i = jnp.arange(CHUNK, dtype=F32)
    diff = i[:, None] - i[None, :]
    decay = jnp.where(diff >= 0, jnp.exp(log_g[:, None, None] * jnp.maximum(diff, 0.0)), 0.0)
    xi = jnp.exp(log_g[None, :] * (i[:, None] + 1.0))
    zeta = jnp.exp(log_g[None, :] * (CHUNK - 1.0 - i[:, None]))
    gchunk = jnp.exp(log_g * CHUNK)

    def col_spec(col):
        return pl.BlockSpec((batch, CHUNK, width), lambda n: (0, n, col // width))

    proj3 = proj.reshape(batch, seq, proj.shape[1])
    out = pl.pallas_call(
        _retention_kernel,
        grid=(n_chunks,),
        in_specs=[col_spec(COL_RQ), col_spec(COL_RK), col_spec(COL_RV), col_spec(COL_RG),
                  pl.BlockSpec((CHUNK, half), lambda n: (n, 0)),
                  pl.BlockSpec((CHUNK, half), lambda n: (n, 0)),
                  pl.BlockSpec((RET_HEADS, CHUNK, CHUNK), lambda n: (0, 0, 0)),
                  pl.BlockSpec((CHUNK, RET_HEADS), lambda n: (0, 0)),
                  pl.BlockSpec((CHUNK, RET_HEADS), lambda n: (0, 0)),
                  pl.BlockSpec(memory_space=pltpu.SMEM)],
        out_specs=pl.BlockSpec((batch, CHUNK, width), lambda n: (0, n, 0)),
        out_shape=jax.ShapeDtypeStruct((batch, seq, width), BF16),
        scratch_shapes=[pltpu.VMEM((batch, RET_HEADS, RET_DIM, RET_DIM), F32)],
        compiler_params=_params("arbitrary"),
        name="retention",
    )(proj3, proj3, proj3, proj3, cos, sin, decay, xi, zeta, gchunk)
    return out.reshape(T, width)


def _gelu(x):
    return 0.5 * x * (1.0 + lax.erf(x * (2.0 ** -0.5)))


def _gmlp_kernel(u_ref, v_ref, lng_ref, lnb_ref, ws_ref, bs_ref, o_ref):
    v = _gelu(v_ref[...].astype(F32))
    mu = jnp.mean(v, axis=-1, keepdims=True)
    var = jnp.mean(jnp.square(v - mu), axis=-1, keepdims=True)
    vn = ((v - mu) * lax.rsqrt(var + RMS_EPS) * lng_ref[...] + lnb_ref[...]).astype(BF16)
    row = lax.broadcasted_iota(jnp.int32, (CHUNK, CHUNK), 0)
    col = lax.broadcasted_iota(jnp.int32, (CHUNK, CHUNK), 1)
    for g in range(GMLP_GROUPS):
        cols = slice(g * GMLP_GROUP_DIM, (g + 1) * GMLP_GROUP_DIM)
        w = jnp.where(row >= col, ws_ref[g], 0.0).astype(BF16)
        mixed = _dot(w, vn[:, cols]) + bs_ref[:, g:g + 1]
        u = _gelu(u_ref[:, cols].astype(F32))
        o_ref[:, cols] = (u * mixed).astype(o_ref.dtype)


def gmlp(proj, ln_g, ln_b, w_s, b_s):
    T = proj.shape[0]
    width = GMLP_GROUPS * GMLP_GROUP_DIM
    return pl.pallas_call(
        _gmlp_kernel,
        grid=(T // CHUNK,),
        in_specs=[pl.BlockSpec((CHUNK, width), lambda i: (i, COL_GU // width)),
                  pl.BlockSpec((CHUNK, width), lambda i: (i, COL_GV // width)),
                  pl.BlockSpec((1, width), lambda i: (0, 0)),
                  pl.BlockSpec((1, width), lambda i: (0, 0)),
                  pl.BlockSpec((GMLP_GROUPS, CHUNK, CHUNK), lambda i: (0, 0, 0)),
                  pl.BlockSpec((CHUNK, GMLP_GROUPS), lambda i: (0, 0))],
        out_specs=pl.BlockSpec((CHUNK, width), lambda i: (i, 0)),
        out_shape=jax.ShapeDtypeStruct((T, width), BF16),
        compiler_params=_params("parallel"),
        name="gmlp",
    )(proj, proj, ln_g.reshape(1, width), ln_b.reshape(1, width), w_s, b_s.T)


def _dsa_prep_kernel(dq_ref, dc_ref, iq_ref, g_ref, wuk_ref, qabs_ref, iqp_ref, ckv_ref, ckvt_ref):
    c = _rms(dc_ref[...].astype(F32), g_ref[...])
    ckv_ref[...] = c.astype(ckv_ref.dtype)
    ckvt_ref[0, :DSA_LATENT, :] = c.T.astype(ckvt_ref.dtype)
    ckvt_ref[0, DSA_LATENT:, :] = jnp.ones((ROW_TILE, DSA_BLOCK), ckvt_ref.dtype)
    zeros = jnp.zeros((DSA_BLOCK, LANES - IDX_DIM), iqp_ref.dtype)
    for h in range(DSA_HEADS):
        q = dq_ref[:, h * DSA_HEAD_DIM:(h + 1) * DSA_HEAD_DIM]
        qabs_ref[0, h] = (_dot_nt(q, wuk_ref[h]) * (DSA_HEAD_DIM ** -0.5 * LOG2_E)).astype(qabs_ref.dtype)
    for h in range(IDX_HEADS):
        iqp_ref[0, h] = jnp.concatenate([iq_ref[:, h * IDX_DIM:(h + 1) * IDX_DIM], zeros], axis=-1)


def dsa_prep(proj, kv_norm_g, w_uk):
    T = proj.shape[0]
    nblk = T // DSA_BLOCK
    qw = DSA_HEADS * DSA_HEAD_DIM
    iq_w = IDX_HEADS * IDX_DIM
    return pl.pallas_call(
        _dsa_prep_kernel,
        grid=(nblk,),
        in_specs=[pl.BlockSpec((DSA_BLOCK, qw), lambda i: (i, COL_DQ // qw)),
                  pl.BlockSpec((DSA_BLOCK, DSA_LATENT), lambda i: (i, COL_DC // DSA_LATENT)),
                  pl.BlockSpec((DSA_BLOCK, iq_w), lambda i: (i, COL_IQ // iq_w)),
                  pl.BlockSpec((1, DSA_LATENT), lambda i: (0, 0)),
                  pl.BlockSpec((DSA_HEADS, DSA_LATENT, DSA_HEAD_DIM), lambda i: (0, 0, 0))],
        out_specs=[pl.BlockSpec((1, DSA_HEADS, DSA_BLOCK, DSA_LATENT), lambda i: (i, 0, 0, 0)),
                   pl.BlockSpec((1, IDX_HEADS, DSA_BLOCK, LANES), lambda i: (i, 0, 0, 0)),
                   pl.BlockSpec((DSA_BLOCK, DSA_LATENT), lambda i: (i, 0)),
                   pl.BlockSpec((1, DSA_LATENT + ROW_TILE, DSA_BLOCK), lambda i: (i, 0, 0))],
        out_shape=[jax.ShapeDtypeStruct((nblk, DSA_HEADS, DSA_BLOCK, DSA_LATENT), BF16),
                   jax.ShapeDtypeStruct((nblk, IDX_HEADS, DSA_BLOCK, LANES), BF16),
                   jax.ShapeDtypeStruct((T, DSA_LATENT), BF16),
                   jax.ShapeDtypeStruct((nblk, DSA_LATENT + ROW_TILE, DSA_BLOCK), BF16)],
        compiler_params=_params("parallel"),
        name="dsa_prep",
    )(proj, proj, proj, kv_norm_g.reshape(1, DSA_LATENT), w_uk)


def _dsa_kernel(iqp_ref, wq_ref, kidx_ref, ckv_ref, ckvt_ref, qa_ref, wuv_ref, o_ref,
                score_ref, m_ref, acc_ref, *, k_sel):
    QB = DSA_BLOCK
    blk = pl.program_id(1)
    n_tiles = blk + 1
    ksel_f = float(k_sel)
    key_in_tile = lax.broadcasted_iota(jnp.int32, (QB, QB), 0)
    query_in_blk = lax.broadcasted_iota(jnp.int32, (QB, QB), 1)

    def fold8(x, op):
        return op(x.reshape(QB // 8, 8, QB), axis=0)

    def tile_rows(j):
        return pl.ds(pl.multiple_of(j * QB, QB), QB)

    wi_t = wq_ref[...].astype(F32).T * (IDX_HEADS ** -0.5 * IDX_DIM ** -0.5)

    def score_body(j, stats):
        s_max, s_min, n_ge0, n_gt0 = stats
        kt = kidx_ref[tile_rows(j), :]
        sc = jnp.zeros((QB, QB), F32)
        for h in range(IDX_HEADS):
            lg = _dot_nt(kt, iqp_ref[0, h])
            sc = sc + wi_t[IDX_DIM + h:IDX_DIM + h + 1, :] * jnp.maximum(lg, 0.0)
        causal = (j < blk) | (key_in_tile <= query_in_blk)
        masked = jnp.where(causal, sc, -jnp.inf)
        score_ref[j] = masked
        return (jnp.maximum(s_max, fold8(masked, jnp.max)),
                jnp.minimum(s_min, fold8(jnp.where(causal, sc, jnp.inf), jnp.min)),
                n_ge0 + fold8((masked >= 0.0).astype(F32), jnp.sum),
                n_gt0 + fold8((masked > 0.0).astype(F32), jnp.sum))

    stats = lax.fori_loop(0, n_tiles, score_body,
                          (jnp.full((8, QB), -jnp.inf, F32), jnp.full((8, QB), jnp.inf, F32),
                           jnp.zeros((8, QB), F32), jnp.zeros((8, QB), F32)))
    hi0 = jnp.max(stats[0], axis=0, keepdims=True)
    lo0 = jnp.min(stats[1], axis=0, keepdims=True)
    cnt_ge0 = jnp.sum(stats[2], axis=0, keepdims=True)
    cnt_gt0 = jnp.sum(stats[3], axis=0, keepdims=True)

    COUNT_LANES = 4

    def count_negative(diff):
        def body(j, acc):
            sign = lax.shift_right_logical(lax.bitcast_convert_type(diff(score_ref[j]), jnp.uint32),
                                           jnp.uint32(31))
            return acc + jnp.sum(sign.astype(jnp.int32).reshape(-1, COUNT_LANES, 8, QB), axis=0)
        acc = lax.fori_loop(0, n_tiles, body, jnp.zeros((COUNT_LANES, 8, QB), jnp.int32))
        return jnp.sum(acc.reshape(COUNT_LANES * 8, QB), axis=0, keepdims=True).astype(F32)

    n_stored = (n_tiles * QB).astype(F32)

    def count_ge(t):
        return n_stored - count_negative(lambda s: s - t)

    n_causal = (blk * QB + 1 + lax.broadcasted_iota(jnp.int32, (1, QB), 1)).astype(F32)
    need = n_causal > ksel_f
    cnt_hi0 = count_ge(hi0)
    above0 = cnt_gt0 >= ksel_f
    below0 = cnt_ge0 < ksel_f
    lo0 = jnp.where(below0, lo0, 0.0)
    cnt_lo0 = jnp.where(below0, n_causal, cnt_ge0)
    hi0 = jnp.where(above0, hi0, 0.0)
    at_top = above0 & (cnt_hi0 >= ksel_f)
    lo0 = jnp.where(at_top, hi0, lo0)
    cnt_lo0 = jnp.where(at_top, cnt_hi0, cnt_lo0)
    active0 = need & (above0 | below0) & (cnt_lo0 != ksel_f) & (lo0 < hi0)

    def bisect_cond(state):
        return jnp.max(state[3]) > 0.0

    def bisect_body(state):
        lo, hi, cnt_lo, active = state
        mid = lo + 0.5 * (hi - lo)
        c = count_ge(mid)
        ge = c >= ksel_f
        live = active > 0.0
        moved = (mid > lo) & (mid < hi)
        new_lo = jnp.where(live & ge, mid, lo)
        new_hi = jnp.where(live & ~ge, mid, hi)
        new_cnt = jnp.where(live & ge, c, cnt_lo)
        new_active = live & moved & (new_cnt != ksel_f)
        return new_lo, new_hi, new_cnt, new_active.astype(F32)

    lo = lax.while_loop(bisect_cond, bisect_body, (lo0, hi0, cnt_lo0, active0.astype(F32)))[0]
    thr = jnp.where(need, lo, -jnp.inf)
    n_gt = count_negative(lambda s: thr - s)
    tie_quota = jnp.where(need, ksel_f - n_gt, 0.0)

    m_ref[...] = jnp.full_like(m_ref, NEG_BIG)
    acc_ref[...] = jnp.zeros_like(acc_ref)
    strict_lower = (query_in_blk < key_in_tile).astype(BF16)

    def attn_body(j, tie_seen):
        sc = score_ref[j]
        eq = sc == thr
        eq_f = eq.astype(F32)
        prefix = _dot(strict_lower, eq_f.astype(BF16)) + tie_seen
        sel = (sc > thr) | (eq & (prefix < tie_quota))
        bias = jnp.where(sel, 0.0, NEG_BIG)
        c_t = ckv_ref[tile_rows(j), :]
        ct_t = ckvt_ref[j]
        for h in range(DSA_HEADS):
            s = _dot_nt(c_t, qa_ref[0, h]) + bias
            m_old = m_ref[h]
            m_new = jnp.maximum(m_old, jnp.max(fold8(s, jnp.max), axis=0, keepdims=True))
            alpha = jnp.exp2(m_old - m_new)
            p = jnp.exp2(s - m_new)
            acc_ref[h] = alpha * acc_ref[h] + _dot(ct_t, p.astype(BF16))
            m_ref[h] = m_new
        return tie_seen + jnp.sum(fold8(eq_f, jnp.sum), axis=0, keepdims=True)

    lax.fori_loop(0, n_tiles, attn_body, jnp.zeros((1, QB), F32))

    for h in range(DSA_HEADS):
        acc = acc_ref[h]
        o_lat = (acc[:DSA_LATENT] / acc[DSA_LATENT:DSA_LATENT + 1]).T.astype(BF16)
        o_ref[:, h * DSA_HEAD_DIM:(h + 1) * DSA_HEAD_DIM] = _dot(o_lat, wuv_ref[h]).astype(o_ref.dtype)


def dsa_attention(proj, q_abs, iq_pad, ckv, ckv_t, w_uv, batch, seq):
    T = proj.shape[0]
    nb = seq // DSA_BLOCK
    k_sel = min(DSA_TOPK_MAX, seq // 4)
    out_w = DSA_HEADS * DSA_HEAD_DIM
    return pl.pallas_call(
        functools.partial(_dsa_kernel, k_sel=k_sel),
        grid=(batch, nb),
        in_specs=[pl.BlockSpec((1, IDX_HEADS, DSA_BLOCK, LANES), lambda b, i: (b * nb + i, 0, 0, 0)),
                  pl.BlockSpec((DSA_BLOCK, LANES), lambda b, i: (b * nb + i, COL_IKW // LANES)),
                  pl.BlockSpec((seq, LANES), lambda b, i: (b, COL_IKW // LANES)),
                  pl.BlockSpec((seq, DSA_LATENT), lambda b, i: (b, 0)),
                  pl.BlockSpec((nb, DSA_LATENT + ROW_TILE, DSA_BLOCK), lambda b, i: (b, 0, 0)),
                  pl.BlockSpec((1, DSA_HEADS, DSA_BLOCK, DSA_LATENT), lambda b, i: (b * nb + i, 0, 0, 0)),
                  pl.BlockSpec((DSA_HEADS, DSA_LATENT, DSA_HEAD_DIM), lambda b, i: (0, 0, 0))],
        out_specs=pl.BlockSpec((DSA_BLOCK, out_w), lambda b, i: (b * nb + i, 0)),
        out_shape=jax.ShapeDtypeStruct((T, out_w), BF16),
        scratch_shapes=[pltpu.VMEM((nb, DSA_BLOCK, DSA_BLOCK), F32),
                        pltpu.VMEM((DSA_HEADS, 1, DSA_BLOCK), F32),
                        pltpu.VMEM((DSA_HEADS, DSA_LATENT + ROW_TILE, DSA_BLOCK), F32)],
        compiler_params=_params("parallel", "arbitrary"),
        name="dsa_attention",
    )(iq_pad, proj, proj, ckv, ckv_t, q_abs, w_uv)


def _merge_kernel(ret_ref, gm_ref, ds_ref, wr_ref, wg_ref, wd_ref, ga_ref, gb_ref, gc_ref, o_ref):
    def branch(a_ref, w_ref, gate_ref):
        return jax.nn.sigmoid(gate_ref[...].astype(F32)) * _dot(a_ref[...], w_ref[...])

    merged = branch(ret_ref, wr_ref, ga_ref) + branch(gm_ref, wg_ref, gb_ref) + branch(ds_ref, wd_ref, gc_ref)
    o_ref[...] = merged.astype(o_ref.dtype)


def merge_branches(ret, gm, ds, w_ret_o, w_gmlp_o, w_dsa_o, proj, d_model, *, tm=1024, tn=512):
    T, K = ret.shape
    gate_blk = COL_GATES // tn
    per_gate = d_model // tn
    act = pl.BlockSpec((tm, K), lambda i, j: (i, 0))
    wgt = pl.BlockSpec((K, tn), lambda i, j: (0, j))

    def gate(which):
        return pl.BlockSpec((tm, tn), lambda i, j: (i, gate_blk + which * per_gate + j))

    return pl.pallas_call(
        _merge_kernel,
        grid=(T // tm, d_model // tn),
        in_specs=[act, act, act, wgt, wgt, wgt, gate(0), gate(1), gate(2)],
        out_specs=pl.BlockSpec((tm, tn), lambda i, j: (i, j)),
        out_shape=jax.ShapeDtypeStruct((T, d_model), BF16),
        compiler_params=_params("parallel", "arbitrary"),
        name="merge_branches",
    )(ret, gm, ds, w_ret_o, w_gmlp_o, w_dsa_o, proj, proj, proj)


def _out_proj_kernel(a_ref, w_ref, x_ref, o_ref):
    o_ref[...] = x_ref[...] + _dot(a_ref[...], w_ref[...])


def out_proj_residual(merged, w_out, x, *, tm=1024, tn=512):
    T, K = merged.shape
    N = w_out.shape[1]
    return pl.pallas_call(
        _out_proj_kernel,
        grid=(T // tm, N // tn),
        in_specs=[pl.BlockSpec((tm, K), lambda i, j: (i, 0)),
                  pl.BlockSpec((K, tn), lambda i, j: (0, j)),
                  pl.BlockSpec((tm, tn), lambda i, j: (i, j))],
        out_specs=pl.BlockSpec((tm, tn), lambda i, j: (i, j)),
        out_shape=jax.ShapeDtypeStruct((T, N), F32),
        compiler_params=_params("parallel", "arbitrary"),
        name="out_proj_residual",
    )(merged, w_out, x)


def _pack_rows(h_bf16, o_ref):
    n, d = h_bf16.shape
    bits = lax.bitcast_convert_type(h_bf16.astype(F32), jnp.uint32)
    packed = (bits[:, :d // 2] >> 16) | (bits[:, d // 2:] & jnp.uint32(0xFFFF0000))
    for c in range(d // 2 // LANES):
        o_ref[pl.ds(c, n, stride=ROW_TILE), :] = packed[:, c * LANES:(c + 1) * LANES]


def _unpack_rows(x_ref, o_ref):
    n, d = o_ref.shape
    for c in range(d // 2 // LANES):
        w = x_ref[pl.ds(c, n, stride=ROW_TILE), :]
        lo = lax.bitcast_convert_type(w << 16, F32)
        hi = lax.bitcast_convert_type(w & jnp.uint32(0xFFFF0000), F32)
        o_ref[:, c * LANES:(c + 1) * LANES] = lo.astype(o_ref.dtype)
        o_ref[:, d // 2 + c * LANES:d // 2 + (c + 1) * LANES] = hi.astype(o_ref.dtype)


def _router_kernel(x_ref, g_ref, w_ref, b_ref, hp_ref, ids_ref, wts_ref):
    h = _rms(x_ref[...], g_ref[...]).astype(BF16)
    _pack_rows(h, hp_ref)
    logits = _dot(h, w_ref[...]) + b_ref[...]
    lane = lax.broadcasted_iota(jnp.int32, logits.shape, 1)
    big = jnp.int32(LANES)

    def first_argmax(vals, valid):
        v = jnp.where(valid, vals, -jnp.inf)
        m = jnp.max(v, axis=-1, keepdims=True)
        idx = jnp.min(jnp.where(valid & (v == m), lane, big), axis=-1, keepdims=True)
        return m, idx

    is_grp = lane < MOE_GROUPS
    g_max, g_sel = first_argmax(logits, is_grp)
    g_den = jnp.sum(jnp.where(is_grp, jnp.exp(logits - g_max), 0.0), axis=-1, keepdims=True)
    p_g = 1.0 / g_den
    e_lo = MOE_GROUPS + g_sel * MOE_EXPERTS_PER_GROUP
    in_grp = (lane >= e_lo) & (lane < e_lo + MOE_EXPERTS_PER_GROUP)
    m1, i1 = first_argmax(logits, in_grp)
    m2, i2 = first_argmax(logits, in_grp & (lane != i1))
    e2 = jnp.exp(m2 - m1)
    w1 = p_g / (1.0 + e2)
    w2 = p_g * e2 / (1.0 + e2)
    ids_ref[...] = jnp.where(lane == 0, i1 - MOE_GROUPS, jnp.where(lane == 1, i2 - MOE_GROUPS, 0))
    wts_ref[...] = jnp.where(lane == 0, w1, jnp.where(lane == 1, w2, 0.0))


def moe_router(x, g, w_router, b_router, *, tm=512):
    T, D = x.shape
    assert D == 2 * ROW_TILE * LANES
    return pl.pallas_call(
        _router_kernel,
        grid=(T // tm,),
        in_specs=[pl.BlockSpec((tm, D), lambda i: (i, 0)),
                  pl.BlockSpec((1, D), lambda i: (0, 0)),
                  pl.BlockSpec((D, LANES), lambda i: (0, 0)),
                  pl.BlockSpec((1, LANES), lambda i: (0, 0))],
        out_specs=[pl.BlockSpec((tm * ROW_TILE, LANES), lambda i: (i, 0)),
                   pl.BlockSpec((tm, LANES), lambda i: (i, 0)),
                   pl.BlockSpec((tm, LANES), lambda i: (i, 0))],
        out_shape=[jax.ShapeDtypeStruct((T * ROW_TILE, LANES), jnp.uint32),
                   jax.ShapeDtypeStruct((T, LANES), jnp.int32),
                   jax.ShapeDtypeStruct((T, LANES), F32)],
        compiler_params=_params("parallel"),
        name="moe_router",
    )(x, g, w_router, b_router)


def _route_tables(ids, n_rows):
    T = ids.shape[0]
    e = ids[:, :MOE_TOPK].reshape(-1)
    onehot = (e[:, None] == jnp.arange(MOE_EXPERTS, dtype=jnp.int32)[None, :]).astype(jnp.int32)
    rank = jnp.sum((jnp.cumsum(onehot, axis=0) - onehot) * onehot, axis=1)
    counts = jnp.sum(onehot, axis=0)
    padded = (counts + MOE_TILE - 1) // MOE_TILE * MOE_TILE
    ends = jnp.cumsum(padded)
    pos = ((ends - padded)[e] + rank).astype(jnp.int32)
    tile_start = jnp.arange(n_rows // MOE_TILE, dtype=jnp.int32) * MOE_TILE
    tile_expert = jnp.sum((ends[None, :] <= tile_start[:, None]).astype(jnp.int32), axis=1)
    tile_expert = jnp.minimum(tile_expert, MOE_EXPERTS - 1)
    n_used = (ends[-1] // MOE_TILE).astype(jnp.int32).reshape(1)
    return pos, tile_expert, n_used


def _token_rows(r, rows_per_token):
    return pl.ds(pl.multiple_of(r * rows_per_token, rows_per_token), rows_per_token)


def _scatter_rows_kernel(pos_ref, hp_ref, xs_in_hbm, xs_hbm, sem):
    del xs_in_hbm
    n = hp_ref.shape[0] // ROW_TILE
    base = pl.program_id(0) * n
    for slot in range(MOE_TOPK):
        def issue(r, carry, slot=slot):
            row = pos_ref[(base + r) * MOE_TOPK + slot]
            pltpu.make_async_copy(hp_ref.at[_token_rows(r, ROW_TILE)],
                                  xs_hbm.at[_token_rows(row, ROW_TILE)], sem.at[slot]).start()
            return carry
        lax.fori_loop(0, n, issue, 0, unroll=DMA_ISSUE_UNROLL)
    for slot in range(MOE_TOPK):
        pltpu.make_async_copy(hp_ref, xs_hbm.at[pl.ds(0, n * ROW_TILE)], sem.at[slot]).wait()


def scatter_rows(hp, pos, n_rows, *, tile=256):
    T = hp.shape[0] // ROW_TILE
    return pl.pallas_call(
        _scatter_rows_kernel,
        grid_spec=pltpu.PrefetchScalarGridSpec(
            num_scalar_prefetch=1,
            grid=(T // tile,),
            in_specs=[pl.BlockSpec((tile * ROW_TILE, LANES), lambda i, pos: (i, 0)),
                      pl.BlockSpec(memory_space=pl.ANY)],
            out_specs=pl.BlockSpec(memory_space=pl.ANY),
            scratch_shapes=[pltpu.SemaphoreType.DMA((MOE_TOPK,))]),
        out_shape=jax.ShapeDtypeStruct((n_rows * ROW_TILE, LANES), jnp.uint32),
        input_output_aliases={2: 0},
        compiler_params=_params("arbitrary"),
        name="moe_scatter_rows",
    )(pos, hp, jnp.zeros((n_rows * ROW_TILE, LANES), jnp.uint32))


def _moe_experts_kernel(tile_expert_ref, n_used_ref, x_ref, w1_ref, w3_ref, w2_ref, o_ref,
                        xb_ref, w1b_ref, w3b_ref, w2b_ref):
    i = pl.program_id(0)

    @pl.when((i == 0) | (tile_expert_ref[i] != tile_expert_ref[jnp.maximum(i - 1, 0)]))
    def _():
        w1b_ref[...] = w1_ref[0].astype(BF16)
        w3b_ref[...] = w3_ref[0].astype(BF16)
        w2b_ref[...] = w2_ref[0].astype(BF16)

    @pl.when(i < n_used_ref[0])
    def _():
        _unpack_rows(x_ref, xb_ref)
        x = xb_ref[...]
        a = _dot(x, w1b_ref[...])
        hid = a * jax.nn.sigmoid(a) * _dot(x, w3b_ref[...])
        o_ref[...] = _dot(hid.astype(BF16), w2b_ref[...])

    @pl.when(i >= n_used_ref[0])
    def _():
        o_ref[...] = jnp.zeros_like(o_ref)


def moe_experts(xs, tile_expert, n_used, w1, w3, w2):
    E, D, F = w1.shape
    P = xs.shape[0] // ROW_TILE

    def wspec(shape):
        return pl.BlockSpec((1,) + shape, lambda i, te, nu: (te[i], 0, 0))

    return pl.pallas_call(
        _moe_experts_kernel,
        grid_spec=pltpu.PrefetchScalarGridSpec(
            num_scalar_prefetch=2,
            grid=(P // MOE_TILE,),
            in_specs=[pl.BlockSpec((MOE_TILE * ROW_TILE, LANES), lambda i, te, nu: (i, 0)),
                      wspec((D, F)), wspec((D, F)), wspec((F, D))],
            out_specs=pl.BlockSpec((MOE_TILE, D), lambda i, te, nu: (i, 0)),
            scratch_shapes=[pltpu.VMEM((MOE_TILE, D), BF16), pltpu.VMEM((D, F), BF16),
                            pltpu.VMEM((D, F), BF16), pltpu.VMEM((F, D), BF16)]),
        out_shape=jax.ShapeDtypeStruct((P, D), F32),
        compiler_params=_params("arbitrary"),
        name="moe_experts",
    )(tile_expert, n_used, xs, w1, w3, w2)


def _moe_combine_kernel(pos_ref, ys_hbm, x_ref, wts_ref, g_ref, o_ref, y0_ref, y1_ref, sem, *,
                        final_norm):
    n = x_ref.shape[0]
    base = pl.program_id(0) * n
    for slot, y_ref in enumerate((y0_ref, y1_ref)):
        def issue(r, carry, slot=slot, y_ref=y_ref):
            row = pos_ref[(base + r) * MOE_TOPK + slot]
            pltpu.make_async_copy(ys_hbm.at[pl.ds(row, 1)], y_ref.at[pl.ds(r, 1)], sem.at[slot]).start()
            return carry
        lax.fori_loop(0, n, issue, 0, unroll=DMA_ISSUE_UNROLL)
    for slot, y_ref in enumerate((y0_ref, y1_ref)):
        pltpu.make_async_copy(ys_hbm.at[pl.ds(0, n)], y_ref, sem.at[slot]).wait()
    out = x_ref[...] + (wts_ref[:, 0:1] * y0_ref[...] + wts_ref[:, 1:2] * y1_ref[...])
    o_ref[...] = _rms(out, g_ref[...]) if final_norm else out


def moe_combine(ys, pos, wts, x, g, *, final_norm, tile=256):
    T, D = x.shape
    return pl.pallas_call(
        functools.partial(_moe_combine_kernel, final_norm=final_norm),
        grid_spec=pltpu.PrefetchScalarGridSpec(
            num_scalar_prefetch=1,
            grid=(T // tile,),
            in_specs=[pl.BlockSpec(memory_space=pl.ANY),
                      pl.BlockSpec((tile, D), lambda i, pos: (i, 0)),
                      pl.BlockSpec((tile, LANES), lambda i, pos: (i, 0)),
                      pl.BlockSpec((1, D), lambda i, pos: (0, 0))],
            out_specs=pl.BlockSpec((tile, D), lambda i, pos: (i, 0)),
            scratch_shapes=[pltpu.VMEM((tile, D), F32), pltpu.VMEM((tile, D), F32),
                            pltpu.SemaphoreType.DMA((MOE_TOPK,))]),
        out_shape=jax.ShapeDtypeStruct((T, D), F32),
        compiler_params=_params("arbitrary"),
        name="moe_combine",
    )(pos, ys, x, wts, g)


def _pack_w_in(w):
    D = w.shape[0]
    widths = (1024, 1024, 1024, 1024, 1024, 1024, 1024, 256, 512, 64, 8, D, D, D)
    offs = [0]
    for wd in widths:
        offs.append(offs[-1] + wd)
    rq, rk, rv, rg, gu, gv, dq, dc, iq, ik, iw, ga, gb, gc = (
        w[:, offs[i]:offs[i + 1]] for i in range(len(widths)))
    ikw_pad = jnp.zeros((D, COL_GATES - COL_IKW - IDX_DIM - IDX_HEADS), w.dtype)
    packed = jnp.concatenate([rq, rk, rv, rg, gu, gv, dq, iq, dc, ik, iw, ikw_pad, ga, gb, gc], axis=1)
    assert packed.shape[1] == PROJ_WIDTH
    return packed.astype(BF16)


def _pack_router(w_group, b_group, w_expert, b_expert):
    D = w_group.shape[0]
    w_e = jnp.transpose(w_expert, (1, 0, 2)).reshape(D, MOE_EXPERTS)
    pad = LANES - MOE_GROUPS - MOE_EXPERTS
    w = jnp.concatenate([w_group, w_e, jnp.zeros((D, pad), w_group.dtype)], axis=1).astype(BF16)
    b = jnp.concatenate([b_group, b_expert.reshape(MOE_EXPERTS), jnp.zeros((pad,), b_group.dtype)])
    return w, b.reshape(1, LANES).astype(F32)


def kernel(x, norm_mix_g, w_in, w_ret_o, gmlp_ln_g, gmlp_ln_b, gmlp_w_s, gmlp_b_s, w_gmlp_o, dsa_kv_norm_g, dsa_w_uk, dsa_w_uv, w_dsa_o, w_out, norm_ffn_g, moe_w_group, moe_b_group, moe_w_expert, moe_b_expert, moe_w1, moe_w3, moe_w2, final_norm_g):
    B, S, D = x.shape
    depth = w_in.shape[0]
    assert S % DSA_BLOCK == 0 and D == 2048
    xt = x.reshape(B * S, D)
    for l in range(depth):
        proj = norm_matmul(xt, norm_mix_g[l].reshape(1, D), _pack_w_in(w_in[l]))
        ret = retention(proj, B, S)
        gm = gmlp(proj, gmlp_ln_g[l], gmlp_ln_b[l], gmlp_w_s[l], gmlp_b_s[l])
        q_abs, iq_pad, ckv, ckv_t = dsa_prep(proj, dsa_kv_norm_g[l], dsa_w_uk[l].astype(BF16))
        ds = dsa_attention(proj, q_abs, iq_pad, ckv, ckv_t, dsa_w_uv[l].astype(BF16), B, S)
        merged = merge_branches(ret, gm, ds, w_ret_o[l].astype(BF16), w_gmlp_o[l].astype(BF16),
                                w_dsa_o[l].astype(BF16), proj, D)
        xt = out_proj_residual(merged, w_out[l].astype(BF16), xt)
        w_router, b_router = _pack_router(moe_w_group[l], moe_b_group[l], moe_w_expert[l], moe_b_expert[l])
        hp, ids, wts = moe_router(xt, norm_ffn_g[l].reshape(1, D), w_router, b_router)
        n_rows = MOE_TOPK * B * S + MOE_EXPERTS * MOE_TILE
        pos, tile_expert, n_used = _route_tables(ids, n_rows)
        xs = scatter_rows(hp, pos, n_rows)
        F = moe_w1.shape[-1]
        ys = moe_experts(xs, tile_expert + l * MOE_EXPERTS, n_used,
                         moe_w1.reshape(depth * MOE_EXPERTS, D, F),
                         moe_w3.reshape(depth * MOE_EXPERTS, D, F),
                         moe_w2.reshape(depth * MOE_EXPERTS, F, D))
        last = l == depth - 1
        xt = moe_combine(ys, pos, wts, xt, final_norm_g.reshape(1, D), final_norm=last)
    return xt.reshape(B, S, D)
```

```python
import functools
import math

import jax
import jax.numpy as jnp
from jax import lax
from jax.experimental import pallas as pl
from jax.experimental.pallas import tpu as pltpu

F32 = jnp.float32
BF16 = jnp.bfloat16

RMS_EPS = 1e-6
RET_HEADS = 4
RET_DIM = 256
CHUNK = 128
ROPE_BASE = 10000.0
GMLP_GROUPS = 4
GMLP_GROUP_DIM = 256
DSA_HEADS = 8
DSA_HEAD_DIM = 128
DSA_LATENT = 256
IDX_HEADS = 8
IDX_DIM = 64
DSA_TOPK_MAX = 256
DSA_BLOCK = 256
MOE_GROUPS = 4
MOE_EXPERTS_PER_GROUP = 8
MOE_EXPERTS = MOE_GROUPS * MOE_EXPERTS_PER_GROUP
MOE_HIDDEN = 512
MOE_TOPK = 2
MOE_TILE = 512
LANES = 128
ROW_TILE = 8
DMA_ISSUE_UNROLL = 8
NEG_BIG = -1e30
LOG2_E = math.log2(math.e)

COL_RQ, COL_RK, COL_RV, COL_RG = 0, 1024, 2048, 3072
COL_GU, COL_GV = 4096, 5120
COL_DQ, COL_IQ, COL_DC, COL_IKW = 6144, 7168, 7680, 7936
COL_GATES = 8192
PROJ_WIDTH = 14336
VMEM_LIMIT = 56 * 1024 * 1024


def _params(*sem):
    return pltpu.CompilerParams(dimension_semantics=sem, vmem_limit_bytes=VMEM_LIMIT)


def _rms(x, g):
    return x * lax.rsqrt(jnp.mean(x * x, axis=-1, keepdims=True) + RMS_EPS) * g


def _dot(a, b):
    return jnp.dot(a, b, preferred_element_type=F32)


def _dot_nt(a, b):
    return lax.dot_general(a, b, (((1,), (1,)), ((), ())), preferred_element_type=F32)


def _norm_matmul_kernel(x_ref, g_ref, w_ref, o_ref, h_ref):
    @pl.when(pl.program_id(1) == 0)
    def _():
        h_ref[...] = _rms(x_ref[...], g_ref[...]).astype(BF16)

    o_ref[...] = _dot(h_ref[...], w_ref[...]).astype(o_ref.dtype)


def norm_matmul(x, g, w, *, tm=1024, tn=512, out_dtype=BF16):
    T, D = x.shape
    N = w.shape[1]
    return pl.pallas_call(
        _norm_matmul_kernel,
        grid=(T // tm, N // tn),
        in_specs=[pl.BlockSpec((tm, D), lambda i, j: (i, 0)),
                  pl.BlockSpec((1, D), lambda i, j: (0, 0)),
                  pl.BlockSpec((D, tn), lambda i, j: (0, j))],
        out_specs=pl.BlockSpec((tm, tn), lambda i, j: (i, j)),
        out_shape=jax.ShapeDtypeStruct((T, N), out_dtype),
        scratch_shapes=[pltpu.VMEM((tm, D), BF16)],
        compiler_params=_params("parallel", "arbitrary"),
        name="norm_in_proj",
    )(x, g, w)


def _retention_kernel(q_ref, k_ref, v_ref, gate_ref, cos_ref, sin_ref, decay_ref, xi_ref,
                      zeta_ref, gchunk_ref, o_ref, r_ref):
    @pl.when(pl.program_id(0) == 0)
    def _():
        r_ref[...] = jnp.zeros_like(r_ref)

    cos = cos_ref[...]
    sin = sin_ref[...]
    half = RET_DIM // 2

    def rotary(t):
        t1, t2 = t[:, :half], t[:, half:]
        return jnp.concatenate([t1 * cos - t2 * sin, t1 * sin + t2 * cos], axis=-1)

    for b in range(q_ref.shape[0]):
        for h in range(RET_HEADS):
            cols = slice(h * RET_DIM, (h + 1) * RET_DIM)
            q = rotary(q_ref[b, :, cols].astype(F32)).astype(BF16)
            k = rotary(k_ref[b, :, cols].astype(F32)) * (RET_DIM ** -0.5)
            v = v_ref[b, :, cols]
            inner = _dot_nt(q, k.astype(BF16)) * decay_ref[h]
            r_old = r_ref[b, h]
            o = _dot(inner.astype(BF16), v) + _dot(q, r_old.astype(BF16)) * xi_ref[:, h:h + 1]
            kz = (k * zeta_ref[:, h:h + 1]).astype(BF16)
            r_ref[b, h] = r_old * gchunk_ref[h] + _dot(kz.T, v)
            o = o * lax.rsqrt(jnp.mean(o * o, axis=-1, keepdims=True) + RMS_EPS)
            gate = gate_ref[b, :, cols].astype(F32)
            o_ref[b, :, cols] = (gate * jax.nn.sigmoid(gate) * o).astype(o_ref.dtype)


def retention(proj, batch, seq):
    T = proj.shape[0]
    n_chunks = seq // CHUNK
    width = RET_HEADS * RET_DIM
    half = RET_DIM // 2
    inv = ROPE_BASE ** (-jnp.arange(half, dtype=F32) / half)
    ang = jnp.arange(seq).astype(F32)[:, None] * inv[None, :]
    cos, sin = jnp.cos(ang), jnp.sin(ang)
    log_g = jnp.log(1.0 - 2.0 ** (-5.0 - jnp.arange(RET_HEADS, dtype=F32)))
    i = jnp.arange(CHUNK, dtype=F32)
    diff = i[:, None] - i[None, :]
    decay = jnp.where(diff >= 0, jnp.exp(log_g[:, None, None] * jnp.maximum(diff, 0.0)), 0.0)
    xi = jnp.exp(log_g[None, :] * (i[:, None] + 1.0))
    zeta = jnp.exp(log_g[None, :] * (CHUNK - 1.0 - i[:, None]))
    gchunk = jnp.exp(log_g * CHUNK)

    def col_spec(col):
        return pl.BlockSpec((batch, CHUNK, width), lambda n: (0, n, col // width))

    proj3 = proj.reshape(batch, seq, proj.shape[1])
    out = pl.pallas_call(
        _retention_kernel,
        grid=(n_chunks,),
        in_specs=[col_spec(COL_RQ), col_spec(COL_RK), col_spec(COL_RV), col_spec(COL_RG),
                  pl.BlockSpec((CHUNK, half), lambda n: (n, 0)),
                  pl.BlockSpec((CHUNK, half), lambda n: (n, 0)),
                  pl.BlockSpec((RET_HEADS, CHUNK, CHUNK), lambda n: (0, 0, 0)),
                  pl.BlockSpec((CHUNK, RET_HEADS), lambda n: (0, 0)),
                  pl.BlockSpec((CHUNK, RET_HEADS), lambda n: (0, 0)),
                  pl.BlockSpec(memory_space=pltpu.SMEM)],
        out_specs=pl.BlockSpec((batch, CHUNK, width), lambda n: (0, n, 0)),
        out_shape=jax.ShapeDtypeStruct((batch, seq, width), BF16),
        scratch_shapes=[pltpu.VMEM((batch, RET_HEADS, RET_DIM, RET_DIM), F32)],
        compiler_params=_params("arbitrary"),
        name="retention",
    )(proj3, proj3, proj3, proj3, cos, sin, decay, xi, zeta, gchunk)
    return out.reshape(T, width)


def _gelu(x):
    return 0.5 * x * (1.0 + lax.erf(x * (2.0 ** -0.5)))


def _gmlp_kernel(u_ref, v_ref, lng_ref, lnb_ref, ws_ref, bs_ref, o_ref):
    v = _gelu(v_ref[...].astype(F32))
    mu = jnp.mean(v, axis=-1, keepdims=True)
    var = jnp.mean(jnp.square(v - mu), axis=-1, keepdims=True)
    vn = ((v - mu) * lax.rsqrt(var + RMS_EPS) * lng_ref[...] + lnb_ref[...]).astype(BF16)
    row = lax.broadcasted_iota(jnp.int32, (CHUNK, CHUNK), 0)
    col = lax.broadcasted_iota(jnp.int32, (CHUNK, CHUNK), 1)
    for g in range(GMLP_GROUPS):
        cols = slice(g * GMLP_GROUP_DIM, (g + 1) * GMLP_GROUP_DIM)
        w = jnp.where(row >= col, ws_ref[g], 0.0).astype(BF16)
        mixed = _dot(w, vn[:, cols]) + bs_ref[:, g:g + 1]
        u = _gelu(u_ref[:, cols].astype(F32))
        o_ref[:, cols] = (u * mixed).astype(o_ref.dtype)


def gmlp(proj, ln_g, ln_b, w_s, b_s):
    T = proj.shape[0]
    width = GMLP_GROUPS * GMLP_GROUP_DIM
    return pl.pallas_call(
        _gmlp_kernel,
        grid=(T // CHUNK,),
        in_specs=[pl.BlockSpec((CHUNK, width), lambda i: (i, COL_GU // width)),
                  pl.BlockSpec((CHUNK, width), lambda i: (i, COL_GV // width)),
                  pl.BlockSpec((1, width), lambda i: (0, 0)),
                  pl.BlockSpec((1, width), lambda i: (0, 0)),
                  pl.BlockSpec((GMLP_GROUPS, CHUNK, CHUNK), lambda i: (0, 0, 0)),
                  pl.BlockSpec((CHUNK, GMLP_GROUPS), lambda i: (0, 0))],
        out_specs=pl.BlockSpec((CHUNK, width), lambda i: (i, 0)),
        out_shape=jax.ShapeDtypeStruct((T, width), BF16),
        compiler_params=_params("parallel"),
        name="gmlp",
    )(proj, proj, ln_g.reshape(1, width), ln_b.reshape(1, width), w_s, b_s.T)


def _dsa_prep_kernel(dq_ref, dc_ref, iq_ref, g_ref, wuk_ref, qabs_ref, iqp_ref, ckv_ref, ckvt_ref):
    c = _rms(dc_ref[...].astype(F32), g_ref[...])
    ckv_ref[...] = c.astype(ckv_ref.dtype)
    ckvt_ref[0, :DSA_LATENT, :] = c.T.astype(ckvt_ref.dtype)
    ckvt_ref[0, DSA_LATENT:, :] = jnp.ones((ROW_TILE, DSA_BLOCK), ckvt_ref.dtype)
    zeros = jnp.zeros((DSA_BLOCK, LANES - IDX_DIM), iqp_ref.dtype)
    for h in range(DSA_HEADS):
        q = dq_ref[:, h * DSA_HEAD_DIM:(h + 1) * DSA_HEAD_DIM]
        qabs_ref[0, h] = (_dot_nt(q, wuk_ref[h]) * (DSA_HEAD_DIM ** -0.5 * LOG2_E)).astype(qabs_ref.dtype)
    for h in range(IDX_HEADS):
        iqp_ref[0, h] = jnp.concatenate([iq_ref[:, h * IDX_DIM:(h + 1) * IDX_DIM], zeros], axis=-1)


def dsa_prep(proj, kv_norm_g, w_uk):
    T = proj.shape[0]
    nblk = T // DSA_BLOCK
    qw = DSA_HEADS * DSA_HEAD_DIM
    iq_w = IDX_HEADS * IDX_DIM
    return pl.pallas_call(
        _dsa_prep_kernel,
        grid=(nblk,),
        in_specs=[pl.BlockSpec((DSA_BLOCK, qw), lambda i: (i, COL_DQ // qw)),
                  pl.BlockSpec((DSA_BLOCK, DSA_LATENT), lambda i: (i, COL_DC // DSA_LATENT)),
                  pl.BlockSpec((DSA_BLOCK, iq_w), lambda i: (i, COL_IQ // iq_w)),
                  pl.BlockSpec((1, DSA_LATENT), lambda i: (0, 0)),
                  pl.BlockSpec((DSA_HEADS, DSA_LATENT, DSA_HEAD_DIM), lambda i: (0, 0, 0))],
        out_specs=[pl.BlockSpec((1, DSA_HEADS, DSA_BLOCK, DSA_LATENT), lambda i: (i, 0, 0, 0)),
                   pl.BlockSpec((1, IDX_HEADS, DSA_BLOCK, LANES), lambda i: (i, 0, 0, 0)),
                   pl.BlockSpec((DSA_BLOCK, DSA_LATENT), lambda i: (i, 0)),
                   pl.BlockSpec((1, DSA_LATENT + ROW_TILE, DSA_BLOCK), lambda i: (i, 0, 0))],
        out_shape=[jax.ShapeDtypeStruct((nblk, DSA_HEADS, DSA_BLOCK, DSA_LATENT), BF16),
                   jax.ShapeDtypeStruct((nblk, IDX_HEADS, DSA_BLOCK, LANES), BF16),
                   jax.ShapeDtypeStruct((T, DSA_LATENT), BF16),
                   jax.ShapeDtypeStruct((nblk, DSA_LATENT + ROW_TILE, DSA_BLOCK), BF16)],
        compiler_params=_params("parallel"),
        name="dsa_prep",
    )(proj, proj, proj, kv_norm_g.reshape(1, DSA_LATENT), w_uk)


def _dsa_kernel(iqp_ref, wq_ref, kidx_ref, ckv_ref, ckvt_ref, qa_ref, wuv_ref, o_ref,
                score_ref, m_ref, acc_ref, *, k_sel):
    QB = DSA_BLOCK
    blk = pl.program_id(1)
    n_tiles = blk + 1
    ksel_f = float(k_sel)
    key_in_tile = lax.broadcasted_iota(jnp.int32, (QB, QB), 0)
    query_in_blk = lax.broadcasted_iota(jnp.int32, (QB, QB), 1)

    def fold8(x, op):
        return op(x.reshape(QB // 8, 8, QB), axis=0)

    def tile_rows(j):
        return pl.ds(pl.multiple_of(j * QB, QB), QB)

    wi_t = wq_ref[...].astype(F32).T * (IDX_HEADS ** -0.5 * IDX_DIM ** -0.5)

    def score_body(j, stats):
        s_max, s_min, n_ge0, n_gt0 = stats
        kt = kidx_ref[tile_rows(j), :]
        sc = jnp.zeros((QB, QB), F32)
        for h in range(IDX_HEADS):
            lg = _dot_nt(kt, iqp_ref[0, h])
            sc = sc + wi_t[IDX_DIM + h:IDX_DIM + h + 1, :] * jnp.maximum(lg, 0.0)
        causal = (j < blk) | (key_in_tile <= query_in_blk)
        masked = jnp.where(causal, sc, -jnp.inf)
        score_ref[j] = masked
        return (jnp.maximum(s_max, fold8(masked, jnp.max)),
                jnp.minimum(s_min, fold8(jnp.where(causal, sc, jnp.inf), jnp.min)),
                n_ge0 + fold8((masked >= 0.0).astype(F32), jnp.sum),
                n_gt0 + fold8((masked > 0.0).astype(F32), jnp.sum))

    stats = lax.fori_loop(0, n_tiles, score_body,
                          (jnp.full((8, QB), -jnp.inf, F32), jnp.full((8, QB), jnp.inf, F32),
                           jnp.zeros((8, QB), F32), jnp.zeros((8, QB), F32)))
    hi0 = jnp.max(stats[0], axis=0, keepdims=True)
    lo0 = jnp.min(stats[1], axis=0, keepdims=True)
    cnt_ge0 = jnp.sum(stats[2], axis=0, keepdims=True)
    cnt_gt0 = jnp.sum(stats[3], axis=0, keepdims=True)

    COUNT_LANES = 4

    def count_negative(diff):
        def body(j, acc):
            sign = lax.shift_right_logical(lax.bitcast_convert_type(diff(score_ref[j]), jnp.uint32),
                                           jnp.uint32(31))
            return acc + jnp.sum(sign.astype(jnp.int32).reshape(-1, COUNT_LANES, 8, QB), axis=0)
        acc = lax.fori_loop(0, n_tiles, body, jnp.zeros((COUNT_LANES, 8, QB), jnp.int32))
        return jnp.sum(acc.reshape(COUNT_LANES * 8, QB), axis=0, keepdims=True).astype(F32)

    n_stored = (n_tiles * QB).astype(F32)

    def count_ge(t):
        return n_stored - count_negative(lambda s: s - t)

    n_causal = (blk * QB + 1 + lax.broadcasted_iota(jnp.int32, (1, QB), 1)).astype(F32)
    need = n_causal > ksel_f
    cnt_hi0 = count_ge(hi0)
    above0 = cnt_gt0 >= ksel_f
    below0 = cnt_ge0 < ksel_f
    lo0 = jnp.where(below0, lo0, 0.0)
    cnt_lo0 = jnp.where(below0, n_causal, cnt_ge0)
    hi0 = jnp.where(above0, hi0, 0.0)
    at_top = above0 & (cnt_hi0 >= ksel_f)
    lo0 = jnp.where(at_top, hi0, lo0)
    cnt_lo0 = jnp.where(at_top, cnt_hi0, cnt_lo0)
    cnt_hi0 = jnp.where(above0, cnt_hi0, cnt_ge0)

    def unresolved(lo, hi, cnt_lo, cnt_hi):
        return (cnt_lo != ksel_f) & (lo < hi) & (cnt_lo - cnt_hi > 2.0)

    active0 = need & (above0 | below0) & unresolved(lo0, hi0, cnt_lo0, cnt_hi0)

    def bisect_cond(state):
        return jnp.max(state[4]) > 0.0

    def bisect_body(state):
        lo, hi, cnt_lo, cnt_hi, active = state
        mid = lo + 0.5 * (hi - lo)
        c = count_ge(mid)
        ge = c >= ksel_f
        live = active > 0.0
        moved = (mid > lo) & (mid < hi)
        new_lo = jnp.where(live & ge, mid, lo)
        new_hi = jnp.where(live & ~ge, mid, hi)
        new_cnt_lo = jnp.where(live & ge, c, cnt_lo)
        new_cnt_hi = jnp.where(live & ~ge, c, cnt_hi)
        new_active = live & moved & unresolved(new_lo, new_hi, new_cnt_lo, new_cnt_hi)
        return new_lo, new_hi, new_cnt_lo, new_cnt_hi, new_active.astype(F32)

    lo, hi, cnt_lo, cnt_hi, _ = lax.while_loop(
        bisect_cond, bisect_body, (lo0, hi0, cnt_lo0, cnt_hi0, active0.astype(F32)))

    def max_below_body(j, acc):
        s = score_ref[j]
        return jnp.maximum(acc, jnp.max(jnp.where(s < hi, s, -jnp.inf).reshape(-1, COUNT_LANES, 8, QB), axis=0))
    max_below = lax.fori_loop(0, n_tiles, max_below_body, jnp.full((COUNT_LANES, 8, QB), -jnp.inf, F32))
    max_below = jnp.max(max_below.reshape(COUNT_LANES * 8, QB), axis=0, keepdims=True)
    is_pair = (cnt_lo != ksel_f) & (lo < hi) & (cnt_lo - cnt_hi == 2.0)
    thr = jnp.where(need, jnp.where(is_pair, max_below, lo), -jnp.inf)
    n_gt = count_negative(lambda s: thr - s)
    tie_quota = jnp.where(need, ksel_f - n_gt, 0.0)

    m_ref[...] = jnp.full_like(m_ref, NEG_BIG)
    acc_ref[...] = jnp.zeros_like(acc_ref)
    strict_lower = (query_in_blk < key_in_tile).astype(BF16)

    def attn_body(j, tie_seen):
        sc = score_ref[j]
        eq = sc == thr
        eq_f = eq.astype(F32)
        prefix = _dot(strict_lower, eq_f.astype(BF16)) + tie_seen
        sel = (sc > thr) | (eq & (prefix < tie_quota))
        bias = jnp.where(sel, 0.0, NEG_BIG)
        c_t = ckv_ref[tile_rows(j), :]
        ct_t = ckvt_ref[j]
        for h in range(DSA_HEADS):
            s = _dot_nt(c_t, qa_ref[0, h]) + bias
            m_old = m_ref[h]
            m_new = jnp.maximum(m_old, jnp.max(fold8(s, jnp.max), axis=0, keepdims=True))
            alpha = jnp.exp2(m_old - m_new)
            p = jnp.exp2(s - m_new)
            acc_ref[h] = alpha * acc_ref[h] + _dot(ct_t, p.astype(BF16))
            m_ref[h] = m_new
        return tie_seen + jnp.sum(fold8(eq_f, jnp.sum), axis=0, keepdims=True)

    lax.fori_loop(0, n_tiles, attn_body, jnp.zeros((1, QB), F32))

    for h in range(DSA_HEADS):
        acc = acc_ref[h]
        o_lat = (acc[:DSA_LATENT] / acc[DSA_LATENT:DSA_LATENT + 1]).T.astype(BF16)
        o_ref[:, h * DSA_HEAD_DIM:(h + 1) * DSA_HEAD_DIM] = _dot(o_lat, wuv_ref[h]).astype(o_ref.dtype)


def dsa_attention(proj, q_abs, iq_pad, ckv, ckv_t, w_uv, batch, seq):
    T = proj.shape[0]
    nb = seq // DSA_BLOCK
    k_sel = min(DSA_TOPK_MAX, seq // 4)
    out_w = DSA_HEADS * DSA_HEAD_DIM
    return pl.pallas_call(
        functools.partial(_dsa_kernel, k_sel=k_sel),
        grid=(batch, nb),
        in_specs=[pl.BlockSpec((1, IDX_HEADS, DSA_BLOCK, LANES), lambda b, i: (b * nb + i, 0, 0, 0)),
                  pl.BlockSpec((DSA_BLOCK, LANES), lambda b, i: (b * nb + i, COL_IKW // LANES)),
                  pl.BlockSpec((seq, LANES), lambda b, i: (b, COL_IKW // LANES)),
                  pl.BlockSpec((seq, DSA_LATENT), lambda b, i: (b, 0)),
                  pl.BlockSpec((nb, DSA_LATENT + ROW_TILE, DSA_BLOCK), lambda b, i: (b, 0, 0)),
                  pl.BlockSpec((1, DSA_HEADS, DSA_BLOCK, DSA_LATENT), lambda b, i: (b * nb + i, 0, 0, 0)),
                  pl.BlockSpec((DSA_HEADS, DSA_LATENT, DSA_HEAD_DIM), lambda b, i: (0, 0, 0))],
        out_specs=pl.BlockSpec((DSA_BLOCK, out_w), lambda b, i: (b * nb + i, 0)),
        out_shape=jax.ShapeDtypeStruct((T, out_w), BF16),
        scratch_shapes=[pltpu.VMEM((nb, DSA_BLOCK, DSA_BLOCK), F32),
                        pltpu.VMEM((DSA_HEADS, 1, DSA_BLOCK), F32),
                        pltpu.VMEM((DSA_HEADS, DSA_LATENT + ROW_TILE, DSA_BLOCK), F32)],
        compiler_params=_params("parallel", "arbitrary"),
        name="dsa_attention",
    )(iq_pad, proj, proj, ckv, ckv_t, q_abs, w_uv)


def _merge_kernel(ret_ref, gm_ref, ds_ref, wr_ref, wg_ref, wd_ref, ga_ref, gb_ref, gc_ref, o_ref):
    def branch(a_ref, w_ref, gate_ref):
        return jax.nn.sigmoid(gate_ref[...].astype(F32)) * _dot(a_ref[...], w_ref[...])

    merged = branch(ret_ref, wr_ref, ga_ref) + branch(gm_ref, wg_ref, gb_ref) + branch(ds_ref, wd_ref, gc_ref)
    o_ref[...] = merged.astype(o_ref.dtype)


def merge_branches(ret, gm, ds, w_ret_o, w_gmlp_o, w_dsa_o, proj, d_model, *, tm=1024, tn=512):
    T, K = ret.shape
    gate_blk = COL_GATES // tn
    per_gate = d_model // tn
    act = pl.BlockSpec((tm, K), lambda i, j: (i, 0))
    wgt = pl.BlockSpec((K, tn), lambda i, j: (0, j))

    def gate(which):
        return pl.BlockSpec((tm, tn), lambda i, j: (i, gate_blk + which * per_gate + j))

    return pl.pallas_call(
        _merge_kernel,
        grid=(T // tm, d_model // tn),
        in_specs=[act, act, act, wgt, wgt, wgt, gate(0), gate(1), gate(2)],
        out_specs=pl.BlockSpec((tm, tn), lambda i, j: (i, j)),
        out_shape=jax.ShapeDtypeStruct((T, d_model), BF16),
        compiler_params=_params("parallel", "arbitrary"),
        name="merge_branches",
    )(ret, gm, ds, w_ret_o, w_gmlp_o, w_dsa_o, proj, proj, proj)


def _out_proj_kernel(a_ref, w_ref, x_ref, o_ref):
    o_ref[...] = x_ref[...] + _dot(a_ref[...], w_ref[...])


def out_proj_residual(merged, w_out, x, *, tm=1024, tn=512):
    T, K = merged.shape
    N = w_out.shape[1]
    return pl.pallas_call(
        _out_proj_kernel,
        grid=(T // tm, N // tn),
        in_specs=[pl.BlockSpec((tm, K), lambda i, j: (i, 0)),
                  pl.BlockSpec((K, tn), lambda i, j: (0, j)),
                  pl.BlockSpec((tm, tn), lambda i, j: (i, j))],
        out_specs=pl.BlockSpec((tm, tn), lambda i, j: (i, j)),
        out_shape=jax.ShapeDtypeStruct((T, N), F32),
        compiler_params=_params("parallel", "arbitrary"),
        name="out_proj_residual",
    )(merged, w_out, x)


def _pack_rows(h_bf16, o_ref):
    n, d = h_bf16.shape
    bits = lax.bitcast_convert_type(h_bf16.astype(F32), jnp.uint32)
    packed = (bits[:, :d // 2] >> 16) | (bits[:, d // 2:] & jnp.uint32(0xFFFF0000))
    for c in range(d // 2 // LANES):
        o_ref[pl.ds(c, n, stride=ROW_TILE), :] = packed[:, c * LANES:(c + 1) * LANES]


def _unpack_rows(x_ref, o_ref):
    n, d = o_ref.shape
    for c in range(d // 2 // LANES):
        w = x_ref[pl.ds(c, n, stride=ROW_TILE), :]
        lo = lax.bitcast_convert_type(w << 16, F32)
        hi = lax.bitcast_convert_type(w & jnp.uint32(0xFFFF0000), F32)
        o_ref[:, c * LANES:(c + 1) * LANES] = lo.astype(o_ref.dtype)
        o_ref[:, d // 2 + c * LANES:d // 2 + (c + 1) * LANES] = hi.astype(o_ref.dtype)


def _router_kernel(x_ref, g_ref, w_ref, b_ref, hp_ref, ids_ref, wts_ref):
    h = _rms(x_ref[...], g_ref[...]).astype(BF16)
    _pack_rows(h, hp_ref)
    logits = _dot(h, w_ref[...]) + b_ref[...]
    lane = lax.broadcasted_iota(jnp.int32, logits.shape, 1)
    big = jnp.int32(LANES)

    def first_argmax(vals, valid):
        v = jnp.where(valid, vals, -jnp.inf)
        m = jnp.max(v, axis=-1, keepdims=True)
        idx = jnp.min(jnp.where(valid & (v == m), lane, big), axis=-1, keepdims=True)
        return m, idx

    is_grp = lane < MOE_GROUPS
    g_max, g_sel = first_argmax(logits, is_grp)
    g_den = jnp.sum(jnp.where(is_grp, jnp.exp(logits - g_max), 0.0), axis=-1, keepdims=True)
    p_g = 1.0 / g_den
    e_lo = MOE_GROUPS + g_sel * MOE_EXPERTS_PER_GROUP
    in_grp = (lane >= e_lo) & (lane < e_lo + MOE_EXPERTS_PER_GROUP)
    m1, i1 = first_argmax(logits, in_grp)
    m2, i2 = first_argmax(logits, in_grp & (lane != i1))
    e2 = jnp.exp(m2 - m1)
    w1 = p_g / (1.0 + e2)
    w2 = p_g * e2 / (1.0 + e2)
    ids_ref[...] = jnp.where(lane == 0, i1 - MOE_GROUPS, jnp.where(lane == 1, i2 - MOE_GROUPS, 0))
    wts_ref[...] = jnp.where(lane == 0, w1, jnp.where(lane == 1, w2, 0.0))


def moe_router(x, g, w_router, b_router, *, tm=512):
    T, D = x.shape
    assert D == 2 * ROW_TILE * LANES
    return pl.pallas_call(
        _router_kernel,
        grid=(T // tm,),
        in_specs=[pl.BlockSpec((tm, D), lambda i: (i, 0)),
                  pl.BlockSpec((1, D), lambda i: (0, 0)),
                  pl.BlockSpec((D, LANES), lambda i: (0, 0)),
                  pl.BlockSpec((1, LANES), lambda i: (0, 0))],
        out_specs=[pl.BlockSpec((tm * ROW_TILE, LANES), lambda i: (i, 0)),
                   pl.BlockSpec((tm, LANES), lambda i: (i, 0)),
                   pl.BlockSpec((tm, LANES), lambda i: (i, 0))],
        out_shape=[jax.ShapeDtypeStruct((T * ROW_TILE, LANES), jnp.uint32),
                   jax.ShapeDtypeStruct((T, LANES), jnp.int32),
                   jax.ShapeDtypeStruct((T, LANES), F32)],
        compiler_params=_params("parallel"),
        name="moe_router",
    )(x, g, w_router, b_router)


def _route_tables(ids, n_rows):
    T = ids.shape[0]
    e = ids[:, :MOE_TOPK].reshape(-1)
    onehot = (e[:, None] == jnp.arange(MOE_EXPERTS, dtype=jnp.int32)[None, :]).astype(jnp.int32)
    rank = jnp.sum((jnp.cumsum(onehot, axis=0) - onehot) * onehot, axis=1)
    counts = jnp.sum(onehot, axis=0)
    padded = (counts + MOE_TILE - 1) // MOE_TILE * MOE_TILE
    ends = jnp.cumsum(padded)
    pos = ((ends - padded)[e] + rank).astype(jnp.int32)
    tile_start = jnp.arange(n_rows // MOE_TILE, dtype=jnp.int32) * MOE_TILE
    tile_expert = jnp.sum((ends[None, :] <= tile_start[:, None]).astype(jnp.int32), axis=1)
    tile_expert = jnp.minimum(tile_expert, MOE_EXPERTS - 1)
    n_used = (ends[-1] // MOE_TILE).astype(jnp.int32).reshape(1)
    return pos, tile_expert, n_used


def _token_rows(r, rows_per_token):
    return pl.ds(pl.multiple_of(r * rows_per_token, rows_per_token), rows_per_token)


def _scatter_rows_kernel(pos_ref, hp_ref, xs_in_hbm, xs_hbm, sem):
    del xs_in_hbm
    n = hp_ref.shape[0] // ROW_TILE
    base = pl.program_id(0) * n
    for slot in range(MOE_TOPK):
        def issue(r, carry, slot=slot):
            row = pos_ref[(base + r) * MOE_TOPK + slot]
            pltpu.make_async_copy(hp_ref.at[_token_rows(r, ROW_TILE)],
                                  xs_hbm.at[_token_rows(row, ROW_TILE)], sem.at[slot]).start()
            return carry
        lax.fori_loop(0, n, issue, 0, unroll=DMA_ISSUE_UNROLL)
    for slot in range(MOE_TOPK):
        pltpu.make_async_copy(hp_ref, xs_hbm.at[pl.ds(0, n * ROW_TILE)], sem.at[slot]).wait()


def scatter_rows(hp, pos, n_rows, *, tile=256):
    T = hp.shape[0] // ROW_TILE
    return pl.pallas_call(
        _scatter_rows_kernel,
        grid_spec=pltpu.PrefetchScalarGridSpec(
            num_scalar_prefetch=1,
            grid=(T // tile,),
            in_specs=[pl.BlockSpec((tile * ROW_TILE, LANES), lambda i, pos: (i, 0)),
                      pl.BlockSpec(memory_space=pl.ANY)],
            out_specs=pl.BlockSpec(memory_space=pl.ANY),
            scratch_shapes=[pltpu.SemaphoreType.DMA((MOE_TOPK,))]),
        out_shape=jax.ShapeDtypeStruct((n_rows * ROW_TILE, LANES), jnp.uint32),
        input_output_aliases={2: 0},
        compiler_params=_params("arbitrary"),
        name="moe_scatter_rows",
    )(pos, hp, jnp.zeros((n_rows * ROW_TILE, LANES), jnp.uint32))


def _moe_experts_kernel(tile_expert_ref, n_used_ref, x_ref, w1_ref, w3_ref, w2_ref, o_ref,
                        xb_ref, w1b_ref, w3b_ref, w2b_ref):
    i = pl.program_id(0)

    @pl.when((i == 0) | (tile_expert_ref[i] != tile_expert_ref[jnp.maximum(i - 1, 0)]))
    def _():
        w1b_ref[...] = w1_ref[0].astype(BF16)
        w3b_ref[...] = w3_ref[0].astype(BF16)
        w2b_ref[...] = w2_ref[0].astype(BF16)

    @pl.when(i < n_used_ref[0])
    def _():
        _unpack_rows(x_ref, xb_ref)
        x = xb_ref[...]
        a = _dot(x, w1b_ref[...])
        hid = a * jax.nn.sigmoid(a) * _dot(x, w3b_ref[...])
        o_ref[...] = _dot(hid.astype(BF16), w2b_ref[...])

    @pl.when(i >= n_used_ref[0])
    def _():
        o_ref[...] = jnp.zeros_like(o_ref)


def moe_experts(xs, tile_expert, n_used, w1, w3, w2):
    E, D, F = w1.shape
    P = xs.shape[0] // ROW_TILE

    def wspec(shape):
        return pl.BlockSpec((1,) + shape, lambda i, te, nu: (te[i], 0, 0))

    return pl.pallas_call(
        _moe_experts_kernel,
        grid_spec=pltpu.PrefetchScalarGridSpec(
            num_scalar_prefetch=2,
            grid=(P // MOE_TILE,),
            in_specs=[pl.BlockSpec((MOE_TILE * ROW_TILE, LANES), lambda i, te, nu: (i, 0)),
                      wspec((D, F)), wspec((D, F)), wspec((F, D))],
            out_specs=pl.BlockSpec((MOE_TILE, D), lambda i, te, nu: (i, 0)),
            scratch_shapes=[pltpu.VMEM((MOE_TILE, D), BF16), pltpu.VMEM((D, F), BF16),
                            pltpu.VMEM((D, F), BF16), pltpu.VMEM((F, D), BF16)]),
        out_shape=jax.ShapeDtypeStruct((P, D), F32),
        compiler_params=_params("arbitrary"),
        name="moe_experts",
    )(tile_expert, n_used, xs, w1, w3, w2)


def _moe_combine_kernel(pos_ref, ys_hbm, x_ref, wts_ref, g_ref, o_ref, y0_ref, y1_ref, sem, *,
                        final_norm):
    n = x_ref.shape[0]
    base = pl.program_id(0) * n
    for slot, y_ref in enumerate((y0_ref, y1_ref)):
        def issue(r, carry, slot=slot, y_ref=y_ref):
            row = pos_ref[(base + r) * MOE_TOPK + slot]
            pltpu.make_async_copy(ys_hbm.at[pl.ds(row, 1)], y_ref.at[pl.ds(r, 1)], sem.at[slot]).start()
            return carry
        lax.fori_loop(0, n, issue, 0, unroll=DMA_ISSUE_UNROLL)
    for slot, y_ref in enumerate((y0_ref, y1_ref)):
        pltpu.make_async_copy(ys_hbm.at[pl.ds(0, n)], y_ref, sem.at[slot]).wait()
    out = x_ref[...] + (wts_ref[:, 0:1] * y0_ref[...] + wts_ref[:, 1:2] * y1_ref[...])
    o_ref[...] = _rms(out, g_ref[...]) if final_norm else out


def moe_combine(ys, pos, wts, x, g, *, final_norm, tile=256):
    T, D = x.shape
    return pl.pallas_call(
        functools.partial(_moe_combine_kernel, final_norm=final_norm),
        grid_spec=pltpu.PrefetchScalarGridSpec(
            num_scalar_prefetch=1,
            grid=(T // tile,),
            in_specs=[pl.BlockSpec(memory_space=pl.ANY),
                      pl.BlockSpec((tile, D), lambda i, pos: (i, 0)),
                      pl.BlockSpec((tile, LANES), lambda i, pos: (i, 0)),
                      pl.BlockSpec((1, D), lambda i, pos: (0, 0))],
            out_specs=pl.BlockSpec((tile, D), lambda i, pos: (i, 0)),
            scratch_shapes=[pltpu.VMEM((tile, D), F32), pltpu.VMEM((tile, D), F32),
                            pltpu.SemaphoreType.DMA((MOE_TOPK,))]),
        out_shape=jax.ShapeDtypeStruct((T, D), F32),
        compiler_params=_params("arbitrary"),
        name="moe_combine",
    )(pos, ys, x, wts, g)


def _pack_w_in(w):
    D = w.shape[0]
    widths = (1024, 1024, 1024, 1024, 1024, 1024, 1024, 256, 512, 64, 8, D, D, D)
    offs = [0]
    for wd in widths:
        offs.append(offs[-1] + wd)
    rq, rk, rv, rg, gu, gv, dq, dc, iq, ik, iw, ga, gb, gc = (
        w[:, offs[i]:offs[i + 1]] for i in range(len(widths)))
    ikw_pad = jnp.zeros((D, COL_GATES - COL_IKW - IDX_DIM - IDX_HEADS), w.dtype)
    packed = jnp.concatenate([rq, rk, rv, rg, gu, gv, dq, iq, dc, ik, iw, ikw_pad, ga, gb, gc], axis=1)
    assert packed.shape[1] == PROJ_WIDTH
    return packed.astype(BF16)


def _pack_router(w_group, b_group, w_expert, b_expert):
    D = w_group.shape[0]
    w_e = jnp.transpose(w_expert, (1, 0, 2)).reshape(D, MOE_EXPERTS)
    pad = LANES - MOE_GROUPS - MOE_EXPERTS
    w = jnp.concatenate([w_group, w_e, jnp.zeros((D, pad), w_group.dtype)], axis=1).astype(BF16)
    b = jnp.concatenate([b_group, b_expert.reshape(MOE_EXPERTS), jnp.zeros((pad,), b_group.dtype)])
    return w, b.reshape(1, LANES).astype(F32)


def kernel(x, norm_mix_g, w_in, w_ret_o, gmlp_ln_g, gmlp_ln_b, gmlp_w_s, gmlp_b_s, w_gmlp_o, dsa_kv_norm_g, dsa_w_uk, dsa_w_uv, w_dsa_o, w_out, norm_ffn_g, moe_w_group, moe_b_group, moe_w_expert, moe_b_expert, moe_w1, moe_w3, moe_w2, final_norm_g):
    B, S, D = x.shape
    depth = w_in.shape[0]
    assert S % DSA_BLOCK == 0 and D == 2048
    xt = x.reshape(B * S, D)
    for l in range(depth):
        proj = norm_matmul(xt, norm_mix_g[l].reshape(1, D), _pack_w_in(w_in[l]))
        ret = retention(proj, B, S)
        gm = gmlp(proj, gmlp_ln_g[l], gmlp_ln_b[l], gmlp_w_s[l], gmlp_b_s[l])
        q_abs, iq_pad, ckv, ckv_t = dsa_prep(proj, dsa_kv_norm_g[l], dsa_w_uk[l].astype(BF16))
        ds = dsa_attention(proj, q_abs, iq_pad, ckv, ckv_t, dsa_w_uv[l].astype(BF16), B, S)
        merged = merge_branches(ret, gm, ds, w_ret_o[l].astype(BF16), w_gmlp_o[l].astype(BF16),
                                w_dsa_o[l].astype(BF16), proj, D)
        xt = out_proj_residual(merged, w_out[l].astype(BF16), xt)
        w_router, b_router = _pack_router(moe_w_group[l], moe_b_group[l], moe_w_expert[l], moe_b_expert[l])
        hp, ids, wts = moe_router(xt, norm_ffn_g[l].reshape(1, D), w_router, b_router)
        n_rows = MOE_TOPK * B * S + MOE_EXPERTS * MOE_TILE
        pos, tile_expert, n_used = _route_tables(ids, n_rows)
        xs = scatter_rows(hp, pos, n_rows)
        F = moe_w1.shape[-1]
        ys = moe_experts(xs, tile_expert + l * MOE_EXPERTS, n_used,
                         moe_w1.reshape(depth * MOE_EXPERTS, D, F),
                         moe_w3.reshape(depth * MOE_EXPERTS, D, F),
                         moe_w2.reshape(depth * MOE_EXPERTS, F, D))
        last = l == depth - 1
        xt = moe_combine(ys, pos, wts, xt, final_norm_g.reshape(1, D), final_norm=last)
    return xt.reshape(B, S, D)
```

```python
import functools
import math

import jax
import jax.numpy as jnp
from jax import lax
from jax.experimental import pallas as pl
from jax.experimental.pallas import tpu as pltpu

F32 = jnp.float32
BF16 = jnp.bfloat16

RMS_EPS = 1e-6
RET_HEADS = 4
RET_DIM = 256
CHUNK = 128
ROPE_BASE = 10000.0
GMLP_GROUPS = 4
GMLP_GROUP_DIM = 256
DSA_HEADS = 8
DSA_HEAD_DIM = 128
DSA_LATENT = 256
IDX_HEADS = 8
IDX_DIM = 64
DSA_TOPK_MAX = 256
DSA_BLOCK = 256
MOE_GROUPS = 4
MOE_EXPERTS_PER_GROUP = 8
MOE_EXPERTS = MOE_GROUPS * MOE_EXPERTS_PER_GROUP
MOE_HIDDEN = 512
MOE_TOPK = 2
MOE_TILE = 512
LANES = 128
ROW_TILE = 8
DMA_ISSUE_UNROLL = 8
NEG_BIG = -1e30
LOG2_E = math.log2(math.e)

COL_RQ, COL_RK, COL_RV, COL_RG = 0, 1024, 2048, 3072
COL_GU, COL_GV = 4096, 5120
COL_DQ, COL_DC, COL_IQ, COL_IKW = 6144, 7168, 7424, 7936
COL_TAIL = 7680
COL_GATES = 8192
PROJ_WIDTH = 14336
VMEM_LIMIT = 56 * 1024 * 1024


def _params(*sem):
    return pltpu.CompilerParams(dimension_semantics=sem, vmem_limit_bytes=VMEM_LIMIT)


def _rms(x, g):
    return x * lax.rsqrt(jnp.mean(x * x, axis=-1, keepdims=True) + RMS_EPS) * g


def _dot(a, b):
    return jnp.dot(a, b, preferred_element_type=F32)


def _dot_nt(a, b):
    return lax.dot_general(a, b, (((1,), (1,)), ((), ())), preferred_element_type=F32)


def _norm_matmul_kernel(x_ref, g_ref, w_main_ref, w_tail_ref, o_ref, h_ref, *, n_main):
    j = pl.program_id(1)

    @pl.when(j == 0)
    def _():
        h_ref[...] = _rms(x_ref[...], g_ref[...]).astype(BF16)

    @pl.when(j < n_main)
    def _():
        o_ref[...] = _dot(h_ref[...], w_main_ref[0].astype(BF16)).astype(o_ref.dtype)

    @pl.when(j >= n_main)
    def _():
        o_ref[...] = _dot(h_ref[...], w_tail_ref[...]).astype(o_ref.dtype)


def norm_matmul(x, g, w_all, layer, w_tail, *, tm=1024, tn=512, out_dtype=BF16):
    T, D = x.shape
    n_main = COL_TAIL // tn
    n_tail = w_tail.shape[1] // tn
    return pl.pallas_call(
        functools.partial(_norm_matmul_kernel, n_main=n_main),
        grid=(T // tm, n_main + n_tail),
        in_specs=[pl.BlockSpec((tm, D), lambda i, j: (i, 0)),
                  pl.BlockSpec((1, D), lambda i, j: (0, 0)),
                  pl.BlockSpec((1, D, tn), lambda i, j: (layer, 0, jnp.minimum(j, n_main - 1))),
                  pl.BlockSpec((D, tn), lambda i, j: (0, jnp.maximum(j - n_main, 0)))],
        out_specs=pl.BlockSpec((tm, tn), lambda i, j: (i, j)),
        out_shape=jax.ShapeDtypeStruct((T, COL_TAIL + w_tail.shape[1]), out_dtype),
        scratch_shapes=[pltpu.VMEM((tm, D), BF16)],
        compiler_params=_params("parallel", "arbitrary"),
        name="norm_in_proj",
    )(x, g, w_all, w_tail)


def _retention_kernel(q_ref, k_ref, v_ref, gate_ref, cos_ref, sin_ref, decay_ref, xi_ref,
                      zeta_ref, gchunk_ref, o_ref, r_ref):
    @pl.when(pl.program_id(0) == 0)
    def _():
        r_ref[...] = jnp.zeros_like(r_ref)

    cos = cos_ref[...]
    sin = sin_ref[...]
    half = RET_DIM // 2

    def rotary(t):
        t1, t2 = t[:, :half], t[:, half:]
        return jnp.concatenate([t1 * cos - t2 * sin, t1 * sin + t2 * cos], axis=-1)

    for b in range(q_ref.shape[0]):
        for h in range(RET_HEADS):
            cols = slice(h * RET_DIM, (h + 1) * RET_DIM)
            q = rotary(q_ref[b, :, cols].astype(F32)).astype(BF16)
            k = rotary(k_ref[b, :, cols].astype(F32)) * (RET_DIM ** -0.5)
            v = v_ref[b, :, cols]
            inner = _dot_nt(q, k.astype(BF16)) * decay_ref[h]
            r_old = r_ref[b, h]
            o = _dot(inner.astype(BF16), v) + _dot(q, r_old.astype(BF16)) * xi_ref[:, h:h + 1]
            kz = (k * zeta_ref[:, h:h + 1]).astype(BF16)
            r_ref[b, h] = r_old * gchunk_ref[h] + _dot(kz.T, v)
            o = o * lax.rsqrt(jnp.mean(o * o, axis=-1, keepdims=True) + RMS_EPS)
            gate = gate_ref[b, :, cols].astype(F32)
            o_ref[b, :, cols] = (gate * jax.nn.sigmoid(gate) * o).astype(o_ref.dtype)


def retention(proj, batch, seq):
    T = proj.shape[0]
    n_chunks = seq // CHUNK
    width = RET_HEADS * RET_DIM
    half = RET_DIM // 2
    inv = ROPE_BASE ** (-jnp.arange(half, dtype=F32) / half)
    ang = jnp.arange(seq).astype(F32)[:, None] * inv[None, :]
    cos, sin = jnp.cos(ang), jnp.sin(ang)
    log_g = jnp.log(1.0 - 2.0 ** (-5.0 - jnp.arange(RET_HEADS, dtype=F32)))
    i = jnp.arange(CHUNK, dtype=F32)
    diff = i[:, None] - i[None, :]
    decay = jnp.where(diff >= 0, jnp.exp(log_g[:, None, None] * jnp.maximum(diff, 0.0)), 0.0)
    xi = jnp.exp(log_g[None, :] * (i[:, None] + 1.0))
    zeta = jnp.exp(log_g[None, :] * (CHUNK - 1.0 - i[:, None]))
    gchunk = jnp.exp(log_g * CHUNK)

    def col_spec(col):
        return pl.BlockSpec((batch, CHUNK, width), lambda n: (0, n, col // width))

    proj3 = proj.reshape(batch, seq, proj.shape[1])
    out = pl.pallas_call(
        _retention_kernel,
        grid=(n_chunks,),
        in_specs=[col_spec(COL_RQ), col_spec(COL_RK), col_spec(COL_RV), col_spec(COL_RG),
                  pl.BlockSpec((CHUNK, half), lambda n: (n, 0)),
                  pl.BlockSpec((CHUNK, half), lambda n: (n, 0)),
                  pl.BlockSpec((RET_HEADS, CHUNK, CHUNK), lambda n: (0, 0, 0)),
                  pl.BlockSpec((CHUNK, RET_HEADS), lambda n: (0, 0)),
                  pl.BlockSpec((CHUNK, RET_HEADS), lambda n: (0, 0)),
                  pl.BlockSpec(memory_space=pltpu.SMEM)],
        out_specs=pl.BlockSpec((batch, CHUNK, width), lambda n: (0, n, 0)),
        out_shape=jax.ShapeDtypeStruct((batch, seq, width), BF16),
        scratch_shapes=[pltpu.VMEM((batch, RET_HEADS, RET_DIM, RET_DIM), F32)],
        compiler_params=_params("arbitrary"),
        name="retention",
    )(proj3, proj3, proj3, proj3, cos, sin, decay, xi, zeta, gchunk)
    return out.reshape(T, width)


def _gelu(x):
    return 0.5 * x * (1.0 + lax.erf(x * (2.0 ** -0.5)))


def _gmlp_kernel(u_ref, v_ref, lng_ref, lnb_ref, ws_ref, bs_ref, o_ref):
    v = _gelu(v_ref[...].astype(F32))
    mu = jnp.mean(v, axis=-1, keepdims=True)
    var = jnp.mean(jnp.square(v - mu), axis=-1, keepdims=True)
    vn = ((v - mu) * lax.rsqrt(var + RMS_EPS) * lng_ref[...] + lnb_ref[...]).astype(BF16)
    row = lax.broadcasted_iota(jnp.int32, (CHUNK, CHUNK), 0)
    col = lax.broadcasted_iota(jnp.int32, (CHUNK, CHUNK), 1)
    for g in range(GMLP_GROUPS):
        cols = slice(g * GMLP_GROUP_DIM, (g + 1) * GMLP_GROUP_DIM)
        w = jnp.where(row >= col, ws_ref[g], 0.0).astype(BF16)
        mixed = _dot(w, vn[:, cols]) + bs_ref[:, g:g + 1]
        u = _gelu(u_ref[:, cols].astype(F32))
        o_ref[:, cols] = (u * mixed).astype(o_ref.dtype)


def gmlp(proj, ln_g, ln_b, w_s, b_s):
    T = proj.shape[0]
    width = GMLP_GROUPS * GMLP_GROUP_DIM
    return pl.pallas_call(
        _gmlp_kernel,
        grid=(T // CHUNK,),
        in_specs=[pl.BlockSpec((CHUNK, width), lambda i: (i, COL_GU // width)),
                  pl.BlockSpec((CHUNK, width), lambda i: (i, COL_GV // width)),
                  pl.BlockSpec((1, width), lambda i: (0, 0)),
                  pl.BlockSpec((1, width), lambda i: (0, 0)),
                  pl.BlockSpec((GMLP_GROUPS, CHUNK, CHUNK), lambda i: (0, 0, 0)),
                  pl.BlockSpec((CHUNK, GMLP_GROUPS), lambda i: (0, 0))],
        out_specs=pl.BlockSpec((CHUNK, width), lambda i: (i, 0)),
        out_shape=jax.ShapeDtypeStruct((T, width), BF16),
        compiler_params=_params("parallel"),
        name="gmlp",
    )(proj, proj, ln_g.reshape(1, width), ln_b.reshape(1, width), w_s, b_s.T)


def _dsa_prep_kernel(dq_ref, dc_ref, iq_lo_ref, iq_hi_ref, g_ref, wuk_ref, qabs_ref, iqp_ref, ckv_ref,
                     ckvt_ref):
    c = _rms(dc_ref[...].astype(F32), g_ref[...])
    ckv_ref[...] = c.astype(ckv_ref.dtype)
    ckvt_ref[0, :DSA_LATENT, :] = c.T.astype(ckvt_ref.dtype)
    ckvt_ref[0, DSA_LATENT:, :] = jnp.ones((ROW_TILE, DSA_BLOCK), ckvt_ref.dtype)
    zeros = jnp.zeros((DSA_BLOCK, LANES - IDX_DIM), iqp_ref.dtype)
    for h in range(DSA_HEADS):
        q = dq_ref[:, h * DSA_HEAD_DIM:(h + 1) * DSA_HEAD_DIM]
        qabs_ref[0, h] = (_dot_nt(q, wuk_ref[h]) * (DSA_HEAD_DIM ** -0.5 * LOG2_E)).astype(qabs_ref.dtype)
    half_heads = IDX_HEADS // 2
    for h in range(IDX_HEADS):
        iq_ref, hh = (iq_lo_ref, h) if h < half_heads else (iq_hi_ref, h - half_heads)
        iqp_ref[0, h] = jnp.concatenate([iq_ref[:, hh * IDX_DIM:(hh + 1) * IDX_DIM], zeros], axis=-1)


def dsa_prep(proj, kv_norm_g, w_uk):
    T = proj.shape[0]
    nblk = T // DSA_BLOCK
    qw = DSA_HEADS * DSA_HEAD_DIM
    iq_w = IDX_HEADS * IDX_DIM
    return pl.pallas_call(
        _dsa_prep_kernel,
        grid=(nblk,),
        in_specs=[pl.BlockSpec((DSA_BLOCK, qw), lambda i: (i, COL_DQ // qw)),
                  pl.BlockSpec((DSA_BLOCK, DSA_LATENT), lambda i: (i, COL_DC // DSA_LATENT)),
                  pl.BlockSpec((DSA_BLOCK, iq_w // 2), lambda i: (i, COL_IQ // (iq_w // 2))),
                  pl.BlockSpec((DSA_BLOCK, iq_w // 2), lambda i: (i, COL_IQ // (iq_w // 2) + 1)),
                  pl.BlockSpec((1, DSA_LATENT), lambda i: (0, 0)),
                  pl.BlockSpec((DSA_HEADS, DSA_LATENT, DSA_HEAD_DIM), lambda i: (0, 0, 0))],
        out_specs=[pl.BlockSpec((1, DSA_HEADS, DSA_BLOCK, DSA_LATENT), lambda i: (i, 0, 0, 0)),
                   pl.BlockSpec((1, IDX_HEADS, DSA_BLOCK, LANES), lambda i: (i, 0, 0, 0)),
                   pl.BlockSpec((DSA_BLOCK, DSA_LATENT), lambda i: (i, 0)),
                   pl.BlockSpec((1, DSA_LATENT + ROW_TILE, DSA_BLOCK), lambda i: (i, 0, 0))],
        out_shape=[jax.ShapeDtypeStruct((nblk, DSA_HEADS, DSA_BLOCK, DSA_LATENT), BF16),
                   jax.ShapeDtypeStruct((nblk, IDX_HEADS, DSA_BLOCK, LANES), BF16),
                   jax.ShapeDtypeStruct((T, DSA_LATENT), BF16),
                   jax.ShapeDtypeStruct((nblk, DSA_LATENT + ROW_TILE, DSA_BLOCK), BF16)],
        compiler_params=_params("parallel"),
        name="dsa_prep",
    )(proj, proj, proj, proj, kv_norm_g.reshape(1, DSA_LATENT), w_uk)


def _dsa_kernel(iqp_ref, wq_ref, kidx_ref, ckv_ref, ckvt_ref, qa_ref, wuv_ref, o_ref,
                score_ref, m_ref, acc_ref, *, k_sel):
    QB = DSA_BLOCK
    blk = pl.program_id(1)
    n_tiles = blk + 1
    ksel_f = float(k_sel)
    key_in_tile = lax.broadcasted_iota(jnp.int32, (QB, QB), 0)
    query_in_blk = lax.broadcasted_iota(jnp.int32, (QB, QB), 1)

    def fold8(x, op):
        return op(x.reshape(QB // 8, 8, QB), axis=0)

    def tile_rows(j):
        return pl.ds(pl.multiple_of(j * QB, QB), QB)

    wi_t = wq_ref[...].astype(F32).T * (IDX_HEADS ** -0.5 * IDX_DIM ** -0.5)

    def score_body(j, stats):
        s_max, s_min, n_ge0, n_gt0 = stats
        kt = kidx_ref[tile_rows(j), :]
        sc = jnp.zeros((QB, QB), F32)
        for h in range(IDX_HEADS):
            lg = _dot_nt(kt, iqp_ref[0, h])
            sc = sc + wi_t[IDX_DIM + h:IDX_DIM + h + 1, :] * jnp.maximum(lg, 0.0)
        causal = (j < blk) | (key_in_tile <= query_in_blk)
        masked = jnp.where(causal, sc, -jnp.inf)
        score_ref[j] = masked
        return (jnp.maximum(s_max, fold8(masked, jnp.max)),
                jnp.minimum(s_min, fold8(jnp.where(causal, sc, jnp.inf), jnp.min)),
                n_ge0 + fold8((masked >= 0.0).astype(F32), jnp.sum),
                n_gt0 + fold8((masked > 0.0).astype(F32), jnp.sum))

    stats = lax.fori_loop(0, n_tiles, score_body,
                          (jnp.full((8, QB), -jnp.inf, F32), jnp.full((8, QB), jnp.inf, F32),
                           jnp.zeros((8, QB), F32), jnp.zeros((8, QB), F32)))
    hi0 = jnp.max(stats[0], axis=0, keepdims=True)
    lo0 = jnp.min(stats[1], axis=0, keepdims=True)
    cnt_ge0 = jnp.sum(stats[2], axis=0, keepdims=True)
    cnt_gt0 = jnp.sum(stats[3], axis=0, keepdims=True)

    COUNT_LANES = 4

    def count_negative(diff):
        def body(j, acc):
            sign = lax.shift_right_logical(lax.bitcast_convert_type(diff(score_ref[j]), jnp.uint32),
                                           jnp.uint32(31))
            return acc + jnp.sum(sign.astype(jnp.int32).reshape(-1, COUNT_LANES, 8, QB), axis=0)
        acc = lax.fori_loop(0, n_tiles, body, jnp.zeros((COUNT_LANES, 8, QB), jnp.int32))
        return jnp.sum(acc.reshape(COUNT_LANES * 8, QB), axis=0, keepdims=True).astype(F32)

    n_stored = (n_tiles * QB).astype(F32)

    def count_ge(t):
        return n_stored - count_negative(lambda s: s - t)

    n_causal = (blk * QB + 1 + lax.broadcasted_iota(jnp.int32, (1, QB), 1)).astype(F32)
    need = n_causal > ksel_f
    cnt_hi0 = count_ge(hi0)
    above0 = cnt_gt0 >= ksel_f
    below0 = cnt_ge0 < ksel_f
    lo0 = jnp.where(below0, lo0, 0.0)
    cnt_lo0 = jnp.where(below0, n_causal, cnt_ge0)
    hi0 = jnp.where(above0, hi0, 0.0)
    at_top = above0 & (cnt_hi0 >= ksel_f)
    lo0 = jnp.where(at_top, hi0, lo0)
    cnt_lo0 = jnp.where(at_top, cnt_hi0, cnt_lo0)
    cnt_hi0 = jnp.where(above0, cnt_hi0, cnt_ge0)

    def unresolved(lo, hi, cnt_lo, cnt_hi):
        return (cnt_lo != ksel_f) & (lo < hi) & (cnt_lo - cnt_hi > 2.0)

    active0 = need & (above0 | below0) & unresolved(lo0, hi0, cnt_lo0, cnt_hi0)

    def bisect_cond(state):
        return jnp.max(state[4]) > 0.0

    def bisect_body(state):
        lo, hi, cnt_lo, cnt_hi, active = state
        mid = lo + 0.5 * (hi - lo)
        c = count_ge(mid)
        ge = c >= ksel_f
        live = active > 0.0
        moved = (mid > lo) & (mid < hi)
        new_lo = jnp.where(live & ge, mid, lo)
        new_hi = jnp.where(live & ~ge, mid, hi)
        new_cnt_lo = jnp.where(live & ge, c, cnt_lo)
        new_cnt_hi = jnp.where(live & ~ge, c, cnt_hi)
        new_active = live & moved & unresolved(new_lo, new_hi, new_cnt_lo, new_cnt_hi)
        return new_lo, new_hi, new_cnt_lo, new_cnt_hi, new_active.astype(F32)

    lo, hi, cnt_lo, cnt_hi, _ = lax.while_loop(
        bisect_cond, bisect_body, (lo0, hi0, cnt_lo0, cnt_hi0, active0.astype(F32)))

    def max_below_body(j, acc):
        s = score_ref[j]
        return jnp.maximum(acc, jnp.max(jnp.where(s < hi, s, -jnp.inf).reshape(-1, COUNT_LANES, 8, QB), axis=0))
    max_below = lax.fori_loop(0, n_tiles, max_below_body, jnp.full((COUNT_LANES, 8, QB), -jnp.inf, F32))
    max_below = jnp.max(max_below.reshape(COUNT_LANES * 8, QB), axis=0, keepdims=True)
    is_pair = (cnt_lo != ksel_f) & (lo < hi) & (cnt_lo - cnt_hi == 2.0)
    thr = jnp.where(need, jnp.where(is_pair, max_below, lo), -jnp.inf)
    n_gt = count_negative(lambda s: thr - s)
    tie_quota = jnp.where(need, ksel_f - n_gt, 0.0)

    m_ref[...] = jnp.full_like(m_ref, NEG_BIG)
    acc_ref[...] = jnp.zeros_like(acc_ref)
    strict_lower = (query_in_blk < key_in_tile).astype(BF16)

    def attn_body(j, tie_seen):
        sc = score_ref[j]
        eq = sc == thr
        eq_f = eq.astype(F32)
        prefix = _dot(strict_lower, eq_f.astype(BF16)) + tie_seen
        sel = (sc > thr) | (eq & (prefix < tie_quota))
        bias = jnp.where(sel, 0.0, NEG_BIG)
        c_t = ckv_ref[tile_rows(j), :]
        ct_t = ckvt_ref[j]
        for h in range(DSA_HEADS):
            s = _dot_nt(c_t, qa_ref[0, h]) + bias
            m_old = m_ref[h]
            m_new = jnp.maximum(m_old, jnp.max(fold8(s, jnp.max), axis=0, keepdims=True))
            alpha = jnp.exp2(m_old - m_new)
            p = jnp.exp2(s - m_new)
            acc_ref[h] = alpha * acc_ref[h] + _dot(ct_t, p.astype(BF16))
            m_ref[h] = m_new
        return tie_seen + jnp.sum(fold8(eq_f, jnp.sum), axis=0, keepdims=True)

    lax.fori_loop(0, n_tiles, attn_body, jnp.zeros((1, QB), F32))

    for h in range(DSA_HEADS):
        acc = acc_ref[h]
        o_lat = (acc[:DSA_LATENT] / acc[DSA_LATENT:DSA_LATENT + 1]).T.astype(BF16)
        o_ref[:, h * DSA_HEAD_DIM:(h + 1) * DSA_HEAD_DIM] = _dot(o_lat, wuv_ref[h]).astype(o_ref.dtype)


def dsa_attention(proj, q_abs, iq_pad, ckv, ckv_t, w_uv, batch, seq):
    T = proj.shape[0]
    nb = seq // DSA_BLOCK
    k_sel = min(DSA_TOPK_MAX, seq // 4)
    out_w = DSA_HEADS * DSA_HEAD_DIM
    return pl.pallas_call(
        functools.partial(_dsa_kernel, k_sel=k_sel),
        grid=(batch, nb),
        in_specs=[pl.BlockSpec((1, IDX_HEADS, DSA_BLOCK, LANES), lambda b, i: (b * nb + i, 0, 0, 0)),
                  pl.BlockSpec((DSA_BLOCK, LANES), lambda b, i: (b * nb + i, COL_IKW // LANES)),
                  pl.BlockSpec((seq, LANES), lambda b, i: (b, COL_IKW // LANES)),
                  pl.BlockSpec((seq, DSA_LATENT), lambda b, i: (b, 0)),
                  pl.BlockSpec((nb, DSA_LATENT + ROW_TILE, DSA_BLOCK), lambda b, i: (b, 0, 0)),
                  pl.BlockSpec((1, DSA_HEADS, DSA_BLOCK, DSA_LATENT), lambda b, i: (b * nb + i, 0, 0, 0)),
                  pl.BlockSpec((DSA_HEADS, DSA_LATENT, DSA_HEAD_DIM), lambda b, i: (0, 0, 0))],
        out_specs=pl.BlockSpec((DSA_BLOCK, out_w), lambda b, i: (b * nb + i, 0)),
        out_shape=jax.ShapeDtypeStruct((T, out_w), BF16),
        scratch_shapes=[pltpu.VMEM((nb, DSA_BLOCK, DSA_BLOCK), F32),
                        pltpu.VMEM((DSA_HEADS, 1, DSA_BLOCK), F32),
                        pltpu.VMEM((DSA_HEADS, DSA_LATENT + ROW_TILE, DSA_BLOCK), F32)],
        compiler_params=_params("parallel", "arbitrary"),
        name="dsa_attention",
    )(iq_pad, proj, proj, ckv, ckv_t, q_abs, w_uv)


def _merge_kernel(ret_ref, gm_ref, ds_ref, wr_ref, wg_ref, wd_ref, ga_ref, gb_ref, gc_ref, o_ref):
    def branch(a_ref, w_ref, gate_ref):
        return jax.nn.sigmoid(gate_ref[...].astype(F32)) * _dot(a_ref[...], w_ref[0].astype(BF16))

    merged = branch(ret_ref, wr_ref, ga_ref) + branch(gm_ref, wg_ref, gb_ref) + branch(ds_ref, wd_ref, gc_ref)
    o_ref[...] = merged.astype(o_ref.dtype)


def merge_branches(ret, gm, ds, w_ret_o, w_gmlp_o, w_dsa_o, layer, proj, d_model, *, tm=1024, tn=512):
    T, K = ret.shape
    gate_blk = COL_GATES // tn
    per_gate = d_model // tn
    act = pl.BlockSpec((tm, K), lambda i, j: (i, 0))
    wgt = pl.BlockSpec((1, K, tn), lambda i, j: (layer, 0, j))

    def gate(which):
        return pl.BlockSpec((tm, tn), lambda i, j: (i, gate_blk + which * per_gate + j))

    return pl.pallas_call(
        _merge_kernel,
        grid=(T // tm, d_model // tn),
        in_specs=[act, act, act, wgt, wgt, wgt, gate(0), gate(1), gate(2)],
        out_specs=pl.BlockSpec((tm, tn), lambda i, j: (i, j)),
        out_shape=jax.ShapeDtypeStruct((T, d_model), BF16),
        compiler_params=_params("parallel", "arbitrary"),
        name="merge_branches",
    )(ret, gm, ds, w_ret_o, w_gmlp_o, w_dsa_o, proj, proj, proj)


def _out_proj_kernel(a_ref, w_ref, x_ref, o_ref):
    o_ref[...] = x_ref[...] + _dot(a_ref[...], w_ref[0].astype(BF16))


def out_proj_residual(merged, w_out, layer, x, *, tm=1024, tn=512):
    T, K = merged.shape
    N = w_out.shape[2]
    return pl.pallas_call(
        _out_proj_kernel,
        grid=(T // tm, N // tn),
        in_specs=[pl.BlockSpec((tm, K), lambda i, j: (i, 0)),
                  pl.BlockSpec((1, K, tn), lambda i, j: (layer, 0, j)),
                  pl.BlockSpec((tm, tn), lambda i, j: (i, j))],
        out_specs=pl.BlockSpec((tm, tn), lambda i, j: (i, j)),
        out_shape=jax.ShapeDtypeStruct((T, N), F32),
        compiler_params=_params("parallel", "arbitrary"),
        name="out_proj_residual",
    )(merged, w_out, x)


def _pack_rows(h_bf16, o_ref):
    n, d = h_bf16.shape
    bits = lax.bitcast_convert_type(h_bf16.astype(F32), jnp.uint32)
    packed = (bits[:, :d // 2] >> 16) | (bits[:, d // 2:] & jnp.uint32(0xFFFF0000))
    for c in range(d // 2 // LANES):
        o_ref[pl.ds(c, n, stride=ROW_TILE), :] = packed[:, c * LANES:(c + 1) * LANES]


def _unpack_rows(x_ref, o_ref):
    n, d = o_ref.shape
    for c in range(d // 2 // LANES):
        w = x_ref[pl.ds(c, n, stride=ROW_TILE), :]
        lo = lax.bitcast_convert_type(w << 16, F32)
        hi = lax.bitcast_convert_type(w & jnp.uint32(0xFFFF0000), F32)
        o_ref[:, c * LANES:(c + 1) * LANES] = lo.astype(o_ref.dtype)
        o_ref[:, d // 2 + c * LANES:d // 2 + (c + 1) * LANES] = hi.astype(o_ref.dtype)


def _router_kernel(x_ref, g_ref, w_ref, b_ref, hp_ref, ids_ref, wts_ref):
    h = _rms(x_ref[...], g_ref[...]).astype(BF16)
    _pack_rows(h, hp_ref)
    logits = _dot(h, w_ref[...]) + b_ref[...]
    lane = lax.broadcasted_iota(jnp.int32, logits.shape, 1)
    big = jnp.int32(LANES)

    def first_argmax(vals, valid):
        v = jnp.where(valid, vals, -jnp.inf)
        m = jnp.max(v, axis=-1, keepdims=True)
        idx = jnp.min(jnp.where(valid & (v == m), lane, big), axis=-1, keepdims=True)
        return m, idx

    is_grp = lane < MOE_GROUPS
    g_max, g_sel = first_argmax(logits, is_grp)
    g_den = jnp.sum(jnp.where(is_grp, jnp.exp(logits - g_max), 0.0), axis=-1, keepdims=True)
    p_g = 1.0 / g_den
    e_lo = MOE_GROUPS + g_sel * MOE_EXPERTS_PER_GROUP
    in_grp = (lane >= e_lo) & (lane < e_lo + MOE_EXPERTS_PER_GROUP)
    m1, i1 = first_argmax(logits, in_grp)
    m2, i2 = first_argmax(logits, in_grp & (lane != i1))
    e2 = jnp.exp(m2 - m1)
    w1 = p_g / (1.0 + e2)
    w2 = p_g * e2 / (1.0 + e2)
    ids_ref[...] = jnp.where(lane == 0, i1 - MOE_GROUPS, jnp.where(lane == 1, i2 - MOE_GROUPS, 0))
    wts_ref[...] = jnp.where(lane == 0, w1, jnp.where(lane == 1, w2, 0.0))


def moe_router(x, g, w_router, b_router, *, tm=512):
    T, D = x.shape
    assert D == 2 * ROW_TILE * LANES
    return pl.pallas_call(
        _router_kernel,
        grid=(T // tm,),
        in_specs=[pl.BlockSpec((tm, D), lambda i: (i, 0)),
                  pl.BlockSpec((1, D), lambda i: (0, 0)),
                  pl.BlockSpec((D, LANES), lambda i: (0, 0)),
                  pl.BlockSpec((1, LANES), lambda i: (0, 0))],
        out_specs=[pl.BlockSpec((tm * ROW_TILE, LANES), lambda i: (i, 0)),
                   pl.BlockSpec((tm, LANES), lambda i: (i, 0)),
                   pl.BlockSpec((tm, LANES), lambda i: (i, 0))],
        out_shape=[jax.ShapeDtypeStruct((T * ROW_TILE, LANES), jnp.uint32),
                   jax.ShapeDtypeStruct((T, LANES), jnp.int32),
                   jax.ShapeDtypeStruct((T, LANES), F32)],
        compiler_params=_params("parallel"),
        name="moe_router",
    )(x, g, w_router, b_router)


def _route_tables(ids, n_rows):
    T = ids.shape[0]
    e = ids[:, :MOE_TOPK].reshape(-1)
    onehot = (e[:, None] == jnp.arange(MOE_EXPERTS, dtype=jnp.int32)[None, :]).astype(jnp.int32)
    rank = jnp.sum((jnp.cumsum(onehot, axis=0) - onehot) * onehot, axis=1)
    counts = jnp.sum(onehot, axis=0)
    padded = (counts + MOE_TILE - 1) // MOE_TILE * MOE_TILE
    ends = jnp.cumsum(padded)
    pos = ((ends - padded)[e] + rank).astype(jnp.int32)
    tile_start = jnp.arange(n_rows // MOE_TILE, dtype=jnp.int32) * MOE_TILE
    tile_expert = jnp.sum((ends[None, :] <= tile_start[:, None]).astype(jnp.int32), axis=1)
    tile_expert = jnp.minimum(tile_expert, MOE_EXPERTS - 1)
    n_used = (ends[-1] // MOE_TILE).astype(jnp.int32).reshape(1)
    return pos, tile_expert, n_used


def _token_rows(r, rows_per_token):
    return pl.ds(pl.multiple_of(r * rows_per_token, rows_per_token), rows_per_token)


def _scatter_rows_kernel(pos_ref, hp_ref, xs_in_hbm, xs_hbm, sem):
    del xs_in_hbm
    n = hp_ref.shape[0] // ROW_TILE
    base = pl.program_id(0) * n
    for slot in range(MOE_TOPK):
        def issue(r, carry, slot=slot):
            row = pos_ref[(base + r) * MOE_TOPK + slot]
            pltpu.make_async_copy(hp_ref.at[_token_rows(r, ROW_TILE)],
                                  xs_hbm.at[_token_rows(row, ROW_TILE)], sem.at[slot]).start()
            return carry
        lax.fori_loop(0, n, issue, 0, unroll=DMA_ISSUE_UNROLL)
    for slot in range(MOE_TOPK):
        pltpu.make_async_copy(hp_ref, xs_hbm.at[pl.ds(0, n * ROW_TILE)], sem.at[slot]).wait()


def scatter_rows(hp, pos, n_rows, *, tile=256):
    T = hp.shape[0] // ROW_TILE
    return pl.pallas_call(
        _scatter_rows_kernel,
        grid_spec=pltpu.PrefetchScalarGridSpec(
            num_scalar_prefetch=1,
            grid=(T // tile,),
            in_specs=[pl.BlockSpec((tile * ROW_TILE, LANES), lambda i, pos: (i, 0)),
                      pl.BlockSpec(memory_space=pl.ANY)],
            out_specs=pl.BlockSpec(memory_space=pl.ANY),
            scratch_shapes=[pltpu.SemaphoreType.DMA((MOE_TOPK,))]),
        out_shape=jax.ShapeDtypeStruct((n_rows * ROW_TILE, LANES), jnp.uint32),
        input_output_aliases={2: 0},
        compiler_params=_params("arbitrary"),
        name="moe_scatter_rows",
    )(pos, hp, jnp.zeros((n_rows * ROW_TILE, LANES), jnp.uint32))


def _moe_experts_kernel(tile_expert_ref, n_used_ref, x_ref, w1_ref, w3_ref, w2_ref, o_ref,
                        xb_ref, w1b_ref, w3b_ref, w2b_ref):
    i = pl.program_id(0)

    @pl.when((i == 0) | (tile_expert_ref[i] != tile_expert_ref[jnp.maximum(i - 1, 0)]))
    def _():
        w1b_ref[...] = w1_ref[0].astype(BF16)
        w3b_ref[...] = w3_ref[0].astype(BF16)
        w2b_ref[...] = w2_ref[0].astype(BF16)

    @pl.when(i < n_used_ref[0])
    def _():
        _unpack_rows(x_ref, xb_ref)
        x = xb_ref[...]
        a = _dot(x, w1b_ref[...])
        hid = a * jax.nn.sigmoid(a) * _dot(x, w3b_ref[...])
        o_ref[...] = _dot(hid.astype(BF16), w2b_ref[...])

    @pl.when(i >= n_used_ref[0])
    def _():
        o_ref[...] = jnp.zeros_like(o_ref)


def moe_experts(xs, tile_expert, n_used, w1, w3, w2):
    E, D, F = w1.shape
    P = xs.shape[0] // ROW_TILE

    def wspec(shape):
        return pl.BlockSpec((1,) + shape, lambda i, te, nu: (te[i], 0, 0))

    return pl.pallas_call(
        _moe_experts_kernel,
        grid_spec=pltpu.PrefetchScalarGridSpec(
            num_scalar_prefetch=2,
            grid=(P // MOE_TILE,),
            in_specs=[pl.BlockSpec((MOE_TILE * ROW_TILE, LANES), lambda i, te, nu: (i, 0)),
                      wspec((D, F)), wspec((D, F)), wspec((F, D))],
            out_specs=pl.BlockSpec((MOE_TILE, D), lambda i, te, nu: (i, 0)),
            scratch_shapes=[pltpu.VMEM((MOE_TILE, D), BF16), pltpu.VMEM((D, F), BF16),
                            pltpu.VMEM((D, F), BF16), pltpu.VMEM((F, D), BF16)]),
        out_shape=jax.ShapeDtypeStruct((P, D), F32),
        compiler_params=_params("arbitrary"),
        name="moe_experts",
    )(tile_expert, n_used, xs, w1, w3, w2)


def _moe_combine_kernel(pos_ref, ys_hbm, x_ref, wts_ref, g_ref, o_ref, y0_ref, y1_ref, sem, *,
                        final_norm):
    n = x_ref.shape[0]
    base = pl.program_id(0) * n
    for slot, y_ref in enumerate((y0_ref, y1_ref)):
        def issue(r, carry, slot=slot, y_ref=y_ref):
            row = pos_ref[(base + r) * MOE_TOPK + slot]
            pltpu.make_async_copy(ys_hbm.at[pl.ds(row, 1)], y_ref.at[pl.ds(r, 1)], sem.at[slot]).start()
            return carry
        lax.fori_loop(0, n, issue, 0, unroll=DMA_ISSUE_UNROLL)
    for slot, y_ref in enumerate((y0_ref, y1_ref)):
        pltpu.make_async_copy(ys_hbm.at[pl.ds(0, n)], y_ref, sem.at[slot]).wait()
    out = x_ref[...] + (wts_ref[:, 0:1] * y0_ref[...] + wts_ref[:, 1:2] * y1_ref[...])
    o_ref[...] = _rms(out, g_ref[...]) if final_norm else out


def moe_combine(ys, pos, wts, x, g, *, final_norm, tile=256):
    T, D = x.shape
    return pl.pallas_call(
        functools.partial(_moe_combine_kernel, final_norm=final_norm),
        grid_spec=pltpu.PrefetchScalarGridSpec(
            num_scalar_prefetch=1,
            grid=(T // tile,),
            in_specs=[pl.BlockSpec(memory_space=pl.ANY),
                      pl.BlockSpec((tile, D), lambda i, pos: (i, 0)),
                      pl.BlockSpec((tile, LANES), lambda i, pos: (i, 0)),
                      pl.BlockSpec((1, D), lambda i, pos: (0, 0))],
            out_specs=pl.BlockSpec((tile, D), lambda i, pos: (i, 0)),
            scratch_shapes=[pltpu.VMEM((tile, D), F32), pltpu.VMEM((tile, D), F32),
                            pltpu.SemaphoreType.DMA((MOE_TOPK,))]),
        out_shape=jax.ShapeDtypeStruct((T, D), F32),
        compiler_params=_params("arbitrary"),
        name="moe_combine",
    )(pos, ys, x, wts, g)


def _pack_w_tail(w):
    D = w.shape[0]
    gates_src = COL_IKW + IDX_DIM + IDX_HEADS
    pad = jnp.zeros((D, COL_GATES - gates_src), w.dtype)
    packed = jnp.concatenate([w[:, COL_TAIL:gates_src], pad, w[:, gates_src:]], axis=1)
    assert packed.shape[1] == PROJ_WIDTH - COL_TAIL
    return packed.astype(BF16)


def _pack_router(w_group, b_group, w_expert, b_expert):
    D = w_group.shape[0]
    w_e = jnp.transpose(w_expert, (1, 0, 2)).reshape(D, MOE_EXPERTS)
    pad = LANES - MOE_GROUPS - MOE_EXPERTS
    w = jnp.concatenate([w_group, w_e, jnp.zeros((D, pad), w_group.dtype)], axis=1).astype(BF16)
    b = jnp.concatenate([b_group, b_expert.reshape(MOE_EXPERTS), jnp.zeros((pad,), b_group.dtype)])
    return w, b.reshape(1, LANES).astype(F32)


def kernel(x, norm_mix_g, w_in, w_ret_o, gmlp_ln_g, gmlp_ln_b, gmlp_w_s, gmlp_b_s, w_gmlp_o, dsa_kv_norm_g, dsa_w_uk, dsa_w_uv, w_dsa_o, w_out, norm_ffn_g, moe_w_group, moe_b_group, moe_w_expert, moe_b_expert, moe_w1, moe_w3, moe_w2, final_norm_g):
    B, S, D = x.shape
    depth = w_in.shape[0]
    assert S % DSA_BLOCK == 0 and D == 2048
    xt = x.reshape(B * S, D)
    for l in range(depth):
        proj = norm_matmul(xt, norm_mix_g[l].reshape(1, D), w_in, l, _pack_w_tail(w_in[l]))
        ret = retention(proj, B, S)
        gm = gmlp(proj, gmlp_ln_g[l], gmlp_ln_b[l], gmlp_w_s[l], gmlp_b_s[l])
        q_abs, iq_pad, ckv, ckv_t = dsa_prep(proj, dsa_kv_norm_g[l], dsa_w_uk[l].astype(BF16))
        ds = dsa_attention(proj, q_abs, iq_pad, ckv, ckv_t, dsa_w_uv[l].astype(BF16), B, S)
        merged = merge_branches(ret, gm, ds, w_ret_o, w_gmlp_o, w_dsa_o, l, proj, D)
        xt = out_proj_residual(merged, w_out, l, xt)
        w_router, b_router = _pack_router(moe_w_group[l], moe_b_group[l], moe_w_expert[l], moe_b_expert[l])
        hp, ids, wts = moe_router(xt, norm_ffn_g[l].reshape(1, D), w_router, b_router)
        n_rows = MOE_TOPK * B * S + MOE_EXPERTS * MOE_TILE
        pos, tile_expert, n_used = _route_tables(ids, n_rows)
        xs = scatter_rows(hp, pos, n_rows)
        F = moe_w1.shape[-1]
        ys = moe_experts(xs, tile_expert + l * MOE_EXPERTS, n_used,
                         moe_w1.reshape(depth * MOE_EXPERTS, D, F),
                         moe_w3.reshape(depth * MOE_EXPERTS, D, F),
                         moe_w2.reshape(depth * MOE_EXPERTS, F, D))
        last = l == depth - 1
        xt = moe_combine(ys, pos, wts, xt, final_norm_g.reshape(1, D), final_norm=last)
    return xt.reshape(B, S, D)
```

```python
import functools
import math

import jax
import jax.numpy as jnp
from jax import lax
from jax.experimental import pallas as pl
from jax.experimental.pallas import tpu as pltpu

F32 = jnp.float32
BF16 = jnp.bfloat16

RMS_EPS = 1e-6
RET_HEADS = 4
RET_DIM = 256
CHUNK = 128
ROPE_BASE = 10000.0
GMLP_GROUPS = 4
GMLP_GROUP_DIM = 256
DSA_HEADS = 8
DSA_HEAD_DIM = 128
DSA_LATENT = 256
IDX_HEADS = 8
IDX_DIM = 64
DSA_TOPK_MAX = 256
DSA_BLOCK = 256
MOE_GROUPS = 4
MOE_EXPERTS_PER_GROUP = 8
MOE_EXPERTS = MOE_GROUPS * MOE_EXPERTS_PER_GROUP
MOE_HIDDEN = 512
MOE_TOPK = 2
MOE_TILE = 256
LANES = 128
ROW_TILE = 8
DMA_ISSUE_UNROLL = 8
NEG_BIG = -1e30
LOG2_E = math.log2(math.e)

COL_RQ, COL_RK, COL_RV, COL_RG = 0, 1024, 2048, 3072
COL_GU, COL_GV = 4096, 5120
COL_DQ, COL_IQ, COL_DC, COL_IKW = 6144, 7168, 7680, 7936
COL_GATES = 8192
PROJ_WIDTH = 14336
VMEM_LIMIT = 56 * 1024 * 1024


def _params(*sem):
    return pltpu.CompilerParams(dimension_semantics=sem, vmem_limit_bytes=VMEM_LIMIT)


def _rms(x, g):
    return x * lax.rsqrt(jnp.mean(x * x, axis=-1, keepdims=True) + RMS_EPS) * g


def _dot(a, b):
    return jnp.dot(a, b, preferred_element_type=F32)


def _dot_nt(a, b):
    return lax.dot_general(a, b, (((1,), (1,)), ((), ())), preferred_element_type=F32)


def _norm_matmul_kernel(x_ref, g_ref, w_ref, o_ref, h_ref):
    @pl.when(pl.program_id(1) == 0)
    def _():
        h_ref[...] = _rms(x_ref[...], g_ref[...]).astype(BF16)

    o_ref[...] = _dot(h_ref[...], w_ref[...]).astype(o_ref.dtype)


def norm_matmul(x, g, w, *, tm=1024, tn=512, out_dtype=BF16):
    T, D = x.shape
    N = w.shape[1]
    return pl.pallas_call(
        _norm_matmul_kernel,
        grid=(T // tm, N // tn),
        in_specs=[pl.BlockSpec((tm, D), lambda i, j: (i, 0)),
                  pl.BlockSpec((1, D), lambda i, j: (0, 0)),
                  pl.BlockSpec((D, tn), lambda i, j: (0, j))],
        out_specs=pl.BlockSpec((tm, tn), lambda i, j: (i, j)),
        out_shape=jax.ShapeDtypeStruct((T, N), out_dtype),
        scratch_shapes=[pltpu.VMEM((tm, D), BF16)],
        compiler_params=_params("parallel", "arbitrary"),
        name="norm_in_proj",
    )(x, g, w)


def _retention_kernel(q_ref, k_ref, v_ref, gate_ref, cos_ref, sin_ref, decay_ref, xi_ref,
                      zeta_ref, gchunk_ref, o_ref, r_ref):
    @pl.when(pl.program_id(0) == 0)
    def _():
        r_ref[...] = jnp.zeros_like(r_ref)

    cos = cos_ref[...]
    sin = sin_ref[...]
    half = RET_DIM // 2

    def rotary(t):
        t1, t2 = t[:, :half], t[:, half:]
        return jnp.concatenate([t1 * cos - t2 * sin, t1 * sin + t2 * cos], axis=-1)

    for b in range(q_ref.shape[0]):
        for h in range(RET_HEADS):
            cols = slice(h * RET_DIM, (h + 1) * RET_DIM)
            q = rotary(q_ref[b, :, cols].astype(F32)).astype(BF16)
            k = rotary(k_ref[b, :, cols].astype(F32)) * (RET_DIM ** -0.5)
            v = v_ref[b, :, cols]
            inner = _dot_nt(q, k.astype(BF16)) * decay_ref[h]
            r_old = r_ref[b, h]
            o = _dot(inner.astype(BF16), v) + _dot(q, r_old.astype(BF16)) * xi_ref[:, h:h + 1]
            kz = (k * zeta_ref[:, h:h + 1]).astype(BF16)
            r_ref[b, h] = r_old * gchunk_ref[h] + _dot(kz.T, v)
            o = o * lax.rsqrt(jnp.mean(o * o, axis=-1, keepdims=True) + RMS_EPS)
            gate = gate_ref[b, :, cols].astype(F32)
            o_ref[b, :, cols] = (gate * jax.nn.sigmoid(gate) * o).astype(o_ref.dtype)


def retention(proj, batch, seq):
    T = proj.shape[0]
    n_chunks = seq // CHUNK
    width = RET_HEADS * RET_DIM
    half = RET_DIM // 2
    inv = ROPE_BASE ** (-jnp.arange(half, dtype=F32) / half)
    ang = jnp.arange(seq).astype(F32)[:, None] * inv[None, :]
    cos, sin = jnp.cos(ang), jnp.sin(ang)
    log_g = jnp.log(1.0 - 2.0 ** (-5.0 - jnp.arange(RET_HEADS, dtype=F32)))
    i = jnp.arange(CHUNK, dtype=F32)
    diff = i[:, None] - i[None, :]
    decay = jnp.where(diff >= 0, jnp.exp(log_g[:, None, None] * jnp.maximum(diff, 0.0)), 0.0)
    xi = jnp.exp(log_g[None, :] * (i[:, None] + 1.0))
    zeta = jnp.exp(log_g[None, :] * (CHUNK - 1.0 - i[:, None]))
    gchunk = jnp.exp(log_g * CHUNK)

    def col_spec(col):
        return pl.BlockSpec((batch, CHUNK, width), lambda n: (0, n, col // width))

    proj3 = proj.reshape(batch, seq, proj.shape[1])
    out = pl.pallas_call(
        _retention_kernel,
        grid=(n_chunks,),
        in_specs=[col_spec(COL_RQ), col_spec(COL_RK), col_spec(COL_RV), col_spec(COL_RG),
                  pl.BlockSpec((CHUNK, half), lambda n: (n, 0)),
                  pl.BlockSpec((CHUNK, half), lambda n: (n, 0)),
                  pl.BlockSpec((RET_HEADS, CHUNK, CHUNK), lambda n: (0, 0, 0)),
                  pl.BlockSpec((CHUNK, RET_HEADS), lambda n: (0, 0)),
                  pl.BlockSpec((CHUNK, RET_HEADS), lambda n: (0, 0)),
                  pl.BlockSpec(memory_space=pltpu.SMEM)],
        out_specs=pl.BlockSpec((batch, CHUNK, width), lambda n: (0, n, 0)),
        out_shape=jax.ShapeDtypeStruct((batch, seq, width), BF16),
        scratch_shapes=[pltpu.VMEM((batch, RET_HEADS, RET_DIM, RET_DIM), F32)],
        compiler_params=_params("arbitrary"),
        name="retention",
    )(proj3, proj3, proj3, proj3, cos, sin, decay, xi, zeta, gchunk)
    return out.reshape(T, width)


def _gelu(x):
    return 0.5 * x * (1.0 + lax.erf(x * (2.0 ** -0.5)))


def _gmlp_kernel(u_ref, v_ref, lng_ref, lnb_ref, ws_ref, bs_ref, o_ref):
    v = _gelu(v_ref[...].astype(F32))
    mu = jnp.mean(v, axis=-1, keepdims=True)
    var = jnp.mean(jnp.square(v - mu), axis=-1, keepdims=True)
    vn = ((v - mu) * lax.rsqrt(var + RMS_EPS) * lng_ref[...] + lnb_ref[...]).astype(BF16)
    row = lax.broadcasted_iota(jnp.int32, (CHUNK, CHUNK), 0)
    col = lax.broadcasted_iota(jnp.int32, (CHUNK, CHUNK), 1)
    for g in range(GMLP_GROUPS):
        cols = slice(g * GMLP_GROUP_DIM, (g + 1) * GMLP_GROUP_DIM)
        w = jnp.where(row >= col, ws_ref[g], 0.0).astype(BF16)
        mixed = _dot(w, vn[:, cols]) + bs_ref[:, g:g + 1]
        u = _gelu(u_ref[:, cols].astype(F32))
        o_ref[:, cols] = (u * mixed).astype(o_ref.dtype)


def gmlp(proj, ln_g, ln_b, w_s, b_s):
    T = proj.shape[0]
    width = GMLP_GROUPS * GMLP_GROUP_DIM
    return pl.pallas_call(
        _gmlp_kernel,
        grid=(T // CHUNK,),
        in_specs=[pl.BlockSpec((CHUNK, width), lambda i: (i, COL_GU // width)),
                  pl.BlockSpec((CHUNK, width), lambda i: (i, COL_GV // width)),
                  pl.BlockSpec((1, width), lambda i: (0, 0)),
                  pl.BlockSpec((1, width), lambda i: (0, 0)),
                  pl.BlockSpec((GMLP_GROUPS, CHUNK, CHUNK), lambda i: (0, 0, 0)),
                  pl.BlockSpec((CHUNK, GMLP_GROUPS), lambda i: (0, 0))],
        out_specs=pl.BlockSpec((CHUNK, width), lambda i: (i, 0)),
        out_shape=jax.ShapeDtypeStruct((T, width), BF16),
        compiler_params=_params("parallel"),
        name="gmlp",
    )(proj, proj, ln_g.reshape(1, width), ln_b.reshape(1, width), w_s, b_s.T)


def _dsa_prep_kernel(dq_ref, dc_ref, iq_ref, g_ref, wuk_ref, qabs_ref, iqp_ref, ckv_ref, ckvt_ref):
    c = _rms(dc_ref[...].astype(F32), g_ref[...])
    ckv_ref[...] = c.astype(ckv_ref.dtype)
    ckvt_ref[0, :DSA_LATENT, :] = c.T.astype(ckvt_ref.dtype)
    ckvt_ref[0, DSA_LATENT:, :] = jnp.ones((ROW_TILE, DSA_BLOCK), ckvt_ref.dtype)
    zeros = jnp.zeros((DSA_BLOCK, LANES - IDX_DIM), iqp_ref.dtype)
    for h in range(DSA_HEADS):
        q = dq_ref[:, h * DSA_HEAD_DIM:(h + 1) * DSA_HEAD_DIM]
        qabs_ref[0, h] = (_dot_nt(q, wuk_ref[h]) * (DSA_HEAD_DIM ** -0.5 * LOG2_E)).astype(qabs_ref.dtype)
    for h in range(IDX_HEADS):
        iqp_ref[0, h] = jnp.concatenate([iq_ref[:, h * IDX_DIM:(h + 1) * IDX_DIM], zeros], axis=-1)


def dsa_prep(proj, kv_norm_g, w_uk):
    T = proj.shape[0]
    nblk = T // DSA_BLOCK
    qw = DSA_HEADS * DSA_HEAD_DIM
    iq_w = IDX_HEADS * IDX_DIM
    return pl.pallas_call(
        _dsa_prep_kernel,
        grid=(nblk,),
        in_specs=[pl.BlockSpec((DSA_BLOCK, qw), lambda i: (i, COL_DQ // qw)),
                  pl.BlockSpec((DSA_BLOCK, DSA_LATENT), lambda i: (i, COL_DC // DSA_LATENT)),
                  pl.BlockSpec((DSA_BLOCK, iq_w), lambda i: (i, COL_IQ // iq_w)),
                  pl.BlockSpec((1, DSA_LATENT), lambda i: (0, 0)),
                  pl.BlockSpec((DSA_HEADS, DSA_LATENT, DSA_HEAD_DIM), lambda i: (0, 0, 0))],
        out_specs=[pl.BlockSpec((1, DSA_HEADS, DSA_BLOCK, DSA_LATENT), lambda i: (i, 0, 0, 0)),
                   pl.BlockSpec((1, IDX_HEADS, DSA_BLOCK, LANES), lambda i: (i, 0, 0, 0)),
                   pl.BlockSpec((DSA_BLOCK, DSA_LATENT), lambda i: (i, 0)),
                   pl.BlockSpec((1, DSA_LATENT + ROW_TILE, DSA_BLOCK), lambda i: (i, 0, 0))],
        out_shape=[jax.ShapeDtypeStruct((nblk, DSA_HEADS, DSA_BLOCK, DSA_LATENT), BF16),
                   jax.ShapeDtypeStruct((nblk, IDX_HEADS, DSA_BLOCK, LANES), BF16),
                   jax.ShapeDtypeStruct((T, DSA_LATENT), BF16),
                   jax.ShapeDtypeStruct((nblk, DSA_LATENT + ROW_TILE, DSA_BLOCK), BF16)],
        compiler_params=_params("parallel"),
        name="dsa_prep",
    )(proj, proj, proj, kv_norm_g.reshape(1, DSA_LATENT), w_uk)


def _dsa_kernel(iqp_ref, wq_ref, kidx_ref, ckv_ref, ckvt_ref, qa_ref, wuv_ref, o_ref,
                score_ref, m_ref, acc_ref, *, k_sel):
    QB = DSA_BLOCK
    blk = pl.program_id(1)
    n_tiles = blk + 1
    ksel_f = float(k_sel)
    key_in_tile = lax.broadcasted_iota(jnp.int32, (QB, QB), 0)
    query_in_blk = lax.broadcasted_iota(jnp.int32, (QB, QB), 1)

    def fold8(x, op):
        return op(x.reshape(QB // 8, 8, QB), axis=0)

    def tile_rows(j):
        return pl.ds(pl.multiple_of(j * QB, QB), QB)

    wi_t = wq_ref[...].astype(F32).T * (IDX_HEADS ** -0.5 * IDX_DIM ** -0.5)

    def score_body(j, stats):
        s_max, s_min, n_ge0, n_gt0 = stats
        kt = kidx_ref[tile_rows(j), :]
        sc = jnp.zeros((QB, QB), F32)
        for h in range(IDX_HEADS):
            lg = _dot_nt(kt, iqp_ref[0, h])
            sc = sc + wi_t[IDX_DIM + h:IDX_DIM + h + 1, :] * jnp.maximum(lg, 0.0)
        causal = (j < blk) | (key_in_tile <= query_in_blk)
        masked = jnp.where(causal, sc, -jnp.inf)
        score_ref[j] = masked
        return (jnp.maximum(s_max, fold8(masked, jnp.max)),
                jnp.minimum(s_min, fold8(jnp.where(causal, sc, jnp.inf), jnp.min)),
                n_ge0 + fold8((masked >= 0.0).astype(F32), jnp.sum),
                n_gt0 + fold8((masked > 0.0).astype(F32), jnp.sum))

    stats = lax.fori_loop(0, n_tiles, score_body,
                          (jnp.full((8, QB), -jnp.inf, F32), jnp.full((8, QB), jnp.inf, F32),
                           jnp.zeros((8, QB), F32), jnp.zeros((8, QB), F32)))
    hi0 = jnp.max(stats[0], axis=0, keepdims=True)
    lo0 = jnp.min(stats[1], axis=0, keepdims=True)
    cnt_ge0 = jnp.sum(stats[2], axis=0, keepdims=True)
    cnt_gt0 = jnp.sum(stats[3], axis=0, keepdims=True)

    COUNT_LANES = 4

    def count_negative(diff):
        def body(j, acc):
            sign = lax.shift_right_logical(lax.bitcast_convert_type(diff(score_ref[j]), jnp.uint32),
                                           jnp.uint32(31))
            return acc + jnp.sum(sign.astype(jnp.int32).reshape(-1, COUNT_LANES, 8, QB), axis=0)
        acc = lax.fori_loop(0, n_tiles, body, jnp.zeros((COUNT_LANES, 8, QB), jnp.int32))
        return jnp.sum(acc.reshape(COUNT_LANES * 8, QB), axis=0, keepdims=True).astype(F32)

    n_stored = (n_tiles * QB).astype(F32)

    def count_ge(t):
        return n_stored - count_negative(lambda s: s - t)

    n_causal = (blk * QB + 1 + lax.broadcasted_iota(jnp.int32, (1, QB), 1)).astype(F32)
    need = n_causal > ksel_f
    cnt_hi0 = count_ge(hi0)
    above0 = cnt_gt0 >= ksel_f
    below0 = cnt_ge0 < ksel_f
    lo0 = jnp.where(below0, lo0, 0.0)
    cnt_lo0 = jnp.where(below0, n_causal, cnt_ge0)
    hi0 = jnp.where(above0, hi0, 0.0)
    at_top = above0 & (cnt_hi0 >= ksel_f)
    lo0 = jnp.where(at_top, hi0, lo0)
    cnt_lo0 = jnp.where(at_top, cnt_hi0, cnt_lo0)
    cnt_hi0 = jnp.where(above0, cnt_hi0, cnt_ge0)

    def unresolved(lo, hi, cnt_lo, cnt_hi):
        return (cnt_lo != ksel_f) & (lo < hi) & (cnt_lo - cnt_hi > 2.0)

    active0 = need & (above0 | below0) & unresolved(lo0, hi0, cnt_lo0, cnt_hi0)

    def bisect_cond(state):
        return jnp.max(state[4]) > 0.0

    def bisect_body(state):
        lo, hi, cnt_lo, cnt_hi, active = state
        mid = lo + 0.5 * (hi - lo)
        c = count_ge(mid)
        ge = c >= ksel_f
        live = active > 0.0
        moved = (mid > lo) & (mid < hi)
        new_lo = jnp.where(live & ge, mid, lo)
        new_hi = jnp.where(live & ~ge, mid, hi)
        new_cnt_lo = jnp.where(live & ge, c, cnt_lo)
        new_cnt_hi = jnp.where(live & ~ge, c, cnt_hi)
        new_active = live & moved & unresolved(new_lo, new_hi, new_cnt_lo, new_cnt_hi)
        return new_lo, new_hi, new_cnt_lo, new_cnt_hi, new_active.astype(F32)

    lo, hi, cnt_lo, cnt_hi, _ = lax.while_loop(
        bisect_cond, bisect_body, (lo0, hi0, cnt_lo0, cnt_hi0, active0.astype(F32)))

    def max_below_body(j, acc):
        s = score_ref[j]
        return jnp.maximum(acc, jnp.max(jnp.where(s < hi, s, -jnp.inf).reshape(-1, COUNT_LANES, 8, QB), axis=0))
    max_below = lax.fori_loop(0, n_tiles, max_below_body, jnp.full((COUNT_LANES, 8, QB), -jnp.inf, F32))
    max_below = jnp.max(max_below.reshape(COUNT_LANES * 8, QB), axis=0, keepdims=True)
    is_pair = (cnt_lo != ksel_f) & (lo < hi) & (cnt_lo - cnt_hi == 2.0)
    thr = jnp.where(need, jnp.where(is_pair, max_below, lo), -jnp.inf)
    n_gt = count_negative(lambda s: thr - s)
    tie_quota = jnp.where(need, ksel_f - n_gt, 0.0)

    m_ref[...] = jnp.full_like(m_ref, NEG_BIG)
    acc_ref[...] = jnp.zeros_like(acc_ref)
    strict_lower = (query_in_blk < key_in_tile).astype(BF16)

    def attn_body(j, tie_seen):
        sc = score_ref[j]
        eq = sc == thr
        eq_f = eq.astype(F32)
        prefix = _dot(strict_lower, eq_f.astype(BF16)) + tie_seen
        sel = (sc > thr) | (eq & (prefix < tie_quota))
        bias = jnp.where(sel, 0.0, NEG_BIG)
        c_t = ckv_ref[tile_rows(j), :]
        ct_t = ckvt_ref[j]
        for h in range(DSA_HEADS):
            s = _dot_nt(c_t, qa_ref[0, h]) + bias
            m_old = m_ref[h]
            m_new = jnp.maximum(m_old, jnp.max(fold8(s, jnp.max), axis=0, keepdims=True))
            alpha = jnp.exp2(m_old - m_new)
            p = jnp.exp2(s - m_new)
            acc_ref[h] = alpha * acc_ref[h] + _dot(ct_t, p.astype(BF16))
            m_ref[h] = m_new
        return tie_seen + jnp.sum(fold8(eq_f, jnp.sum), axis=0, keepdims=True)

    lax.fori_loop(0, n_tiles, attn_body, jnp.zeros((1, QB), F32))

    for h in range(DSA_HEADS):
        acc = acc_ref[h]
        o_lat = (acc[:DSA_LATENT] / acc[DSA_LATENT:DSA_LATENT + 1]).T.astype(BF16)
        o_ref[:, h * DSA_HEAD_DIM:(h + 1) * DSA_HEAD_DIM] = _dot(o_lat, wuv_ref[h]).astype(o_ref.dtype)


def dsa_attention(proj, q_abs, iq_pad, ckv, ckv_t, w_uv, batch, seq):
    T = proj.shape[0]
    nb = seq // DSA_BLOCK
    k_sel = min(DSA_TOPK_MAX, seq // 4)
    out_w = DSA_HEADS * DSA_HEAD_DIM
    return pl.pallas_call(
        functools.partial(_dsa_kernel, k_sel=k_sel),
        grid=(batch, nb),
        in_specs=[pl.BlockSpec((1, IDX_HEADS, DSA_BLOCK, LANES), lambda b, i: (b * nb + i, 0, 0, 0)),
                  pl.BlockSpec((DSA_BLOCK, LANES), lambda b, i: (b * nb + i, COL_IKW // LANES)),
                  pl.BlockSpec((seq, LANES), lambda b, i: (b, COL_IKW // LANES)),
                  pl.BlockSpec((seq, DSA_LATENT), lambda b, i: (b, 0)),
                  pl.BlockSpec((nb, DSA_LATENT + ROW_TILE, DSA_BLOCK), lambda b, i: (b, 0, 0)),
                  pl.BlockSpec((1, DSA_HEADS, DSA_BLOCK, DSA_LATENT), lambda b, i: (b * nb + i, 0, 0, 0)),
                  pl.BlockSpec((DSA_HEADS, DSA_LATENT, DSA_HEAD_DIM), lambda b, i: (0, 0, 0))],
        out_specs=pl.BlockSpec((DSA_BLOCK, out_w), lambda b, i: (b * nb + i, 0)),
        out_shape=jax.ShapeDtypeStruct((T, out_w), BF16),
        scratch_shapes=[pltpu.VMEM((nb, DSA_BLOCK, DSA_BLOCK), F32),
                        pltpu.VMEM((DSA_HEADS, 1, DSA_BLOCK), F32),
                        pltpu.VMEM((DSA_HEADS, DSA_LATENT + ROW_TILE, DSA_BLOCK), F32)],
        compiler_params=_params("parallel", "arbitrary"),
        name="dsa_attention",
    )(iq_pad, proj, proj, ckv, ckv_t, q_abs, w_uv)


def _merge_kernel(ret_ref, gm_ref, ds_ref, wr_ref, wg_ref, wd_ref, ga_ref, gb_ref, gc_ref, o_ref):
    def branch(a_ref, w_ref, gate_ref):
        return jax.nn.sigmoid(gate_ref[...].astype(F32)) * _dot(a_ref[...], w_ref[...])

    merged = branch(ret_ref, wr_ref, ga_ref) + branch(gm_ref, wg_ref, gb_ref) + branch(ds_ref, wd_ref, gc_ref)
    o_ref[...] = merged.astype(o_ref.dtype)


def merge_branches(ret, gm, ds, w_ret_o, w_gmlp_o, w_dsa_o, proj, d_model, *, tm=1024, tn=512):
    T, K = ret.shape
    gate_blk = COL_GATES // tn
    per_gate = d_model // tn
    act = pl.BlockSpec((tm, K), lambda i, j: (i, 0))
    wgt = pl.BlockSpec((K, tn), lambda i, j: (0, j))

    def gate(which):
        return pl.BlockSpec((tm, tn), lambda i, j: (i, gate_blk + which * per_gate + j))

    return pl.pallas_call(
        _merge_kernel,
        grid=(T // tm, d_model // tn),
        in_specs=[act, act, act, wgt, wgt, wgt, gate(0), gate(1), gate(2)],
        out_specs=pl.BlockSpec((tm, tn), lambda i, j: (i, j)),
        out_shape=jax.ShapeDtypeStruct((T, d_model), BF16),
        compiler_params=_params("parallel", "arbitrary"),
        name="merge_branches",
    )(ret, gm, ds, w_ret_o, w_gmlp_o, w_dsa_o, proj, proj, proj)


def _out_proj_kernel(a_ref, w_ref, x_ref, o_ref):
    o_ref[...] = x_ref[...] + _dot(a_ref[...], w_ref[...])


def out_proj_residual(merged, w_out, x, *, tm=1024, tn=512):
    T, K = merged.shape
    N = w_out.shape[1]
    return pl.pallas_call(
        _out_proj_kernel,
        grid=(T // tm, N // tn),
        in_specs=[pl.BlockSpec((tm, K), lambda i, j: (i, 0)),
                  pl.BlockSpec((K, tn), lambda i, j: (0, j)),
                  pl.BlockSpec((tm, tn), lambda i, j: (i, j))],
        out_specs=pl.BlockSpec((tm, tn), lambda i, j: (i, j)),
        out_shape=jax.ShapeDtypeStruct((T, N), F32),
        compiler_params=_params("parallel", "arbitrary"),
        name="out_proj_residual",
    )(merged, w_out, x)


def _pack_rows(h_bf16, o_ref):
    n, d = h_bf16.shape
    bits = lax.bitcast_convert_type(h_bf16.astype(F32), jnp.uint32)
    packed = (bits[:, :d // 2] >> 16) | (bits[:, d // 2:] & jnp.uint32(0xFFFF0000))
    for c in range(d // 2 // LANES):
        o_ref[pl.ds(c, n, stride=ROW_TILE), :] = packed[:, c * LANES:(c + 1) * LANES]


def _unpack_rows(x_ref, o_ref):
    n, d = o_ref.shape
    for c in range(d // 2 // LANES):
        w = x_ref[pl.ds(c, n, stride=ROW_TILE), :]
        lo = lax.bitcast_convert_type(w << 16, F32)
        hi = lax.bitcast_convert_type(w & jnp.uint32(0xFFFF0000), F32)
        o_ref[:, c * LANES:(c + 1) * LANES] = lo.astype(o_ref.dtype)
        o_ref[:, d // 2 + c * LANES:d // 2 + (c + 1) * LANES] = hi.astype(o_ref.dtype)


def _router_kernel(x_ref, g_ref, w_ref, b_ref, hp_ref, ids_ref, wts_ref):
    h = _rms(x_ref[...], g_ref[...]).astype(BF16)
    _pack_rows(h, hp_ref)
    logits = _dot(h, w_ref[...]) + b_ref[...]
    lane = lax.broadcasted_iota(jnp.int32, logits.shape, 1)
    big = jnp.int32(LANES)

    def first_argmax(vals, valid):
        v = jnp.where(valid, vals, -jnp.inf)
        m = jnp.max(v, axis=-1, keepdims=True)
        idx = jnp.min(jnp.where(valid & (v == m), lane, big), axis=-1, keepdims=True)
        return m, idx

    is_grp = lane < MOE_GROUPS
    g_max, g_sel = first_argmax(logits, is_grp)
    g_den = jnp.sum(jnp.where(is_grp, jnp.exp(logits - g_max), 0.0), axis=-1, keepdims=True)
    p_g = 1.0 / g_den
    e_lo = MOE_GROUPS + g_sel * MOE_EXPERTS_PER_GROUP
    in_grp = (lane >= e_lo) & (lane < e_lo + MOE_EXPERTS_PER_GROUP)
    m1, i1 = first_argmax(logits, in_grp)
    m2, i2 = first_argmax(logits, in_grp & (lane != i1))
    e2 = jnp.exp(m2 - m1)
    w1 = p_g / (1.0 + e2)
    w2 = p_g * e2 / (1.0 + e2)
    ids_ref[...] = jnp.where(lane == 0, i1 - MOE_GROUPS, jnp.where(lane == 1, i2 - MOE_GROUPS, 0))
    wts_ref[...] = jnp.where(lane == 0, w1, jnp.where(lane == 1, w2, 0.0))


def moe_router(x, g, w_router, b_router, *, tm=512):
    T, D = x.shape
    assert D == 2 * ROW_TILE * LANES
    return pl.pallas_call(
        _router_kernel,
        grid=(T // tm,),
        in_specs=[pl.BlockSpec((tm, D), lambda i: (i, 0)),
                  pl.BlockSpec((1, D), lambda i: (0, 0)),
                  pl.BlockSpec((D, LANES), lambda i: (0, 0)),
                  pl.BlockSpec((1, LANES), lambda i: (0, 0))],
        out_specs=[pl.BlockSpec((tm * ROW_TILE, LANES), lambda i: (i, 0)),
                   pl.BlockSpec((tm, LANES), lambda i: (i, 0)),
                   pl.BlockSpec((tm, LANES), lambda i: (i, 0))],
        out_shape=[jax.ShapeDtypeStruct((T * ROW_TILE, LANES), jnp.uint32),
                   jax.ShapeDtypeStruct((T, LANES), jnp.int32),
                   jax.ShapeDtypeStruct((T, LANES), F32)],
        compiler_params=_params("parallel"),
        name="moe_router",
    )(x, g, w_router, b_router)


def _route_tables(ids, n_rows):
    T = ids.shape[0]
    e = ids[:, :MOE_TOPK].reshape(-1)
    onehot = (e[:, None] == jnp.arange(MOE_EXPERTS, dtype=jnp.int32)[None, :]).astype(jnp.int32)
    rank = jnp.sum((jnp.cumsum(onehot, axis=0) - onehot) * onehot, axis=1)
    counts = jnp.sum(onehot, axis=0)
    padded = (counts + MOE_TILE - 1) // MOE_TILE * MOE_TILE
    ends = jnp.cumsum(padded)
    pos = ((ends - padded)[e] + rank).astype(jnp.int32)
    tile_start = jnp.arange(n_rows // MOE_TILE, dtype=jnp.int32) * MOE_TILE
    tile_expert = jnp.sum((ends[None, :] <= tile_start[:, None]).astype(jnp.int32), axis=1)
    tile_expert = jnp.minimum(tile_expert, MOE_EXPERTS - 1)
    n_used = (ends[-1] // MOE_TILE).astype(jnp.int32).reshape(1)
    return pos, tile_expert, n_used


def _token_rows(r, rows_per_token):
    return pl.ds(pl.multiple_of(r * rows_per_token, rows_per_token), rows_per_token)


def _scatter_rows_kernel(pos_ref, hp_ref, xs_in_hbm, xs_hbm, sem):
    del xs_in_hbm
    n = hp_ref.shape[0] // ROW_TILE
    base = pl.program_id(0) * n
    for slot in range(MOE_TOPK):
        def issue(r, carry, slot=slot):
            row = pos_ref[(base + r) * MOE_TOPK + slot]
            pltpu.make_async_copy(hp_ref.at[_token_rows(r, ROW_TILE)],
                                  xs_hbm.at[_token_rows(row, ROW_TILE)], sem.at[slot]).start()
            return carry
        lax.fori_loop(0, n, issue, 0, unroll=DMA_ISSUE_UNROLL)
    for slot in range(MOE_TOPK):
        pltpu.make_async_copy(hp_ref, xs_hbm.at[pl.ds(0, n * ROW_TILE)], sem.at[slot]).wait()


def scatter_rows(hp, pos, n_rows, *, tile=256):
    T = hp.shape[0] // ROW_TILE
    return pl.pallas_call(
        _scatter_rows_kernel,
        grid_spec=pltpu.PrefetchScalarGridSpec(
            num_scalar_prefetch=1,
            grid=(T // tile,),
            in_specs=[pl.BlockSpec((tile * ROW_TILE, LANES), lambda i, pos: (i, 0)),
                      pl.BlockSpec(memory_space=pl.ANY)],
            out_specs=pl.BlockSpec(memory_space=pl.ANY),
            scratch_shapes=[pltpu.SemaphoreType.DMA((MOE_TOPK,))]),
        out_shape=jax.ShapeDtypeStruct((n_rows * ROW_TILE, LANES), jnp.uint32),
        input_output_aliases={2: 0},
        compiler_params=_params("arbitrary"),
        name="moe_scatter_rows",
    )(pos, hp, jnp.zeros((n_rows * ROW_TILE, LANES), jnp.uint32))


def _moe_experts_kernel(tile_expert_ref, n_used_ref, x_ref, w1_ref, w3_ref, w2_ref, o_ref,
                        xb_ref, w1b_ref, w3b_ref, w2b_ref):
    i = pl.program_id(0)

    @pl.when((i == 0) | (tile_expert_ref[i] != tile_expert_ref[jnp.maximum(i - 1, 0)]))
    def _():
        w1b_ref[...] = w1_ref[0].astype(BF16)
        w3b_ref[...] = w3_ref[0].astype(BF16)
        w2b_ref[...] = w2_ref[0].astype(BF16)

    @pl.when(i < n_used_ref[0])
    def _():
        _unpack_rows(x_ref, xb_ref)
        x = xb_ref[...]
        a = _dot(x, w1b_ref[...])
        hid = a * jax.nn.sigmoid(a) * _dot(x, w3b_ref[...])
        o_ref[...] = _dot(hid.astype(BF16), w2b_ref[...])

    @pl.when(i >= n_used_ref[0])
    def _():
        o_ref[...] = jnp.zeros_like(o_ref)


def moe_experts(xs, tile_expert, n_used, w1, w3, w2):
    E, D, F = w1.shape
    P = xs.shape[0] // ROW_TILE

    def wspec(shape):
        return pl.BlockSpec((1,) + shape, lambda i, te, nu: (te[i], 0, 0))

    return pl.pallas_call(
        _moe_experts_kernel,
        grid_spec=pltpu.PrefetchScalarGridSpec(
            num_scalar_prefetch=2,
            grid=(P // MOE_TILE,),
            in_specs=[pl.BlockSpec((MOE_TILE * ROW_TILE, LANES), lambda i, te, nu: (i, 0)),
                      wspec((D, F)), wspec((D, F)), wspec((F, D))],
            out_specs=pl.BlockSpec((MOE_TILE, D), lambda i, te, nu: (i, 0)),
            scratch_shapes=[pltpu.VMEM((MOE_TILE, D), BF16), pltpu.VMEM((D, F), BF16),
                            pltpu.VMEM((D, F), BF16), pltpu.VMEM((F, D), BF16)]),
        out_shape=jax.ShapeDtypeStruct((P, D), F32),
        compiler_params=_params("arbitrary"),
        name="moe_experts",
    )(tile_expert, n_used, xs, w1, w3, w2)


def _moe_combine_kernel(pos_ref, ys_hbm, x_ref, wts_ref, g_ref, o_ref, y_ref, sem, *, final_norm):
    n = x_ref.shape[0]
    i = pl.program_id(0)

    def start_gather(step, buf):
        for slot in range(MOE_TOPK):
            def issue(r, carry, slot=slot):
                row = pos_ref[(step * n + r) * MOE_TOPK + slot]
                pltpu.make_async_copy(ys_hbm.at[pl.ds(row, 1)], y_ref.at[buf, slot, pl.ds(r, 1)],
                                      sem.at[buf, slot]).start()
                return carry
            lax.fori_loop(0, n, issue, 0, unroll=DMA_ISSUE_UNROLL)

    @pl.when(i == 0)
    def _():
        start_gather(0, 0)

    @pl.when(i + 1 < pl.num_programs(0))
    def _():
        start_gather(i + 1, (i + 1) % 2)

    buf = i % 2
    for slot in range(MOE_TOPK):
        pltpu.make_async_copy(ys_hbm.at[pl.ds(0, n)], y_ref.at[buf, slot], sem.at[buf, slot]).wait()
    out = x_ref[...] + (wts_ref[:, 0:1] * y_ref[buf, 0] + wts_ref[:, 1:2] * y_ref[buf, 1])
    o_ref[...] = _rms(out, g_ref[...]) if final_norm else out


def moe_combine(ys, pos, wts, x, g, *, final_norm, tile=256):
    T, D = x.shape
    return pl.pallas_call(
        functools.partial(_moe_combine_kernel, final_norm=final_norm),
        grid_spec=pltpu.PrefetchScalarGridSpec(
            num_scalar_prefetch=1,
            grid=(T // tile,),
            in_specs=[pl.BlockSpec(memory_space=pl.ANY),
                      pl.BlockSpec((tile, D), lambda i, pos: (i, 0)),
                      pl.BlockSpec((tile, LANES), lambda i, pos: (i, 0)),
                      pl.BlockSpec((1, D), lambda i, pos: (0, 0))],
            out_specs=pl.BlockSpec((tile, D), lambda i, pos: (i, 0)),
            scratch_shapes=[pltpu.VMEM((2, MOE_TOPK, tile, D), F32),
                            pltpu.SemaphoreType.DMA((2, MOE_TOPK))]),
        out_shape=jax.ShapeDtypeStruct((T, D), F32),
        compiler_params=_params("arbitrary"),
        name="moe_combine",
    )(pos, ys, x, wts, g)


def _pack_w_in(w):
    D = w.shape[0]
    widths = (1024, 1024, 1024, 1024, 1024, 1024, 1024, 256, 512, 64, 8, D, D, D)
    offs = [0]
    for wd in widths:
        offs.append(offs[-1] + wd)
    rq, rk, rv, rg, gu, gv, dq, dc, iq, ik, iw, ga, gb, gc = (
        w[:, offs[i]:offs[i + 1]] for i in range(len(widths)))
    ikw_pad = jnp.zeros((D, COL_GATES - COL_IKW - IDX_DIM - IDX_HEADS), w.dtype)
    packed = jnp.concatenate([rq, rk, rv, rg, gu, gv, dq, iq, dc, ik, iw, ikw_pad, ga, gb, gc], axis=1)
    assert packed.shape[1] == PROJ_WIDTH
    return packed.astype(BF16)


def _pack_router(w_group, b_group, w_expert, b_expert):
    D = w_group.shape[0]
    w_e = jnp.transpose(w_expert, (1, 0, 2)).reshape(D, MOE_EXPERTS)
    pad = LANES - MOE_GROUPS - MOE_EXPERTS
    w = jnp.concatenate([w_group, w_e, jnp.zeros((D, pad), w_group.dtype)], axis=1).astype(BF16)
    b = jnp.concatenate([b_group, b_expert.reshape(MOE_EXPERTS), jnp.zeros((pad,), b_group.dtype)])
    return w, b.reshape(1, LANES).astype(F32)


def kernel(x, norm_mix_g, w_in, w_ret_o, gmlp_ln_g, gmlp_ln_b, gmlp_w_s, gmlp_b_s, w_gmlp_o, dsa_kv_norm_g, dsa_w_uk, dsa_w_uv, w_dsa_o, w_out, norm_ffn_g, moe_w_group, moe_b_group, moe_w_expert, moe_b_expert, moe_w1, moe_w3, moe_w2, final_norm_g):
    B, S, D = x.shape
    depth = w_in.shape[0]
    assert S % DSA_BLOCK == 0 and D == 2048
    xt = x.reshape(B * S, D)
    for l in range(depth):
        proj = norm_matmul(xt, norm_mix_g[l].reshape(1, D), _pack_w_in(w_in[l]))
        ret = retention(proj, B, S)
        gm = gmlp(proj, gmlp_ln_g[l], gmlp_ln_b[l], gmlp_w_s[l], gmlp_b_s[l])
        q_abs, iq_pad, ckv, ckv_t = dsa_prep(proj, dsa_kv_norm_g[l], dsa_w_uk[l].astype(BF16))
        ds = dsa_attention(proj, q_abs, iq_pad, ckv, ckv_t, dsa_w_uv[l].astype(BF16), B, S)
        merged = merge_branches(ret, gm, ds, w_ret_o[l].astype(BF16), w_gmlp_o[l].astype(BF16),
                                w_dsa_o[l].astype(BF16), proj, D)
        xt = out_proj_residual(merged, w_out[l].astype(BF16), xt)
        w_router, b_router = _pack_router(moe_w_group[l], moe_b_group[l], moe_w_expert[l], moe_b_expert[l])
        hp, ids, wts = moe_router(xt, norm_ffn_g[l].reshape(1, D), w_router, b_router)
        n_rows = MOE_TOPK * B * S + MOE_EXPERTS * MOE_TILE
        pos, tile_expert, n_used = _route_tables(ids, n_rows)
        xs = scatter_rows(hp, pos, n_rows)
        F = moe_w1.shape[-1]
        ys = moe_experts(xs, tile_expert + l * MOE_EXPERTS, n_used,
                         moe_w1.reshape(depth * MOE_EXPERTS, D, F),
                         moe_w3.reshape(depth * MOE_EXPERTS, D, F),
                         moe_w2.reshape(depth * MOE_EXPERTS, F, D))
        last = l == depth - 1
        xt = moe_combine(ys, pos, wts, xt, final_norm_g.reshape(1, D), final_norm=last)
    return xt.reshape(B, S, D)
```

```python
import functools
import math

import jax
import jax.numpy as jnp
from jax import lax
from jax.experimental import pallas as pl
from jax.experimental.pallas import tpu as pltpu

F32 = jnp.float32
BF16 = jnp.bfloat16

RMS_EPS = 1e-6
RET_HEADS = 4
RET_DIM = 256
CHUNK = 128
ROPE_BASE = 10000.0
GMLP_GROUPS = 4
GMLP_GROUP_DIM = 256
DSA_HEADS = 8
DSA_HEAD_DIM = 128
DSA_LATENT = 256
IDX_HEADS = 8
IDX_DIM = 64
DSA_TOPK_MAX = 256
DSA_BLOCK = 256
MOE_GROUPS = 4
MOE_EXPERTS_PER_GROUP = 8
MOE_EXPERTS = MOE_GROUPS * MOE_EXPERTS_PER_GROUP
MOE_HIDDEN = 512
MOE_TOPK = 2
MOE_TILE = 256
LANES = 128
ROW_TILE = 8
DMA_ISSUE_UNROLL = 8
NEG_BIG = -1e30
LOG2_E = math.log2(math.e)

COL_RQ, COL_RK, COL_RV, COL_RG = 0, 1024, 2048, 3072
COL_GU, COL_GV = 4096, 5120
COL_DQ, COL_IQ, COL_DC, COL_IKW = 6144, 7168, 7680, 7936
COL_GATES = 8192
PROJ_WIDTH = 14336
VMEM_LIMIT = 56 * 1024 * 1024


def _params(*sem):
    return pltpu.CompilerParams(dimension_semantics=sem, vmem_limit_bytes=VMEM_LIMIT)


def _rms(x, g):
    return x * lax.rsqrt(jnp.mean(x * x, axis=-1, keepdims=True) + RMS_EPS) * g


def _dot(a, b):
    return jnp.dot(a, b, preferred_element_type=F32)


def _dot_nt(a, b):
    return lax.dot_general(a, b, (((1,), (1,)), ((), ())), preferred_element_type=F32)


def _norm_matmul_kernel(x_ref, g_ref, w_ref, o_ref, h_ref):
    @pl.when(pl.program_id(1) == 0)
    def _():
        h_ref[...] = _rms(x_ref[...], g_ref[...]).astype(BF16)

    o_ref[...] = _dot(h_ref[...], w_ref[...]).astype(o_ref.dtype)


def norm_matmul(x, g, w, *, tm=1024, tn=512, out_dtype=BF16):
    T, D = x.shape
    N = w.shape[1]
    return pl.pallas_call(
        _norm_matmul_kernel,
        grid=(T // tm, N // tn),
        in_specs=[pl.BlockSpec((tm, D), lambda i, j: (i, 0)),
                  pl.BlockSpec((1, D), lambda i, j: (0, 0)),
                  pl.BlockSpec((D, tn), lambda i, j: (0, j))],
        out_specs=pl.BlockSpec((tm, tn), lambda i, j: (i, j)),
        out_shape=jax.ShapeDtypeStruct((T, N), out_dtype),
        scratch_shapes=[pltpu.VMEM((tm, D), BF16)],
        compiler_params=_params("parallel", "arbitrary"),
        name="norm_in_proj",
    )(x, g, w)


def _retention_kernel(q_ref, k_ref, v_ref, gate_ref, cos_ref, sin_ref, decay_ref, xi_ref,
                      zeta_ref, gchunk_ref, o_ref, r_ref):
    @pl.when(pl.program_id(0) == 0)
    def _():
        r_ref[...] = jnp.zeros_like(r_ref)

    cos = cos_ref[...]
    sin = sin_ref[...]
    half = RET_DIM // 2

    def rotary(t):
        t1, t2 = t[:, :half], t[:, half:]
        return jnp.concatenate([t1 * cos - t2 * sin, t1 * sin + t2 * cos], axis=-1)

    for b in range(q_ref.shape[0]):
        for h in range(RET_HEADS):
            cols = slice(h * RET_DIM, (h + 1) * RET_DIM)
            q = rotary(q_ref[b, :, cols].astype(F32)).astype(BF16)
            k = rotary(k_ref[b, :, cols].astype(F32)) * (RET_DIM ** -0.5)
            v = v_ref[b, :, cols]
            inner = _dot_nt(q, k.astype(BF16)) * decay_ref[h]
            r_old = r_ref[b, h]
            o = _dot(inner.astype(BF16), v) + _dot(q, r_old.astype(BF16)) * xi_ref[:, h:h + 1]
            kz = (k * zeta_ref[:, h:h + 1]).astype(BF16)
            r_ref[b, h] = r_old * gchunk_ref[h] + _dot(kz.T, v)
            o = o * lax.rsqrt(jnp.mean(o * o, axis=-1, keepdims=True) + RMS_EPS)
            gate = gate_ref[b, :, cols].astype(F32)
            o_ref[b, :, cols] = (gate * jax.nn.sigmoid(gate) * o).astype(o_ref.dtype)


def retention(proj, batch, seq):
    T = proj.shape[0]
    n_chunks = seq // CHUNK
    width = RET_HEADS * RET_DIM
    half = RET_DIM // 2
    inv = ROPE_BASE ** (-jnp.arange(half, dtype=F32) / half)
    ang = jnp.arange(seq).astype(F32)[:, None] * inv[None, :]
    cos, sin = jnp.cos(ang), jnp.sin(ang)
    log_g = jnp.log(1.0 - 2.0 ** (-5.0 - jnp.arange(RET_HEADS, dtype=F32)))
    i = jnp.arange(CHUNK, dtype=F32)
    diff = i[:, None] - i[None, :]
    decay = jnp.where(diff >= 0, jnp.exp(log_g[:, None, None] * jnp.maximum(diff, 0.0)), 0.0)
    xi = jnp.exp(log_g[None, :] * (i[:, None] + 1.0))
    zeta = jnp.exp(log_g[None, :] * (CHUNK - 1.0 - i[:, None]))
    gchunk = jnp.exp(log_g * CHUNK)

    def col_spec(col):
        return pl.BlockSpec((batch, CHUNK, width), lambda n: (0, n, col // width))

    proj3 = proj.reshape(batch, seq, proj.shape[1])
    out = pl.pallas_call(
        _retention_kernel,
        grid=(n_chunks,),
        in_specs=[col_spec(COL_RQ), col_spec(COL_RK), col_spec(COL_RV), col_spec(COL_RG),
                  pl.BlockSpec((CHUNK, half), lambda n: (n, 0)),
                  pl.BlockSpec((CHUNK, half), lambda n: (n, 0)),
                  pl.BlockSpec((RET_HEADS, CHUNK, CHUNK), lambda n: (0, 0, 0)),
                  pl.BlockSpec((CHUNK, RET_HEADS), lambda n: (0, 0)),
                  pl.BlockSpec((CHUNK, RET_HEADS), lambda n: (0, 0)),
                  pl.BlockSpec(memory_space=pltpu.SMEM)],
        out_specs=pl.BlockSpec((batch, CHUNK, width), lambda n: (0, n, 0)),
        out_shape=jax.ShapeDtypeStruct((batch, seq, width), BF16),
        scratch_shapes=[pltpu.VMEM((batch, RET_HEADS, RET_DIM, RET_DIM), F32)],
        compiler_params=_params("arbitrary"),
        name="retention",
    )(proj3, proj3, proj3, proj3, cos, sin, decay, xi, zeta, gchunk)
    return out.reshape(T, width)


def _gelu(x):
    return 0.5 * x * (1.0 + lax.erf(x * (2.0 ** -0.5)))


def _gmlp_kernel(u_ref, v_ref, lng_ref, lnb_ref, ws_ref, bs_ref, o_ref):
    v = _gelu(v_ref[...].astype(F32))
    mu = jnp.mean(v, axis=-1, keepdims=True)
    var = jnp.mean(jnp.square(v - mu), axis=-1, keepdims=True)
    vn = ((v - mu) * lax.rsqrt(var + RMS_EPS) * lng_ref[...] + lnb_ref[...]).astype(BF16)
    row = lax.broadcasted_iota(jnp.int32, (CHUNK, CHUNK), 0)
    col = lax.broadcasted_iota(jnp.int32, (CHUNK, CHUNK), 1)
    for g in range(GMLP_GROUPS):
        cols = slice(g * GMLP_GROUP_DIM, (g + 1) * GMLP_GROUP_DIM)
        w = jnp.where(row >= col, ws_ref[g], 0.0).astype(BF16)
        mixed = _dot(w, vn[:, cols]) + bs_ref[:, g:g + 1]
        u = _gelu(u_ref[:, cols].astype(F32))
        o_ref[:, cols] = (u * mixed).astype(o_ref.dtype)


def gmlp(proj, ln_g, ln_b, w_s, b_s):
    T = proj.shape[0]
    width = GMLP_GROUPS * GMLP_GROUP_DIM
    return pl.pallas_call(
        _gmlp_kernel,
        grid=(T // CHUNK,),
        in_specs=[pl.BlockSpec((CHUNK, width), lambda i: (i, COL_GU // width)),
                  pl.BlockSpec((CHUNK, width), lambda i: (i, COL_GV // width)),
                  pl.BlockSpec((1, width), lambda i: (0, 0)),
                  pl.BlockSpec((1, width), lambda i: (0, 0)),
                  pl.BlockSpec((GMLP_GROUPS, CHUNK, CHUNK), lambda i: (0, 0, 0)),
                  pl.BlockSpec((CHUNK, GMLP_GROUPS), lambda i: (0, 0))],
        out_specs=pl.BlockSpec((CHUNK, width), lambda i: (i, 0)),
        out_shape=jax.ShapeDtypeStruct((T, width), BF16),
        compiler_params=_params("parallel"),
        name="gmlp",
    )(proj, proj, ln_g.reshape(1, width), ln_b.reshape(1, width), w_s, b_s.T)


def _dsa_prep_kernel(dq_ref, dc_ref, iq_ref, g_ref, wuk_ref, qabs_ref, iqp_ref, ckv_ref, ckvt_ref):
    c = _rms(dc_ref[...].astype(F32), g_ref[...])
    ckv_ref[...] = c.astype(ckv_ref.dtype)
    ckvt_ref[0, :DSA_LATENT, :] = c.T.astype(ckvt_ref.dtype)
    ckvt_ref[0, DSA_LATENT:, :] = jnp.ones((ROW_TILE, DSA_BLOCK), ckvt_ref.dtype)
    zeros = jnp.zeros((DSA_BLOCK, LANES - IDX_DIM), iqp_ref.dtype)
    for h in range(DSA_HEADS):
        q = dq_ref[:, h * DSA_HEAD_DIM:(h + 1) * DSA_HEAD_DIM]
        qabs_ref[0, h] = (_dot_nt(q, wuk_ref[h]) * (DSA_HEAD_DIM ** -0.5 * LOG2_E)).astype(qabs_ref.dtype)
    for h in range(IDX_HEADS):
        iqp_ref[0, h] = jnp.concatenate([iq_ref[:, h * IDX_DIM:(h + 1) * IDX_DIM], zeros], axis=-1)


def dsa_prep(proj, kv_norm_g, w_uk):
    T = proj.shape[0]
    nblk = T // DSA_BLOCK
    qw = DSA_HEADS * DSA_HEAD_DIM
    iq_w = IDX_HEADS * IDX_DIM
    return pl.pallas_call(
        _dsa_prep_kernel,
        grid=(nblk,),
        in_specs=[pl.BlockSpec((DSA_BLOCK, qw), lambda i: (i, COL_DQ // qw)),
                  pl.BlockSpec((DSA_BLOCK, DSA_LATENT), lambda i: (i, COL_DC // DSA_LATENT)),
                  pl.BlockSpec((DSA_BLOCK, iq_w), lambda i: (i, COL_IQ // iq_w)),
                  pl.BlockSpec((1, DSA_LATENT), lambda i: (0, 0)),
                  pl.BlockSpec((DSA_HEADS, DSA_LATENT, DSA_HEAD_DIM), lambda i: (0, 0, 0))],
        out_specs=[pl.BlockSpec((1, DSA_HEADS, DSA_BLOCK, DSA_LATENT), lambda i: (i, 0, 0, 0)),
                   pl.BlockSpec((1, IDX_HEADS, DSA_BLOCK, LANES), lambda i: (i, 0, 0, 0)),
                   pl.BlockSpec((DSA_BLOCK, DSA_LATENT), lambda i: (i, 0)),
                   pl.BlockSpec((1, DSA_LATENT + ROW_TILE, DSA_BLOCK), lambda i: (i, 0, 0))],
        out_shape=[jax.ShapeDtypeStruct((nblk, DSA_HEADS, DSA_BLOCK, DSA_LATENT), BF16),
                   jax.ShapeDtypeStruct((nblk, IDX_HEADS, DSA_BLOCK, LANES), BF16),
                   jax.ShapeDtypeStruct((T, DSA_LATENT), BF16),
                   jax.ShapeDtypeStruct((nblk, DSA_LATENT + ROW_TILE, DSA_BLOCK), BF16)],
        compiler_params=_params("parallel"),
        name="dsa_prep",
    )(proj, proj, proj, kv_norm_g.reshape(1, DSA_LATENT), w_uk)


def _dsa_kernel(iqp_ref, wq_ref, kidx_ref, ckv_ref, ckvt_ref, qa_ref, wuv_ref, o_ref,
                score_ref, m_ref, acc_ref, *, k_sel):
    QB = DSA_BLOCK
    blk = pl.program_id(1)
    n_tiles = blk + 1
    ksel_f = float(k_sel)
    key_in_tile = lax.broadcasted_iota(jnp.int32, (QB, QB), 0)
    query_in_blk = lax.broadcasted_iota(jnp.int32, (QB, QB), 1)

    def fold8(x, op):
        return op(x.reshape(QB // 8, 8, QB), axis=0)

    def tile_rows(j):
        return pl.ds(pl.multiple_of(j * QB, QB), QB)

    wi_t = wq_ref[...].astype(F32).T * (IDX_HEADS ** -0.5 * IDX_DIM ** -0.5)

    def score_body(j, stats):
        s_max, s_min, n_ge0, n_gt0 = stats
        kt = kidx_ref[tile_rows(j), :]
        sc = jnp.zeros((QB, QB), F32)
        for h in range(IDX_HEADS):
            lg = _dot_nt(kt, iqp_ref[0, h])
            sc = sc + wi_t[IDX_DIM + h:IDX_DIM + h + 1, :] * jnp.maximum(lg, 0.0)
        causal = (j < blk) | (key_in_tile <= query_in_blk)
        masked = jnp.where(causal, sc, -jnp.inf)
        score_ref[j] = masked
        return (jnp.maximum(s_max, fold8(masked, jnp.max)),
                jnp.minimum(s_min, fold8(jnp.where(causal, sc, jnp.inf), jnp.min)),
                n_ge0 + fold8((masked >= 0.0).astype(F32), jnp.sum),
                n_gt0 + fold8((masked > 0.0).astype(F32), jnp.sum))

    stats = lax.fori_loop(0, n_tiles, score_body,
                          (jnp.full((8, QB), -jnp.inf, F32), jnp.full((8, QB), jnp.inf, F32),
                           jnp.zeros((8, QB), F32), jnp.zeros((8, QB), F32)))
    hi0 = jnp.max(stats[0], axis=0, keepdims=True)
    lo0 = jnp.min(stats[1], axis=0, keepdims=True)
    cnt_ge0 = jnp.sum(stats[2], axis=0, keepdims=True)
    cnt_gt0 = jnp.sum(stats[3], axis=0, keepdims=True)

    COUNT_LANES = 4

    def count_negative(diff):
        def body(j, acc):
            sign = lax.shift_right_logical(lax.bitcast_convert_type(diff(score_ref[j]), jnp.uint32),
                                           jnp.uint32(31))
            return acc + jnp.sum(sign.astype(jnp.int32).reshape(-1, COUNT_LANES, 8, QB), axis=0)
        acc = lax.fori_loop(0, n_tiles, body, jnp.zeros((COUNT_LANES, 8, QB), jnp.int32))
        return jnp.sum(acc.reshape(COUNT_LANES * 8, QB), axis=0, keepdims=True).astype(F32)

    n_stored = (n_tiles * QB).astype(F32)

    def count_ge(t):
        return n_stored - count_negative(lambda s: s - t)

    n_causal = (blk * QB + 1 + lax.broadcasted_iota(jnp.int32, (1, QB), 1)).astype(F32)
    need = n_causal > ksel_f
    cnt_hi0 = count_ge(hi0)
    above0 = cnt_gt0 >= ksel_f
    below0 = cnt_ge0 < ksel_f
    lo0 = jnp.where(below0, lo0, 0.0)
    cnt_lo0 = jnp.where(below0, n_causal, cnt_ge0)
    hi0 = jnp.where(above0, hi0, 0.0)
    at_top = above0 & (cnt_hi0 >= ksel_f)
    lo0 = jnp.where(at_top, hi0, lo0)
    cnt_lo0 = jnp.where(at_top, cnt_hi0, cnt_lo0)
    cnt_hi0 = jnp.where(above0, cnt_hi0, cnt_ge0)

    def unresolved(lo, hi, cnt_lo, cnt_hi):
        return (cnt_lo != ksel_f) & (lo < hi) & (cnt_lo - cnt_hi > 2.0)

    active0 = need & (above0 | below0) & unresolved(lo0, hi0, cnt_lo0, cnt_hi0)

    def bisect_cond(state):
        return jnp.max(state[4]) > 0.0

    def bisect_body(state):
        lo, hi, cnt_lo, cnt_hi, active = state
        mid = lo + 0.5 * (hi - lo)
        c = count_ge(mid)
        ge = c >= ksel_f
        live = active > 0.0
        moved = (mid > lo) & (mid < hi)
        new_lo = jnp.where(live & ge, mid, lo)
        new_hi = jnp.where(live & ~ge, mid, hi)
        new_cnt_lo = jnp.where(live & ge, c, cnt_lo)
        new_cnt_hi = jnp.where(live & ~ge, c, cnt_hi)
        new_active = live & moved & unresolved(new_lo, new_hi, new_cnt_lo, new_cnt_hi)
        return new_lo, new_hi, new_cnt_lo, new_cnt_hi, new_active.astype(F32)

    lo, hi, cnt_lo, cnt_hi, _ = lax.while_loop(
        bisect_cond, bisect_body, (lo0, hi0, cnt_lo0, cnt_hi0, active0.astype(F32)))

    def max_below_body(j, acc):
        s = score_ref[j]
        return jnp.maximum(acc, jnp.max(jnp.where(s < hi, s, -jnp.inf).reshape(-1, COUNT_LANES, 8, QB), axis=0))
    max_below = lax.fori_loop(0, n_tiles, max_below_body, jnp.full((COUNT_LANES, 8, QB), -jnp.inf, F32))
    max_below = jnp.max(max_below.reshape(COUNT_LANES * 8, QB), axis=0, keepdims=True)
    is_pair = (cnt_lo != ksel_f) & (lo < hi) & (cnt_lo - cnt_hi == 2.0)
    thr = jnp.where(need, jnp.where(is_pair, max_below, lo), -jnp.inf)
    n_gt = count_negative(lambda s: thr - s)
    tie_quota = jnp.where(need, ksel_f - n_gt, 0.0)

    m_ref[...] = jnp.full_like(m_ref, NEG_BIG)
    acc_ref[...] = jnp.zeros_like(acc_ref)
    strict_lower = (query_in_blk < key_in_tile).astype(BF16)

    def attn_body(j, tie_seen):
        sc = score_ref[j]
        eq = sc == thr
        eq_f = eq.astype(F32)
        prefix = _dot(strict_lower, eq_f.astype(BF16)) + tie_seen
        sel = (sc > thr) | (eq & (prefix < tie_quota))
        bias = jnp.where(sel, 0.0, NEG_BIG)
        c_t = ckv_ref[tile_rows(j), :]
        ct_t = ckvt_ref[j]
        for h in range(DSA_HEADS):
            s = _dot_nt(c_t, qa_ref[0, h]) + bias
            m_old = m_ref[h]
            m_new = jnp.maximum(m_old, jnp.max(fold8(s, jnp.max), axis=0, keepdims=True))
            alpha = jnp.exp2(m_old - m_new)
            p = jnp.exp2(s - m_new)
            acc_ref[h] = alpha * acc_ref[h] + _dot(ct_t, p.astype(BF16))
            m_ref[h] = m_new
        return tie_seen + jnp.sum(fold8(eq_f, jnp.sum), axis=0, keepdims=True)

    lax.fori_loop(0, n_tiles, attn_body, jnp.zeros((1, QB), F32))

    for h in range(DSA_HEADS):
        acc = acc_ref[h]
        o_lat = (acc[:DSA_LATENT] / acc[DSA_LATENT:DSA_LATENT + 1]).T.astype(BF16)
        o_ref[:, h * DSA_HEAD_DIM:(h + 1) * DSA_HEAD_DIM] = _dot(o_lat, wuv_ref[h]).astype(o_ref.dtype)


def dsa_attention(proj, q_abs, iq_pad, ckv, ckv_t, w_uv, batch, seq):
    T = proj.shape[0]
    nb = seq // DSA_BLOCK
    k_sel = min(DSA_TOPK_MAX, seq // 4)
    out_w = DSA_HEADS * DSA_HEAD_DIM
    return pl.pallas_call(
        functools.partial(_dsa_kernel, k_sel=k_sel),
        grid=(batch, nb),
        in_specs=[pl.BlockSpec((1, IDX_HEADS, DSA_BLOCK, LANES), lambda b, i: (b * nb + i, 0, 0, 0)),
                  pl.BlockSpec((DSA_BLOCK, LANES), lambda b, i: (b * nb + i, COL_IKW // LANES)),
                  pl.BlockSpec((seq, LANES), lambda b, i: (b, COL_IKW // LANES)),
                  pl.BlockSpec((seq, DSA_LATENT), lambda b, i: (b, 0)),
                  pl.BlockSpec((nb, DSA_LATENT + ROW_TILE, DSA_BLOCK), lambda b, i: (b, 0, 0)),
                  pl.BlockSpec((1, DSA_HEADS, DSA_BLOCK, DSA_LATENT), lambda b, i: (b * nb + i, 0, 0, 0)),
                  pl.BlockSpec((DSA_HEADS, DSA_LATENT, DSA_HEAD_DIM), lambda b, i: (0, 0, 0))],
        out_specs=pl.BlockSpec((DSA_BLOCK, out_w), lambda b, i: (b * nb + i, 0)),
        out_shape=jax.ShapeDtypeStruct((T, out_w), BF16),
        scratch_shapes=[pltpu.VMEM((nb, DSA_BLOCK, DSA_BLOCK), F32),
                        pltpu.VMEM((DSA_HEADS, 1, DSA_BLOCK), F32),
                        pltpu.VMEM((DSA_HEADS, DSA_LATENT + ROW_TILE, DSA_BLOCK), F32)],
        compiler_params=_params("parallel", "arbitrary"),
        name="dsa_attention",
    )(iq_pad, proj, proj, ckv, ckv_t, q_abs, w_uv)


def _merge_kernel(ret_ref, gm_ref, ds_ref, wr_ref, wg_ref, wd_ref, ga_ref, gb_ref, gc_ref, o_ref):
    def branch(a_ref, w_ref, gate_ref):
        return jax.nn.sigmoid(gate_ref[...].astype(F32)) * _dot(a_ref[...], w_ref[...])

    merged = branch(ret_ref, wr_ref, ga_ref) + branch(gm_ref, wg_ref, gb_ref) + branch(ds_ref, wd_ref, gc_ref)
    o_ref[...] = merged.astype(o_ref.dtype)


def merge_branches(ret, gm, ds, w_ret_o, w_gmlp_o, w_dsa_o, proj, d_model, *, tm=1024, tn=512):
    T, K = ret.shape
    gate_blk = COL_GATES // tn
    per_gate = d_model // tn
    act = pl.BlockSpec((tm, K), lambda i, j: (i, 0))
    wgt = pl.BlockSpec((K, tn), lambda i, j: (0, j))

    def gate(which):
        return pl.BlockSpec((tm, tn), lambda i, j: (i, gate_blk + which * per_gate + j))

    return pl.pallas_call(
        _merge_kernel,
        grid=(T // tm, d_model // tn),
        in_specs=[act, act, act, wgt, wgt, wgt, gate(0), gate(1), gate(2)],
        out_specs=pl.BlockSpec((tm, tn), lambda i, j: (i, j)),
        out_shape=jax.ShapeDtypeStruct((T, d_model), BF16),
        compiler_params=_params("parallel", "arbitrary"),
        name="merge_branches",
    )(ret, gm, ds, w_ret_o, w_gmlp_o, w_dsa_o, proj, proj, proj)


def _out_proj_kernel(a_ref, w_ref, x_ref, o_ref):
    o_ref[...] = x_ref[...] + _dot(a_ref[...], w_ref[...])


def out_proj_residual(merged, w_out, x, *, tm=1024, tn=512):
    T, K = merged.shape
    N = w_out.shape[1]
    return pl.pallas_call(
        _out_proj_kernel,
        grid=(T // tm, N // tn),
        in_specs=[pl.BlockSpec((tm, K), lambda i, j: (i, 0)),
                  pl.BlockSpec((K, tn), lambda i, j: (0, j)),
                  pl.BlockSpec((tm, tn), lambda i, j: (i, j))],
        out_specs=pl.BlockSpec((tm, tn), lambda i, j: (i, j)),
        out_shape=jax.ShapeDtypeStruct((T, N), F32),
        compiler_params=_params("parallel", "arbitrary"),
        name="out_proj_residual",
    )(merged, w_out, x)


def _pack_rows(h_bf16, o_ref):
    n, d = h_bf16.shape
    bits = lax.bitcast_convert_type(h_bf16.astype(F32), jnp.uint32)
    packed = (bits[:, :d // 2] >> 16) | (bits[:, d // 2:] & jnp.uint32(0xFFFF0000))
    for c in range(d // 2 // LANES):
        o_ref[pl.ds(c, n, stride=ROW_TILE), :] = packed[:, c * LANES:(c + 1) * LANES]


def _unpack_rows(x_ref, o_ref):
    n, d = o_ref.shape
    for c in range(d // 2 // LANES):
        w = x_ref[pl.ds(c, n, stride=ROW_TILE), :]
        lo = lax.bitcast_convert_type(w << 16, F32)
        hi = lax.bitcast_convert_type(w & jnp.uint32(0xFFFF0000), F32)
        o_ref[:, c * LANES:(c + 1) * LANES] = lo.astype(o_ref.dtype)
        o_ref[:, d // 2 + c * LANES:d // 2 + (c + 1) * LANES] = hi.astype(o_ref.dtype)


def _router_kernel(x_ref, g_ref, w_ref, b_ref, hp_ref, ids_ref, wts_ref):
    h = _rms(x_ref[...], g_ref[...]).astype(BF16)
    _pack_rows(h, hp_ref)
    logits = _dot(h, w_ref[...]) + b_ref[...]
    lane = lax.broadcasted_iota(jnp.int32, logits.shape, 1)
    big = jnp.int32(LANES)

    def first_argmax(vals, valid):
        v = jnp.where(valid, vals, -jnp.inf)
        m = jnp.max(v, axis=-1, keepdims=True)
        idx = jnp.min(jnp.where(valid & (v == m), lane, big), axis=-1, keepdims=True)
        return m, idx

    is_grp = lane < MOE_GROUPS
    g_max, g_sel = first_argmax(logits, is_grp)
    g_den = jnp.sum(jnp.where(is_grp, jnp.exp(logits - g_max), 0.0), axis=-1, keepdims=True)
    p_g = 1.0 / g_den
    e_lo = MOE_GROUPS + g_sel * MOE_EXPERTS_PER_GROUP
    in_grp = (lane >= e_lo) & (lane < e_lo + MOE_EXPERTS_PER_GROUP)
    m1, i1 = first_argmax(logits, in_grp)
    m2, i2 = first_argmax(logits, in_grp & (lane != i1))
    e2 = jnp.exp(m2 - m1)
    w1 = p_g / (1.0 + e2)
    w2 = p_g * e2 / (1.0 + e2)
    ids_ref[...] = jnp.where(lane == 0, i1 - MOE_GROUPS, jnp.where(lane == 1, i2 - MOE_GROUPS, 0))
    wts_ref[...] = jnp.where(lane == 0, w1, jnp.where(lane == 1, w2, 0.0))


def moe_router(x, g, w_router, b_router, *, tm=512):
    T, D = x.shape
    assert D == 2 * ROW_TILE * LANES
    return pl.pallas_call(
        _router_kernel,
        grid=(T // tm,),
        in_specs=[pl.BlockSpec((tm, D), lambda i: (i, 0)),
                  pl.BlockSpec((1, D), lambda i: (0, 0)),
                  pl.BlockSpec((D, LANES), lambda i: (0, 0)),
                  pl.BlockSpec((1, LANES), lambda i: (0, 0))],
        out_specs=[pl.BlockSpec((tm * ROW_TILE, LANES), lambda i: (i, 0)),
                   pl.BlockSpec((tm, LANES), lambda i: (i, 0)),
                   pl.BlockSpec((tm, LANES), lambda i: (i, 0))],
        out_shape=[jax.ShapeDtypeStruct((T * ROW_TILE, LANES), jnp.uint32),
                   jax.ShapeDtypeStruct((T, LANES), jnp.int32),
                   jax.ShapeDtypeStruct((T, LANES), F32)],
        compiler_params=_params("parallel"),
        name="moe_router",
    )(x, g, w_router, b_router)


def _route_tables(ids, n_rows):
    T = ids.shape[0]
    e = ids[:, :MOE_TOPK].reshape(-1)
    onehot = (e[:, None] == jnp.arange(MOE_EXPERTS, dtype=jnp.int32)[None, :]).astype(jnp.int32)
    rank = jnp.sum((jnp.cumsum(onehot, axis=0) - onehot) * onehot, axis=1)
    counts = jnp.sum(onehot, axis=0)
    padded = (counts + MOE_TILE - 1) // MOE_TILE * MOE_TILE
    ends = jnp.cumsum(padded)
    pos = ((ends - padded)[e] + rank).astype(jnp.int32)
    tile_start = jnp.arange(n_rows // MOE_TILE, dtype=jnp.int32) * MOE_TILE
    tile_expert = jnp.sum((ends[None, :] <= tile_start[:, None]).astype(jnp.int32), axis=1)
    tile_expert = jnp.minimum(tile_expert, MOE_EXPERTS - 1)
    n_used = (ends[-1] // MOE_TILE).astype(jnp.int32).reshape(1)
    return pos, tile_expert, n_used


def _token_rows(r, rows_per_token):
    return pl.ds(pl.multiple_of(r * rows_per_token, rows_per_token), rows_per_token)


def _scatter_rows_kernel(pos_ref, hp_hbm, xs_in_hbm, xs_hbm, sem, *, n):
    del xs_in_hbm
    i = pl.program_id(0)

    def wait_step(buf):
        for slot in range(MOE_TOPK):
            pltpu.make_async_copy(hp_hbm.at[pl.ds(0, n * ROW_TILE)], xs_hbm.at[pl.ds(0, n * ROW_TILE)],
                                  sem.at[buf, slot]).wait()

    for slot in range(MOE_TOPK):
        def issue(r, carry, slot=slot):
            row = pos_ref[(i * n + r) * MOE_TOPK + slot]
            pltpu.make_async_copy(hp_hbm.at[_token_rows(i * n + r, ROW_TILE)],
                                  xs_hbm.at[_token_rows(row, ROW_TILE)], sem.at[i % 2, slot]).start()
            return carry
        lax.fori_loop(0, n, issue, 0, unroll=DMA_ISSUE_UNROLL)

    @pl.when(i > 0)
    def _():
        wait_step((i - 1) % 2)

    @pl.when(i == pl.num_programs(0) - 1)
    def _():
        wait_step(i % 2)


def scatter_rows(hp, pos, n_rows, *, tile=256):
    T = hp.shape[0] // ROW_TILE
    return pl.pallas_call(
        functools.partial(_scatter_rows_kernel, n=tile),
        grid_spec=pltpu.PrefetchScalarGridSpec(
            num_scalar_prefetch=1,
            grid=(T // tile,),
            in_specs=[pl.BlockSpec(memory_space=pl.ANY),
                      pl.BlockSpec(memory_space=pl.ANY)],
            out_specs=pl.BlockSpec(memory_space=pl.ANY),
            scratch_shapes=[pltpu.SemaphoreType.DMA((2, MOE_TOPK))]),
        out_shape=jax.ShapeDtypeStruct((n_rows * ROW_TILE, LANES), jnp.uint32),
        input_output_aliases={2: 0},
        compiler_params=_params("arbitrary"),
        name="moe_scatter_rows",
    )(pos, hp, jnp.zeros((n_rows * ROW_TILE, LANES), jnp.uint32))


def _moe_experts_kernel(tile_expert_ref, n_used_ref, x_ref, w1_ref, w3_ref, w2_ref, o_ref,
                        xb_ref, w1b_ref, w3b_ref, w2b_ref):
    i = pl.program_id(0)

    @pl.when((i == 0) | (tile_expert_ref[i] != tile_expert_ref[jnp.maximum(i - 1, 0)]))
    def _():
        w1b_ref[...] = w1_ref[0].astype(BF16)
        w3b_ref[...] = w3_ref[0].astype(BF16)
        w2b_ref[...] = w2_ref[0].astype(BF16)

    @pl.when(i < n_used_ref[0])
    def _():
        _unpack_rows(x_ref, xb_ref)
        x = xb_ref[...]
        a = _dot(x, w1b_ref[...])
        hid = a * jax.nn.sigmoid(a) * _dot(x, w3b_ref[...])
        o_ref[...] = _dot(hid.astype(BF16), w2b_ref[...])

    @pl.when(i >= n_used_ref[0])
    def _():
        o_ref[...] = jnp.zeros_like(o_ref)


def moe_experts(xs, tile_expert, n_used, w1, w3, w2):
    E, D, F = w1.shape
    P = xs.shape[0] // ROW_TILE

    def wspec(shape):
        return pl.BlockSpec((1,) + shape, lambda i, te, nu: (te[i], 0, 0))

    return pl.pallas_call(
        _moe_experts_kernel,
        grid_spec=pltpu.PrefetchScalarGridSpec(
            num_scalar_prefetch=2,
            grid=(P // MOE_TILE,),
            in_specs=[pl.BlockSpec((MOE_TILE * ROW_TILE, LANES), lambda i, te, nu: (i, 0)),
                      wspec((D, F)), wspec((D, F)), wspec((F, D))],
            out_specs=pl.BlockSpec((MOE_TILE, D), lambda i, te, nu: (i, 0)),
            scratch_shapes=[pltpu.VMEM((MOE_TILE, D), BF16), pltpu.VMEM((D, F), BF16),
                            pltpu.VMEM((D, F), BF16), pltpu.VMEM((F, D), BF16)]),
        out_shape=jax.ShapeDtypeStruct((P, D), F32),
        compiler_params=_params("arbitrary"),
        name="moe_experts",
    )(tile_expert, n_used, xs, w1, w3, w2)


def _moe_combine_kernel(pos_ref, ys_hbm, x_ref, wts_ref, g_ref, o_ref, y_ref, sem, *, final_norm):
    n = x_ref.shape[0]
    i = pl.program_id(0)

    def start_gather(step, buf):
        for slot in range(MOE_TOPK):
            def issue(r, carry, slot=slot):
                row = pos_ref[(step * n + r) * MOE_TOPK + slot]
                pltpu.make_async_copy(ys_hbm.at[pl.ds(row, 1)], y_ref.at[buf, slot, pl.ds(r, 1)],
                                      sem.at[buf, slot]).start()
                return carry
            lax.fori_loop(0, n, issue, 0, unroll=DMA_ISSUE_UNROLL)

    @pl.when(i == 0)
    def _():
        start_gather(0, 0)

    @pl.when(i + 1 < pl.num_programs(0))
    def _():
        start_gather(i + 1, (i + 1) % 2)

    buf = i % 2
    for slot in range(MOE_TOPK):
        pltpu.make_async_copy(ys_hbm.at[pl.ds(0, n)], y_ref.at[buf, slot], sem.at[buf, slot]).wait()
    out = x_ref[...] + (wts_ref[:, 0:1] * y_ref[buf, 0] + wts_ref[:, 1:2] * y_ref[buf, 1])
    o_ref[...] = _rms(out, g_ref[...]) if final_norm else out


def moe_combine(ys, pos, wts, x, g, *, final_norm, tile=256):
    T, D = x.shape
    return pl.pallas_call(
        functools.partial(_moe_combine_kernel, final_norm=final_norm),
        grid_spec=pltpu.PrefetchScalarGridSpec(
            num_scalar_prefetch=1,
            grid=(T // tile,),
            in_specs=[pl.BlockSpec(memory_space=pl.ANY),
                      pl.BlockSpec((tile, D), lambda i, pos: (i, 0)),
                      pl.BlockSpec((tile, LANES), lambda i, pos: (i, 0)),
                      pl.BlockSpec((1, D), lambda i, pos: (0, 0))],
            out_specs=pl.BlockSpec((tile, D), lambda i, pos: (i, 0)),
            scratch_shapes=[pltpu.VMEM((2, MOE_TOPK, tile, D), F32),
                            pltpu.SemaphoreType.DMA((2, MOE_TOPK))]),
        out_shape=jax.ShapeDtypeStruct((T, D), F32),
        compiler_params=_params("arbitrary"),
        name="moe_combine",
    )(pos, ys, x, wts, g)


def _pack_w_in(w):
    D = w.shape[0]
    widths = (1024, 1024, 1024, 1024, 1024, 1024, 1024, 256, 512, 64, 8, D, D, D)
    offs = [0]
    for wd in widths:
        offs.append(offs[-1] + wd)
    rq, rk, rv, rg, gu, gv, dq, dc, iq, ik, iw, ga, gb, gc = (
        w[:, offs[i]:offs[i + 1]] for i in range(len(widths)))
    ikw_pad = jnp.zeros((D, COL_GATES - COL_IKW - IDX_DIM - IDX_HEADS), w.dtype)
    packed = jnp.concatenate([rq, rk, rv, rg, gu, gv, dq, iq, dc, ik, iw, ikw_pad, ga, gb, gc], axis=1)
    assert packed.shape[1] == PROJ_WIDTH
    return packed.astype(BF16)


def _pack_router(w_group, b_group, w_expert, b_expert):
    D = w_group.shape[0]
    w_e = jnp.transpose(w_expert, (1, 0, 2)).reshape(D, MOE_EXPERTS)
    pad = LANES - MOE_GROUPS - MOE_EXPERTS
    w = jnp.concatenate([w_group, w_e, jnp.zeros((D, pad), w_group.dtype)], axis=1).astype(BF16)
    b = jnp.concatenate([b_group, b_expert.reshape(MOE_EXPERTS), jnp.zeros((pad,), b_group.dtype)])
    return w, b.reshape(1, LANES).astype(F32)


def kernel(x, norm_mix_g, w_in, w_ret_o, gmlp_ln_g, gmlp_ln_b, gmlp_w_s, gmlp_b_s, w_gmlp_o, dsa_kv_norm_g, dsa_w_uk, dsa_w_uv, w_dsa_o, w_out, norm_ffn_g, moe_w_group, moe_b_group, moe_w_expert, moe_b_expert, moe_w1, moe_w3, moe_w2, final_norm_g):
    B, S, D = x.shape
    depth = w_in.shape[0]
    assert S % DSA_BLOCK == 0 and D == 2048
    xt = x.reshape(B * S, D)
    for l in range(depth):
        proj = norm_matmul(xt, norm_mix_g[l].reshape(1, D), _pack_w_in(w_in[l]))
        ret = retention(proj, B, S)
        gm = gmlp(proj, gmlp_ln_g[l], gmlp_ln_b[l], gmlp_w_s[l], gmlp_b_s[l])
        q_abs, iq_pad, ckv, ckv_t = dsa_prep(proj, dsa_kv_norm_g[l], dsa_w_uk[l].astype(BF16))
        ds = dsa_attention(proj, q_abs, iq_pad, ckv, ckv_t, dsa_w_uv[l].astype(BF16), B, S)
        merged = merge_branches(ret, gm, ds, w_ret_o[l].astype(BF16), w_gmlp_o[l].astype(BF16),
                                w_dsa_o[l].astype(BF16), proj, D)
        xt = out_proj_residual(merged, w_out[l].astype(BF16), xt)
        w_router, b_router = _pack_router(moe_w_group[l], moe_b_group[l], moe_w_expert[l], moe_b_expert[l])
        hp, ids, wts = moe_router(xt, norm_ffn_g[l].reshape(1, D), w_router, b_router)
        n_rows = MOE_TOPK * B * S + MOE_EXPERTS * MOE_TILE
        pos, tile_expert, n_used = _route_tables(ids, n_rows)
        xs = scatter_rows(hp, pos, n_rows)
        F = moe_w1.shape[-1]
        ys = moe_experts(xs, tile_expert + l * MOE_EXPERTS, n_used,
                         moe_w1.reshape(depth * MOE_EXPERTS, D, F),
                         moe_w3.reshape(depth * MOE_EXPERTS, D, F),
                         moe_w2.reshape(depth * MOE_EXPERTS, F, D))
        last = l == depth - 1
        xt = moe_combine(ys, pos, wts, xt, final_norm_g.reshape(1, D), final_norm=last)
    return xt.reshape(B, S, D)
```

```python
import functools
import math

import jax
import jax.numpy as jnp
from jax import lax
from jax.experimental import pallas as pl
from jax.experimental.pallas import tpu as pltpu

F32 = jnp.float32
BF16 = jnp.bfloat16

RMS_EPS = 1e-6
RET_HEADS = 4
RET_DIM = 256
CHUNK = 128
ROPE_BASE = 10000.0
GMLP_GROUPS = 4
GMLP_GROUP_DIM = 256
DSA_HEADS = 8
DSA_HEAD_DIM = 128
DSA_LATENT = 256
IDX_HEADS = 8
IDX_DIM = 64
DSA_TOPK_MAX = 256
DSA_BLOCK = 256
MOE_GROUPS = 4
MOE_EXPERTS_PER_GROUP = 8
MOE_EXPERTS = MOE_GROUPS * MOE_EXPERTS_PER_GROUP
MOE_HIDDEN = 512
MOE_TOPK = 2
MOE_TILE = 256
LANES = 128
ROW_TILE = 8
DMA_ISSUE_UNROLL = 8
OUT_PROJ_CHUNK = 512
NEG_BIG = -1e30
LOG2_E = math.log2(math.e)

COL_RQ, COL_RK, COL_RV, COL_RG = 0, 1024, 2048, 3072
COL_GU, COL_GV = 4096, 5120
COL_DQ, COL_IQ, COL_DC, COL_IKW = 6144, 7168, 7680, 7936
COL_GATES = 8192
PROJ_WIDTH = 14336
VMEM_LIMIT = 56 * 1024 * 1024


def _params(*sem):
    return pltpu.CompilerParams(dimension_semantics=sem, vmem_limit_bytes=VMEM_LIMIT)


def _rms(x, g):
    return x * lax.rsqrt(jnp.mean(x * x, axis=-1, keepdims=True) + RMS_EPS) * g


def _dot(a, b):
    return jnp.dot(a, b, preferred_element_type=F32)


def _dot_nt(a, b):
    return lax.dot_general(a, b, (((1,), (1,)), ((), ())), preferred_element_type=F32)


def _norm_matmul_kernel(x_ref, g_ref, w_ref, o_ref, h_ref):
    @pl.when(pl.program_id(1) == 0)
    def _():
        h_ref[...] = _rms(x_ref[...], g_ref[...]).astype(BF16)

    o_ref[...] = _dot(h_ref[...], w_ref[...]).astype(o_ref.dtype)


def norm_matmul(x, g, w, *, tm=1024, tn=512, out_dtype=BF16):
    T, D = x.shape
    N = w.shape[1]
    return pl.pallas_call(
        _norm_matmul_kernel,
        grid=(T // tm, N // tn),
        in_specs=[pl.BlockSpec((tm, D), lambda i, j: (i, 0)),
                  pl.BlockSpec((1, D), lambda i, j: (0, 0)),
                  pl.BlockSpec((D, tn), lambda i, j: (0, j))],
        out_specs=pl.BlockSpec((tm, tn), lambda i, j: (i, j)),
        out_shape=jax.ShapeDtypeStruct((T, N), out_dtype),
        scratch_shapes=[pltpu.VMEM((tm, D), BF16)],
        compiler_params=_params("parallel", "arbitrary"),
        name="norm_in_proj",
    )(x, g, w)


def _retention_kernel(q_ref, k_ref, v_ref, gate_ref, cos_ref, sin_ref, decay_ref, xi_ref,
                      zeta_ref, gchunk_ref, o_ref, r_ref):
    @pl.when(pl.program_id(0) == 0)
    def _():
        r_ref[...] = jnp.zeros_like(r_ref)

    cos = cos_ref[...]
    sin = sin_ref[...]
    half = RET_DIM // 2

    def rotary(t):
        t1, t2 = t[:, :half], t[:, half:]
        return jnp.concatenate([t1 * cos - t2 * sin, t1 * sin + t2 * cos], axis=-1)

    for b in range(q_ref.shape[0]):
        for h in range(RET_HEADS):
            cols = slice(h * RET_DIM, (h + 1) * RET_DIM)
            q = rotary(q_ref[b, :, cols].astype(F32)).astype(BF16)
            k = rotary(k_ref[b, :, cols].astype(F32)) * (RET_DIM ** -0.5)
            v = v_ref[b, :, cols]
            inner = _dot_nt(q, k.astype(BF16)) * decay_ref[h]
            r_old = r_ref[b, h]
            o = _dot(inner.astype(BF16), v) + _dot(q, r_old.astype(BF16)) * xi_ref[:, h:h + 1]
            kz = (k * zeta_ref[:, h:h + 1]).astype(BF16)
            r_ref[b, h] = r_old * gchunk_ref[h] + _dot(kz.T, v)
            o = o * lax.rsqrt(jnp.mean(o * o, axis=-1, keepdims=True) + RMS_EPS)
            gate = gate_ref[b, :, cols].astype(F32)
            o_ref[b, :, cols] = (gate * jax.nn.sigmoid(gate) * o).astype(o_ref.dtype)


def retention(proj, batch, seq):
    T = proj.shape[0]
    n_chunks = seq // CHUNK
    width = RET_HEADS * RET_DIM
    half = RET_DIM // 2
    inv = ROPE_BASE ** (-jnp.arange(half, dtype=F32) / half)
    ang = jnp.arange(seq).astype(F32)[:, None] * inv[None, :]
    cos, sin = jnp.cos(ang), jnp.sin(ang)
    log_g = jnp.log(1.0 - 2.0 ** (-5.0 - jnp.arange(RET_HEADS, dtype=F32)))
    i = jnp.arange(CHUNK, dtype=F32)
    diff = i[:, None] - i[None, :]
    decay = jnp.where(diff >= 0, jnp.exp(log_g[:, None, None] * jnp.maximum(diff, 0.0)), 0.0)
    xi = jnp.exp(log_g[None, :] * (i[:, None] + 1.0))
    zeta = jnp.exp(log_g[None, :] * (CHUNK - 1.0 - i[:, None]))
    gchunk = jnp.exp(log_g * CHUNK)

    def col_spec(col):
        return pl.BlockSpec((batch, CHUNK, width), lambda n: (0, n, col // width))

    proj3 = proj.reshape(batch, seq, proj.shape[1])
    out = pl.pallas_call(
        _retention_kernel,
        grid=(n_chunks,),
        in_specs=[col_spec(COL_RQ), col_spec(COL_RK), col_spec(COL_RV), col_spec(COL_RG),
                  pl.BlockSpec((CHUNK, half), lambda n: (n, 0)),
                  pl.BlockSpec((CHUNK, half), lambda n: (n, 0)),
                  pl.BlockSpec((RET_HEADS, CHUNK, CHUNK), lambda n: (0, 0, 0)),
                  pl.BlockSpec((CHUNK, RET_HEADS), lambda n: (0, 0)),
                  pl.BlockSpec((CHUNK, RET_HEADS), lambda n: (0, 0)),
                  pl.BlockSpec(memory_space=pltpu.SMEM)],
        out_specs=pl.BlockSpec((batch, CHUNK, width), lambda n: (0, n, 0)),
        out_shape=jax.ShapeDtypeStruct((batch, seq, width), BF16),
        scratch_shapes=[pltpu.VMEM((batch, RET_HEADS, RET_DIM, RET_DIM), F32)],
        compiler_params=_params("arbitrary"),
        name="retention",
    )(proj3, proj3, proj3, proj3, cos, sin, decay, xi, zeta, gchunk)
    return out.reshape(T, width)


def _gelu(x):
    return 0.5 * x * (1.0 + lax.erf(x * (2.0 ** -0.5)))


def _gmlp_kernel(u_ref, v_ref, lng_ref, lnb_ref, ws_ref, bs_ref, o_ref):
    v = _gelu(v_ref[...].astype(F32))
    mu = jnp.mean(v, axis=-1, keepdims=True)
    var = jnp.mean(jnp.square(v - mu), axis=-1, keepdims=True)
    vn = ((v - mu) * lax.rsqrt(var + RMS_EPS) * lng_ref[...] + lnb_ref[...]).astype(BF16)
    row = lax.broadcasted_iota(jnp.int32, (CHUNK, CHUNK), 0)
    col = lax.broadcasted_iota(jnp.int32, (CHUNK, CHUNK), 1)
    for g in range(GMLP_GROUPS):
        cols = slice(g * GMLP_GROUP_DIM, (g + 1) * GMLP_GROUP_DIM)
        w = jnp.where(row >= col, ws_ref[g], 0.0).astype(BF16)
        mixed = _dot(w, vn[:, cols]) + bs_ref[:, g:g + 1]
        u = _gelu(u_ref[:, cols].astype(F32))
        o_ref[:, cols] = (u * mixed).astype(o_ref.dtype)


def gmlp(proj, ln_g, ln_b, w_s, b_s):
    T = proj.shape[0]
    width = GMLP_GROUPS * GMLP_GROUP_DIM
    return pl.pallas_call(
        _gmlp_kernel,
        grid=(T // CHUNK,),
        in_specs=[pl.BlockSpec((CHUNK, width), lambda i: (i, COL_GU // width)),
                  pl.BlockSpec((CHUNK, width), lambda i: (i, COL_GV // width)),
                  pl.BlockSpec((1, width), lambda i: (0, 0)),
                  pl.BlockSpec((1, width), lambda i: (0, 0)),
                  pl.BlockSpec((GMLP_GROUPS, CHUNK, CHUNK), lambda i: (0, 0, 0)),
                  pl.BlockSpec((CHUNK, GMLP_GROUPS), lambda i: (0, 0))],
        out_specs=pl.BlockSpec((CHUNK, width), lambda i: (i, 0)),
        out_shape=jax.ShapeDtypeStruct((T, width), BF16),
        compiler_params=_params("parallel"),
        name="gmlp",
    )(proj, proj, ln_g.reshape(1, width), ln_b.reshape(1, width), w_s, b_s.T)


def _dsa_prep_kernel(dq_ref, dc_ref, iq_ref, g_ref, wuk_ref, qabs_ref, iqp_ref, ckv_ref, ckvt_ref):
    c = _rms(dc_ref[...].astype(F32), g_ref[...])
    ckv_ref[...] = c.astype(ckv_ref.dtype)
    ckvt_ref[0, :DSA_LATENT, :] = c.T.astype(ckvt_ref.dtype)
    ckvt_ref[0, DSA_LATENT:, :] = jnp.ones((ROW_TILE, DSA_BLOCK), ckvt_ref.dtype)
    zeros = jnp.zeros((DSA_BLOCK, LANES - IDX_DIM), iqp_ref.dtype)
    for h in range(DSA_HEADS):
        q = dq_ref[:, h * DSA_HEAD_DIM:(h + 1) * DSA_HEAD_DIM]
        qabs_ref[0, h] = (_dot_nt(q, wuk_ref[h]) * (DSA_HEAD_DIM ** -0.5 * LOG2_E)).astype(qabs_ref.dtype)
    for h in range(IDX_HEADS):
        iqp_ref[0, h] = jnp.concatenate([iq_ref[:, h * IDX_DIM:(h + 1) * IDX_DIM], zeros], axis=-1)


def dsa_prep(proj, kv_norm_g, w_uk):
    T = proj.shape[0]
    nblk = T // DSA_BLOCK
    qw = DSA_HEADS * DSA_HEAD_DIM
    iq_w = IDX_HEADS * IDX_DIM
    return pl.pallas_call(
        _dsa_prep_kernel,
        grid=(nblk,),
        in_specs=[pl.BlockSpec((DSA_BLOCK, qw), lambda i: (i, COL_DQ // qw)),
                  pl.BlockSpec((DSA_BLOCK, DSA_LATENT), lambda i: (i, COL_DC // DSA_LATENT)),
                  pl.BlockSpec((DSA_BLOCK, iq_w), lambda i: (i, COL_IQ // iq_w)),
                  pl.BlockSpec((1, DSA_LATENT), lambda i: (0, 0)),
                  pl.BlockSpec((DSA_HEADS, DSA_LATENT, DSA_HEAD_DIM), lambda i: (0, 0, 0))],
        out_specs=[pl.BlockSpec((1, DSA_HEADS, DSA_BLOCK, DSA_LATENT), lambda i: (i, 0, 0, 0)),
                   pl.BlockSpec((1, IDX_HEADS, DSA_BLOCK, LANES), lambda i: (i, 0, 0, 0)),
                   pl.BlockSpec((DSA_BLOCK, DSA_LATENT), lambda i: (i, 0)),
                   pl.BlockSpec((1, DSA_LATENT + ROW_TILE, DSA_BLOCK), lambda i: (i, 0, 0))],
        out_shape=[jax.ShapeDtypeStruct((nblk, DSA_HEADS, DSA_BLOCK, DSA_LATENT), BF16),
                   jax.ShapeDtypeStruct((nblk, IDX_HEADS, DSA_BLOCK, LANES), BF16),
                   jax.ShapeDtypeStruct((T, DSA_LATENT), BF16),
                   jax.ShapeDtypeStruct((nblk, DSA_LATENT + ROW_TILE, DSA_BLOCK), BF16)],
        compiler_params=_params("parallel"),
        name="dsa_prep",
    )(proj, proj, proj, kv_norm_g.reshape(1, DSA_LATENT), w_uk)


def _dsa_kernel(iqp_ref, wq_ref, kidx_ref, ckv_ref, ckvt_ref, qa_ref, wuv_ref, o_ref,
                score_ref, m_ref, acc_ref, *, k_sel):
    QB = DSA_BLOCK
    blk = pl.program_id(1)
    n_tiles = blk + 1
    ksel_f = float(k_sel)
    key_in_tile = lax.broadcasted_iota(jnp.int32, (QB, QB), 0)
    query_in_blk = lax.broadcasted_iota(jnp.int32, (QB, QB), 1)

    def fold8(x, op):
        return op(x.reshape(QB // 8, 8, QB), axis=0)

    def tile_rows(j):
        return pl.ds(pl.multiple_of(j * QB, QB), QB)

    wi_t = wq_ref[...].astype(F32).T * (IDX_HEADS ** -0.5 * IDX_DIM ** -0.5)

    def score_body(j, stats):
        s_max, s_min, n_ge0, n_gt0 = stats
        kt = kidx_ref[tile_rows(j), :]
        sc = jnp.zeros((QB, QB), F32)
        for h in range(IDX_HEADS):
            lg = _dot_nt(kt, iqp_ref[0, h])
            sc = sc + wi_t[IDX_DIM + h:IDX_DIM + h + 1, :] * jnp.maximum(lg, 0.0)
        causal = (j < blk) | (key_in_tile <= query_in_blk)
        masked = jnp.where(causal, sc, -jnp.inf)
        score_ref[j] = masked
        return (jnp.maximum(s_max, fold8(masked, jnp.max)),
                jnp.minimum(s_min, fold8(jnp.where(causal, sc, jnp.inf), jnp.min)),
                n_ge0 + fold8((masked >= 0.0).astype(F32), jnp.sum),
                n_gt0 + fold8((masked > 0.0).astype(F32), jnp.sum))

    stats = lax.fori_loop(0, n_tiles, score_body,
                          (jnp.full((8, QB), -jnp.inf, F32), jnp.full((8, QB), jnp.inf, F32),
                           jnp.zeros((8, QB), F32), jnp.zeros((8, QB), F32)))
    hi0 = jnp.max(stats[0], axis=0, keepdims=True)
    lo0 = jnp.min(stats[1], axis=0, keepdims=True)
    cnt_ge0 = jnp.sum(stats[2], axis=0, keepdims=True)
    cnt_gt0 = jnp.sum(stats[3], axis=0, keepdims=True)

    COUNT_LANES = 4

    def count_negative(diff):
        def body(j, acc):
            sign = lax.shift_right_logical(lax.bitcast_convert_type(diff(score_ref[j]), jnp.uint32),
                                           jnp.uint32(31))
            return acc + jnp.sum(sign.astype(jnp.int32).reshape(-1, COUNT_LANES, 8, QB), axis=0)
        acc = lax.fori_loop(0, n_tiles, body, jnp.zeros((COUNT_LANES, 8, QB), jnp.int32))
        return jnp.sum(acc.reshape(COUNT_LANES * 8, QB), axis=0, keepdims=True).astype(F32)

    n_stored = (n_tiles * QB).astype(F32)

    def count_ge(t):
        return n_stored - count_negative(lambda s: s - t)

    n_causal = (blk * QB + 1 + lax.broadcasted_iota(jnp.int32, (1, QB), 1)).astype(F32)
    need = n_causal > ksel_f
    cnt_hi0 = count_ge(hi0)
    above0 = cnt_gt0 >= ksel_f
    below0 = cnt_ge0 < ksel_f
    lo0 = jnp.where(below0, lo0, 0.0)
    cnt_lo0 = jnp.where(below0, n_causal, cnt_ge0)
    hi0 = jnp.where(above0, hi0, 0.0)
    at_top = above0 & (cnt_hi0 >= ksel_f)
    lo0 = jnp.where(at_top, hi0, lo0)
    cnt_lo0 = jnp.where(at_top, cnt_hi0, cnt_lo0)
    cnt_hi0 = jnp.where(above0, cnt_hi0, cnt_ge0)

    def unresolved(lo, hi, cnt_lo, cnt_hi):
        return (cnt_lo != ksel_f) & (lo < hi) & (cnt_lo - cnt_hi > 2.0)

    active0 = need & (above0 | below0) & unresolved(lo0, hi0, cnt_lo0, cnt_hi0)

    def bisect_cond(state):
        return jnp.max(state[4]) > 0.0

    def bisect_body(state):
        lo, hi, cnt_lo, cnt_hi, active = state
        mid = lo + 0.5 * (hi - lo)
        c = count_ge(mid)
        ge = c >= ksel_f
        live = active > 0.0
        moved = (mid > lo) & (mid < hi)
        new_lo = jnp.where(live & ge, mid, lo)
        new_hi = jnp.where(live & ~ge, mid, hi)
        new_cnt_lo = jnp.where(live & ge, c, cnt_lo)
        new_cnt_hi = jnp.where(live & ~ge, c, cnt_hi)
        new_active = live & moved & unresolved(new_lo, new_hi, new_cnt_lo, new_cnt_hi)
        return new_lo, new_hi, new_cnt_lo, new_cnt_hi, new_active.astype(F32)

    lo, hi, cnt_lo, cnt_hi, _ = lax.while_loop(
        bisect_cond, bisect_body, (lo0, hi0, cnt_lo0, cnt_hi0, active0.astype(F32)))

    def max_below_body(j, acc):
        s = score_ref[j]
        return jnp.maximum(acc, jnp.max(jnp.where(s < hi, s, -jnp.inf).reshape(-1, COUNT_LANES, 8, QB), axis=0))
    max_below = lax.fori_loop(0, n_tiles, max_below_body, jnp.full((COUNT_LANES, 8, QB), -jnp.inf, F32))
    max_below = jnp.max(max_below.reshape(COUNT_LANES * 8, QB), axis=0, keepdims=True)
    is_pair = (cnt_lo != ksel_f) & (lo < hi) & (cnt_lo - cnt_hi == 2.0)
    thr = jnp.where(need, jnp.where(is_pair, max_below, lo), -jnp.inf)
    n_gt = count_negative(lambda s: thr - s)
    tie_quota = jnp.where(need, ksel_f - n_gt, 0.0)

    m_ref[...] = jnp.full_like(m_ref, NEG_BIG)
    acc_ref[...] = jnp.zeros_like(acc_ref)
    strict_lower = (query_in_blk < key_in_tile).astype(BF16)

    def attn_body(j, tie_seen):
        sc = score_ref[j]
        eq = sc == thr
        eq_f = eq.astype(F32)
        prefix = _dot(strict_lower, eq_f.astype(BF16)) + tie_seen
        sel = (sc > thr) | (eq & (prefix < tie_quota))
        bias = jnp.where(sel, 0.0, NEG_BIG)
        c_t = ckv_ref[tile_rows(j), :]
        ct_t = ckvt_ref[j]
        for h in range(DSA_HEADS):
            s = _dot_nt(c_t, qa_ref[0, h]) + bias
            m_old = m_ref[h]
            m_new = jnp.maximum(m_old, jnp.max(fold8(s, jnp.max), axis=0, keepdims=True))
            alpha = jnp.exp2(m_old - m_new)
            p = jnp.exp2(s - m_new)
            acc_ref[h] = alpha * acc_ref[h] + _dot(ct_t, p.astype(BF16))
            m_ref[h] = m_new
        return tie_seen + jnp.sum(fold8(eq_f, jnp.sum), axis=0, keepdims=True)

    lax.fori_loop(0, n_tiles, attn_body, jnp.zeros((1, QB), F32))

    for h in range(DSA_HEADS):
        acc = acc_ref[h]
        o_lat = (acc[:DSA_LATENT] / acc[DSA_LATENT:DSA_LATENT + 1]).T.astype(BF16)
        o_ref[:, h * DSA_HEAD_DIM:(h + 1) * DSA_HEAD_DIM] = _dot(o_lat, wuv_ref[h]).astype(o_ref.dtype)


def dsa_attention(proj, q_abs, iq_pad, ckv, ckv_t, w_uv, batch, seq):
    T = proj.shape[0]
    nb = seq // DSA_BLOCK
    k_sel = min(DSA_TOPK_MAX, seq // 4)
    out_w = DSA_HEADS * DSA_HEAD_DIM
    return pl.pallas_call(
        functools.partial(_dsa_kernel, k_sel=k_sel),
        grid=(batch, nb),
        in_specs=[pl.BlockSpec((1, IDX_HEADS, DSA_BLOCK, LANES), lambda b, i: (b * nb + i, 0, 0, 0)),
                  pl.BlockSpec((DSA_BLOCK, LANES), lambda b, i: (b * nb + i, COL_IKW // LANES)),
                  pl.BlockSpec((seq, LANES), lambda b, i: (b, COL_IKW // LANES)),
                  pl.BlockSpec((seq, DSA_LATENT), lambda b, i: (b, 0)),
                  pl.BlockSpec((nb, DSA_LATENT + ROW_TILE, DSA_BLOCK), lambda b, i: (b, 0, 0)),
                  pl.BlockSpec((1, DSA_HEADS, DSA_BLOCK, DSA_LATENT), lambda b, i: (b * nb + i, 0, 0, 0)),
                  pl.BlockSpec((DSA_HEADS, DSA_LATENT, DSA_HEAD_DIM), lambda b, i: (0, 0, 0))],
        out_specs=pl.BlockSpec((DSA_BLOCK, out_w), lambda b, i: (b * nb + i, 0)),
        out_shape=jax.ShapeDtypeStruct((T, out_w), BF16),
        scratch_shapes=[pltpu.VMEM((nb, DSA_BLOCK, DSA_BLOCK), F32),
                        pltpu.VMEM((DSA_HEADS, 1, DSA_BLOCK), F32),
                        pltpu.VMEM((DSA_HEADS, DSA_LATENT + ROW_TILE, DSA_BLOCK), F32)],
        compiler_params=_params("parallel", "arbitrary"),
        name="dsa_attention",
    )(iq_pad, proj, proj, ckv, ckv_t, q_abs, w_uv)


def _merge_kernel(ret_ref, gm_ref, ds_ref, wr_ref, wg_ref, wd_ref, ga_ref, gb_ref, gc_ref, o_ref):
    def branch(a_ref, w_ref, gate_ref):
        return jax.nn.sigmoid(gate_ref[...].astype(F32)) * _dot(a_ref[...], w_ref[...])

    merged = branch(ret_ref, wr_ref, ga_ref) + branch(gm_ref, wg_ref, gb_ref) + branch(ds_ref, wd_ref, gc_ref)
    o_ref[...] = merged.astype(o_ref.dtype)


def merge_branches(ret, gm, ds, w_ret_o, w_gmlp_o, w_dsa_o, proj, d_model, *, tm=1024, tn=512):
    T, K = ret.shape
    gate_blk = COL_GATES // tn
    per_gate = d_model // tn
    act = pl.BlockSpec((tm, K), lambda i, j: (i, 0))
    wgt = pl.BlockSpec((K, tn), lambda i, j: (0, j))

    def gate(which):
        return pl.BlockSpec((tm, tn), lambda i, j: (i, gate_blk + which * per_gate + j))

    return pl.pallas_call(
        _merge_kernel,
        grid=(T // tm, d_model // tn),
        in_specs=[act, act, act, wgt, wgt, wgt, gate(0), gate(1), gate(2)],
        out_specs=pl.BlockSpec((tm, tn), lambda i, j: (i, j)),
        out_shape=jax.ShapeDtypeStruct((T, d_model), BF16),
        compiler_params=_params("parallel", "arbitrary"),
        name="merge_branches",
    )(ret, gm, ds, w_ret_o, w_gmlp_o, w_dsa_o, proj, proj, proj)


def _pack_rows(h_bf16, o_ref):
    n, d = h_bf16.shape
    bits = lax.bitcast_convert_type(h_bf16.astype(F32), jnp.uint32)
    packed = (bits[:, :d // 2] >> 16) | (bits[:, d // 2:] & jnp.uint32(0xFFFF0000))
    for c in range(d // 2 // LANES):
        o_ref[pl.ds(c, n, stride=ROW_TILE), :] = packed[:, c * LANES:(c + 1) * LANES]


def _unpack_rows(x_ref, o_ref):
    n, d = o_ref.shape
    for c in range(d // 2 // LANES):
        w = x_ref[pl.ds(c, n, stride=ROW_TILE), :]
        lo = lax.bitcast_convert_type(w << 16, F32)
        hi = lax.bitcast_convert_type(w & jnp.uint32(0xFFFF0000), F32)
        o_ref[:, c * LANES:(c + 1) * LANES] = lo.astype(o_ref.dtype)
        o_ref[:, d // 2 + c * LANES:d // 2 + (c + 1) * LANES] = hi.astype(o_ref.dtype)


def _out_proj_router_kernel(a_ref, wo_ref, x_ref, g_ref, w_ref, b_ref, x1_ref, hp_ref, ids_ref, wts_ref):
    d = x_ref.shape[1]
    for c in range(d // OUT_PROJ_CHUNK):
        cols = slice(c * OUT_PROJ_CHUNK, (c + 1) * OUT_PROJ_CHUNK)
        x1_ref[:, cols] = x_ref[:, cols] + _dot(a_ref[...], wo_ref[:, cols])
    h = _rms(x1_ref[...], g_ref[...]).astype(BF16)
    _pack_rows(h, hp_ref)
    logits = _dot(h, w_ref[...]) + b_ref[...]
    lane = lax.broadcasted_iota(jnp.int32, logits.shape, 1)
    big = jnp.int32(LANES)

    def first_argmax(vals, valid):
        v = jnp.where(valid, vals, -jnp.inf)
        m = jnp.max(v, axis=-1, keepdims=True)
        idx = jnp.min(jnp.where(valid & (v == m), lane, big), axis=-1, keepdims=True)
        return m, idx

    is_grp = lane < MOE_GROUPS
    g_max, g_sel = first_argmax(logits, is_grp)
    g_den = jnp.sum(jnp.where(is_grp, jnp.exp(logits - g_max), 0.0), axis=-1, keepdims=True)
    p_g = 1.0 / g_den
    e_lo = MOE_GROUPS + g_sel * MOE_EXPERTS_PER_GROUP
    in_grp = (lane >= e_lo) & (lane < e_lo + MOE_EXPERTS_PER_GROUP)
    m1, i1 = first_argmax(logits, in_grp)
    m2, i2 = first_argmax(logits, in_grp & (lane != i1))
    e2 = jnp.exp(m2 - m1)
    w1 = p_g / (1.0 + e2)
    w2 = p_g * e2 / (1.0 + e2)
    ids_ref[...] = jnp.where(lane == 0, i1 - MOE_GROUPS, jnp.where(lane == 1, i2 - MOE_GROUPS, 0))
    wts_ref[...] = jnp.where(lane == 0, w1, jnp.where(lane == 1, w2, 0.0))


def out_proj_router(merged, w_out, x, g, w_router, b_router, *, tm=512):
    T, D = x.shape
    K = merged.shape[1]
    assert D == 2 * ROW_TILE * LANES
    return pl.pallas_call(
        _out_proj_router_kernel,
        grid=(T // tm,),
        in_specs=[pl.BlockSpec((tm, K), lambda i: (i, 0)),
                  pl.BlockSpec((K, D), lambda i: (0, 0)),
                  pl.BlockSpec((tm, D), lambda i: (i, 0)),
                  pl.BlockSpec((1, D), lambda i: (0, 0)),
                  pl.BlockSpec((D, LANES), lambda i: (0, 0)),
                  pl.BlockSpec((1, LANES), lambda i: (0, 0))],
        out_specs=[pl.BlockSpec((tm, D), lambda i: (i, 0)),
                   pl.BlockSpec((tm * ROW_TILE, LANES), lambda i: (i, 0)),
                   pl.BlockSpec((tm, LANES), lambda i: (i, 0)),
                   pl.BlockSpec((tm, LANES), lambda i: (i, 0))],
        out_shape=[jax.ShapeDtypeStruct((T, D), F32),
                   jax.ShapeDtypeStruct((T * ROW_TILE, LANES), jnp.uint32),
                   jax.ShapeDtypeStruct((T, LANES), jnp.int32),
                   jax.ShapeDtypeStruct((T, LANES), F32)],
        compiler_params=_params("parallel"),
        name="out_proj_router",
    )(merged, w_out, x, g, w_router, b_router)


def _route_tables(ids, n_rows):
    T = ids.shape[0]
    e = ids[:, :MOE_TOPK].reshape(-1)
    onehot = (e[:, None] == jnp.arange(MOE_EXPERTS, dtype=jnp.int32)[None, :]).astype(jnp.int32)
    rank = jnp.sum((jnp.cumsum(onehot, axis=0) - onehot) * onehot, axis=1)
    counts = jnp.sum(onehot, axis=0)
    padded = (counts + MOE_TILE - 1) // MOE_TILE * MOE_TILE
    ends = jnp.cumsum(padded)
    pos = ((ends - padded)[e] + rank).astype(jnp.int32)
    tile_start = jnp.arange(n_rows // MOE_TILE, dtype=jnp.int32) * MOE_TILE
    tile_expert = jnp.sum((ends[None, :] <= tile_start[:, None]).astype(jnp.int32), axis=1)
    tile_expert = jnp.minimum(tile_expert, MOE_EXPERTS - 1)
    n_used = (ends[-1] // MOE_TILE).astype(jnp.int32).reshape(1)
    return pos, tile_expert, n_used


def _token_rows(r, rows_per_token):
    return pl.ds(pl.multiple_of(r * rows_per_token, rows_per_token), rows_per_token)


def _scatter_rows_kernel(pos_ref, hp_ref, xs_in_hbm, xs_hbm, sem):
    del xs_in_hbm
    n = hp_ref.shape[0] // ROW_TILE
    base = pl.program_id(0) * n
    for slot in range(MOE_TOPK):
        def issue(r, carry, slot=slot):
            row = pos_ref[(base + r) * MOE_TOPK + slot]
            pltpu.make_async_copy(hp_ref.at[_token_rows(r, ROW_TILE)],
                                  xs_hbm.at[_token_rows(row, ROW_TILE)], sem.at[slot]).start()
            return carry
        lax.fori_loop(0, n, issue, 0, unroll=DMA_ISSUE_UNROLL)
    for slot in range(MOE_TOPK):
        pltpu.make_async_copy(hp_ref, xs_hbm.at[pl.ds(0, n * ROW_TILE)], sem.at[slot]).wait()


def scatter_rows(hp, pos, n_rows, *, tile=256):
    T = hp.shape[0] // ROW_TILE
    return pl.pallas_call(
        _scatter_rows_kernel,
        grid_spec=pltpu.PrefetchScalarGridSpec(
            num_scalar_prefetch=1,
            grid=(T // tile,),
            in_specs=[pl.BlockSpec((tile * ROW_TILE, LANES), lambda i, pos: (i, 0)),
                      pl.BlockSpec(memory_space=pl.ANY)],
            out_specs=pl.BlockSpec(memory_space=pl.ANY),
            scratch_shapes=[pltpu.SemaphoreType.DMA((MOE_TOPK,))]),
        out_shape=jax.ShapeDtypeStruct((n_rows * ROW_TILE, LANES), jnp.uint32),
        input_output_aliases={2: 0},
        compiler_params=_params("arbitrary"),
        name="moe_scatter_rows",
    )(pos, hp, jnp.zeros((n_rows * ROW_TILE, LANES), jnp.uint32))


def _moe_experts_kernel(tile_expert_ref, n_used_ref, x_ref, w1_ref, w3_ref, w2_ref, o_ref,
                        xb_ref, w1b_ref, w3b_ref, w2b_ref):
    i = pl.program_id(0)

    @pl.when((i == 0) | (tile_expert_ref[i] != tile_expert_ref[jnp.maximum(i - 1, 0)]))
    def _():
        w1b_ref[...] = w1_ref[0].astype(BF16)
        w3b_ref[...] = w3_ref[0].astype(BF16)
        w2b_ref[...] = w2_ref[0].astype(BF16)

    @pl.when(i < n_used_ref[0])
    def _():
        _unpack_rows(x_ref, xb_ref)
        x = xb_ref[...]
        a = _dot(x, w1b_ref[...])
        hid = a * jax.nn.sigmoid(a) * _dot(x, w3b_ref[...])
        o_ref[...] = _dot(hid.astype(BF16), w2b_ref[...])

    @pl.when(i >= n_used_ref[0])
    def _():
        o_ref[...] = jnp.zeros_like(o_ref)


def moe_experts(xs, tile_expert, n_used, w1, w3, w2):
    E, D, F = w1.shape
    P = xs.shape[0] // ROW_TILE

    def wspec(shape):
        return pl.BlockSpec((1,) + shape, lambda i, te, nu: (te[i], 0, 0))

    return pl.pallas_call(
        _moe_experts_kernel,
        grid_spec=pltpu.PrefetchScalarGridSpec(
            num_scalar_prefetch=2,
            grid=(P // MOE_TILE,),
            in_specs=[pl.BlockSpec((MOE_TILE * ROW_TILE, LANES), lambda i, te, nu: (i, 0)),
                      wspec((D, F)), wspec((D, F)), wspec((F, D))],
            out_specs=pl.BlockSpec((MOE_TILE, D), lambda i, te, nu: (i, 0)),
            scratch_shapes=[pltpu.VMEM((MOE_TILE, D), BF16), pltpu.VMEM((D, F), BF16),
                            pltpu.VMEM((D, F), BF16), pltpu.VMEM((F, D), BF16)]),
        out_shape=jax.ShapeDtypeStruct((P, D), F32),
        compiler_params=_params("arbitrary"),
        name="moe_experts",
    )(tile_expert, n_used, xs, w1, w3, w2)


def _moe_combine_kernel(pos_ref, ys_hbm, x_ref, wts_ref, g_ref, o_ref, y_ref, sem, *, final_norm):
    n = x_ref.shape[0]
    i = pl.program_id(0)

    def start_gather(step, buf):
        for slot in range(MOE_TOPK):
            def issue(r, carry, slot=slot):
                row = pos_ref[(step * n + r) * MOE_TOPK + slot]
                pltpu.make_async_copy(ys_hbm.at[pl.ds(row, 1)], y_ref.at[buf, slot, pl.ds(r, 1)],
                                      sem.at[buf, slot]).start()
                return carry
            lax.fori_loop(0, n, issue, 0, unroll=DMA_ISSUE_UNROLL)

    @pl.when(i == 0)
    def _():
        start_gather(0, 0)

    @pl.when(i + 1 < pl.num_programs(0))
    def _():
        start_gather(i + 1, (i + 1) % 2)

    buf = i % 2
    for slot in range(MOE_TOPK):
        pltpu.make_async_copy(ys_hbm.at[pl.ds(0, n)], y_ref.at[buf, slot], sem.at[buf, slot]).wait()
    out = x_ref[...] + (wts_ref[:, 0:1] * y_ref[buf, 0] + wts_ref[:, 1:2] * y_ref[buf, 1])
    o_ref[...] = _rms(out, g_ref[...]) if final_norm else out


def moe_combine(ys, pos, wts, x, g, *, final_norm, tile=256):
    T, D = x.shape
    return pl.pallas_call(
        functools.partial(_moe_combine_kernel, final_norm=final_norm),
        grid_spec=pltpu.PrefetchScalarGridSpec(
            num_scalar_prefetch=1,
            grid=(T // tile,),
            in_specs=[pl.BlockSpec(memory_space=pl.ANY),
                      pl.BlockSpec((tile, D), lambda i, pos: (i, 0)),
                      pl.BlockSpec((tile, LANES), lambda i, pos: (i, 0)),
                      pl.BlockSpec((1, D), lambda i, pos: (0, 0))],
            out_specs=pl.BlockSpec((tile, D), lambda i, pos: (i, 0)),
            scratch_shapes=[pltpu.VMEM((2, MOE_TOPK, tile, D), F32),
                            pltpu.SemaphoreType.DMA((2, MOE_TOPK))]),
        out_shape=jax.ShapeDtypeStruct((T, D), F32),
        compiler_params=_params("arbitrary"),
        name="moe_combine",
    )(pos, ys, x, wts, g)


def _pack_w_in(w):
    D = w.shape[0]
    widths = (1024, 1024, 1024, 1024, 1024, 1024, 1024, 256, 512, 64, 8, D, D, D)
    offs = [0]
    for wd in widths:
        offs.append(offs[-1] + wd)
    rq, rk, rv, rg, gu, gv, dq, dc, iq, ik, iw, ga, gb, gc = (
        w[:, offs[i]:offs[i + 1]] for i in range(len(widths)))
    ikw_pad = jnp.zeros((D, COL_GATES - COL_IKW - IDX_DIM - IDX_HEADS), w.dtype)
    packed = jnp.concatenate([rq, rk, rv, rg, gu, gv, dq, iq, dc, ik, iw, ikw_pad, ga, gb, gc], axis=1)
    assert packed.shape[1] == PROJ_WIDTH
    return packed.astype(BF16)


def _pack_router(w_group, b_group, w_expert, b_expert):
    D = w_group.shape[0]
    w_e = jnp.transpose(w_expert, (1, 0, 2)).reshape(D, MOE_EXPERTS)
    pad = LANES - MOE_GROUPS - MOE_EXPERTS
    w = jnp.concatenate([w_group, w_e, jnp.zeros((D, pad), w_group.dtype)], axis=1).astype(BF16)
    b = jnp.concatenate([b_group, b_expert.reshape(MOE_EXPERTS), jnp.zeros((pad,), b_group.dtype)])
    return w, b.reshape(1, LANES).astype(F32)


def kernel(x, norm_mix_g, w_in, w_ret_o, gmlp_ln_g, gmlp_ln_b, gmlp_w_s, gmlp_b_s, w_gmlp_o, dsa_kv_norm_g, dsa_w_uk, dsa_w_uv, w_dsa_o, w_out, norm_ffn_g, moe_w_group, moe_b_group, moe_w_expert, moe_b_expert, moe_w1, moe_w3, moe_w2, final_norm_g):
    B, S, D = x.shape
    depth = w_in.shape[0]
    assert S % DSA_BLOCK == 0 and D == 2048
    xt = x.reshape(B * S, D)
    for l in range(depth):
        proj = norm_matmul(xt, norm_mix_g[l].reshape(1, D), _pack_w_in(w_in[l]))
        ret = retention(proj, B, S)
        gm = gmlp(proj, gmlp_ln_g[l], gmlp_ln_b[l], gmlp_w_s[l], gmlp_b_s[l])
        q_abs, iq_pad, ckv, ckv_t = dsa_prep(proj, dsa_kv_norm_g[l], dsa_w_uk[l].astype(BF16))
        ds = dsa_attention(proj, q_abs, iq_pad, ckv, ckv_t, dsa_w_uv[l].astype(BF16), B, S)
        merged = merge_branches(ret, gm, ds, w_ret_o[l].astype(BF16), w_gmlp_o[l].astype(BF16),
                                w_dsa_o[l].astype(BF16), proj, D)
        w_router, b_router = _pack_router(moe_w_group[l], moe_b_group[l], moe_w_expert[l], moe_b_expert[l])
        xt, hp, ids, wts = out_proj_router(merged, w_out[l].astype(BF16), xt, norm_ffn_g[l].reshape(1, D),
                                           w_router, b_router)
        n_rows = MOE_TOPK * B * S + MOE_EXPERTS * MOE_TILE
        pos, tile_expert, n_used = _route_tables(ids, n_rows)
        xs = scatter_rows(hp, pos, n_rows)
        F = moe_w1.shape[-1]
        ys = moe_experts(xs, tile_expert + l * MOE_EXPERTS, n_used,
                         moe_w1.reshape(depth * MOE_EXPERTS, D, F),
                         moe_w3.reshape(depth * MOE_EXPERTS, D, F),
                         moe_w2.reshape(depth * MOE_EXPERTS, F, D))
        last = l == depth - 1
        xt = moe_combine(ys, pos, wts, xt, final_norm_g.reshape(1, D), final_norm=last)
    return xt.reshape(B, S, D)
```

```python
import functools
import math

import jax
import jax.numpy as jnp
from jax import lax
from jax.experimental import pallas as pl
from jax.experimental.pallas import tpu as pltpu

F32 = jnp.float32
BF16 = jnp.bfloat16

RMS_EPS = 1e-6
RET_HEADS = 4
RET_DIM = 256
CHUNK = 128
ROPE_BASE = 10000.0
GMLP_GROUPS = 4
GMLP_GROUP_DIM = 256
DSA_HEADS = 8
DSA_HEAD_DIM = 128
DSA_LATENT = 256
IDX_HEADS = 8
IDX_DIM = 64
DSA_TOPK_MAX = 256
DSA_BLOCK = 256
MOE_GROUPS = 4
MOE_EXPERTS_PER_GROUP = 8
MOE_EXPERTS = MOE_GROUPS * MOE_EXPERTS_PER_GROUP
MOE_HIDDEN = 512
MOE_TOPK = 2
MOE_TILE = 256
LANES = 128
ROW_TILE = 8
DMA_ISSUE_UNROLL = 8
OUT_PROJ_CHUNK = 512
NEG_BIG = -1e30
LOG2_E = math.log2(math.e)

COL_RQ, COL_RK, COL_RV, COL_RG = 0, 1024, 2048, 3072
COL_GU, COL_GV = 4096, 5120
COL_DQ, COL_IQ, COL_DC, COL_IKW = 6144, 7168, 7680, 7936
COL_GATES = 8192
PROJ_WIDTH = 14336
VMEM_LIMIT = 56 * 1024 * 1024


def _params(*sem):
    return pltpu.CompilerParams(dimension_semantics=sem, vmem_limit_bytes=VMEM_LIMIT)


def _rms(x, g):
    return x * lax.rsqrt(jnp.mean(x * x, axis=-1, keepdims=True) + RMS_EPS) * g


def _dot(a, b):
    return jnp.dot(a, b, preferred_element_type=F32)


def _dot_nt(a, b):
    return lax.dot_general(a, b, (((1,), (1,)), ((), ())), preferred_element_type=F32)


def _norm_matmul_kernel(x_ref, g_ref, w_ref, o_ref, h_ref):
    @pl.when(pl.program_id(1) == 0)
    def _():
        h_ref[...] = _rms(x_ref[...], g_ref[...]).astype(BF16)

    o_ref[...] = _dot(h_ref[...], w_ref[...]).astype(o_ref.dtype)


def norm_matmul(x, g, w, *, tm=1024, tn=1024, out_dtype=BF16):
    T, D = x.shape
    N = w.shape[1]
    return pl.pallas_call(
        _norm_matmul_kernel,
        grid=(T // tm, N // tn),
        in_specs=[pl.BlockSpec((tm, D), lambda i, j: (i, 0)),
                  pl.BlockSpec((1, D), lambda i, j: (0, 0)),
                  pl.BlockSpec((D, tn), lambda i, j: (0, j))],
        out_specs=pl.BlockSpec((tm, tn), lambda i, j: (i, j)),
        out_shape=jax.ShapeDtypeStruct((T, N), out_dtype),
        scratch_shapes=[pltpu.VMEM((tm, D), BF16)],
        compiler_params=_params("parallel", "arbitrary"),
        name="norm_in_proj",
    )(x, g, w)


def _retention_kernel(q_ref, k_ref, v_ref, gate_ref, cos_ref, sin_ref, decay_ref, xi_ref,
                      zeta_ref, gchunk_ref, o_ref, r_ref):
    @pl.when(pl.program_id(0) == 0)
    def _():
        r_ref[...] = jnp.zeros_like(r_ref)

    cos = cos_ref[...]
    sin = sin_ref[...]
    half = RET_DIM // 2

    def rotary(t):
        t1, t2 = t[:, :half], t[:, half:]
        return jnp.concatenate([t1 * cos - t2 * sin, t1 * sin + t2 * cos], axis=-1)

    for b in range(q_ref.shape[0]):
        for h in range(RET_HEADS):
            cols = slice(h * RET_DIM, (h + 1) * RET_DIM)
            q = rotary(q_ref[b, :, cols].astype(F32)).astype(BF16)
            k = rotary(k_ref[b, :, cols].astype(F32)) * (RET_DIM ** -0.5)
            v = v_ref[b, :, cols]
            inner = _dot_nt(q, k.astype(BF16)) * decay_ref[h]
            r_old = r_ref[b, h]
            o = _dot(inner.astype(BF16), v) + _dot(q, r_old.astype(BF16)) * xi_ref[:, h:h + 1]
            kz = (k * zeta_ref[:, h:h + 1]).astype(BF16)
            r_ref[b, h] = r_old * gchunk_ref[h] + _dot(kz.T, v)
            o = o * lax.rsqrt(jnp.mean(o * o, axis=-1, keepdims=True) + RMS_EPS)
            gate = gate_ref[b, :, cols].astype(F32)
            o_ref[b, :, cols] = (gate * jax.nn.sigmoid(gate) * o).astype(o_ref.dtype)


def retention(proj, batch, seq):
    T = proj.shape[0]
    n_chunks = seq // CHUNK
    width = RET_HEADS * RET_DIM
    half = RET_DIM // 2
    inv = ROPE_BASE ** (-jnp.arange(half, dtype=F32) / half)
    ang = jnp.arange(seq).astype(F32)[:, None] * inv[None, :]
    cos, sin = jnp.cos(ang), jnp.sin(ang)
    log_g = jnp.log(1.0 - 2.0 ** (-5.0 - jnp.arange(RET_HEADS, dtype=F32)))
    i = jnp.arange(CHUNK, dtype=F32)
    diff = i[:, None] - i[None, :]
    decay = jnp.where(diff >= 0, jnp.exp(log_g[:, None, None] * jnp.maximum(diff, 0.0)), 0.0)
    xi = jnp.exp(log_g[None, :] * (i[:, None] + 1.0))
    zeta = jnp.exp(log_g[None, :] * (CHUNK - 1.0 - i[:, None]))
    gchunk = jnp.exp(log_g * CHUNK)

    def col_spec(col):
        return pl.BlockSpec((batch, CHUNK, width), lambda n: (0, n, col // width))

    proj3 = proj.reshape(batch, seq, proj.shape[1])
    out = pl.pallas_call(
        _retention_kernel,
        grid=(n_chunks,),
        in_specs=[col_spec(COL_RQ), col_spec(COL_RK), col_spec(COL_RV), col_spec(COL_RG),
                  pl.BlockSpec((CHUNK, half), lambda n: (n, 0)),
                  pl.BlockSpec((CHUNK, half), lambda n: (n, 0)),
                  pl.BlockSpec((RET_HEADS, CHUNK, CHUNK), lambda n: (0, 0, 0)),
                  pl.BlockSpec((CHUNK, RET_HEADS), lambda n: (0, 0)),
                  pl.BlockSpec((CHUNK, RET_HEADS), lambda n: (0, 0)),
                  pl.BlockSpec(memory_space=pltpu.SMEM)],
        out_specs=pl.BlockSpec((batch, CHUNK, width), lambda n: (0, n, 0)),
        out_shape=jax.ShapeDtypeStruct((batch, seq, width), BF16),
        scratch_shapes=[pltpu.VMEM((batch, RET_HEADS, RET_DIM, RET_DIM), F32)],
        compiler_params=_params("arbitrary"),
        name="retention",
    )(proj3, proj3, proj3, proj3, cos, sin, decay, xi, zeta, gchunk)
    return out.reshape(T, width)


def _gelu(x):
    return 0.5 * x * (1.0 + lax.erf(x * (2.0 ** -0.5)))


def _gmlp_kernel(u_ref, v_ref, lng_ref, lnb_ref, ws_ref, bs_ref, o_ref):
    v = _gelu(v_ref[...].astype(F32))
    mu = jnp.mean(v, axis=-1, keepdims=True)
    var = jnp.mean(jnp.square(v - mu), axis=-1, keepdims=True)
    vn = ((v - mu) * lax.rsqrt(var + RMS_EPS) * lng_ref[...] + lnb_ref[...]).astype(BF16)
    row = lax.broadcasted_iota(jnp.int32, (CHUNK, CHUNK), 0)
    col = lax.broadcasted_iota(jnp.int32, (CHUNK, CHUNK), 1)
    for g in range(GMLP_GROUPS):
        cols = slice(g * GMLP_GROUP_DIM, (g + 1) * GMLP_GROUP_DIM)
        w = jnp.where(row >= col, ws_ref[g], 0.0).astype(BF16)
        mixed = _dot(w, vn[:, cols]) + bs_ref[:, g:g + 1]
        u = _gelu(u_ref[:, cols].astype(F32))
        o_ref[:, cols] = (u * mixed).astype(o_ref.dtype)


def gmlp(proj, ln_g, ln_b, w_s, b_s):
    T = proj.shape[0]
    width = GMLP_GROUPS * GMLP_GROUP_DIM
    return pl.pallas_call(
        _gmlp_kernel,
        grid=(T // CHUNK,),
        in_specs=[pl.BlockSpec((CHUNK, width), lambda i: (i, COL_GU // width)),
                  pl.BlockSpec((CHUNK, width), lambda i: (i, COL_GV // width)),
                  pl.BlockSpec((1, width), lambda i: (0, 0)),
                  pl.BlockSpec((1, width), lambda i: (0, 0)),
                  pl.BlockSpec((GMLP_GROUPS, CHUNK, CHUNK), lambda i: (0, 0, 0)),
                  pl.BlockSpec((CHUNK, GMLP_GROUPS), lambda i: (0, 0))],
        out_specs=pl.BlockSpec((CHUNK, width), lambda i: (i, 0)),
        out_shape=jax.ShapeDtypeStruct((T, width), BF16),
        compiler_params=_params("parallel"),
        name="gmlp",
    )(proj, proj, ln_g.reshape(1, width), ln_b.reshape(1, width), w_s, b_s.T)


def _dsa_prep_kernel(dq_ref, dc_ref, iq_ref, g_ref, wuk_ref, qabs_ref, iqp_ref, ckv_ref, ckvt_ref):
    c = _rms(dc_ref[...].astype(F32), g_ref[...])
    ckv_ref[...] = c.astype(ckv_ref.dtype)
    ckvt_ref[0, :DSA_LATENT, :] = c.T.astype(ckvt_ref.dtype)
    ckvt_ref[0, DSA_LATENT:, :] = jnp.ones((ROW_TILE, DSA_BLOCK), ckvt_ref.dtype)
    zeros = jnp.zeros((DSA_BLOCK, LANES - IDX_DIM), iqp_ref.dtype)
    for h in range(DSA_HEADS):
        q = dq_ref[:, h * DSA_HEAD_DIM:(h + 1) * DSA_HEAD_DIM]
        qabs_ref[0, h] = (_dot_nt(q, wuk_ref[h]) * (DSA_HEAD_DIM ** -0.5 * LOG2_E)).astype(qabs_ref.dtype)
    for h in range(IDX_HEADS):
        iqp_ref[0, h] = jnp.concatenate([iq_ref[:, h * IDX_DIM:(h + 1) * IDX_DIM], zeros], axis=-1)


def dsa_prep(proj, kv_norm_g, w_uk):
    T = proj.shape[0]
    nblk = T // DSA_BLOCK
    qw = DSA_HEADS * DSA_HEAD_DIM
    iq_w = IDX_HEADS * IDX_DIM
    return pl.pallas_call(
        _dsa_prep_kernel,
        grid=(nblk,),
        in_specs=[pl.BlockSpec((DSA_BLOCK, qw), lambda i: (i, COL_DQ // qw)),
                  pl.BlockSpec((DSA_BLOCK, DSA_LATENT), lambda i: (i, COL_DC // DSA_LATENT)),
                  pl.BlockSpec((DSA_BLOCK, iq_w), lambda i: (i, COL_IQ // iq_w)),
                  pl.BlockSpec((1, DSA_LATENT), lambda i: (0, 0)),
                  pl.BlockSpec((DSA_HEADS, DSA_LATENT, DSA_HEAD_DIM), lambda i: (0, 0, 0))],
        out_specs=[pl.BlockSpec((1, DSA_HEADS, DSA_BLOCK, DSA_LATENT), lambda i: (i, 0, 0, 0)),
                   pl.BlockSpec((1, IDX_HEADS, DSA_BLOCK, LANES), lambda i: (i, 0, 0, 0)),
                   pl.BlockSpec((DSA_BLOCK, DSA_LATENT), lambda i: (i, 0)),
                   pl.BlockSpec((1, DSA_LATENT + ROW_TILE, DSA_BLOCK), lambda i: (i, 0, 0))],
        out_shape=[jax.ShapeDtypeStruct((nblk, DSA_HEADS, DSA_BLOCK, DSA_LATENT), BF16),
                   jax.ShapeDtypeStruct((nblk, IDX_HEADS, DSA_BLOCK, LANES), BF16),
                   jax.ShapeDtypeStruct((T, DSA_LATENT), BF16),
                   jax.ShapeDtypeStruct((nblk, DSA_LATENT + ROW_TILE, DSA_BLOCK), BF16)],
        compiler_params=_params("parallel"),
        name="dsa_prep",
    )(proj, proj, proj, kv_norm_g.reshape(1, DSA_LATENT), w_uk)


def _dsa_kernel(iqp_ref, wq_ref, kidx_ref, ckv_ref, ckvt_ref, qa_ref, wuv_ref, o_ref,
                score_ref, m_ref, acc_ref, *, k_sel):
    QB = DSA_BLOCK
    blk = pl.program_id(1)
    n_tiles = blk + 1
    ksel_f = float(k_sel)
    key_in_tile = lax.broadcasted_iota(jnp.int32, (QB, QB), 0)
    query_in_blk = lax.broadcasted_iota(jnp.int32, (QB, QB), 1)

    def fold8(x, op):
        return op(x.reshape(QB // 8, 8, QB), axis=0)

    def tile_rows(j):
        return pl.ds(pl.multiple_of(j * QB, QB), QB)

    wi_t = wq_ref[...].astype(F32).T * (IDX_HEADS ** -0.5 * IDX_DIM ** -0.5)

    def score_body(j, stats):
        s_max, s_min, n_ge0, n_gt0 = stats
        kt = kidx_ref[tile_rows(j), :]
        sc = jnp.zeros((QB, QB), F32)
        for h in range(IDX_HEADS):
            lg = _dot_nt(kt, iqp_ref[0, h])
            sc = sc + wi_t[IDX_DIM + h:IDX_DIM + h + 1, :] * jnp.maximum(lg, 0.0)
        causal = (j < blk) | (key_in_tile <= query_in_blk)
        masked = jnp.where(causal, sc, -jnp.inf)
        score_ref[j] = masked
        return (jnp.maximum(s_max, fold8(masked, jnp.max)),
                jnp.minimum(s_min, fold8(jnp.where(causal, sc, jnp.inf), jnp.min)),
                n_ge0 + fold8((masked >= 0.0).astype(F32), jnp.sum),
                n_gt0 + fold8((masked > 0.0).astype(F32), jnp.sum))

    stats = lax.fori_loop(0, n_tiles, score_body,
                          (jnp.full((8, QB), -jnp.inf, F32), jnp.full((8, QB), jnp.inf, F32),
                           jnp.zeros((8, QB), F32), jnp.zeros((8, QB), F32)))
    hi0 = jnp.max(stats[0], axis=0, keepdims=True)
    lo0 = jnp.min(stats[1], axis=0, keepdims=True)
    cnt_ge0 = jnp.sum(stats[2], axis=0, keepdims=True)
    cnt_gt0 = jnp.sum(stats[3], axis=0, keepdims=True)

    COUNT_LANES = 4

    def count_negative(diff):
        def body(j, acc):
            sign = lax.shift_right_logical(lax.bitcast_convert_type(diff(score_ref[j]), jnp.uint32),
                                           jnp.uint32(31))
            return acc + jnp.sum(sign.astype(jnp.int32).reshape(-1, COUNT_LANES, 8, QB), axis=0)
        acc = lax.fori_loop(0, n_tiles, body, jnp.zeros((COUNT_LANES, 8, QB), jnp.int32))
        return jnp.sum(acc.reshape(COUNT_LANES * 8, QB), axis=0, keepdims=True).astype(F32)

    n_stored = (n_tiles * QB).astype(F32)

    def count_ge(t):
        return n_stored - count_negative(lambda s: s - t)

    n_causal = (blk * QB + 1 + lax.broadcasted_iota(jnp.int32, (1, QB), 1)).astype(F32)
    need = n_causal > ksel_f
    cnt_hi0 = count_ge(hi0)
    above0 = cnt_gt0 >= ksel_f
    below0 = cnt_ge0 < ksel_f
    lo0 = jnp.where(below0, lo0, 0.0)
    cnt_lo0 = jnp.where(below0, n_causal, cnt_ge0)
    hi0 = jnp.where(above0, hi0, 0.0)
    at_top = above0 & (cnt_hi0 >= ksel_f)
    lo0 = jnp.where(at_top, hi0, lo0)
    cnt_lo0 = jnp.where(at_top, cnt_hi0, cnt_lo0)
    cnt_hi0 = jnp.where(above0, cnt_hi0, cnt_ge0)

    def unresolved(lo, hi, cnt_lo, cnt_hi):
        return (cnt_lo != ksel_f) & (lo < hi) & (cnt_lo - cnt_hi > 2.0)

    active0 = need & (above0 | below0) & unresolved(lo0, hi0, cnt_lo0, cnt_hi0)

    def bisect_cond(state):
        return jnp.max(state[4]) > 0.0

    def bisect_body(state):
        lo, hi, cnt_lo, cnt_hi, active = state
        mid = lo + 0.5 * (hi - lo)
        c = count_ge(mid)
        ge = c >= ksel_f
        live = active > 0.0
        moved = (mid > lo) & (mid < hi)
        new_lo = jnp.where(live & ge, mid, lo)
        new_hi = jnp.where(live & ~ge, mid, hi)
        new_cnt_lo = jnp.where(live & ge, c, cnt_lo)
        new_cnt_hi = jnp.where(live & ~ge, c, cnt_hi)
        new_active = live & moved & unresolved(new_lo, new_hi, new_cnt_lo, new_cnt_hi)
        return new_lo, new_hi, new_cnt_lo, new_cnt_hi, new_active.astype(F32)

    lo, hi, cnt_lo, cnt_hi, _ = lax.while_loop(
        bisect_cond, bisect_body, (lo0, hi0, cnt_lo0, cnt_hi0, active0.astype(F32)))

    def max_below_body(j, acc):
        s = score_ref[j]
        return jnp.maximum(acc, jnp.max(jnp.where(s < hi, s, -jnp.inf).reshape(-1, COUNT_LANES, 8, QB), axis=0))
    max_below = lax.fori_loop(0, n_tiles, max_below_body, jnp.full((COUNT_LANES, 8, QB), -jnp.inf, F32))
    max_below = jnp.max(max_below.reshape(COUNT_LANES * 8, QB), axis=0, keepdims=True)
    is_pair = (cnt_lo != ksel_f) & (lo < hi) & (cnt_lo - cnt_hi == 2.0)
    thr = jnp.where(need, jnp.where(is_pair, max_below, lo), -jnp.inf)
    n_gt = count_negative(lambda s: thr - s)
    tie_quota = jnp.where(need, ksel_f - n_gt, 0.0)

    m_ref[...] = jnp.full_like(m_ref, NEG_BIG)
    acc_ref[...] = jnp.zeros_like(acc_ref)
    strict_lower = (query_in_blk < key_in_tile).astype(BF16)

    def attn_body(j, tie_seen):
        sc = score_ref[j]
        eq = sc == thr
        eq_f = eq.astype(F32)
        prefix = _dot(strict_lower, eq_f.astype(BF16)) + tie_seen
        sel = (sc > thr) | (eq & (prefix < tie_quota))
        bias = jnp.where(sel, 0.0, NEG_BIG)
        c_t = ckv_ref[tile_rows(j), :]
        ct_t = ckvt_ref[j]
        for h in range(DSA_HEADS):
            s = _dot_nt(c_t, qa_ref[0, h]) + bias
            m_old = m_ref[h]
            m_new = jnp.maximum(m_old, jnp.max(fold8(s, jnp.max), axis=0, keepdims=True))
            alpha = jnp.exp2(m_old - m_new)
            p = jnp.exp2(s - m_new)
            acc_ref[h] = alpha * acc_ref[h] + _dot(ct_t, p.astype(BF16))
            m_ref[h] = m_new
        return tie_seen + jnp.sum(fold8(eq_f, jnp.sum), axis=0, keepdims=True)

    lax.fori_loop(0, n_tiles, attn_body, jnp.zeros((1, QB), F32))

    for h in range(DSA_HEADS):
        acc = acc_ref[h]
        o_lat = (acc[:DSA_LATENT] / acc[DSA_LATENT:DSA_LATENT + 1]).T.astype(BF16)
        o_ref[:, h * DSA_HEAD_DIM:(h + 1) * DSA_HEAD_DIM] = _dot(o_lat, wuv_ref[h]).astype(o_ref.dtype)


def dsa_attention(proj, q_abs, iq_pad, ckv, ckv_t, w_uv, batch, seq):
    T = proj.shape[0]
    nb = seq // DSA_BLOCK
    k_sel = min(DSA_TOPK_MAX, seq // 4)
    out_w = DSA_HEADS * DSA_HEAD_DIM
    return pl.pallas_call(
        functools.partial(_dsa_kernel, k_sel=k_sel),
        grid=(batch, nb),
        in_specs=[pl.BlockSpec((1, IDX_HEADS, DSA_BLOCK, LANES), lambda b, i: (b * nb + i, 0, 0, 0)),
                  pl.BlockSpec((DSA_BLOCK, LANES), lambda b, i: (b * nb + i, COL_IKW // LANES)),
                  pl.BlockSpec((seq, LANES), lambda b, i: (b, COL_IKW // LANES)),
                  pl.BlockSpec((seq, DSA_LATENT), lambda b, i: (b, 0)),
                  pl.BlockSpec((nb, DSA_LATENT + ROW_TILE, DSA_BLOCK), lambda b, i: (b, 0, 0)),
                  pl.BlockSpec((1, DSA_HEADS, DSA_BLOCK, DSA_LATENT), lambda b, i: (b * nb + i, 0, 0, 0)),
                  pl.BlockSpec((DSA_HEADS, DSA_LATENT, DSA_HEAD_DIM), lambda b, i: (0, 0, 0))],
        out_specs=pl.BlockSpec((DSA_BLOCK, out_w), lambda b, i: (b * nb + i, 0)),
        out_shape=jax.ShapeDtypeStruct((T, out_w), BF16),
        scratch_shapes=[pltpu.VMEM((nb, DSA_BLOCK, DSA_BLOCK), F32),
                        pltpu.VMEM((DSA_HEADS, 1, DSA_BLOCK), F32),
                        pltpu.VMEM((DSA_HEADS, DSA_LATENT + ROW_TILE, DSA_BLOCK), F32)],
        compiler_params=_params("parallel", "arbitrary"),
        name="dsa_attention",
    )(iq_pad, proj, proj, ckv, ckv_t, q_abs, w_uv)


def _merge_kernel(ret_ref, gm_ref, ds_ref, wr_ref, wg_ref, wd_ref, ga_ref, gb_ref, gc_ref, o_ref):
    def branch(a_ref, w_ref, gate_ref):
        return jax.nn.sigmoid(gate_ref[...].astype(F32)) * _dot(a_ref[...], w_ref[...])

    merged = branch(ret_ref, wr_ref, ga_ref) + branch(gm_ref, wg_ref, gb_ref) + branch(ds_ref, wd_ref, gc_ref)
    o_ref[...] = merged.astype(o_ref.dtype)


def merge_branches(ret, gm, ds, w_ret_o, w_gmlp_o, w_dsa_o, proj, d_model, *, tm=1024, tn=512):
    T, K = ret.shape
    gate_blk = COL_GATES // tn
    per_gate = d_model // tn
    act = pl.BlockSpec((tm, K), lambda i, j: (i, 0))
    wgt = pl.BlockSpec((K, tn), lambda i, j: (0, j))

    def gate(which):
        return pl.BlockSpec((tm, tn), lambda i, j: (i, gate_blk + which * per_gate + j))

    return pl.pallas_call(
        _merge_kernel,
        grid=(T // tm, d_model // tn),
        in_specs=[act, act, act, wgt, wgt, wgt, gate(0), gate(1), gate(2)],
        out_specs=pl.BlockSpec((tm, tn), lambda i, j: (i, j)),
        out_shape=jax.ShapeDtypeStruct((T, d_model), BF16),
        compiler_params=_params("parallel", "arbitrary"),
        name="merge_branches",
    )(ret, gm, ds, w_ret_o, w_gmlp_o, w_dsa_o, proj, proj, proj)


def _pack_rows(h_bf16, o_ref):
    n, d = h_bf16.shape
    bits = lax.bitcast_convert_type(h_bf16.astype(F32), jnp.uint32)
    packed = (bits[:, :d // 2] >> 16) | (bits[:, d // 2:] & jnp.uint32(0xFFFF0000))
    for c in range(d // 2 // LANES):
        o_ref[pl.ds(c, n, stride=ROW_TILE), :] = packed[:, c * LANES:(c + 1) * LANES]


def _unpack_rows(x_ref, o_ref):
    n, d = o_ref.shape
    for c in range(d // 2 // LANES):
        w = x_ref[pl.ds(c, n, stride=ROW_TILE), :]
        lo = lax.bitcast_convert_type(w << 16, F32)
        hi = lax.bitcast_convert_type(w & jnp.uint32(0xFFFF0000), F32)
        o_ref[:, c * LANES:(c + 1) * LANES] = lo.astype(o_ref.dtype)
        o_ref[:, d // 2 + c * LANES:d // 2 + (c + 1) * LANES] = hi.astype(o_ref.dtype)


def _out_proj_router_kernel(a_ref, wo_ref, x_ref, g_ref, w_ref, b_ref, x1_ref, hp_ref, ids_ref, wts_ref):
    d = x_ref.shape[1]
    for c in range(d // OUT_PROJ_CHUNK):
        cols = slice(c * OUT_PROJ_CHUNK, (c + 1) * OUT_PROJ_CHUNK)
        x1_ref[:, cols] = x_ref[:, cols] + _dot(a_ref[...], wo_ref[:, cols])
    h = _rms(x1_ref[...], g_ref[...]).astype(BF16)
    _pack_rows(h, hp_ref)
    logits = _dot(h, w_ref[...]) + b_ref[...]
    lane = lax.broadcasted_iota(jnp.int32, logits.shape, 1)
    big = jnp.int32(LANES)

    def first_argmax(vals, valid):
        v = jnp.where(valid, vals, -jnp.inf)
        m = jnp.max(v, axis=-1, keepdims=True)
        idx = jnp.min(jnp.where(valid & (v == m), lane, big), axis=-1, keepdims=True)
        return m, idx

    is_grp = lane < MOE_GROUPS
    g_max, g_sel = first_argmax(logits, is_grp)
    g_den = jnp.sum(jnp.where(is_grp, jnp.exp(logits - g_max), 0.0), axis=-1, keepdims=True)
    p_g = 1.0 / g_den
    e_lo = MOE_GROUPS + g_sel * MOE_EXPERTS_PER_GROUP
    in_grp = (lane >= e_lo) & (lane < e_lo + MOE_EXPERTS_PER_GROUP)
    m1, i1 = first_argmax(logits, in_grp)
    m2, i2 = first_argmax(logits, in_grp & (lane != i1))
    e2 = jnp.exp(m2 - m1)
    w1 = p_g / (1.0 + e2)
    w2 = p_g * e2 / (1.0 + e2)
    ids_ref[...] = jnp.where(lane == 0, i1 - MOE_GROUPS, jnp.where(lane == 1, i2 - MOE_GROUPS, 0))
    wts_ref[...] = jnp.where(lane == 0, w1, jnp.where(lane == 1, w2, 0.0))


def out_proj_router(merged, w_out, x, g, w_router, b_router, *, tm=512):
    T, D = x.shape
    K = merged.shape[1]
    assert D == 2 * ROW_TILE * LANES
    return pl.pallas_call(
        _out_proj_router_kernel,
        grid=(T // tm,),
        in_specs=[pl.BlockSpec((tm, K), lambda i: (i, 0)),
                  pl.BlockSpec((K, D), lambda i: (0, 0)),
                  pl.BlockSpec((tm, D), lambda i: (i, 0)),
                  pl.BlockSpec((1, D), lambda i: (0, 0)),
                  pl.BlockSpec((D, LANES), lambda i: (0, 0)),
                  pl.BlockSpec((1, LANES), lambda i: (0, 0))],
        out_specs=[pl.BlockSpec((tm, D), lambda i: (i, 0)),
                   pl.BlockSpec((tm * ROW_TILE, LANES), lambda i: (i, 0)),
                   pl.BlockSpec((tm, LANES), lambda i: (i, 0)),
                   pl.BlockSpec((tm, LANES), lambda i: (i, 0))],
        out_shape=[jax.ShapeDtypeStruct((T, D), F32),
                   jax.ShapeDtypeStruct((T * ROW_TILE, LANES), jnp.uint32),
                   jax.ShapeDtypeStruct((T, LANES), jnp.int32),
                   jax.ShapeDtypeStruct((T, LANES), F32)],
        compiler_params=_params("parallel"),
        name="out_proj_router",
    )(merged, w_out, x, g, w_router, b_router)


def _route_tables(ids, n_rows):
    T = ids.shape[0]
    e = ids[:, :MOE_TOPK].reshape(-1)
    onehot = (e[:, None] == jnp.arange(MOE_EXPERTS, dtype=jnp.int32)[None, :]).astype(jnp.int32)
    rank = jnp.sum((jnp.cumsum(onehot, axis=0) - onehot) * onehot, axis=1)
    counts = jnp.sum(onehot, axis=0)
    padded = (counts + MOE_TILE - 1) // MOE_TILE * MOE_TILE
    ends = jnp.cumsum(padded)
    pos = ((ends - padded)[e] + rank).astype(jnp.int32)
    tile_start = jnp.arange(n_rows // MOE_TILE, dtype=jnp.int32) * MOE_TILE
    tile_expert = jnp.sum((ends[None, :] <= tile_start[:, None]).astype(jnp.int32), axis=1)
    tile_expert = jnp.minimum(tile_expert, MOE_EXPERTS - 1)
    n_used = (ends[-1] // MOE_TILE).astype(jnp.int32).reshape(1)
    return pos, tile_expert, n_used


def _token_rows(r, rows_per_token):
    return pl.ds(pl.multiple_of(r * rows_per_token, rows_per_token), rows_per_token)


def _scatter_rows_kernel(pos_ref, hp_ref, xs_in_hbm, xs_hbm, sem):
    del xs_in_hbm
    n = hp_ref.shape[0] // ROW_TILE
    base = pl.program_id(0) * n
    for slot in range(MOE_TOPK):
        def issue(r, carry, slot=slot):
            row = pos_ref[(base + r) * MOE_TOPK + slot]
            pltpu.make_async_copy(hp_ref.at[_token_rows(r, ROW_TILE)],
                                  xs_hbm.at[_token_rows(row, ROW_TILE)], sem.at[slot]).start()
            return carry
        lax.fori_loop(0, n, issue, 0, unroll=DMA_ISSUE_UNROLL)
    for slot in range(MOE_TOPK):
        pltpu.make_async_copy(hp_ref, xs_hbm.at[pl.ds(0, n * ROW_TILE)], sem.at[slot]).wait()


def scatter_rows(hp, pos, n_rows, *, tile=256):
    T = hp.shape[0] // ROW_TILE
    return pl.pallas_call(
        _scatter_rows_kernel,
        grid_spec=pltpu.PrefetchScalarGridSpec(
            num_scalar_prefetch=1,
            grid=(T // tile,),
            in_specs=[pl.BlockSpec((tile * ROW_TILE, LANES), lambda i, pos: (i, 0)),
                      pl.BlockSpec(memory_space=pl.ANY)],
            out_specs=pl.BlockSpec(memory_space=pl.ANY),
            scratch_shapes=[pltpu.SemaphoreType.DMA((MOE_TOPK,))]),
        out_shape=jax.ShapeDtypeStruct((n_rows * ROW_TILE, LANES), jnp.uint32),
        input_output_aliases={2: 0},
        compiler_params=_params("arbitrary"),
        name="moe_scatter_rows",
    )(pos, hp, jnp.zeros((n_rows * ROW_TILE, LANES), jnp.uint32))


def _moe_experts_kernel(tile_expert_ref, n_used_ref, x_ref, w1_ref, w3_ref, w2_ref, o_ref,
                        xb_ref, w1b_ref, w3b_ref, w2b_ref):
    i = pl.program_id(0)

    @pl.when((i == 0) | (tile_expert_ref[i] != tile_expert_ref[jnp.maximum(i - 1, 0)]))
    def _():
        w1b_ref[...] = w1_ref[0].astype(BF16)
        w3b_ref[...] = w3_ref[0].astype(BF16)
        w2b_ref[...] = w2_ref[0].astype(BF16)

    @pl.when(i < n_used_ref[0])
    def _():
        _unpack_rows(x_ref, xb_ref)
        x = xb_ref[...]
        a = _dot(x, w1b_ref[...])
        hid = a * jax.nn.sigmoid(a) * _dot(x, w3b_ref[...])
        o_ref[...] = _dot(hid.astype(BF16), w2b_ref[...])

    @pl.when(i >= n_used_ref[0])
    def _():
        o_ref[...] = jnp.zeros_like(o_ref)


def moe_experts(xs, tile_expert, n_used, w1, w3, w2):
    E, D, F = w1.shape
    P = xs.shape[0] // ROW_TILE

    def wspec(shape):
        return pl.BlockSpec((1,) + shape, lambda i, te, nu: (te[i], 0, 0))

    return pl.pallas_call(
        _moe_experts_kernel,
        grid_spec=pltpu.PrefetchScalarGridSpec(
            num_scalar_prefetch=2,
            grid=(P // MOE_TILE,),
            in_specs=[pl.BlockSpec((MOE_TILE * ROW_TILE, LANES), lambda i, te, nu: (i, 0)),
                      wspec((D, F)), wspec((D, F)), wspec((F, D))],
            out_specs=pl.BlockSpec((MOE_TILE, D), lambda i, te, nu: (i, 0)),
            scratch_shapes=[pltpu.VMEM((MOE_TILE, D), BF16), pltpu.VMEM((D, F), BF16),
                            pltpu.VMEM((D, F), BF16), pltpu.VMEM((F, D), BF16)]),
        out_shape=jax.ShapeDtypeStruct((P, D), F32),
        compiler_params=_params("arbitrary"),
        name="moe_experts",
    )(tile_expert, n_used, xs, w1, w3, w2)


def _moe_combine_kernel(pos_ref, ys_hbm, x_ref, wts_ref, g_ref, o_ref, y_ref, sem, *, final_norm):
    n = x_ref.shape[0]
    i = pl.program_id(0)

    def start_gather(step, buf):
        for slot in range(MOE_TOPK):
            def issue(r, carry, slot=slot):
                row = pos_ref[(step * n + r) * MOE_TOPK + slot]
                pltpu.make_async_copy(ys_hbm.at[pl.ds(row, 1)], y_ref.at[buf, slot, pl.ds(r, 1)],
                                      sem.at[buf, slot]).start()
                return carry
            lax.fori_loop(0, n, issue, 0, unroll=DMA_ISSUE_UNROLL)

    @pl.when(i == 0)
    def _():
        start_gather(0, 0)

    @pl.when(i + 1 < pl.num_programs(0))
    def _():
        start_gather(i + 1, (i + 1) % 2)

    buf = i % 2
    for slot in range(MOE_TOPK):
        pltpu.make_async_copy(ys_hbm.at[pl.ds(0, n)], y_ref.at[buf, slot], sem.at[buf, slot]).wait()
    out = x_ref[...] + (wts_ref[:, 0:1] * y_ref[buf, 0] + wts_ref[:, 1:2] * y_ref[buf, 1])
    o_ref[...] = _rms(out, g_ref[...]) if final_norm else out


def moe_combine(ys, pos, wts, x, g, *, final_norm, tile=256):
    T, D = x.shape
    return pl.pallas_call(
        functools.partial(_moe_combine_kernel, final_norm=final_norm),
        grid_spec=pltpu.PrefetchScalarGridSpec(
            num_scalar_prefetch=1,
            grid=(T // tile,),
            in_specs=[pl.BlockSpec(memory_space=pl.ANY),
                      pl.BlockSpec((tile, D), lambda i, pos: (i, 0)),
                      pl.BlockSpec((tile, LANES), lambda i, pos: (i, 0)),
                      pl.BlockSpec((1, D), lambda i, pos: (0, 0))],
            out_specs=pl.BlockSpec((tile, D), lambda i, pos: (i, 0)),
            scratch_shapes=[pltpu.VMEM((2, MOE_TOPK, tile, D), F32),
                            pltpu.SemaphoreType.DMA((2, MOE_TOPK))]),
        out_shape=jax.ShapeDtypeStruct((T, D), F32),
        compiler_params=_params("arbitrary"),
        name="moe_combine",
    )(pos, ys, x, wts, g)


def _pack_w_in(w):
    D = w.shape[0]
    widths = (1024, 1024, 1024, 1024, 1024, 1024, 1024, 256, 512, 64, 8, D, D, D)
    offs = [0]
    for wd in widths:
        offs.append(offs[-1] + wd)
    rq, rk, rv, rg, gu, gv, dq, dc, iq, ik, iw, ga, gb, gc = (
        w[:, offs[i]:offs[i + 1]] for i in range(len(widths)))
    ikw_pad = jnp.zeros((D, COL_GATES - COL_IKW - IDX_DIM - IDX_HEADS), w.dtype)
    packed = jnp.concatenate([rq, rk, rv, rg, gu, gv, dq, iq, dc, ik, iw, ikw_pad, ga, gb, gc], axis=1)
    assert packed.shape[1] == PROJ_WIDTH
    return packed.astype(BF16)


def _pack_router(w_group, b_group, w_expert, b_expert):
    D = w_group.shape[0]
    w_e = jnp.transpose(w_expert, (1, 0, 2)).reshape(D, MOE_EXPERTS)
    pad = LANES - MOE_GROUPS - MOE_EXPERTS
    w = jnp.concatenate([w_group, w_e, jnp.zeros((D, pad), w_group.dtype)], axis=1).astype(BF16)
    b = jnp.concatenate([b_group, b_expert.reshape(MOE_EXPERTS), jnp.zeros((pad,), b_group.dtype)])
    return w, b.reshape(1, LANES).astype(F32)


def kernel(x, norm_mix_g, w_in, w_ret_o, gmlp_ln_g, gmlp_ln_b, gmlp_w_s, gmlp_b_s, w_gmlp_o, dsa_kv_norm_g, dsa_w_uk, dsa_w_uv, w_dsa_o, w_out, norm_ffn_g, moe_w_group, moe_b_group, moe_w_expert, moe_b_expert, moe_w1, moe_w3, moe_w2, final_norm_g):
    B, S, D = x.shape
    depth = w_in.shape[0]
    assert S % DSA_BLOCK == 0 and D == 2048
    xt = x.reshape(B * S, D)
    for l in range(depth):
        proj = norm_matmul(xt, norm_mix_g[l].reshape(1, D), _pack_w_in(w_in[l]))
        ret = retention(proj, B, S)
        gm = gmlp(proj, gmlp_ln_g[l], gmlp_ln_b[l], gmlp_w_s[l], gmlp_b_s[l])
        q_abs, iq_pad, ckv, ckv_t = dsa_prep(proj, dsa_kv_norm_g[l], dsa_w_uk[l].astype(BF16))
        ds = dsa_attention(proj, q_abs, iq_pad, ckv, ckv_t, dsa_w_uv[l].astype(BF16), B, S)
        merged = merge_branches(ret, gm, ds, w_ret_o[l].astype(BF16), w_gmlp_o[l].astype(BF16),
                                w_dsa_o[l].astype(BF16), proj, D)
        w_router, b_router = _pack_router(moe_w_group[l], moe_b_group[l], moe_w_expert[l], moe_b_expert[l])
        xt, hp, ids, wts = out_proj_router(merged, w_out[l].astype(BF16), xt, norm_ffn_g[l].reshape(1, D),
                                           w_router, b_router)
        n_rows = MOE_TOPK * B * S + MOE_EXPERTS * MOE_TILE
        pos, tile_expert, n_used = _route_tables(ids, n_rows)
        xs = scatter_rows(hp, pos, n_rows)
        F = moe_w1.shape[-1]
        ys = moe_experts(xs, tile_expert + l * MOE_EXPERTS, n_used,
                         moe_w1.reshape(depth * MOE_EXPERTS, D, F),
                         moe_w3.reshape(depth * MOE_EXPERTS, D, F),
                         moe_w2.reshape(depth * MOE_EXPERTS, F, D))
        last = l == depth - 1
        xt = moe_combine(ys, pos, wts, xt, final_norm_g.reshape(1, D), final_norm=last)
    return xt.reshape(B, S, D)
```

```python
import functools
import math

import jax
import jax.numpy as jnp
from jax import lax
from jax.experimental import pallas as pl
from jax.experimental.pallas import tpu as pltpu

F32 = jnp.float32
BF16 = jnp.bfloat16

RMS_EPS = 1e-6
RET_HEADS = 4
RET_DIM = 256
CHUNK = 128
ROPE_BASE = 10000.0
GMLP_GROUPS = 4
GMLP_GROUP_DIM = 256
DSA_HEADS = 8
DSA_HEAD_DIM = 128
DSA_LATENT = 256
IDX_HEADS = 8
IDX_DIM = 64
DSA_TOPK_MAX = 256
DSA_BLOCK = 256
MOE_GROUPS = 4
MOE_EXPERTS_PER_GROUP = 8
MOE_EXPERTS = MOE_GROUPS * MOE_EXPERTS_PER_GROUP
MOE_HIDDEN = 512
MOE_TOPK = 2
MOE_TILE = 256
LANES = 128
ROW_TILE = 8
DMA_ISSUE_UNROLL = 8
OUT_PROJ_CHUNK = 512
NEG_BIG = -1e30
LOG2_E = math.log2(math.e)

COL_RQ, COL_RK, COL_RV, COL_RG = 0, 1024, 2048, 3072
COL_GU, COL_GV = 4096, 5120
COL_DQ, COL_IQ, COL_DC, COL_IKW = 6144, 7168, 7680, 7936
COL_GATES = 8192
PROJ_WIDTH = 14336
VMEM_LIMIT = 56 * 1024 * 1024


def _params(*sem):
    return pltpu.CompilerParams(dimension_semantics=sem, vmem_limit_bytes=VMEM_LIMIT)


def _rms(x, g):
    return x * lax.rsqrt(jnp.mean(x * x, axis=-1, keepdims=True) + RMS_EPS) * g


def _dot(a, b):
    return jnp.dot(a, b, preferred_element_type=F32)


def _dot_nt(a, b):
    return lax.dot_general(a, b, (((1,), (1,)), ((), ())), preferred_element_type=F32)


def _norm_matmul_kernel(x_ref, g_ref, w_ref, o_ref, h_ref):
    @pl.when(pl.program_id(1) == 0)
    def _():
        h_ref[...] = _rms(x_ref[...], g_ref[...]).astype(BF16)

    o_ref[...] = _dot(h_ref[...], w_ref[...]).astype(o_ref.dtype)


def norm_matmul(x, g, w, *, tm=1024, tn=2048, out_dtype=BF16):
    T, D = x.shape
    N = w.shape[1]
    return pl.pallas_call(
        _norm_matmul_kernel,
        grid=(T // tm, N // tn),
        in_specs=[pl.BlockSpec((tm, D), lambda i, j: (i, 0)),
                  pl.BlockSpec((1, D), lambda i, j: (0, 0)),
                  pl.BlockSpec((D, tn), lambda i, j: (0, j))],
        out_specs=pl.BlockSpec((tm, tn), lambda i, j: (i, j)),
        out_shape=jax.ShapeDtypeStruct((T, N), out_dtype),
        scratch_shapes=[pltpu.VMEM((tm, D), BF16)],
        compiler_params=_params("parallel", "arbitrary"),
        name="norm_in_proj",
    )(x, g, w)


def _retention_kernel(q_ref, k_ref, v_ref, gate_ref, cos_ref, sin_ref, decay_ref, xi_ref,
                      zeta_ref, gchunk_ref, o_ref, r_ref):
    @pl.when(pl.program_id(0) == 0)
    def _():
        r_ref[...] = jnp.zeros_like(r_ref)

    cos = cos_ref[...]
    sin = sin_ref[...]
    half = RET_DIM // 2

    def rotary(t):
        t1, t2 = t[:, :half], t[:, half:]
        return jnp.concatenate([t1 * cos - t2 * sin, t1 * sin + t2 * cos], axis=-1)

    for b in range(q_ref.shape[0]):
        for h in range(RET_HEADS):
            cols = slice(h * RET_DIM, (h + 1) * RET_DIM)
            q = rotary(q_ref[b, :, cols].astype(F32)).astype(BF16)
            k = rotary(k_ref[b, :, cols].astype(F32)) * (RET_DIM ** -0.5)
            v = v_ref[b, :, cols]
            inner = _dot_nt(q, k.astype(BF16)) * decay_ref[h]
            r_old = r_ref[b, h]
            o = _dot(inner.astype(BF16), v) + _dot(q, r_old.astype(BF16)) * xi_ref[:, h:h + 1]
            kz = (k * zeta_ref[:, h:h + 1]).astype(BF16)
            r_ref[b, h] = r_old * gchunk_ref[h] + _dot(kz.T, v)
            o = o * lax.rsqrt(jnp.mean(o * o, axis=-1, keepdims=True) + RMS_EPS)
            gate = gate_ref[b, :, cols].astype(F32)
            o_ref[b, :, cols] = (gate * jax.nn.sigmoid(gate) * o).astype(o_ref.dtype)


def retention(proj, batch, seq):
    T = proj.shape[0]
    n_chunks = seq // CHUNK
    width = RET_HEADS * RET_DIM
    half = RET_DIM // 2
    inv = ROPE_BASE ** (-jnp.arange(half, dtype=F32) / half)
    ang = jnp.arange(seq).astype(F32)[:, None] * inv[None, :]
    cos, sin = jnp.cos(ang), jnp.sin(ang)
    log_g = jnp.log(1.0 - 2.0 ** (-5.0 - jnp.arange(RET_HEADS, dtype=F32)))
    i = jnp.arange(CHUNK, dtype=F32)
    diff = i[:, None] - i[None, :]
    decay = jnp.where(diff >= 0, jnp.exp(log_g[:, None, None] * jnp.maximum(diff, 0.0)), 0.0)
    xi = jnp.exp(log_g[None, :] * (i[:, None] + 1.0))
    zeta = jnp.exp(log_g[None, :] * (CHUNK - 1.0 - i[:, None]))
    gchunk = jnp.exp(log_g * CHUNK)

    def col_spec(col):
        return pl.BlockSpec((batch, CHUNK, width), lambda n: (0, n, col // width))

    proj3 = proj.reshape(batch, seq, proj.shape[1])
    out = pl.pallas_call(
        _retention_kernel,
        grid=(n_chunks,),
        in_specs=[col_spec(COL_RQ), col_spec(COL_RK), col_spec(COL_RV), col_spec(COL_RG),
                  pl.BlockSpec((CHUNK, half), lambda n: (n, 0)),
                  pl.BlockSpec((CHUNK, half), lambda n: (n, 0)),
                  pl.BlockSpec((RET_HEADS, CHUNK, CHUNK), lambda n: (0, 0, 0)),
                  pl.BlockSpec((CHUNK, RET_HEADS), lambda n: (0, 0)),
                  pl.BlockSpec((CHUNK, RET_HEADS), lambda n: (0, 0)),
                  pl.BlockSpec(memory_space=pltpu.SMEM)],
        out_specs=pl.BlockSpec((batch, CHUNK, width), lambda n: (0, n, 0)),
        out_shape=jax.ShapeDtypeStruct((batch, seq, width), BF16),
        scratch_shapes=[pltpu.VMEM((batch, RET_HEADS, RET_DIM, RET_DIM), F32)],
        compiler_params=_params("arbitrary"),
        name="retention",
    )(proj3, proj3, proj3, proj3, cos, sin, decay, xi, zeta, gchunk)
    return out.reshape(T, width)


def _gelu(x):
    return 0.5 * x * (1.0 + lax.erf(x * (2.0 ** -0.5)))


def _gmlp_kernel(u_ref, v_ref, lng_ref, lnb_ref, ws_ref, bs_ref, o_ref):
    v = _gelu(v_ref[...].astype(F32))
    mu = jnp.mean(v, axis=-1, keepdims=True)
    var = jnp.mean(jnp.square(v - mu), axis=-1, keepdims=True)
    vn = ((v - mu) * lax.rsqrt(var + RMS_EPS) * lng_ref[...] + lnb_ref[...]).astype(BF16)
    row = lax.broadcasted_iota(jnp.int32, (CHUNK, CHUNK), 0)
    col = lax.broadcasted_iota(jnp.int32, (CHUNK, CHUNK), 1)
    for g in range(GMLP_GROUPS):
        cols = slice(g * GMLP_GROUP_DIM, (g + 1) * GMLP_GROUP_DIM)
        w = jnp.where(row >= col, ws_ref[g], 0.0).astype(BF16)
        mixed = _dot(w, vn[:, cols]) + bs_ref[:, g:g + 1]
        u = _gelu(u_ref[:, cols].astype(F32))
        o_ref[:, cols] = (u * mixed).astype(o_ref.dtype)


def gmlp(proj, ln_g, ln_b, w_s, b_s):
    T = proj.shape[0]
    width = GMLP_GROUPS * GMLP_GROUP_DIM
    return pl.pallas_call(
        _gmlp_kernel,
        grid=(T // CHUNK,),
        in_specs=[pl.BlockSpec((CHUNK, width), lambda i: (i, COL_GU // width)),
                  pl.BlockSpec((CHUNK, width), lambda i: (i, COL_GV // width)),
                  pl.BlockSpec((1, width), lambda i: (0, 0)),
                  pl.BlockSpec((1, width), lambda i: (0, 0)),
                  pl.BlockSpec((GMLP_GROUPS, CHUNK, CHUNK), lambda i: (0, 0, 0)),
                  pl.BlockSpec((CHUNK, GMLP_GROUPS), lambda i: (0, 0))],
        out_specs=pl.BlockSpec((CHUNK, width), lambda i: (i, 0)),
        out_shape=jax.ShapeDtypeStruct((T, width), BF16),
        compiler_params=_params("parallel"),
        name="gmlp",
    )(proj, proj, ln_g.reshape(1, width), ln_b.reshape(1, width), w_s, b_s.T)


def _dsa_prep_kernel(dq_ref, dc_ref, iq_ref, g_ref, wuk_ref, qabs_ref, iqp_ref, ckv_ref, ckvt_ref):
    c = _rms(dc_ref[...].astype(F32), g_ref[...])
    ckv_ref[...] = c.astype(ckv_ref.dtype)
    ckvt_ref[0, :DSA_LATENT, :] = c.T.astype(ckvt_ref.dtype)
    ckvt_ref[0, DSA_LATENT:, :] = jnp.ones((ROW_TILE, DSA_BLOCK), ckvt_ref.dtype)
    zeros = jnp.zeros((DSA_BLOCK, LANES - IDX_DIM), iqp_ref.dtype)
    for h in range(DSA_HEADS):
        q = dq_ref[:, h * DSA_HEAD_DIM:(h + 1) * DSA_HEAD_DIM]
        qabs_ref[0, h] = (_dot_nt(q, wuk_ref[h]) * (DSA_HEAD_DIM ** -0.5 * LOG2_E)).astype(qabs_ref.dtype)
    for h in range(IDX_HEADS):
        iqp_ref[0, h] = jnp.concatenate([iq_ref[:, h * IDX_DIM:(h + 1) * IDX_DIM], zeros], axis=-1)


def dsa_prep(proj, kv_norm_g, w_uk):
    T = proj.shape[0]
    nblk = T // DSA_BLOCK
    qw = DSA_HEADS * DSA_HEAD_DIM
    iq_w = IDX_HEADS * IDX_DIM
    return pl.pallas_call(
        _dsa_prep_kernel,
        grid=(nblk,),
        in_specs=[pl.BlockSpec((DSA_BLOCK, qw), lambda i: (i, COL_DQ // qw)),
                  pl.BlockSpec((DSA_BLOCK, DSA_LATENT), lambda i: (i, COL_DC // DSA_LATENT)),
                  pl.BlockSpec((DSA_BLOCK, iq_w), lambda i: (i, COL_IQ // iq_w)),
                  pl.BlockSpec((1, DSA_LATENT), lambda i: (0, 0)),
                  pl.BlockSpec((DSA_HEADS, DSA_LATENT, DSA_HEAD_DIM), lambda i: (0, 0, 0))],
        out_specs=[pl.BlockSpec((1, DSA_HEADS, DSA_BLOCK, DSA_LATENT), lambda i: (i, 0, 0, 0)),
                   pl.BlockSpec((1, IDX_HEADS, DSA_BLOCK, LANES), lambda i: (i, 0, 0, 0)),
                   pl.BlockSpec((DSA_BLOCK, DSA_LATENT), lambda i: (i, 0)),
                   pl.BlockSpec((1, DSA_LATENT + ROW_TILE, DSA_BLOCK), lambda i: (i, 0, 0))],
        out_shape=[jax.ShapeDtypeStruct((nblk, DSA_HEADS, DSA_BLOCK, DSA_LATENT), BF16),
                   jax.ShapeDtypeStruct((nblk, IDX_HEADS, DSA_BLOCK, LANES), BF16),
                   jax.ShapeDtypeStruct((T, DSA_LATENT), BF16),
                   jax.ShapeDtypeStruct((nblk, DSA_LATENT + ROW_TILE, DSA_BLOCK), BF16)],
        compiler_params=_params("parallel"),
        name="dsa_prep",
    )(proj, proj, proj, kv_norm_g.reshape(1, DSA_LATENT), w_uk)


def _dsa_kernel(iqp_ref, wq_ref, kidx_ref, ckv_ref, ckvt_ref, qa_ref, wuv_ref, o_ref,
                score_ref, m_ref, acc_ref, *, k_sel):
    QB = DSA_BLOCK
    blk = pl.program_id(1)
    n_tiles = blk + 1
    ksel_f = float(k_sel)
    key_in_tile = lax.broadcasted_iota(jnp.int32, (QB, QB), 0)
    query_in_blk = lax.broadcasted_iota(jnp.int32, (QB, QB), 1)

    def fold8(x, op):
        return op(x.reshape(QB // 8, 8, QB), axis=0)

    def tile_rows(j):
        return pl.ds(pl.multiple_of(j * QB, QB), QB)

    wi_t = wq_ref[...].astype(F32).T * (IDX_HEADS ** -0.5 * IDX_DIM ** -0.5)

    def score_body(j, stats):
        s_max, s_min, n_ge0, n_gt0 = stats
        kt = kidx_ref[tile_rows(j), :]
        sc = jnp.zeros((QB, QB), F32)
        for h in range(IDX_HEADS):
            lg = _dot_nt(kt, iqp_ref[0, h])
            sc = sc + wi_t[IDX_DIM + h:IDX_DIM + h + 1, :] * jnp.maximum(lg, 0.0)
        causal = (j < blk) | (key_in_tile <= query_in_blk)
        masked = jnp.where(causal, sc, -jnp.inf)
        score_ref[j] = masked
        return (jnp.maximum(s_max, fold8(masked, jnp.max)),
                jnp.minimum(s_min, fold8(jnp.where(causal, sc, jnp.inf), jnp.min)),
                n_ge0 + fold8((masked >= 0.0).astype(F32), jnp.sum),
                n_gt0 + fold8((masked > 0.0).astype(F32), jnp.sum))

    stats = lax.fori_loop(0, n_tiles, score_body,
                          (jnp.full((8, QB), -jnp.inf, F32), jnp.full((8, QB), jnp.inf, F32),
                           jnp.zeros((8, QB), F32), jnp.zeros((8, QB), F32)))
    hi0 = jnp.max(stats[0], axis=0, keepdims=True)
    lo0 = jnp.min(stats[1], axis=0, keepdims=True)
    cnt_ge0 = jnp.sum(stats[2], axis=0, keepdims=True)
    cnt_gt0 = jnp.sum(stats[3], axis=0, keepdims=True)

    COUNT_LANES = 4

    def count_negative(diff):
        def body(j, acc):
            sign = lax.shift_right_logical(lax.bitcast_convert_type(diff(score_ref[j]), jnp.uint32),
                                           jnp.uint32(31))
            return acc + jnp.sum(sign.astype(jnp.int32).reshape(-1, COUNT_LANES, 8, QB), axis=0)
        acc = lax.fori_loop(0, n_tiles, body, jnp.zeros((COUNT_LANES, 8, QB), jnp.int32))
        return jnp.sum(acc.reshape(COUNT_LANES * 8, QB), axis=0, keepdims=True).astype(F32)

    n_stored = (n_tiles * QB).astype(F32)

    def count_ge(t):
        return n_stored - count_negative(lambda s: s - t)

    n_causal = (blk * QB + 1 + lax.broadcasted_iota(jnp.int32, (1, QB), 1)).astype(F32)
    need = n_causal > ksel_f
    cnt_hi0 = count_ge(hi0)
    above0 = cnt_gt0 >= ksel_f
    below0 = cnt_ge0 < ksel_f
    lo0 = jnp.where(below0, lo0, 0.0)
    cnt_lo0 = jnp.where(below0, n_causal, cnt_ge0)
    hi0 = jnp.where(above0, hi0, 0.0)
    at_top = above0 & (cnt_hi0 >= ksel_f)
    lo0 = jnp.where(at_top, hi0, lo0)
    cnt_lo0 = jnp.where(at_top, cnt_hi0, cnt_lo0)
    cnt_hi0 = jnp.where(above0, cnt_hi0, cnt_ge0)

    def unresolved(lo, hi, cnt_lo, cnt_hi):
        return (cnt_lo != ksel_f) & (lo < hi) & (cnt_lo - cnt_hi > 2.0)

    active0 = need & (above0 | below0) & unresolved(lo0, hi0, cnt_lo0, cnt_hi0)

    def bisect_cond(state):
        return jnp.max(state[4]) > 0.0

    def bisect_body(state):
        lo, hi, cnt_lo, cnt_hi, active = state
        mid = lo + 0.5 * (hi - lo)
        c = count_ge(mid)
        ge = c >= ksel_f
        live = active > 0.0
        moved = (mid > lo) & (mid < hi)
        new_lo = jnp.where(live & ge, mid, lo)
        new_hi = jnp.where(live & ~ge, mid, hi)
        new_cnt_lo = jnp.where(live & ge, c, cnt_lo)
        new_cnt_hi = jnp.where(live & ~ge, c, cnt_hi)
        new_active = live & moved & unresolved(new_lo, new_hi, new_cnt_lo, new_cnt_hi)
        return new_lo, new_hi, new_cnt_lo, new_cnt_hi, new_active.astype(F32)

    lo, hi, cnt_lo, cnt_hi, _ = lax.while_loop(
        bisect_cond, bisect_body, (lo0, hi0, cnt_lo0, cnt_hi0, active0.astype(F32)))

    def max_below_body(j, acc):
        s = score_ref[j]
        return jnp.maximum(acc, jnp.max(jnp.where(s < hi, s, -jnp.inf).reshape(-1, COUNT_LANES, 8, QB), axis=0))
    max_below = lax.fori_loop(0, n_tiles, max_below_body, jnp.full((COUNT_LANES, 8, QB), -jnp.inf, F32))
    max_below = jnp.max(max_below.reshape(COUNT_LANES * 8, QB), axis=0, keepdims=True)
    is_pair = (cnt_lo != ksel_f) & (lo < hi) & (cnt_lo - cnt_hi == 2.0)
    thr = jnp.where(need, jnp.where(is_pair, max_below, lo), -jnp.inf)
    n_gt = count_negative(lambda s: thr - s)
    tie_quota = jnp.where(need, ksel_f - n_gt, 0.0)

    m_ref[...] = jnp.full_like(m_ref, NEG_BIG)
    acc_ref[...] = jnp.zeros_like(acc_ref)
    strict_lower = (query_in_blk < key_in_tile).astype(BF16)

    def attn_body(j, tie_seen):
        sc = score_ref[j]
        eq = sc == thr
        eq_f = eq.astype(F32)
        prefix = _dot(strict_lower, eq_f.astype(BF16)) + tie_seen
        sel = (sc > thr) | (eq & (prefix < tie_quota))
        bias = jnp.where(sel, 0.0, NEG_BIG)
        c_t = ckv_ref[tile_rows(j), :]
        ct_t = ckvt_ref[j]
        for h in range(DSA_HEADS):
            s = _dot_nt(c_t, qa_ref[0, h]) + bias
            m_old = m_ref[h]
            m_new = jnp.maximum(m_old, jnp.max(fold8(s, jnp.max), axis=0, keepdims=True))
            alpha = jnp.exp2(m_old - m_new)
            p = jnp.exp2(s - m_new)
            acc_ref[h] = alpha * acc_ref[h] + _dot(ct_t, p.astype(BF16))
            m_ref[h] = m_new
        return tie_seen + jnp.sum(fold8(eq_f, jnp.sum), axis=0, keepdims=True)

    lax.fori_loop(0, n_tiles, attn_body, jnp.zeros((1, QB), F32))

    for h in range(DSA_HEADS):
        acc = acc_ref[h]
        o_lat = (acc[:DSA_LATENT] / acc[DSA_LATENT:DSA_LATENT + 1]).T.astype(BF16)
        o_ref[:, h * DSA_HEAD_DIM:(h + 1) * DSA_HEAD_DIM] = _dot(o_lat, wuv_ref[h]).astype(o_ref.dtype)


def dsa_attention(proj, q_abs, iq_pad, ckv, ckv_t, w_uv, batch, seq):
    T = proj.shape[0]
    nb = seq // DSA_BLOCK
    k_sel = min(DSA_TOPK_MAX, seq // 4)
    out_w = DSA_HEADS * DSA_HEAD_DIM
    return pl.pallas_call(
        functools.partial(_dsa_kernel, k_sel=k_sel),
        grid=(batch, nb),
        in_specs=[pl.BlockSpec((1, IDX_HEADS, DSA_BLOCK, LANES), lambda b, i: (b * nb + i, 0, 0, 0)),
                  pl.BlockSpec((DSA_BLOCK, LANES), lambda b, i: (b * nb + i, COL_IKW // LANES)),
                  pl.BlockSpec((seq, LANES), lambda b, i: (b, COL_IKW // LANES)),
                  pl.BlockSpec((seq, DSA_LATENT), lambda b, i: (b, 0)),
                  pl.BlockSpec((nb, DSA_LATENT + ROW_TILE, DSA_BLOCK), lambda b, i: (b, 0, 0)),
                  pl.BlockSpec((1, DSA_HEADS, DSA_BLOCK, DSA_LATENT), lambda b, i: (b * nb + i, 0, 0, 0)),
                  pl.BlockSpec((DSA_HEADS, DSA_LATENT, DSA_HEAD_DIM), lambda b, i: (0, 0, 0))],
        out_specs=pl.BlockSpec((DSA_BLOCK, out_w), lambda b, i: (b * nb + i, 0)),
        out_shape=jax.ShapeDtypeStruct((T, out_w), BF16),
        scratch_shapes=[pltpu.VMEM((nb, DSA_BLOCK, DSA_BLOCK), F32),
                        pltpu.VMEM((DSA_HEADS, 1, DSA_BLOCK), F32),
                        pltpu.VMEM((DSA_HEADS, DSA_LATENT + ROW_TILE, DSA_BLOCK), F32)],
        compiler_params=_params("parallel", "arbitrary"),
        name="dsa_attention",
    )(iq_pad, proj, proj, ckv, ckv_t, q_abs, w_uv)


def _merge_kernel(ret_ref, gm_ref, ds_ref, wr_ref, wg_ref, wd_ref, ga_ref, gb_ref, gc_ref, o_ref):
    def branch(a_ref, w_ref, gate_ref):
        return jax.nn.sigmoid(gate_ref[...].astype(F32)) * _dot(a_ref[...], w_ref[...])

    merged = branch(ret_ref, wr_ref, ga_ref) + branch(gm_ref, wg_ref, gb_ref) + branch(ds_ref, wd_ref, gc_ref)
    o_ref[...] = merged.astype(o_ref.dtype)


def merge_branches(ret, gm, ds, w_ret_o, w_gmlp_o, w_dsa_o, proj, d_model, *, tm=1024, tn=1024):
    T, K = ret.shape
    gate_blk = COL_GATES // tn
    per_gate = d_model // tn
    act = pl.BlockSpec((tm, K), lambda i, j: (i, 0))
    wgt = pl.BlockSpec((K, tn), lambda i, j: (0, j))

    def gate(which):
        return pl.BlockSpec((tm, tn), lambda i, j: (i, gate_blk + which * per_gate + j))

    return pl.pallas_call(
        _merge_kernel,
        grid=(T // tm, d_model // tn),
        in_specs=[act, act, act, wgt, wgt, wgt, gate(0), gate(1), gate(2)],
        out_specs=pl.BlockSpec((tm, tn), lambda i, j: (i, j)),
        out_shape=jax.ShapeDtypeStruct((T, d_model), BF16),
        compiler_params=_params("parallel", "arbitrary"),
        name="merge_branches",
    )(ret, gm, ds, w_ret_o, w_gmlp_o, w_dsa_o, proj, proj, proj)


def _pack_rows(h_bf16, o_ref):
    n, d = h_bf16.shape
    bits = lax.bitcast_convert_type(h_bf16.astype(F32), jnp.uint32)
    packed = (bits[:, :d // 2] >> 16) | (bits[:, d // 2:] & jnp.uint32(0xFFFF0000))
    for c in range(d // 2 // LANES):
        o_ref[pl.ds(c, n, stride=ROW_TILE), :] = packed[:, c * LANES:(c + 1) * LANES]


def _unpack_rows(x_ref, o_ref):
    n, d = o_ref.shape
    for c in range(d // 2 // LANES):
        w = x_ref[pl.ds(c, n, stride=ROW_TILE), :]
        lo = lax.bitcast_convert_type(w << 16, F32)
        hi = lax.bitcast_convert_type(w & jnp.uint32(0xFFFF0000), F32)
        o_ref[:, c * LANES:(c + 1) * LANES] = lo.astype(o_ref.dtype)
        o_ref[:, d // 2 + c * LANES:d // 2 + (c + 1) * LANES] = hi.astype(o_ref.dtype)


def _out_proj_router_kernel(a_ref, wo_ref, x_ref, g_ref, w_ref, b_ref, x1_ref, hp_ref, ids_ref, wts_ref):
    d = x_ref.shape[1]
    for c in range(d // OUT_PROJ_CHUNK):
        cols = slice(c * OUT_PROJ_CHUNK, (c + 1) * OUT_PROJ_CHUNK)
        x1_ref[:, cols] = x_ref[:, cols] + _dot(a_ref[...], wo_ref[:, cols])
    h = _rms(x1_ref[...], g_ref[...]).astype(BF16)
    _pack_rows(h, hp_ref)
    logits = _dot(h, w_ref[...]) + b_ref[...]
    lane = lax.broadcasted_iota(jnp.int32, logits.shape, 1)
    big = jnp.int32(LANES)

    def first_argmax(vals, valid):
        v = jnp.where(valid, vals, -jnp.inf)
        m = jnp.max(v, axis=-1, keepdims=True)
        idx = jnp.min(jnp.where(valid & (v == m), lane, big), axis=-1, keepdims=True)
        return m, idx

    is_grp = lane < MOE_GROUPS
    g_max, g_sel = first_argmax(logits, is_grp)
    g_den = jnp.sum(jnp.where(is_grp, jnp.exp(logits - g_max), 0.0), axis=-1, keepdims=True)
    p_g = 1.0 / g_den
    e_lo = MOE_GROUPS + g_sel * MOE_EXPERTS_PER_GROUP
    in_grp = (lane >= e_lo) & (lane < e_lo + MOE_EXPERTS_PER_GROUP)
    m1, i1 = first_argmax(logits, in_grp)
    m2, i2 = first_argmax(logits, in_grp & (lane != i1))
    e2 = jnp.exp(m2 - m1)
    w1 = p_g / (1.0 + e2)
    w2 = p_g * e2 / (1.0 + e2)
    ids_ref[...] = jnp.where(lane == 0, i1 - MOE_GROUPS, jnp.where(lane == 1, i2 - MOE_GROUPS, 0))
    wts_ref[...] = jnp.where(lane == 0, w1, jnp.where(lane == 1, w2, 0.0))


def out_proj_router(merged, w_out, x, g, w_router, b_router, *, tm=512):
    T, D = x.shape
    K = merged.shape[1]
    assert D == 2 * ROW_TILE * LANES
    return pl.pallas_call(
        _out_proj_router_kernel,
        grid=(T // tm,),
        in_specs=[pl.BlockSpec((tm, K), lambda i: (i, 0)),
                  pl.BlockSpec((K, D), lambda i: (0, 0)),
                  pl.BlockSpec((tm, D), lambda i: (i, 0)),
                  pl.BlockSpec((1, D), lambda i: (0, 0)),
                  pl.BlockSpec((D, LANES), lambda i: (0, 0)),
                  pl.BlockSpec((1, LANES), lambda i: (0, 0))],
        out_specs=[pl.BlockSpec((tm, D), lambda i: (i, 0)),
                   pl.BlockSpec((tm * ROW_TILE, LANES), lambda i: (i, 0)),
                   pl.BlockSpec((tm, LANES), lambda i: (i, 0)),
                   pl.BlockSpec((tm, LANES), lambda i: (i, 0))],
        out_shape=[jax.ShapeDtypeStruct((T, D), F32),
                   jax.ShapeDtypeStruct((T * ROW_TILE, LANES), jnp.uint32),
                   jax.ShapeDtypeStruct((T, LANES), jnp.int32),
                   jax.ShapeDtypeStruct((T, LANES), F32)],
        compiler_params=_params("parallel"),
        name="out_proj_router",
    )(merged, w_out, x, g, w_router, b_router)


def _route_tables(ids, n_rows):
    T = ids.shape[0]
    e = ids[:, :MOE_TOPK].reshape(-1)
    onehot = (e[:, None] == jnp.arange(MOE_EXPERTS, dtype=jnp.int32)[None, :]).astype(jnp.int32)
    rank = jnp.sum((jnp.cumsum(onehot, axis=0) - onehot) * onehot, axis=1)
    counts = jnp.sum(onehot, axis=0)
    padded = (counts + MOE_TILE - 1) // MOE_TILE * MOE_TILE
    ends = jnp.cumsum(padded)
    pos = ((ends - padded)[e] + rank).astype(jnp.int32)
    tile_start = jnp.arange(n_rows // MOE_TILE, dtype=jnp.int32) * MOE_TILE
    tile_expert = jnp.sum((ends[None, :] <= tile_start[:, None]).astype(jnp.int32), axis=1)
    tile_expert = jnp.minimum(tile_expert, MOE_EXPERTS - 1)
    n_used = (ends[-1] // MOE_TILE).astype(jnp.int32).reshape(1)
    return pos, tile_expert, n_used


def _token_rows(r, rows_per_token):
    return pl.ds(pl.multiple_of(r * rows_per_token, rows_per_token), rows_per_token)


def _scatter_rows_kernel(pos_ref, hp_ref, xs_in_hbm, xs_hbm, sem):
    del xs_in_hbm
    n = hp_ref.shape[0] // ROW_TILE
    base = pl.program_id(0) * n
    for slot in range(MOE_TOPK):
        def issue(r, carry, slot=slot):
            row = pos_ref[(base + r) * MOE_TOPK + slot]
            pltpu.make_async_copy(hp_ref.at[_token_rows(r, ROW_TILE)],
                                  xs_hbm.at[_token_rows(row, ROW_TILE)], sem.at[slot]).start()
            return carry
        lax.fori_loop(0, n, issue, 0, unroll=DMA_ISSUE_UNROLL)
    for slot in range(MOE_TOPK):
        pltpu.make_async_copy(hp_ref, xs_hbm.at[pl.ds(0, n * ROW_TILE)], sem.at[slot]).wait()


def scatter_rows(hp, pos, n_rows, *, tile=256):
    T = hp.shape[0] // ROW_TILE
    return pl.pallas_call(
        _scatter_rows_kernel,
        grid_spec=pltpu.PrefetchScalarGridSpec(
            num_scalar_prefetch=1,
            grid=(T // tile,),
            in_specs=[pl.BlockSpec((tile * ROW_TILE, LANES), lambda i, pos: (i, 0)),
                      pl.BlockSpec(memory_space=pl.ANY)],
            out_specs=pl.BlockSpec(memory_space=pl.ANY),
            scratch_shapes=[pltpu.SemaphoreType.DMA((MOE_TOPK,))]),
        out_shape=jax.ShapeDtypeStruct((n_rows * ROW_TILE, LANES), jnp.uint32),
        input_output_aliases={2: 0},
        compiler_params=_params("arbitrary"),
        name="moe_scatter_rows",
    )(pos, hp, jnp.zeros((n_rows * ROW_TILE, LANES), jnp.uint32))


def _moe_experts_kernel(tile_expert_ref, n_used_ref, x_ref, w1_ref, w3_ref, w2_ref, o_ref,
                        xb_ref, w1b_ref, w3b_ref, w2b_ref):
    i = pl.program_id(0)

    @pl.when((i == 0) | (tile_expert_ref[i] != tile_expert_ref[jnp.maximum(i - 1, 0)]))
    def _():
        w1b_ref[...] = w1_ref[0].astype(BF16)
        w3b_ref[...] = w3_ref[0].astype(BF16)
        w2b_ref[...] = w2_ref[0].astype(BF16)

    @pl.when(i < n_used_ref[0])
    def _():
        _unpack_rows(x_ref, xb_ref)
        x = xb_ref[...]
        a = _dot(x, w1b_ref[...])
        hid = a * jax.nn.sigmoid(a) * _dot(x, w3b_ref[...])
        o_ref[...] = _dot(hid.astype(BF16), w2b_ref[...])

    @pl.when(i >= n_used_ref[0])
    def _():
        o_ref[...] = jnp.zeros_like(o_ref)


def moe_experts(xs, tile_expert, n_used, w1, w3, w2):
    E, D, F = w1.shape
    P = xs.shape[0] // ROW_TILE

    def wspec(shape):
        return pl.BlockSpec((1,) + shape, lambda i, te, nu: (te[i], 0, 0))

    return pl.pallas_call(
        _moe_experts_kernel,
        grid_spec=pltpu.PrefetchScalarGridSpec(
            num_scalar_prefetch=2,
            grid=(P // MOE_TILE,),
            in_specs=[pl.BlockSpec((MOE_TILE * ROW_TILE, LANES), lambda i, te, nu: (i, 0)),
                      wspec((D, F)), wspec((D, F)), wspec((F, D))],
            out_specs=pl.BlockSpec((MOE_TILE, D), lambda i, te, nu: (i, 0)),
            scratch_shapes=[pltpu.VMEM((MOE_TILE, D), BF16), pltpu.VMEM((D, F), BF16),
                            pltpu.VMEM((D, F), BF16), pltpu.VMEM((F, D), BF16)]),
        out_shape=jax.ShapeDtypeStruct((P, D), F32),
        compiler_params=_params("arbitrary"),
        name="moe_experts",
    )(tile_expert, n_used, xs, w1, w3, w2)


def _moe_combine_kernel(pos_ref, ys_hbm, x_ref, wts_ref, g_ref, o_ref, y_ref, sem, *, final_norm):
    n = x_ref.shape[0]
    i = pl.program_id(0)

    def start_gather(step, buf):
        for slot in range(MOE_TOPK):
            def issue(r, carry, slot=slot):
                row = pos_ref[(step * n + r) * MOE_TOPK + slot]
                pltpu.make_async_copy(ys_hbm.at[pl.ds(row, 1)], y_ref.at[buf, slot, pl.ds(r, 1)],
                                      sem.at[buf, slot]).start()
                return carry
            lax.fori_loop(0, n, issue, 0, unroll=DMA_ISSUE_UNROLL)

    @pl.when(i == 0)
    def _():
        start_gather(0, 0)

    @pl.when(i + 1 < pl.num_programs(0))
    def _():
        start_gather(i + 1, (i + 1) % 2)

    buf = i % 2
    for slot in range(MOE_TOPK):
        pltpu.make_async_copy(ys_hbm.at[pl.ds(0, n)], y_ref.at[buf, slot], sem.at[buf, slot]).wait()
    out = x_ref[...] + (wts_ref[:, 0:1] * y_ref[buf, 0] + wts_ref[:, 1:2] * y_ref[buf, 1])
    o_ref[...] = _rms(out, g_ref[...]) if final_norm else out


def moe_combine(ys, pos, wts, x, g, *, final_norm, tile=256):
    T, D = x.shape
    return pl.pallas_call(
        functools.partial(_moe_combine_kernel, final_norm=final_norm),
        grid_spec=pltpu.PrefetchScalarGridSpec(
            num_scalar_prefetch=1,
            grid=(T // tile,),
            in_specs=[pl.BlockSpec(memory_space=pl.ANY),
                      pl.BlockSpec((tile, D), lambda i, pos: (i, 0)),
                      pl.BlockSpec((tile, LANES), lambda i, pos: (i, 0)),
                      pl.BlockSpec((1, D), lambda i, pos: (0, 0))],
            out_specs=pl.BlockSpec((tile, D), lambda i, pos: (i, 0)),
            scratch_shapes=[pltpu.VMEM((2, MOE_TOPK, tile, D), F32),
                            pltpu.SemaphoreType.DMA((2, MOE_TOPK))]),
        out_shape=jax.ShapeDtypeStruct((T, D), F32),
        compiler_params=_params("arbitrary"),
        name="moe_combine",
    )(pos, ys, x, wts, g)


def _pack_w_in(w):
    D = w.shape[0]
    widths = (1024, 1024, 1024, 1024, 1024, 1024, 1024, 256, 512, 64, 8, D, D, D)
    offs = [0]
    for wd in widths:
        offs.append(offs[-1] + wd)
    rq, rk, rv, rg, gu, gv, dq, dc, iq, ik, iw, ga, gb, gc = (
        w[:, offs[i]:offs[i + 1]] for i in range(len(widths)))
    ikw_pad = jnp.zeros((D, COL_GATES - COL_IKW - IDX_DIM - IDX_HEADS), w.dtype)
    packed = jnp.concatenate([rq, rk, rv, rg, gu, gv, dq, iq, dc, ik, iw, ikw_pad, ga, gb, gc], axis=1)
    assert packed.shape[1] == PROJ_WIDTH
    return packed.astype(BF16)


def _pack_router(w_group, b_group, w_expert, b_expert):
    D = w_group.shape[0]
    w_e = jnp.transpose(w_expert, (1, 0, 2)).reshape(D, MOE_EXPERTS)
    pad = LANES - MOE_GROUPS - MOE_EXPERTS
    w = jnp.concatenate([w_group, w_e, jnp.zeros((D, pad), w_group.dtype)], axis=1).astype(BF16)
    b = jnp.concatenate([b_group, b_expert.reshape(MOE_EXPERTS), jnp.zeros((pad,), b_group.dtype)])
    return w, b.reshape(1, LANES).astype(F32)


def kernel(x, norm_mix_g, w_in, w_ret_o, gmlp_ln_g, gmlp_ln_b, gmlp_w_s, gmlp_b_s, w_gmlp_o, dsa_kv_norm_g, dsa_w_uk, dsa_w_uv, w_dsa_o, w_out, norm_ffn_g, moe_w_group, moe_b_group, moe_w_expert, moe_b_expert, moe_w1, moe_w3, moe_w2, final_norm_g):
    B, S, D = x.shape
    depth = w_in.shape[0]
    assert S % DSA_BLOCK == 0 and D == 2048
    xt = x.reshape(B * S, D)
    for l in range(depth):
        proj = norm_matmul(xt, norm_mix_g[l].reshape(1, D), _pack_w_in(w_in[l]))
        ret = retention(proj, B, S)
        gm = gmlp(proj, gmlp_ln_g[l], gmlp_ln_b[l], gmlp_w_s[l], gmlp_b_s[l])
        q_abs, iq_pad, ckv, ckv_t = dsa_prep(proj, dsa_kv_norm_g[l], dsa_w_uk[l].astype(BF16))
        ds = dsa_attention(proj, q_abs, iq_pad, ckv, ckv_t, dsa_w_uv[l].astype(BF16), B, S)
        merged = merge_branches(ret, gm, ds, w_ret_o[l].astype(BF16), w_gmlp_o[l].astype(BF16),
                                w_dsa_o[l].astype(BF16), proj, D)
        w_router, b_router = _pack_router(moe_w_group[l], moe_b_group[l], moe_w_expert[l], moe_b_expert[l])
        xt, hp, ids, wts = out_proj_router(merged, w_out[l].astype(BF16), xt, norm_ffn_g[l].reshape(1, D),
                                           w_router, b_router)
        n_rows = MOE_TOPK * B * S + MOE_EXPERTS * MOE_TILE
        pos, tile_expert, n_used = _route_tables(ids, n_rows)
        xs = scatter_rows(hp, pos, n_rows)
        F = moe_w1.shape[-1]
        ys = moe_experts(xs, tile_expert + l * MOE_EXPERTS, n_used,
                         moe_w1.reshape(depth * MOE_EXPERTS, D, F),
                         moe_w3.reshape(depth * MOE_EXPERTS, D, F),
                         moe_w2.reshape(depth * MOE_EXPERTS, F, D))
        last = l == depth - 1
        xt = moe_combine(ys, pos, wts, xt, final_norm_g.reshape(1, D), final_norm=last)
    return xt.reshape(B, S, D)
```

```python
import functools
import math

import jax
import jax.numpy as jnp
from jax import lax
from jax.experimental import pallas as pl
from jax.experimental.pallas import tpu as pltpu

F32 = jnp.float32
BF16 = jnp.bfloat16

RMS_EPS = 1e-6
RET_HEADS = 4
RET_DIM = 256
CHUNK = 128
ROPE_BASE = 10000.0
GMLP_GROUPS = 4
GMLP_GROUP_DIM = 256
DSA_HEADS = 8
DSA_HEAD_DIM = 128
DSA_LATENT = 256
IDX_HEADS = 8
IDX_DIM = 64
DSA_TOPK_MAX = 256
DSA_BLOCK = 256
MOE_GROUPS = 4
MOE_EXPERTS_PER_GROUP = 8
MOE_EXPERTS = MOE_GROUPS * MOE_EXPERTS_PER_GROUP
MOE_HIDDEN = 512
MOE_TOPK = 2
MOE_TILE = 256
LANES = 128
ROW_TILE = 8
DMA_ISSUE_UNROLL = 8
OUT_PROJ_CHUNK = 512
NEG_BIG = -1e30
LOG2_E = math.log2(math.e)

COL_RQ, COL_RK, COL_RV, COL_RG = 0, 1024, 2048, 3072
COL_GU, COL_GV = 4096, 5120
COL_DQ, COL_IQ, COL_DC, COL_IKW = 6144, 7168, 7680, 7936
COL_GATES = 8192
PROJ_WIDTH = 14336
VMEM_LIMIT = 56 * 1024 * 1024


def _params(*sem):
    return pltpu.CompilerParams(dimension_semantics=sem, vmem_limit_bytes=VMEM_LIMIT)


def _rms(x, g):
    return x * lax.rsqrt(jnp.mean(x * x, axis=-1, keepdims=True) + RMS_EPS) * g


def _dot(a, b):
    return jnp.dot(a, b, preferred_element_type=F32)


def _dot_nt(a, b):
    return lax.dot_general(a, b, (((1,), (1,)), ((), ())), preferred_element_type=F32)


def _norm_matmul_kernel(x_ref, g_ref, w_ref, o_ref, h_ref):
    @pl.when(pl.program_id(1) == 0)
    def _():
        h_ref[...] = _rms(x_ref[...], g_ref[...]).astype(BF16)

    o_ref[...] = _dot(h_ref[...], w_ref[...]).astype(o_ref.dtype)


def norm_matmul(x, g, w, *, tm=1024, tn=2048, out_dtype=BF16):
    T, D = x.shape
    N = w.shape[1]
    return pl.pallas_call(
        _norm_matmul_kernel,
        grid=(T // tm, N // tn),
        in_specs=[pl.BlockSpec((tm, D), lambda i, j: (i, 0)),
                  pl.BlockSpec((1, D), lambda i, j: (0, 0)),
                  pl.BlockSpec((D, tn), lambda i, j: (0, j))],
        out_specs=pl.BlockSpec((tm, tn), lambda i, j: (i, j)),
        out_shape=jax.ShapeDtypeStruct((T, N), out_dtype),
        scratch_shapes=[pltpu.VMEM((tm, D), BF16)],
        compiler_params=_params("parallel", "arbitrary"),
        name="norm_in_proj",
    )(x, g, w)


def _retention_kernel(q_ref, k_ref, v_ref, gate_ref, cos_ref, sin_ref, decay_ref, xi_ref,
                      zeta_ref, gchunk_ref, o_ref, r_ref):
    @pl.when(pl.program_id(0) == 0)
    def _():
        r_ref[...] = jnp.zeros_like(r_ref)

    cos = cos_ref[...]
    sin = sin_ref[...]
    half = RET_DIM // 2

    def rotary(t):
        t1, t2 = t[:, :half], t[:, half:]
        return jnp.concatenate([t1 * cos - t2 * sin, t1 * sin + t2 * cos], axis=-1)

    for b in range(q_ref.shape[0]):
        for h in range(RET_HEADS):
            cols = slice(h * RET_DIM, (h + 1) * RET_DIM)
            q = rotary(q_ref[b, :, cols].astype(F32)).astype(BF16)
            k = rotary(k_ref[b, :, cols].astype(F32)) * (RET_DIM ** -0.5)
            v = v_ref[b, :, cols]
            inner = _dot_nt(q, k.astype(BF16)) * decay_ref[h]
            r_old = r_ref[b, h]
            o = _dot(inner.astype(BF16), v) + _dot(q, r_old.astype(BF16)) * xi_ref[:, h:h + 1]
            kz = (k * zeta_ref[:, h:h + 1]).astype(BF16)
            r_ref[b, h] = r_old * gchunk_ref[h] + _dot(kz.T, v)
            o = o * lax.rsqrt(jnp.mean(o * o, axis=-1, keepdims=True) + RMS_EPS)
            gate = gate_ref[b, :, cols].astype(F32)
            o_ref[b, :, cols] = (gate * jax.nn.sigmoid(gate) * o).astype(o_ref.dtype)


def retention(proj, batch, seq):
    T = proj.shape[0]
    n_chunks = seq // CHUNK
    width = RET_HEADS * RET_DIM
    half = RET_DIM // 2
    inv = ROPE_BASE ** (-jnp.arange(half, dtype=F32) / half)
    ang = jnp.arange(seq).astype(F32)[:, None] * inv[None, :]
    cos, sin = jnp.cos(ang), jnp.sin(ang)
    log_g = jnp.log(1.0 - 2.0 ** (-5.0 - jnp.arange(RET_HEADS, dtype=F32)))
    i = jnp.arange(CHUNK, dtype=F32)
    diff = i[:, None] - i[None, :]
    decay = jnp.where(diff >= 0, jnp.exp(log_g[:, None, None] * jnp.maximum(diff, 0.0)), 0.0)
    xi = jnp.exp(log_g[None, :] * (i[:, None] + 1.0))
    zeta = jnp.exp(log_g[None, :] * (CHUNK - 1.0 - i[:, None]))
    gchunk = jnp.exp(log_g * CHUNK)

    def col_spec(col):
        return pl.BlockSpec((batch, CHUNK, width), lambda n: (0, n, col // width))

    proj3 = proj.reshape(batch, seq, proj.shape[1])
    out = pl.pallas_call(
        _retention_kernel,
        grid=(n_chunks,),
        in_specs=[col_spec(COL_RQ), col_spec(COL_RK), col_spec(COL_RV), col_spec(COL_RG),
                  pl.BlockSpec((CHUNK, half), lambda n: (n, 0)),
                  pl.BlockSpec((CHUNK, half), lambda n: (n, 0)),
                  pl.BlockSpec((RET_HEADS, CHUNK, CHUNK), lambda n: (0, 0, 0)),
                  pl.BlockSpec((CHUNK, RET_HEADS), lambda n: (0, 0)),
                  pl.BlockSpec((CHUNK, RET_HEADS), lambda n: (0, 0)),
                  pl.BlockSpec(memory_space=pltpu.SMEM)],
        out_specs=pl.BlockSpec((batch, CHUNK, width), lambda n: (0, n, 0)),
        out_shape=jax.ShapeDtypeStruct((batch, seq, width), BF16),
        scratch_shapes=[pltpu.VMEM((batch, RET_HEADS, RET_DIM, RET_DIM), F32)],
        compiler_params=_params("arbitrary"),
        name="retention",
    )(proj3, proj3, proj3, proj3, cos, sin, decay, xi, zeta, gchunk)
    return out.reshape(T, width)


def _gelu(x):
    return 0.5 * x * (1.0 + lax.erf(x * (2.0 ** -0.5)))


def _gmlp_kernel(u_ref, v_ref, lng_ref, lnb_ref, ws_ref, bs_ref, o_ref):
    row = lax.broadcasted_iota(jnp.int32, (CHUNK, CHUNK), 0)
    col = lax.broadcasted_iota(jnp.int32, (CHUNK, CHUNK), 1)
    w_tril = [jnp.where(row >= col, ws_ref[g], 0.0).astype(BF16) for g in range(GMLP_GROUPS)]
    for c in range(u_ref.shape[0] // CHUNK):
        rows = slice(c * CHUNK, (c + 1) * CHUNK)
        v = _gelu(v_ref[rows, :].astype(F32))
        mu = jnp.mean(v, axis=-1, keepdims=True)
        var = jnp.mean(jnp.square(v - mu), axis=-1, keepdims=True)
        vn = ((v - mu) * lax.rsqrt(var + RMS_EPS) * lng_ref[...] + lnb_ref[...]).astype(BF16)
        for g in range(GMLP_GROUPS):
            cols = slice(g * GMLP_GROUP_DIM, (g + 1) * GMLP_GROUP_DIM)
            mixed = _dot(w_tril[g], vn[:, cols]) + bs_ref[:, g:g + 1]
            u = _gelu(u_ref[rows, cols].astype(F32))
            o_ref[rows, cols] = (u * mixed).astype(o_ref.dtype)


def gmlp(proj, ln_g, ln_b, w_s, b_s, *, chunks_per_step=4):
    T = proj.shape[0]
    width = GMLP_GROUPS * GMLP_GROUP_DIM
    rows = chunks_per_step * CHUNK
    return pl.pallas_call(
        _gmlp_kernel,
        grid=(T // rows,),
        in_specs=[pl.BlockSpec((rows, width), lambda i: (i, COL_GU // width)),
                  pl.BlockSpec((rows, width), lambda i: (i, COL_GV // width)),
                  pl.BlockSpec((1, width), lambda i: (0, 0)),
                  pl.BlockSpec((1, width), lambda i: (0, 0)),
                  pl.BlockSpec((GMLP_GROUPS, CHUNK, CHUNK), lambda i: (0, 0, 0)),
                  pl.BlockSpec((CHUNK, GMLP_GROUPS), lambda i: (0, 0))],
        out_specs=pl.BlockSpec((rows, width), lambda i: (i, 0)),
        out_shape=jax.ShapeDtypeStruct((T, width), BF16),
        compiler_params=_params("parallel"),
        name="gmlp",
    )(proj, proj, ln_g.reshape(1, width), ln_b.reshape(1, width), w_s, b_s.T)


def _dsa_prep_kernel(dq_ref, dc_ref, iq_ref, g_ref, wuk_ref, qabs_ref, iqp_ref, ckv_ref, ckvt_ref):
    c = _rms(dc_ref[...].astype(F32), g_ref[...])
    ckv_ref[...] = c.astype(ckv_ref.dtype)
    ckvt_ref[0, :DSA_LATENT, :] = c.T.astype(ckvt_ref.dtype)
    ckvt_ref[0, DSA_LATENT:, :] = jnp.ones((ROW_TILE, DSA_BLOCK), ckvt_ref.dtype)
    zeros = jnp.zeros((DSA_BLOCK, LANES - IDX_DIM), iqp_ref.dtype)
    for h in range(DSA_HEADS):
        q = dq_ref[:, h * DSA_HEAD_DIM:(h + 1) * DSA_HEAD_DIM]
        qabs_ref[0, h] = (_dot_nt(q, wuk_ref[h]) * (DSA_HEAD_DIM ** -0.5 * LOG2_E)).astype(qabs_ref.dtype)
    for h in range(IDX_HEADS):
        iqp_ref[0, h] = jnp.concatenate([iq_ref[:, h * IDX_DIM:(h + 1) * IDX_DIM], zeros], axis=-1)


def dsa_prep(proj, kv_norm_g, w_uk):
    T = proj.shape[0]
    nblk = T // DSA_BLOCK
    qw = DSA_HEADS * DSA_HEAD_DIM
    iq_w = IDX_HEADS * IDX_DIM
    return pl.pallas_call(
        _dsa_prep_kernel,
        grid=(nblk,),
        in_specs=[pl.BlockSpec((DSA_BLOCK, qw), lambda i: (i, COL_DQ // qw)),
                  pl.BlockSpec((DSA_BLOCK, DSA_LATENT), lambda i: (i, COL_DC // DSA_LATENT)),
                  pl.BlockSpec((DSA_BLOCK, iq_w), lambda i: (i, COL_IQ // iq_w)),
                  pl.BlockSpec((1, DSA_LATENT), lambda i: (0, 0)),
                  pl.BlockSpec((DSA_HEADS, DSA_LATENT, DSA_HEAD_DIM), lambda i: (0, 0, 0))],
        out_specs=[pl.BlockSpec((1, DSA_HEADS, DSA_BLOCK, DSA_LATENT), lambda i: (i, 0, 0, 0)),
                   pl.BlockSpec((1, IDX_HEADS, DSA_BLOCK, LANES), lambda i: (i, 0, 0, 0)),
                   pl.BlockSpec((DSA_BLOCK, DSA_LATENT), lambda i: (i, 0)),
                   pl.BlockSpec((1, DSA_LATENT + ROW_TILE, DSA_BLOCK), lambda i: (i, 0, 0))],
        out_shape=[jax.ShapeDtypeStruct((nblk, DSA_HEADS, DSA_BLOCK, DSA_LATENT), BF16),
                   jax.ShapeDtypeStruct((nblk, IDX_HEADS, DSA_BLOCK, LANES), BF16),
                   jax.ShapeDtypeStruct((T, DSA_LATENT), BF16),
                   jax.ShapeDtypeStruct((nblk, DSA_LATENT + ROW_TILE, DSA_BLOCK), BF16)],
        compiler_params=_params("parallel"),
        name="dsa_prep",
    )(proj, proj, proj, kv_norm_g.reshape(1, DSA_LATENT), w_uk)


def _dsa_kernel(iqp_ref, wq_ref, kidx_ref, ckv_ref, ckvt_ref, qa_ref, wuv_ref, o_ref,
                score_ref, m_ref, acc_ref, *, k_sel):
    QB = DSA_BLOCK
    blk = pl.program_id(1)
    n_tiles = blk + 1
    ksel_f = float(k_sel)
    key_in_tile = lax.broadcasted_iota(jnp.int32, (QB, QB), 0)
    query_in_blk = lax.broadcasted_iota(jnp.int32, (QB, QB), 1)

    def fold8(x, op):
        return op(x.reshape(QB // 8, 8, QB), axis=0)

    def tile_rows(j):
        return pl.ds(pl.multiple_of(j * QB, QB), QB)

    wi_t = wq_ref[...].astype(F32).T * (IDX_HEADS ** -0.5 * IDX_DIM ** -0.5)

    def score_body(j, stats):
        s_max, s_min, n_ge0, n_gt0 = stats
        kt = kidx_ref[tile_rows(j), :]
        sc = jnp.zeros((QB, QB), F32)
        for h in range(IDX_HEADS):
            lg = _dot_nt(kt, iqp_ref[0, h])
            sc = sc + wi_t[IDX_DIM + h:IDX_DIM + h + 1, :] * jnp.maximum(lg, 0.0)
        causal = (j < blk) | (key_in_tile <= query_in_blk)
        masked = jnp.where(causal, sc, -jnp.inf)
        score_ref[j] = masked
        return (jnp.maximum(s_max, fold8(masked, jnp.max)),
                jnp.minimum(s_min, fold8(jnp.where(causal, sc, jnp.inf), jnp.min)),
                n_ge0 + fold8((masked >= 0.0).astype(F32), jnp.sum),
                n_gt0 + fold8((masked > 0.0).astype(F32), jnp.sum))

    stats = lax.fori_loop(0, n_tiles, score_body,
                          (jnp.full((8, QB), -jnp.inf, F32), jnp.full((8, QB), jnp.inf, F32),
                           jnp.zeros((8, QB), F32), jnp.zeros((8, QB), F32)))
    hi0 = jnp.max(stats[0], axis=0, keepdims=True)
    lo0 = jnp.min(stats[1], axis=0, keepdims=True)
    cnt_ge0 = jnp.sum(stats[2], axis=0, keepdims=True)
    cnt_gt0 = jnp.sum(stats[3], axis=0, keepdims=True)

    COUNT_LANES = 4

    def count_negative(diff):
        def body(j, acc):
            sign = lax.shift_right_logical(lax.bitcast_convert_type(diff(score_ref[j]), jnp.uint32),
                                           jnp.uint32(31))
            return acc + jnp.sum(sign.astype(jnp.int32).reshape(-1, COUNT_LANES, 8, QB), axis=0)
        acc = lax.fori_loop(0, n_tiles, body, jnp.zeros((COUNT_LANES, 8, QB), jnp.int32))
        return jnp.sum(acc.reshape(COUNT_LANES * 8, QB), axis=0, keepdims=True).astype(F32)

    n_stored = (n_tiles * QB).astype(F32)

    def count_ge(t):
        return n_stored - count_negative(lambda s: s - t)

    n_causal = (blk * QB + 1 + lax.broadcasted_iota(jnp.int32, (1, QB), 1)).astype(F32)
    need = n_causal > ksel_f
    cnt_hi0 = count_ge(hi0)
    above0 = cnt_gt0 >= ksel_f
    below0 = cnt_ge0 < ksel_f
    lo0 = jnp.where(below0, lo0, 0.0)
    cnt_lo0 = jnp.where(below0, n_causal, cnt_ge0)
    hi0 = jnp.where(above0, hi0, 0.0)
    at_top = above0 & (cnt_hi0 >= ksel_f)
    lo0 = jnp.where(at_top, hi0, lo0)
    cnt_lo0 = jnp.where(at_top, cnt_hi0, cnt_lo0)
    cnt_hi0 = jnp.where(above0, cnt_hi0, cnt_ge0)

    def unresolved(lo, hi, cnt_lo, cnt_hi):
        return (cnt_lo != ksel_f) & (lo < hi) & (cnt_lo - cnt_hi > 2.0)

    active0 = need & (above0 | below0) & unresolved(lo0, hi0, cnt_lo0, cnt_hi0)

    def bisect_cond(state):
        return jnp.max(state[4]) > 0.0

    def bisect_body(state):
        lo, hi, cnt_lo, cnt_hi, active = state
        mid = lo + 0.5 * (hi - lo)
        c = count_ge(mid)
        ge = c >= ksel_f
        live = active > 0.0
        moved = (mid > lo) & (mid < hi)
        new_lo = jnp.where(live & ge, mid, lo)
        new_hi = jnp.where(live & ~ge, mid, hi)
        new_cnt_lo = jnp.where(live & ge, c, cnt_lo)
        new_cnt_hi = jnp.where(live & ~ge, c, cnt_hi)
        new_active = live & moved & unresolved(new_lo, new_hi, new_cnt_lo, new_cnt_hi)
        return new_lo, new_hi, new_cnt_lo, new_cnt_hi, new_active.astype(F32)

    lo, hi, cnt_lo, cnt_hi, _ = lax.while_loop(
        bisect_cond, bisect_body, (lo0, hi0, cnt_lo0, cnt_hi0, active0.astype(F32)))

    def max_below_body(j, acc):
        s = score_ref[j]
        return jnp.maximum(acc, jnp.max(jnp.where(s < hi, s, -jnp.inf).reshape(-1, COUNT_LANES, 8, QB), axis=0))
    max_below = lax.fori_loop(0, n_tiles, max_below_body, jnp.full((COUNT_LANES, 8, QB), -jnp.inf, F32))
    max_below = jnp.max(max_below.reshape(COUNT_LANES * 8, QB), axis=0, keepdims=True)
    is_pair = (cnt_lo != ksel_f) & (lo < hi) & (cnt_lo - cnt_hi == 2.0)
    thr = jnp.where(need, jnp.where(is_pair, max_below, lo), -jnp.inf)
    n_gt = count_negative(lambda s: thr - s)
    tie_quota = jnp.where(need, ksel_f - n_gt, 0.0)

    m_ref[...] = jnp.full_like(m_ref, NEG_BIG)
    acc_ref[...] = jnp.zeros_like(acc_ref)
    strict_lower = (query_in_blk < key_in_tile).astype(BF16)

    def attn_body(j, tie_seen):
        sc = score_ref[j]
        eq = sc == thr
        eq_f = eq.astype(F32)
        prefix = _dot(strict_lower, eq_f.astype(BF16)) + tie_seen
        sel = (sc > thr) | (eq & (prefix < tie_quota))
        bias = jnp.where(sel, 0.0, NEG_BIG)
        c_t = ckv_ref[tile_rows(j), :]
        ct_t = ckvt_ref[j]
        for h in range(DSA_HEADS):
            s = _dot_nt(c_t, qa_ref[0, h]) + bias
            m_old = m_ref[h]
            m_new = jnp.maximum(m_old, jnp.max(fold8(s, jnp.max), axis=0, keepdims=True))
            alpha = jnp.exp2(m_old - m_new)
            p = jnp.exp2(s - m_new)
            acc_ref[h] = alpha * acc_ref[h] + _dot(ct_t, p.astype(BF16))
            m_ref[h] = m_new
        return tie_seen + jnp.sum(fold8(eq_f, jnp.sum), axis=0, keepdims=True)

    lax.fori_loop(0, n_tiles, attn_body, jnp.zeros((1, QB), F32))

    for h in range(DSA_HEADS):
        acc = acc_ref[h]
        o_lat = (acc[:DSA_LATENT] / acc[DSA_LATENT:DSA_LATENT + 1]).T.astype(BF16)
        o_ref[:, h * DSA_HEAD_DIM:(h + 1) * DSA_HEAD_DIM] = _dot(o_lat, wuv_ref[h]).astype(o_ref.dtype)


def dsa_attention(proj, q_abs, iq_pad, ckv, ckv_t, w_uv, batch, seq):
    T = proj.shape[0]
    nb = seq // DSA_BLOCK
    k_sel = min(DSA_TOPK_MAX, seq // 4)
    out_w = DSA_HEADS * DSA_HEAD_DIM
    return pl.pallas_call(
        functools.partial(_dsa_kernel, k_sel=k_sel),
        grid=(batch, nb),
        in_specs=[pl.BlockSpec((1, IDX_HEADS, DSA_BLOCK, LANES), lambda b, i: (b * nb + i, 0, 0, 0)),
                  pl.BlockSpec((DSA_BLOCK, LANES), lambda b, i: (b * nb + i, COL_IKW // LANES)),
                  pl.BlockSpec((seq, LANES), lambda b, i: (b, COL_IKW // LANES)),
                  pl.BlockSpec((seq, DSA_LATENT), lambda b, i: (b, 0)),
                  pl.BlockSpec((nb, DSA_LATENT + ROW_TILE, DSA_BLOCK), lambda b, i: (b, 0, 0)),
                  pl.BlockSpec((1, DSA_HEADS, DSA_BLOCK, DSA_LATENT), lambda b, i: (b * nb + i, 0, 0, 0)),
                  pl.BlockSpec((DSA_HEADS, DSA_LATENT, DSA_HEAD_DIM), lambda b, i: (0, 0, 0))],
        out_specs=pl.BlockSpec((DSA_BLOCK, out_w), lambda b, i: (b * nb + i, 0)),
        out_shape=jax.ShapeDtypeStruct((T, out_w), BF16),
        scratch_shapes=[pltpu.VMEM((nb, DSA_BLOCK, DSA_BLOCK), F32),
                        pltpu.VMEM((DSA_HEADS, 1, DSA_BLOCK), F32),
                        pltpu.VMEM((DSA_HEADS, DSA_LATENT + ROW_TILE, DSA_BLOCK), F32)],
        compiler_params=_params("parallel", "arbitrary"),
        name="dsa_attention",
    )(iq_pad, proj, proj, ckv, ckv_t, q_abs, w_uv)


def _merge_kernel(ret_ref, gm_ref, ds_ref, wr_ref, wg_ref, wd_ref, ga_ref, gb_ref, gc_ref, o_ref):
    def branch(a_ref, w_ref, gate_ref):
        return jax.nn.sigmoid(gate_ref[...].astype(F32)) * _dot(a_ref[...], w_ref[...])

    merged = branch(ret_ref, wr_ref, ga_ref) + branch(gm_ref, wg_ref, gb_ref) + branch(ds_ref, wd_ref, gc_ref)
    o_ref[...] = merged.astype(o_ref.dtype)


def merge_branches(ret, gm, ds, w_ret_o, w_gmlp_o, w_dsa_o, proj, d_model, *, tm=1024, tn=1024):
    T, K = ret.shape
    gate_blk = COL_GATES // tn
    per_gate = d_model // tn
    act = pl.BlockSpec((tm, K), lambda i, j: (i, 0))
    wgt = pl.BlockSpec((K, tn), lambda i, j: (0, j))

    def gate(which):
        return pl.BlockSpec((tm, tn), lambda i, j: (i, gate_blk + which * per_gate + j))

    return pl.pallas_call(
        _merge_kernel,
        grid=(T // tm, d_model // tn),
        in_specs=[act, act, act, wgt, wgt, wgt, gate(0), gate(1), gate(2)],
        out_specs=pl.BlockSpec((tm, tn), lambda i, j: (i, j)),
        out_shape=jax.ShapeDtypeStruct((T, d_model), BF16),
        compiler_params=_params("parallel", "arbitrary"),
        name="merge_branches",
    )(ret, gm, ds, w_ret_o, w_gmlp_o, w_dsa_o, proj, proj, proj)


def _pack_rows(h_bf16, o_ref):
    n, d = h_bf16.shape
    bits = lax.bitcast_convert_type(h_bf16.astype(F32), jnp.uint32)
    packed = (bits[:, :d // 2] >> 16) | (bits[:, d // 2:] & jnp.uint32(0xFFFF0000))
    for c in range(d // 2 // LANES):
        o_ref[pl.ds(c, n, stride=ROW_TILE), :] = packed[:, c * LANES:(c + 1) * LANES]


def _unpack_rows(x_ref, o_ref):
    n, d = o_ref.shape
    for c in range(d // 2 // LANES):
        w = x_ref[pl.ds(c, n, stride=ROW_TILE), :]
        lo = lax.bitcast_convert_type(w << 16, F32)
        hi = lax.bitcast_convert_type(w & jnp.uint32(0xFFFF0000), F32)
        o_ref[:, c * LANES:(c + 1) * LANES] = lo.astype(o_ref.dtype)
        o_ref[:, d // 2 + c * LANES:d // 2 + (c + 1) * LANES] = hi.astype(o_ref.dtype)


def _out_proj_router_kernel(a_ref, wo_ref, x_ref, g_ref, w_ref, b_ref, x1_ref, hp_ref, ids_ref, wts_ref):
    d = x_ref.shape[1]
    for c in range(d // OUT_PROJ_CHUNK):
        cols = slice(c * OUT_PROJ_CHUNK, (c + 1) * OUT_PROJ_CHUNK)
        x1_ref[:, cols] = x_ref[:, cols] + _dot(a_ref[...], wo_ref[:, cols])
    h = _rms(x1_ref[...], g_ref[...]).astype(BF16)
    _pack_rows(h, hp_ref)
    logits = _dot(h, w_ref[...]) + b_ref[...]
    lane = lax.broadcasted_iota(jnp.int32, logits.shape, 1)
    big = jnp.int32(LANES)

    def first_argmax(vals, valid):
        v = jnp.where(valid, vals, -jnp.inf)
        m = jnp.max(v, axis=-1, keepdims=True)
        idx = jnp.min(jnp.where(valid & (v == m), lane, big), axis=-1, keepdims=True)
        return m, idx

    is_grp = lane < MOE_GROUPS
    g_max, g_sel = first_argmax(logits, is_grp)
    g_den = jnp.sum(jnp.where(is_grp, jnp.exp(logits - g_max), 0.0), axis=-1, keepdims=True)
    p_g = 1.0 / g_den
    e_lo = MOE_GROUPS + g_sel * MOE_EXPERTS_PER_GROUP
    in_grp = (lane >= e_lo) & (lane < e_lo + MOE_EXPERTS_PER_GROUP)
    m1, i1 = first_argmax(logits, in_grp)
    m2, i2 = first_argmax(logits, in_grp & (lane != i1))
    e2 = jnp.exp(m2 - m1)
    w1 = p_g / (1.0 + e2)
    w2 = p_g * e2 / (1.0 + e2)
    ids_ref[...] = jnp.where(lane == 0, i1 - MOE_GROUPS, jnp.where(lane == 1, i2 - MOE_GROUPS, 0))
    wts_ref[...] = jnp.where(lane == 0, w1, jnp.where(lane == 1, w2, 0.0))


def out_proj_router(merged, w_out, x, g, w_router, b_router, *, tm=512):
    T, D = x.shape
    K = merged.shape[1]
    assert D == 2 * ROW_TILE * LANES
    return pl.pallas_call(
        _out_proj_router_kernel,
        grid=(T // tm,),
        in_specs=[pl.BlockSpec((tm, K), lambda i: (i, 0)),
                  pl.BlockSpec((K, D), lambda i: (0, 0)),
                  pl.BlockSpec((tm, D), lambda i: (i, 0)),
                  pl.BlockSpec((1, D), lambda i: (0, 0)),
                  pl.BlockSpec((D, LANES), lambda i: (0, 0)),
                  pl.BlockSpec((1, LANES), lambda i: (0, 0))],
        out_specs=[pl.BlockSpec((tm, D), lambda i: (i, 0)),
                   pl.BlockSpec((tm * ROW_TILE, LANES), lambda i: (i, 0)),
                   pl.BlockSpec((tm, LANES), lambda i: (i, 0)),
                   pl.BlockSpec((tm, LANES), lambda i: (i, 0))],
        out_shape=[jax.ShapeDtypeStruct((T, D), F32),
                   jax.ShapeDtypeStruct((T * ROW_TILE, LANES), jnp.uint32),
                   jax.ShapeDtypeStruct((T, LANES), jnp.int32),
                   jax.ShapeDtypeStruct((T, LANES), F32)],
        compiler_params=_params("parallel"),
        name="out_proj_router",
    )(merged, w_out, x, g, w_router, b_router)


def _route_tables(ids, n_rows):
    T = ids.shape[0]
    e = ids[:, :MOE_TOPK].reshape(-1)
    onehot = (e[:, None] == jnp.arange(MOE_EXPERTS, dtype=jnp.int32)[None, :]).astype(jnp.int32)
    rank = jnp.sum((jnp.cumsum(onehot, axis=0) - onehot) * onehot, axis=1)
    counts = jnp.sum(onehot, axis=0)
    padded = (counts + MOE_TILE - 1) // MOE_TILE * MOE_TILE
    ends = jnp.cumsum(padded)
    pos = ((ends - padded)[e] + rank).astype(jnp.int32)
    tile_start = jnp.arange(n_rows // MOE_TILE, dtype=jnp.int32) * MOE_TILE
    tile_expert = jnp.sum((ends[None, :] <= tile_start[:, None]).astype(jnp.int32), axis=1)
    tile_expert = jnp.minimum(tile_expert, MOE_EXPERTS - 1)
    n_used = (ends[-1] // MOE_TILE).astype(jnp.int32).reshape(1)
    return pos, tile_expert, n_used


def _token_rows(r, rows_per_token):
    return pl.ds(pl.multiple_of(r * rows_per_token, rows_per_token), rows_per_token)


def _scatter_rows_kernel(pos_ref, hp_ref, xs_in_hbm, xs_hbm, sem):
    del xs_in_hbm
    n = hp_ref.shape[0] // ROW_TILE
    base = pl.program_id(0) * n
    for slot in range(MOE_TOPK):
        def issue(r, carry, slot=slot):
            row = pos_ref[(base + r) * MOE_TOPK + slot]
            pltpu.make_async_copy(hp_ref.at[_token_rows(r, ROW_TILE)],
                                  xs_hbm.at[_token_rows(row, ROW_TILE)], sem.at[slot]).start()
            return carry
        lax.fori_loop(0, n, issue, 0, unroll=DMA_ISSUE_UNROLL)
    for slot in range(MOE_TOPK):
        pltpu.make_async_copy(hp_ref, xs_hbm.at[pl.ds(0, n * ROW_TILE)], sem.at[slot]).wait()


def scatter_rows(hp, pos, n_rows, *, tile=256):
    T = hp.shape[0] // ROW_TILE
    return pl.pallas_call(
        _scatter_rows_kernel,
        grid_spec=pltpu.PrefetchScalarGridSpec(
            num_scalar_prefetch=1,
            grid=(T // tile,),
            in_specs=[pl.BlockSpec((tile * ROW_TILE, LANES), lambda i, pos: (i, 0)),
                      pl.BlockSpec(memory_space=pl.ANY)],
            out_specs=pl.BlockSpec(memory_space=pl.ANY),
            scratch_shapes=[pltpu.SemaphoreType.DMA((MOE_TOPK,))]),
        out_shape=jax.ShapeDtypeStruct((n_rows * ROW_TILE, LANES), jnp.uint32),
        input_output_aliases={2: 0},
        compiler_params=_params("arbitrary"),
        name="moe_scatter_rows",
    )(pos, hp, jnp.zeros((n_rows * ROW_TILE, LANES), jnp.uint32))


def _moe_experts_kernel(tile_expert_ref, n_used_ref, x_ref, w1_ref, w3_ref, w2_ref, o_ref,
                        xb_ref, w1b_ref, w3b_ref, w2b_ref):
    i = pl.program_id(0)

    @pl.when((i == 0) | (tile_expert_ref[i] != tile_expert_ref[jnp.maximum(i - 1, 0)]))
    def _():
        w1b_ref[...] = w1_ref[0].astype(BF16)
        w3b_ref[...] = w3_ref[0].astype(BF16)
        w2b_ref[...] = w2_ref[0].astype(BF16)

    @pl.when(i < n_used_ref[0])
    def _():
        _unpack_rows(x_ref, xb_ref)
        x = xb_ref[...]
        a = _dot(x, w1b_ref[...])
        hid = a * jax.nn.sigmoid(a) * _dot(x, w3b_ref[...])
        o_ref[...] = _dot(hid.astype(BF16), w2b_ref[...])

    @pl.when(i >= n_used_ref[0])
    def _():
        o_ref[...] = jnp.zeros_like(o_ref)


def moe_experts(xs, tile_expert, n_used, w1, w3, w2):
    E, D, F = w1.shape
    P = xs.shape[0] // ROW_TILE

    def wspec(shape):
        return pl.BlockSpec((1,) + shape, lambda i, te, nu: (te[i], 0, 0))

    return pl.pallas_call(
        _moe_experts_kernel,
        grid_spec=pltpu.PrefetchScalarGridSpec(
            num_scalar_prefetch=2,
            grid=(P // MOE_TILE,),
            in_specs=[pl.BlockSpec((MOE_TILE * ROW_TILE, LANES), lambda i, te, nu: (i, 0)),
                      wspec((D, F)), wspec((D, F)), wspec((F, D))],
            out_specs=pl.BlockSpec((MOE_TILE, D), lambda i, te, nu: (i, 0)),
            scratch_shapes=[pltpu.VMEM((MOE_TILE, D), BF16), pltpu.VMEM((D, F), BF16),
                            pltpu.VMEM((D, F), BF16), pltpu.VMEM((F, D), BF16)]),
        out_shape=jax.ShapeDtypeStruct((P, D), F32),
        compiler_params=_params("arbitrary"),
        name="moe_experts",
    )(tile_expert, n_used, xs, w1, w3, w2)


def _moe_combine_kernel(pos_ref, ys_hbm, x_ref, wts_ref, g_ref, o_ref, y_ref, sem, *, final_norm):
    n = x_ref.shape[0]
    i = pl.program_id(0)

    def start_gather(step, buf):
        for slot in range(MOE_TOPK):
            def issue(r, carry, slot=slot):
                row = pos_ref[(step * n + r) * MOE_TOPK + slot]
                pltpu.make_async_copy(ys_hbm.at[pl.ds(row, 1)], y_ref.at[buf, slot, pl.ds(r, 1)],
                                      sem.at[buf, slot]).start()
                return carry
            lax.fori_loop(0, n, issue, 0, unroll=DMA_ISSUE_UNROLL)

    @pl.when(i == 0)
    def _():
        start_gather(0, 0)

    @pl.when(i + 1 < pl.num_programs(0))
    def _():
        start_gather(i + 1, (i + 1) % 2)

    buf = i % 2
    for slot in range(MOE_TOPK):
        pltpu.make_async_copy(ys_hbm.at[pl.ds(0, n)], y_ref.at[buf, slot], sem.at[buf, slot]).wait()
    out = x_ref[...] + (wts_ref[:, 0:1] * y_ref[buf, 0] + wts_ref[:, 1:2] * y_ref[buf, 1])
    o_ref[...] = _rms(out, g_ref[...]) if final_norm else out


def moe_combine(ys, pos, wts, x, g, *, final_norm, tile=256):
    T, D = x.shape
    return pl.pallas_call(
        functools.partial(_moe_combine_kernel, final_norm=final_norm),
        grid_spec=pltpu.PrefetchScalarGridSpec(
            num_scalar_prefetch=1,
            grid=(T // tile,),
            in_specs=[pl.BlockSpec(memory_space=pl.ANY),
                      pl.BlockSpec((tile, D), lambda i, pos: (i, 0)),
                      pl.BlockSpec((tile, LANES), lambda i, pos: (i, 0)),
                      pl.BlockSpec((1, D), lambda i, pos: (0, 0))],
            out_specs=pl.BlockSpec((tile, D), lambda i, pos: (i, 0)),
            scratch_shapes=[pltpu.VMEM((2, MOE_TOPK, tile, D), F32),
                            pltpu.SemaphoreType.DMA((2, MOE_TOPK))]),
        out_shape=jax.ShapeDtypeStruct((T, D), F32),
        compiler_params=_params("arbitrary"),
        name="moe_combine",
    )(pos, ys, x, wts, g)


def _pack_w_in(w):
    D = w.shape[0]
    widths = (1024, 1024, 1024, 1024, 1024, 1024, 1024, 256, 512, 64, 8, D, D, D)
    offs = [0]
    for wd in widths:
        offs.append(offs[-1] + wd)
    rq, rk, rv, rg, gu, gv, dq, dc, iq, ik, iw, ga, gb, gc = (
        w[:, offs[i]:offs[i + 1]] for i in range(len(widths)))
    ikw_pad = jnp.zeros((D, COL_GATES - COL_IKW - IDX_DIM - IDX_HEADS), w.dtype)
    packed = jnp.concatenate([rq, rk, rv, rg, gu, gv, dq, iq, dc, ik, iw, ikw_pad, ga, gb, gc], axis=1)
    assert packed.shape[1] == PROJ_WIDTH
    return packed.astype(BF16)


def _pack_router(w_group, b_group, w_expert, b_expert):
    D = w_group.shape[0]
    w_e = jnp.transpose(w_expert, (1, 0, 2)).reshape(D, MOE_EXPERTS)
    pad = LANES - MOE_GROUPS - MOE_EXPERTS
    w = jnp.concatenate([w_group, w_e, jnp.zeros((D, pad), w_group.dtype)], axis=1).astype(BF16)
    b = jnp.concatenate([b_group, b_expert.reshape(MOE_EXPERTS), jnp.zeros((pad,), b_group.dtype)])
    return w, b.reshape(1, LANES).astype(F32)


def kernel(x, norm_mix_g, w_in, w_ret_o, gmlp_ln_g, gmlp_ln_b, gmlp_w_s, gmlp_b_s, w_gmlp_o, dsa_kv_norm_g, dsa_w_uk, dsa_w_uv, w_dsa_o, w_out, norm_ffn_g, moe_w_group, moe_b_group, moe_w_expert, moe_b_expert, moe_w1, moe_w3, moe_w2, final_norm_g):
    B, S, D = x.shape
    depth = w_in.shape[0]
    assert S % DSA_BLOCK == 0 and D == 2048
    xt = x.reshape(B * S, D)
    for l in range(depth):
        proj = norm_matmul(xt, norm_mix_g[l].reshape(1, D), _pack_w_in(w_in[l]))
        ret = retention(proj, B, S)
        gm = gmlp(proj, gmlp_ln_g[l], gmlp_ln_b[l], gmlp_w_s[l], gmlp_b_s[l])
        q_abs, iq_pad, ckv, ckv_t = dsa_prep(proj, dsa_kv_norm_g[l], dsa_w_uk[l].astype(BF16))
        ds = dsa_attention(proj, q_abs, iq_pad, ckv, ckv_t, dsa_w_uv[l].astype(BF16), B, S)
        merged = merge_branches(ret, gm, ds, w_ret_o[l].astype(BF16), w_gmlp_o[l].astype(BF16),
                                w_dsa_o[l].astype(BF16), proj, D)
        w_router, b_router = _pack_router(moe_w_group[l], moe_b_group[l], moe_w_expert[l], moe_b_expert[l])
        xt, hp, ids, wts = out_proj_router(merged, w_out[l].astype(BF16), xt, norm_ffn_g[l].reshape(1, D),
                                           w_router, b_router)
        n_rows = MOE_TOPK * B * S + MOE_EXPERTS * MOE_TILE
        pos, tile_expert, n_used = _route_tables(ids, n_rows)
        xs = scatter_rows(hp, pos, n_rows)
        F = moe_w1.shape[-1]
        ys = moe_experts(xs, tile_expert + l * MOE_EXPERTS, n_used,
                         moe_w1.reshape(depth * MOE_EXPERTS, D, F),
                         moe_w3.reshape(depth * MOE_EXPERTS, D, F),
                         moe_w2.reshape(depth * MOE_EXPERTS, F, D))
        last = l == depth - 1
        xt = moe_combine(ys, pos, wts, xt, final_norm_g.reshape(1, D), final_norm=last)
    return xt.reshape(B, S, D)
```

```python
import functools
import math

import jax
import jax.numpy as jnp
from jax import lax
from jax.experimental import pallas as pl
from jax.experimental.pallas import tpu as pltpu

F32 = jnp.float32
BF16 = jnp.bfloat16

RMS_EPS = 1e-6
RET_HEADS = 4
RET_DIM = 256
CHUNK = 128
ROPE_BASE = 10000.0
GMLP_GROUPS = 4
GMLP_GROUP_DIM = 256
DSA_HEADS = 8
DSA_HEAD_DIM = 128
DSA_LATENT = 256
IDX_HEADS = 8
IDX_DIM = 64
DSA_TOPK_MAX = 256
DSA_BLOCK = 256
MOE_GROUPS = 4
MOE_EXPERTS_PER_GROUP = 8
MOE_EXPERTS = MOE_GROUPS * MOE_EXPERTS_PER_GROUP
MOE_HIDDEN = 512
MOE_TOPK = 2
MOE_TILE = 256
LANES = 128
ROW_TILE = 8
DMA_ISSUE_UNROLL = 8
OUT_PROJ_CHUNK = 512
NEG_BIG = -1e30
LOG2_E = math.log2(math.e)

COL_RQ, COL_RK, COL_RV, COL_RG = 0, 1024, 2048, 3072
COL_GU, COL_GV = 4096, 5120
COL_DQ, COL_IQ, COL_DC, COL_IKW = 6144, 7168, 7680, 7936
COL_GATES = 8192
PROJ_WIDTH = 14336
VMEM_LIMIT = 56 * 1024 * 1024


def _params(*sem):
    return pltpu.CompilerParams(dimension_semantics=sem, vmem_limit_bytes=VMEM_LIMIT)


def _rms(x, g):
    return x * lax.rsqrt(jnp.mean(x * x, axis=-1, keepdims=True) + RMS_EPS) * g


def _dot(a, b):
    return jnp.dot(a, b, preferred_element_type=F32)


def _dot_nt(a, b):
    return lax.dot_general(a, b, (((1,), (1,)), ((), ())), preferred_element_type=F32)


def _norm_matmul_kernel(x_ref, g_ref, w_ref, o_ref, h_ref):
    @pl.when(pl.program_id(1) == 0)
    def _():
        h_ref[...] = _rms(x_ref[...], g_ref[...]).astype(BF16)

    o_ref[...] = _dot(h_ref[...], w_ref[...]).astype(o_ref.dtype)


def norm_matmul(x, g, w, *, tm=1024, tn=2048, out_dtype=BF16):
    T, D = x.shape
    N = w.shape[1]
    return pl.pallas_call(
        _norm_matmul_kernel,
        grid=(T // tm, N // tn),
        in_specs=[pl.BlockSpec((tm, D), lambda i, j: (i, 0)),
                  pl.BlockSpec((1, D), lambda i, j: (0, 0)),
                  pl.BlockSpec((D, tn), lambda i, j: (0, j))],
        out_specs=pl.BlockSpec((tm, tn), lambda i, j: (i, j)),
        out_shape=jax.ShapeDtypeStruct((T, N), out_dtype),
        scratch_shapes=[pltpu.VMEM((tm, D), BF16)],
        compiler_params=_params("parallel", "arbitrary"),
        name="norm_in_proj",
    )(x, g, w)


def _retention_kernel(q_ref, k_ref, v_ref, gate_ref, cos_ref, sin_ref, decay_ref, xi_ref,
                      zeta_ref, gchunk_ref, o_ref, r_ref):
    @pl.when(pl.program_id(0) == 0)
    def _():
        r_ref[...] = jnp.zeros_like(r_ref)

    cos = cos_ref[...]
    sin = sin_ref[...]
    half = RET_DIM // 2

    def rotary(t):
        t1, t2 = t[:, :half], t[:, half:]
        return jnp.concatenate([t1 * cos - t2 * sin, t1 * sin + t2 * cos], axis=-1)

    for b in range(q_ref.shape[0]):
        for h in range(RET_HEADS):
            cols = slice(h * RET_DIM, (h + 1) * RET_DIM)
            q = rotary(q_ref[b, :, cols].astype(F32)).astype(BF16)
            k = rotary(k_ref[b, :, cols].astype(F32)) * (RET_DIM ** -0.5)
            v = v_ref[b, :, cols]
            inner = _dot_nt(q, k.astype(BF16)) * decay_ref[h]
            r_old = r_ref[b, h]
            o = _dot(inner.astype(BF16), v) + _dot(q, r_old.astype(BF16)) * xi_ref[:, h:h + 1]
            kz = (k * zeta_ref[:, h:h + 1]).astype(BF16)
            r_ref[b, h] = r_old * gchunk_ref[h] + _dot(kz.T, v)
            o = o * lax.rsqrt(jnp.mean(o * o, axis=-1, keepdims=True) + RMS_EPS)
            gate = gate_ref[b, :, cols].astype(F32)
            o_ref[b, :, cols] = (gate * jax.nn.sigmoid(gate) * o).astype(o_ref.dtype)


def retention(proj, batch, seq):
    T = proj.shape[0]
    n_chunks = seq // CHUNK
    width = RET_HEADS * RET_DIM
    half = RET_DIM // 2
    inv = ROPE_BASE ** (-jnp.arange(half, dtype=F32) / half)
    ang = jnp.arange(seq).astype(F32)[:, None] * inv[None, :]
    cos, sin = jnp.cos(ang), jnp.sin(ang)
    log_g = jnp.log(1.0 - 2.0 ** (-5.0 - jnp.arange(RET_HEADS, dtype=F32)))
    i = jnp.arange(CHUNK, dtype=F32)
    diff = i[:, None] - i[None, :]
    decay = jnp.where(diff >= 0, jnp.exp(log_g[:, None, None] * jnp.maximum(diff, 0.0)), 0.0)
    xi = jnp.exp(log_g[None, :] * (i[:, None] + 1.0))
    zeta = jnp.exp(log_g[None, :] * (CHUNK - 1.0 - i[:, None]))
    gchunk = jnp.exp(log_g * CHUNK)

    def col_spec(col):
        return pl.BlockSpec((batch, CHUNK, width), lambda n: (0, n, col // width))

    proj3 = proj.reshape(batch, seq, proj.shape[1])
    out = pl.pallas_call(
        _retention_kernel,
        grid=(n_chunks,),
        in_specs=[col_spec(COL_RQ), col_spec(COL_RK), col_spec(COL_RV), col_spec(COL_RG),
                  pl.BlockSpec((CHUNK, half), lambda n: (n, 0)),
                  pl.BlockSpec((CHUNK, half), lambda n: (n, 0)),
                  pl.BlockSpec((RET_HEADS, CHUNK, CHUNK), lambda n: (0, 0, 0)),
                  pl.BlockSpec((CHUNK, RET_HEADS), lambda n: (0, 0)),
                  pl.BlockSpec((CHUNK, RET_HEADS), lambda n: (0, 0)),
                  pl.BlockSpec(memory_space=pltpu.SMEM)],
        out_specs=pl.BlockSpec((batch, CHUNK, width), lambda n: (0, n, 0)),
        out_shape=jax.ShapeDtypeStruct((batch, seq, width), BF16),
        scratch_shapes=[pltpu.VMEM((batch, RET_HEADS, RET_DIM, RET_DIM), F32)],
        compiler_params=_params("arbitrary"),
        name="retention",
    )(proj3, proj3, proj3, proj3, cos, sin, decay, xi, zeta, gchunk)
    return out.reshape(T, width)


def _gelu(x):
    return 0.5 * x * (1.0 + lax.erf(x * (2.0 ** -0.5)))


def _gmlp_kernel(u_ref, v_ref, lng_ref, lnb_ref, ws_ref, bs_ref, o_ref):
    row = lax.broadcasted_iota(jnp.int32, (CHUNK, CHUNK), 0)
    col = lax.broadcasted_iota(jnp.int32, (CHUNK, CHUNK), 1)
    w_tril = [jnp.where(row >= col, ws_ref[g], 0.0).astype(BF16) for g in range(GMLP_GROUPS)]
    for c in range(u_ref.shape[0] // CHUNK):
        rows = slice(c * CHUNK, (c + 1) * CHUNK)
        v = _gelu(v_ref[rows, :].astype(F32))
        mu = jnp.mean(v, axis=-1, keepdims=True)
        var = jnp.mean(jnp.square(v - mu), axis=-1, keepdims=True)
        vn = ((v - mu) * lax.rsqrt(var + RMS_EPS) * lng_ref[...] + lnb_ref[...]).astype(BF16)
        for g in range(GMLP_GROUPS):
            cols = slice(g * GMLP_GROUP_DIM, (g + 1) * GMLP_GROUP_DIM)
            mixed = _dot(w_tril[g], vn[:, cols]) + bs_ref[:, g:g + 1]
            u = _gelu(u_ref[rows, cols].astype(F32))
            o_ref[rows, cols] = (u * mixed).astype(o_ref.dtype)


def gmlp(proj, ln_g, ln_b, w_s, b_s, *, chunks_per_step=4):
    T = proj.shape[0]
    width = GMLP_GROUPS * GMLP_GROUP_DIM
    rows = chunks_per_step * CHUNK
    return pl.pallas_call(
        _gmlp_kernel,
        grid=(T // rows,),
        in_specs=[pl.BlockSpec((rows, width), lambda i: (i, COL_GU // width)),
                  pl.BlockSpec((rows, width), lambda i: (i, COL_GV // width)),
                  pl.BlockSpec((1, width), lambda i: (0, 0)),
                  pl.BlockSpec((1, width), lambda i: (0, 0)),
                  pl.BlockSpec((GMLP_GROUPS, CHUNK, CHUNK), lambda i: (0, 0, 0)),
                  pl.BlockSpec((CHUNK, GMLP_GROUPS), lambda i: (0, 0))],
        out_specs=pl.BlockSpec((rows, width), lambda i: (i, 0)),
        out_shape=jax.ShapeDtypeStruct((T, width), BF16),
        compiler_params=_params("parallel"),
        name="gmlp",
    )(proj, proj, ln_g.reshape(1, width), ln_b.reshape(1, width), w_s, b_s.T)


def _dsa_prep_kernel(dq_ref, dc_ref, iq_ref, g_ref, wuk_ref, qabs_ref, iqp_ref, ckv_ref, ckvt_ref):
    c = _rms(dc_ref[...].astype(F32), g_ref[...])
    ckv_ref[...] = c.astype(ckv_ref.dtype)
    ckvt_ref[0, :DSA_LATENT, :] = c.T.astype(ckvt_ref.dtype)
    ckvt_ref[0, DSA_LATENT:, :] = jnp.ones((ROW_TILE, DSA_BLOCK), ckvt_ref.dtype)
    zeros = jnp.zeros((DSA_BLOCK, LANES - IDX_DIM), iqp_ref.dtype)
    for h in range(DSA_HEADS):
        q = dq_ref[:, h * DSA_HEAD_DIM:(h + 1) * DSA_HEAD_DIM]
        qabs_ref[0, h] = (_dot_nt(q, wuk_ref[h]) * (DSA_HEAD_DIM ** -0.5 * LOG2_E)).astype(qabs_ref.dtype)
    for h in range(IDX_HEADS):
        iqp_ref[0, h] = jnp.concatenate([iq_ref[:, h * IDX_DIM:(h + 1) * IDX_DIM], zeros], axis=-1)


def dsa_prep(proj, kv_norm_g, w_uk):
    T = proj.shape[0]
    nblk = T // DSA_BLOCK
    qw = DSA_HEADS * DSA_HEAD_DIM
    iq_w = IDX_HEADS * IDX_DIM
    return pl.pallas_call(
        _dsa_prep_kernel,
        grid=(nblk,),
        in_specs=[pl.BlockSpec((DSA_BLOCK, qw), lambda i: (i, COL_DQ // qw)),
                  pl.BlockSpec((DSA_BLOCK, DSA_LATENT), lambda i: (i, COL_DC // DSA_LATENT)),
                  pl.BlockSpec((DSA_BLOCK, iq_w), lambda i: (i, COL_IQ // iq_w)),
                  pl.BlockSpec((1, DSA_LATENT), lambda i: (0, 0)),
                  pl.BlockSpec((DSA_HEADS, DSA_LATENT, DSA_HEAD_DIM), lambda i: (0, 0, 0))],
        out_specs=[pl.BlockSpec((1, DSA_HEADS, DSA_BLOCK, DSA_LATENT), lambda i: (i, 0, 0, 0)),
                   pl.BlockSpec((1, IDX_HEADS, DSA_BLOCK, LANES), lambda i: (i, 0, 0, 0)),
                   pl.BlockSpec((DSA_BLOCK, DSA_LATENT), lambda i: (i, 0)),
                   pl.BlockSpec((1, DSA_LATENT + ROW_TILE, DSA_BLOCK), lambda i: (i, 0, 0))],
        out_shape=[jax.ShapeDtypeStruct((nblk, DSA_HEADS, DSA_BLOCK, DSA_LATENT), BF16),
                   jax.ShapeDtypeStruct((nblk, IDX_HEADS, DSA_BLOCK, LANES), BF16),
                   jax.ShapeDtypeStruct((T, DSA_LATENT), BF16),
                   jax.ShapeDtypeStruct((nblk, DSA_LATENT + ROW_TILE, DSA_BLOCK), BF16)],
        compiler_params=_params("parallel"),
        name="dsa_prep",
    )(proj, proj, proj, kv_norm_g.reshape(1, DSA_LATENT), w_uk)


def _dsa_kernel(iqp_ref, wq_ref, kidx_ref, ckv_ref, ckvt_ref, qa_ref, wuv_ref, o_ref,
                score_ref, m_ref, acc_ref, *, k_sel):
    QB = DSA_BLOCK
    blk = pl.program_id(1)
    n_tiles = blk + 1
    ksel_f = float(k_sel)
    key_in_tile = lax.broadcasted_iota(jnp.int32, (QB, QB), 0)
    query_in_blk = lax.broadcasted_iota(jnp.int32, (QB, QB), 1)

    def fold8(x, op):
        return op(x.reshape(QB // 8, 8, QB), axis=0)

    def tile_rows(j):
        return pl.ds(pl.multiple_of(j * QB, QB), QB)

    wi_t = wq_ref[...].astype(F32).T * (IDX_HEADS ** -0.5 * IDX_DIM ** -0.5)

    def score_body(j, stats):
        s_max, s_min, n_ge0, n_gt0 = stats
        kt = kidx_ref[tile_rows(j), :]
        sc = jnp.zeros((QB, QB), F32)
        for h in range(IDX_HEADS):
            lg = _dot_nt(kt, iqp_ref[0, h])
            sc = sc + wi_t[IDX_DIM + h:IDX_DIM + h + 1, :] * jnp.maximum(lg, 0.0)
        causal = (j < blk) | (key_in_tile <= query_in_blk)
        masked = jnp.where(causal, sc, -jnp.inf)
        score_ref[j] = masked
        return (jnp.maximum(s_max, fold8(masked, jnp.max)),
                jnp.minimum(s_min, fold8(jnp.where(causal, sc, jnp.inf), jnp.min)),
                n_ge0 + fold8((masked >= 0.0).astype(F32), jnp.sum),
                n_gt0 + fold8((masked > 0.0).astype(F32), jnp.sum))

    stats = lax.fori_loop(0, n_tiles, score_body,
                          (jnp.full((8, QB), -jnp.inf, F32), jnp.full((8, QB), jnp.inf, F32),
                           jnp.zeros((8, QB), F32), jnp.zeros((8, QB), F32)))
    hi0 = jnp.max(stats[0], axis=0, keepdims=True)
    lo0 = jnp.min(stats[1], axis=0, keepdims=True)
    cnt_ge0 = jnp.sum(stats[2], axis=0, keepdims=True)
    cnt_gt0 = jnp.sum(stats[3], axis=0, keepdims=True)

    COUNT_LANES = 4

    def count_negative(diff):
        def body(j, acc):
            sign = lax.shift_right_logical(lax.bitcast_convert_type(diff(score_ref[j]), jnp.uint32),
                                           jnp.uint32(31))
            return acc + jnp.sum(sign.astype(jnp.int32).reshape(-1, COUNT_LANES, 8, QB), axis=0)
        acc = lax.fori_loop(0, n_tiles, body, jnp.zeros((COUNT_LANES, 8, QB), jnp.int32))
        return jnp.sum(acc.reshape(COUNT_LANES * 8, QB), axis=0, keepdims=True).astype(F32)

    n_stored = (n_tiles * QB).astype(F32)

    def count_ge(t):
        return n_stored - count_negative(lambda s: s - t)

    n_causal = (blk * QB + 1 + lax.broadcasted_iota(jnp.int32, (1, QB), 1)).astype(F32)
    need = n_causal > ksel_f
    cnt_hi0 = count_ge(hi0)
    above0 = cnt_gt0 >= ksel_f
    below0 = cnt_ge0 < ksel_f
    lo0 = jnp.where(below0, lo0, 0.0)
    cnt_lo0 = jnp.where(below0, n_causal, cnt_ge0)
    hi0 = jnp.where(above0, hi0, 0.0)
    at_top = above0 & (cnt_hi0 >= ksel_f)
    lo0 = jnp.where(at_top, hi0, lo0)
    cnt_lo0 = jnp.where(at_top, cnt_hi0, cnt_lo0)
    cnt_hi0 = jnp.where(above0, cnt_hi0, cnt_ge0)

    def unresolved(lo, hi, cnt_lo, cnt_hi):
        return (cnt_lo != ksel_f) & (lo < hi) & (cnt_lo - cnt_hi > 2.0)

    active0 = need & (above0 | below0) & unresolved(lo0, hi0, cnt_lo0, cnt_hi0)

    def bisect_cond(state):
        return jnp.max(state[4]) > 0.0

    def bisect_body(state):
        lo, hi, cnt_lo, cnt_hi, active = state
        mid = lo + 0.5 * (hi - lo)
        c = count_ge(mid)
        ge = c >= ksel_f
        live = active > 0.0
        moved = (mid > lo) & (mid < hi)
        new_lo = jnp.where(live & ge, mid, lo)
        new_hi = jnp.where(live & ~ge, mid, hi)
        new_cnt_lo = jnp.where(live & ge, c, cnt_lo)
        new_cnt_hi = jnp.where(live & ~ge, c, cnt_hi)
        new_active = live & moved & unresolved(new_lo, new_hi, new_cnt_lo, new_cnt_hi)
        return new_lo, new_hi, new_cnt_lo, new_cnt_hi, new_active.astype(F32)

    lo, hi, cnt_lo, cnt_hi, _ = lax.while_loop(
        bisect_cond, bisect_body, (lo0, hi0, cnt_lo0, cnt_hi0, active0.astype(F32)))

    def max_below_body(j, acc):
        s = score_ref[j]
        return jnp.maximum(acc, jnp.max(jnp.where(s < hi, s, -jnp.inf).reshape(-1, COUNT_LANES, 8, QB), axis=0))
    max_below = lax.fori_loop(0, n_tiles, max_below_body, jnp.full((COUNT_LANES, 8, QB), -jnp.inf, F32))
    max_below = jnp.max(max_below.reshape(COUNT_LANES * 8, QB), axis=0, keepdims=True)
    is_pair = (cnt_lo != ksel_f) & (lo < hi) & (cnt_lo - cnt_hi == 2.0)
    thr = jnp.where(need, jnp.where(is_pair, max_below, lo), -jnp.inf)
    n_gt = count_negative(lambda s: thr - s)
    tie_quota = jnp.where(need, ksel_f - n_gt, 0.0)

    m_ref[...] = jnp.full_like(m_ref, NEG_BIG)
    acc_ref[...] = jnp.zeros_like(acc_ref)
    strict_lower = (query_in_blk < key_in_tile).astype(BF16)

    def attn_body(j, tie_seen):
        sc = score_ref[j]
        eq = sc == thr
        eq_f = eq.astype(F32)
        prefix = _dot(strict_lower, eq_f.astype(BF16)) + tie_seen
        sel = (sc > thr) | (eq & (prefix < tie_quota))
        bias = jnp.where(sel, 0.0, NEG_BIG)
        c_t = ckv_ref[tile_rows(j), :]
        ct_t = ckvt_ref[j]
        for h in range(DSA_HEADS):
            s = _dot_nt(c_t, qa_ref[0, h]) + bias
            m_old = m_ref[h]
            m_new = jnp.maximum(m_old, jnp.max(fold8(s, jnp.max), axis=0, keepdims=True))
            alpha = jnp.exp2(m_old - m_new)
            p = jnp.exp2(s - m_new)
            acc_ref[h] = alpha * acc_ref[h] + _dot(ct_t, p.astype(BF16))
            m_ref[h] = m_new
        return tie_seen + jnp.sum(fold8(eq_f, jnp.sum), axis=0, keepdims=True)

    lax.fori_loop(0, n_tiles, attn_body, jnp.zeros((1, QB), F32))

    for h in range(DSA_HEADS):
        acc = acc_ref[h]
        o_lat = (acc[:DSA_LATENT] / acc[DSA_LATENT:DSA_LATENT + 1]).T.astype(BF16)
        o_ref[:, h * DSA_HEAD_DIM:(h + 1) * DSA_HEAD_DIM] = _dot(o_lat, wuv_ref[h]).astype(o_ref.dtype)


def dsa_attention(proj, q_abs, iq_pad, ckv, ckv_t, w_uv, batch, seq):
    T = proj.shape[0]
    nb = seq // DSA_BLOCK
    k_sel = min(DSA_TOPK_MAX, seq // 4)
    out_w = DSA_HEADS * DSA_HEAD_DIM
    return pl.pallas_call(
        functools.partial(_dsa_kernel, k_sel=k_sel),
        grid=(batch, nb),
        in_specs=[pl.BlockSpec((1, IDX_HEADS, DSA_BLOCK, LANES), lambda b, i: (b * nb + i, 0, 0, 0)),
                  pl.BlockSpec((DSA_BLOCK, LANES), lambda b, i: (b * nb + i, COL_IKW // LANES)),
                  pl.BlockSpec((seq, LANES), lambda b, i: (b, COL_IKW // LANES)),
                  pl.BlockSpec((seq, DSA_LATENT), lambda b, i: (b, 0)),
                  pl.BlockSpec((nb, DSA_LATENT + ROW_TILE, DSA_BLOCK), lambda b, i: (b, 0, 0)),
                  pl.BlockSpec((1, DSA_HEADS, DSA_BLOCK, DSA_LATENT), lambda b, i: (b * nb + i, 0, 0, 0)),
                  pl.BlockSpec((DSA_HEADS, DSA_LATENT, DSA_HEAD_DIM), lambda b, i: (0, 0, 0))],
        out_specs=pl.BlockSpec((DSA_BLOCK, out_w), lambda b, i: (b * nb + i, 0)),
        out_shape=jax.ShapeDtypeStruct((T, out_w), BF16),
        scratch_shapes=[pltpu.VMEM((nb, DSA_BLOCK, DSA_BLOCK), F32),
                        pltpu.VMEM((DSA_HEADS, 1, DSA_BLOCK), F32),
                        pltpu.VMEM((DSA_HEADS, DSA_LATENT + ROW_TILE, DSA_BLOCK), F32)],
        compiler_params=_params("parallel", "arbitrary"),
        name="dsa_attention",
    )(iq_pad, proj, proj, ckv, ckv_t, q_abs, w_uv)


def _merge_kernel(ret_ref, gm_ref, ds_ref, wr_ref, wg_ref, wd_ref, ga_ref, gb_ref, gc_ref, o_ref):
    def branch(a_ref, w_ref, gate_ref):
        return jax.nn.sigmoid(gate_ref[...].astype(F32)) * _dot(a_ref[...], w_ref[...])

    merged = branch(ret_ref, wr_ref, ga_ref) + branch(gm_ref, wg_ref, gb_ref) + branch(ds_ref, wd_ref, gc_ref)
    o_ref[...] = merged.astype(o_ref.dtype)


def merge_branches(ret, gm, ds, w_ret_o, w_gmlp_o, w_dsa_o, proj, d_model, *, tm=1024, tn=1024):
    T, K = ret.shape
    gate_blk = COL_GATES // tn
    per_gate = d_model // tn
    act = pl.BlockSpec((tm, K), lambda i, j: (i, 0))
    wgt = pl.BlockSpec((K, tn), lambda i, j: (0, j))

    def gate(which):
        return pl.BlockSpec((tm, tn), lambda i, j: (i, gate_blk + which * per_gate + j))

    return pl.pallas_call(
        _merge_kernel,
        grid=(T // tm, d_model // tn),
        in_specs=[act, act, act, wgt, wgt, wgt, gate(0), gate(1), gate(2)],
        out_specs=pl.BlockSpec((tm, tn), lambda i, j: (i, j)),
        out_shape=jax.ShapeDtypeStruct((T, d_model), BF16),
        compiler_params=_params("parallel", "arbitrary"),
        name="merge_branches",
    )(ret, gm, ds, w_ret_o, w_gmlp_o, w_dsa_o, proj, proj, proj)


def _pack_rows(h_bf16, o_ref):
    n, d = h_bf16.shape
    bits = lax.bitcast_convert_type(h_bf16.astype(F32), jnp.uint32)
    packed = (bits[:, :d // 2] >> 16) | (bits[:, d // 2:] & jnp.uint32(0xFFFF0000))
    for c in range(d // 2 // LANES):
        o_ref[pl.ds(c, n, stride=ROW_TILE), :] = packed[:, c * LANES:(c + 1) * LANES]


def _unpack_rows(x_ref, o_ref):
    n, d = o_ref.shape
    for c in range(d // 2 // LANES):
        w = x_ref[pl.ds(c, n, stride=ROW_TILE), :]
        lo = lax.bitcast_convert_type(w << 16, F32)
        hi = lax.bitcast_convert_type(w & jnp.uint32(0xFFFF0000), F32)
        o_ref[:, c * LANES:(c + 1) * LANES] = lo.astype(o_ref.dtype)
        o_ref[:, d // 2 + c * LANES:d // 2 + (c + 1) * LANES] = hi.astype(o_ref.dtype)


def _out_proj_router_kernel(a_ref, wo_ref, x_ref, g_ref, w_ref, b_ref, x1_ref, hp_ref, ids_ref, wts_ref):
    d = x_ref.shape[1]
    for c in range(d // OUT_PROJ_CHUNK):
        cols = slice(c * OUT_PROJ_CHUNK, (c + 1) * OUT_PROJ_CHUNK)
        x1_ref[:, cols] = x_ref[:, cols] + _dot(a_ref[...], wo_ref[:, cols])
    h = _rms(x1_ref[...], g_ref[...]).astype(BF16)
    _pack_rows(h, hp_ref)
    logits = _dot(h, w_ref[...]) + b_ref[...]
    lane = lax.broadcasted_iota(jnp.int32, logits.shape, 1)
    big = jnp.int32(LANES)

    def first_argmax(vals, valid):
        v = jnp.where(valid, vals, -jnp.inf)
        m = jnp.max(v, axis=-1, keepdims=True)
        idx = jnp.min(jnp.where(valid & (v == m), lane, big), axis=-1, keepdims=True)
        return m, idx

    is_grp = lane < MOE_GROUPS
    g_max, g_sel = first_argmax(logits, is_grp)
    g_den = jnp.sum(jnp.where(is_grp, jnp.exp(logits - g_max), 0.0), axis=-1, keepdims=True)
    p_g = 1.0 / g_den
    e_lo = MOE_GROUPS + g_sel * MOE_EXPERTS_PER_GROUP
    in_grp = (lane >= e_lo) & (lane < e_lo + MOE_EXPERTS_PER_GROUP)
    m1, i1 = first_argmax(logits, in_grp)
    m2, i2 = first_argmax(logits, in_grp & (lane != i1))
    e2 = jnp.exp(m2 - m1)
    w1 = p_g / (1.0 + e2)
    w2 = p_g * e2 / (1.0 + e2)
    ids_ref[...] = jnp.where(lane == 0, i1 - MOE_GROUPS, jnp.where(lane == 1, i2 - MOE_GROUPS, 0))
    wts_ref[...] = jnp.where(lane == 0, w1, jnp.where(lane == 1, w2, 0.0))


def out_proj_router(merged, w_out, x, g, w_router, b_router, *, tm=512):
    T, D = x.shape
    K = merged.shape[1]
    assert D == 2 * ROW_TILE * LANES
    return pl.pallas_call(
        _out_proj_router_kernel,
        grid=(T // tm,),
        in_specs=[pl.BlockSpec((tm, K), lambda i: (i, 0)),
                  pl.BlockSpec((K, D), lambda i: (0, 0)),
                  pl.BlockSpec((tm, D), lambda i: (i, 0)),
                  pl.BlockSpec((1, D), lambda i: (0, 0)),
                  pl.BlockSpec((D, LANES), lambda i: (0, 0)),
                  pl.BlockSpec((1, LANES), lambda i: (0, 0))],
        out_specs=[pl.BlockSpec((tm, D), lambda i: (i, 0)),
                   pl.BlockSpec((tm * ROW_TILE, LANES), lambda i: (i, 0)),
                   pl.BlockSpec((tm, LANES), lambda i: (i, 0)),
                   pl.BlockSpec((tm, LANES), lambda i: (i, 0))],
        out_shape=[jax.ShapeDtypeStruct((T, D), F32),
                   jax.ShapeDtypeStruct((T * ROW_TILE, LANES), jnp.uint32),
                   jax.ShapeDtypeStruct((T, LANES), jnp.int32),
                   jax.ShapeDtypeStruct((T, LANES), F32)],
        compiler_params=_params("parallel"),
        name="out_proj_router",
    )(merged, w_out, x, g, w_router, b_router)


def _route_tables(ids, n_rows):
    T = ids.shape[0]
    e = ids[:, :MOE_TOPK].reshape(-1)
    onehot = (e[:, None] == jnp.arange(MOE_EXPERTS, dtype=jnp.int32)[None, :]).astype(jnp.int32)
    rank = jnp.sum((jnp.cumsum(onehot, axis=0) - onehot) * onehot, axis=1)
    counts = jnp.sum(onehot, axis=0)
    padded = (counts + MOE_TILE - 1) // MOE_TILE * MOE_TILE
    ends = jnp.cumsum(padded)
    pos = ((ends - padded)[e] + rank).astype(jnp.int32)
    tile_start = jnp.arange(n_rows // MOE_TILE, dtype=jnp.int32) * MOE_TILE
    tile_expert = jnp.sum((ends[None, :] <= tile_start[:, None]).astype(jnp.int32), axis=1)
    tile_expert = jnp.minimum(tile_expert, MOE_EXPERTS - 1)
    n_used = (ends[-1] // MOE_TILE).astype(jnp.int32).reshape(1)
    return pos, tile_expert, n_used


def _token_rows(r, rows_per_token):
    return pl.ds(pl.multiple_of(r * rows_per_token, rows_per_token), rows_per_token)


def _scatter_rows_kernel(pos_ref, hp_ref, xs_in_hbm, xs_hbm, sem):
    del xs_in_hbm
    n = hp_ref.shape[0] // ROW_TILE
    base = pl.program_id(0) * n
    for slot in range(MOE_TOPK):
        def issue(blk, carry, slot=slot):
            for k in range(DMA_ISSUE_UNROLL):
                r = blk * DMA_ISSUE_UNROLL + k
                row = pos_ref[(base + r) * MOE_TOPK + slot]
                pltpu.make_async_copy(hp_ref.at[_token_rows(r, ROW_TILE)],
                                      xs_hbm.at[_token_rows(row, ROW_TILE)],
                                      sem.at[slot]).start(priority=k % 2)
            return carry
        lax.fori_loop(0, n // DMA_ISSUE_UNROLL, issue, 0)
    for slot in range(MOE_TOPK):
        pltpu.make_async_copy(hp_ref, xs_hbm.at[pl.ds(0, n * ROW_TILE)], sem.at[slot]).wait()


def scatter_rows(hp, pos, n_rows, *, tile=256):
    T = hp.shape[0] // ROW_TILE
    return pl.pallas_call(
        _scatter_rows_kernel,
        grid_spec=pltpu.PrefetchScalarGridSpec(
            num_scalar_prefetch=1,
            grid=(T // tile,),
            in_specs=[pl.BlockSpec((tile * ROW_TILE, LANES), lambda i, pos: (i, 0)),
                      pl.BlockSpec(memory_space=pl.ANY)],
            out_specs=pl.BlockSpec(memory_space=pl.ANY),
            scratch_shapes=[pltpu.SemaphoreType.DMA((MOE_TOPK,))]),
        out_shape=jax.ShapeDtypeStruct((n_rows * ROW_TILE, LANES), jnp.uint32),
        input_output_aliases={2: 0},
        compiler_params=_params("arbitrary"),
        name="moe_scatter_rows",
    )(pos, hp, jnp.zeros((n_rows * ROW_TILE, LANES), jnp.uint32))


def _moe_experts_kernel(tile_expert_ref, n_used_ref, x_ref, w1_ref, w3_ref, w2_ref, o_ref,
                        xb_ref, w1b_ref, w3b_ref, w2b_ref):
    i = pl.program_id(0)

    @pl.when((i == 0) | (tile_expert_ref[i] != tile_expert_ref[jnp.maximum(i - 1, 0)]))
    def _():
        w1b_ref[...] = w1_ref[0].astype(BF16)
        w3b_ref[...] = w3_ref[0].astype(BF16)
        w2b_ref[...] = w2_ref[0].astype(BF16)

    @pl.when(i < n_used_ref[0])
    def _():
        _unpack_rows(x_ref, xb_ref)
        x = xb_ref[...]
        a = _dot(x, w1b_ref[...])
        hid = a * jax.nn.sigmoid(a) * _dot(x, w3b_ref[...])
        o_ref[...] = _dot(hid.astype(BF16), w2b_ref[...])

    @pl.when(i >= n_used_ref[0])
    def _():
        o_ref[...] = jnp.zeros_like(o_ref)


def moe_experts(xs, tile_expert, n_used, w1, w3, w2):
    E, D, F = w1.shape
    P = xs.shape[0] // ROW_TILE

    def wspec(shape):
        return pl.BlockSpec((1,) + shape, lambda i, te, nu: (te[i], 0, 0))

    return pl.pallas_call(
        _moe_experts_kernel,
        grid_spec=pltpu.PrefetchScalarGridSpec(
            num_scalar_prefetch=2,
            grid=(P // MOE_TILE,),
            in_specs=[pl.BlockSpec((MOE_TILE * ROW_TILE, LANES), lambda i, te, nu: (i, 0)),
                      wspec((D, F)), wspec((D, F)), wspec((F, D))],
            out_specs=pl.BlockSpec((MOE_TILE, D), lambda i, te, nu: (i, 0)),
            scratch_shapes=[pltpu.VMEM((MOE_TILE, D), BF16), pltpu.VMEM((D, F), BF16),
                            pltpu.VMEM((D, F), BF16), pltpu.VMEM((F, D), BF16)]),
        out_shape=jax.ShapeDtypeStruct((P, D), F32),
        compiler_params=_params("arbitrary"),
        name="moe_experts",
    )(tile_expert, n_used, xs, w1, w3, w2)


def _moe_combine_kernel(pos_ref, ys_hbm, x_ref, wts_ref, g_ref, o_ref, y_ref, sem, *, final_norm):
    n = x_ref.shape[0]
    i = pl.program_id(0)

    def start_gather(step, buf):
        for slot in range(MOE_TOPK):
            def issue(blk, carry, slot=slot):
                for k in range(DMA_ISSUE_UNROLL):
                    r = blk * DMA_ISSUE_UNROLL + k
                    row = pos_ref[(step * n + r) * MOE_TOPK + slot]
                    pltpu.make_async_copy(ys_hbm.at[pl.ds(row, 1)], y_ref.at[buf, slot, pl.ds(r, 1)],
                                          sem.at[buf, slot]).start(priority=k % 2)
                return carry
            lax.fori_loop(0, n // DMA_ISSUE_UNROLL, issue, 0)

    @pl.when(i == 0)
    def _():
        start_gather(0, 0)

    @pl.when(i + 1 < pl.num_programs(0))
    def _():
        start_gather(i + 1, (i + 1) % 2)

    buf = i % 2
    for slot in range(MOE_TOPK):
        pltpu.make_async_copy(ys_hbm.at[pl.ds(0, n)], y_ref.at[buf, slot], sem.at[buf, slot]).wait()
    out = x_ref[...] + (wts_ref[:, 0:1] * y_ref[buf, 0] + wts_ref[:, 1:2] * y_ref[buf, 1])
    o_ref[...] = _rms(out, g_ref[...]) if final_norm else out


def moe_combine(ys, pos, wts, x, g, *, final_norm, tile=256):
    T, D = x.shape
    return pl.pallas_call(
        functools.partial(_moe_combine_kernel, final_norm=final_norm),
        grid_spec=pltpu.PrefetchScalarGridSpec(
            num_scalar_prefetch=1,
            grid=(T // tile,),
            in_specs=[pl.BlockSpec(memory_space=pl.ANY),
                      pl.BlockSpec((tile, D), lambda i, pos: (i, 0)),
                      pl.BlockSpec((tile, LANES), lambda i, pos: (i, 0)),
                      pl.BlockSpec((1, D), lambda i, pos: (0, 0))],
            out_specs=pl.BlockSpec((tile, D), lambda i, pos: (i, 0)),
            scratch_shapes=[pltpu.VMEM((2, MOE_TOPK, tile, D), F32),
                            pltpu.SemaphoreType.DMA((2, MOE_TOPK))]),
        out_shape=jax.ShapeDtypeStruct((T, D), F32),
        compiler_params=_params("arbitrary"),
        name="moe_combine",
    )(pos, ys, x, wts, g)


def _pack_w_in(w):
    D = w.shape[0]
    widths = (1024, 1024, 1024, 1024, 1024, 1024, 1024, 256, 512, 64, 8, D, D, D)
    offs = [0]
    for wd in widths:
        offs.append(offs[-1] + wd)
    rq, rk, rv, rg, gu, gv, dq, dc, iq, ik, iw, ga, gb, gc = (
        w[:, offs[i]:offs[i + 1]] for i in range(len(widths)))
    ikw_pad = jnp.zeros((D, COL_GATES - COL_IKW - IDX_DIM - IDX_HEADS), w.dtype)
    packed = jnp.concatenate([rq, rk, rv, rg, gu, gv, dq, iq, dc, ik, iw, ikw_pad, ga, gb, gc], axis=1)
    assert packed.shape[1] == PROJ_WIDTH
    return packed.astype(BF16)


def _pack_router(w_group, b_group, w_expert, b_expert):
    D = w_group.shape[0]
    w_e = jnp.transpose(w_expert, (1, 0, 2)).reshape(D, MOE_EXPERTS)
    pad = LANES - MOE_GROUPS - MOE_EXPERTS
    w = jnp.concatenate([w_group, w_e, jnp.zeros((D, pad), w_group.dtype)], axis=1).astype(BF16)
    b = jnp.concatenate([b_group, b_expert.reshape(MOE_EXPERTS), jnp.zeros((pad,), b_group.dtype)])
    return w, b.reshape(1, LANES).astype(F32)


def kernel(x, norm_mix_g, w_in, w_ret_o, gmlp_ln_g, gmlp_ln_b, gmlp_w_s, gmlp_b_s, w_gmlp_o, dsa_kv_norm_g, dsa_w_uk, dsa_w_uv, w_dsa_o, w_out, norm_ffn_g, moe_w_group, moe_b_group, moe_w_expert, moe_b_expert, moe_w1, moe_w3, moe_w2, final_norm_g):
    B, S, D = x.shape
    depth = w_in.shape[0]
    assert S % DSA_BLOCK == 0 and D == 2048
    xt = x.reshape(B * S, D)
    for l in range(depth):
        proj = norm_matmul(xt, norm_mix_g[l].reshape(1, D), _pack_w_in(w_in[l]))
        ret = retention(proj, B, S)
        gm = gmlp(proj, gmlp_ln_g[l], gmlp_ln_b[l], gmlp_w_s[l], gmlp_b_s[l])
        q_abs, iq_pad, ckv, ckv_t = dsa_prep(proj, dsa_kv_norm_g[l], dsa_w_uk[l].astype(BF16))
        ds = dsa_attention(proj, q_abs, iq_pad, ckv, ckv_t, dsa_w_uv[l].astype(BF16), B, S)
        merged = merge_branches(ret, gm, ds, w_ret_o[l].astype(BF16), w_gmlp_o[l].astype(BF16),
                                w_dsa_o[l].astype(BF16), proj, D)
        w_router, b_router = _pack_router(moe_w_group[l], moe_b_group[l], moe_w_expert[l], moe_b_expert[l])
        xt, hp, ids, wts = out_proj_router(merged, w_out[l].astype(BF16), xt, norm_ffn_g[l].reshape(1, D),
                                           w_router, b_router)
        n_rows = MOE_TOPK * B * S + MOE_EXPERTS * MOE_TILE
        pos, tile_expert, n_used = _route_tables(ids, n_rows)
        xs = scatter_rows(hp, pos, n_rows)
        F = moe_w1.shape[-1]
        ys = moe_experts(xs, tile_expert + l * MOE_EXPERTS, n_used,
                         moe_w1.reshape(depth * MOE_EXPERTS, D, F),
                         moe_w3.reshape(depth * MOE_EXPERTS, D, F),
                         moe_w2.reshape(depth * MOE_EXPERTS, F, D))
        last = l == depth - 1
        xt = moe_combine(ys, pos, wts, xt, final_norm_g.reshape(1, D), final_norm=last)
    return xt.reshape(B, S, D)
```
